```python
import jax, jax.numpy as jnp
from jax import lax
import numpy as np

D_MODEL = 1024
BATCH = 4
SEQ = 4096
DEPTH = 4

D_MIX = D_MODEL
D_POOL = D_MIX // 2
POOL_WINDOWS = (2, 4, 8, 16)
N_POOL_GROUPS = len(POOL_WINDOWS)
POOL_GROUP = D_POOL // N_POOL_GROUPS
D_CONV = D_MIX - D_POOL
CONV_WIDTH = 3
D_PROJ = D_POOL + 3 * D_CONV

N_EXPERTS = 16
N_EXPERT_GROUPS = 4
EXPERTS_PER_GROUP = N_EXPERTS // N_EXPERT_GROUPS
TOP_K = 2
D_EXPERT = D_MODEL // 2
EXPERT_BLOCK = 128

ALPHA = float((2 * DEPTH) ** 0.25)
BETA = float((8 * DEPTH) ** -0.25)
LN_EPS = 1e-5

kernel_name = "hybrid_pool_conv_grouped_moe_deepnorm"


def layer_norm(x, g, b):
    xf = x.astype(jnp.float32)
    mu = jnp.mean(xf, axis=-1, keepdims=True)
    xc = xf - mu
    var = jnp.mean(xc * xc, axis=-1, keepdims=True)
    return (xc * lax.rsqrt(var + LN_EPS) * g + b).astype(x.dtype)


def causal_multiscale_pool(v, pool_w, pool_scale):
    bsz, s, _ = v.shape
    vf = v.astype(jnp.float32)
    csum = jnp.cumsum(vf, axis=1)
    n_avail = jnp.arange(1, s + 1, dtype=jnp.float32)
    outs = []
    for g, win in enumerate(POOL_WINDOWS):
        sl = slice(g * POOL_GROUP, (g + 1) * POOL_GROUP)
        cg = csum[..., sl]
        prev = jnp.pad(cg, ((0, 0), (win, 0), (0, 0)))[:, :s]
        mean = (cg - prev) / jnp.minimum(n_avail, float(win))[None, :, None]
        outs.append(mean - vf[..., sl])
    pooled = jnp.stack(outs, axis=2).astype(v.dtype)
    mixed = jnp.einsum("bsgc,gcd->bsgd", pooled, pool_w).reshape(bsz, s, D_POOL)
    return mixed * pool_scale


def gated_short_conv(gate_b, gate_c, u, conv_w, conv_b):
    z = gate_c * u
    y = lax.conv_general_dilated(
        z, conv_w[:, None, :], window_strides=(1,), padding=[(CONV_WIDTH - 1, 0)],
        dimension_numbers=("NWC", "WIO", "NWC"), feature_group_count=D_CONV)
    return gate_b * (y + conv_b)


def hybrid_mixer(h, w_in, pool_w, pool_scale, conv_w, conv_b, w_out):
    proj = h @ w_in
    v = proj[..., :D_POOL]
    gb = proj[..., D_POOL:D_POOL + D_CONV]
    gc = proj[..., D_POOL + D_CONV:D_POOL + 2 * D_CONV]
    u = proj[..., D_POOL + 2 * D_CONV:]
    pool_out = causal_multiscale_pool(v, pool_w, pool_scale)
    conv_out = gated_short_conv(gb, gc, u, conv_w, conv_b)
    return jnp.concatenate([pool_out, conv_out], axis=-1) @ w_out


def grouped_moe(x, w_router, router_bias, w1, w3, w2):
    t, d = x.shape
    probs = jax.nn.softmax((x @ w_router).astype(jnp.float32), axis=-1)
    sel = probs + router_bias.astype(jnp.float32)
    grp_score = lax.top_k(sel.reshape(t, N_EXPERT_GROUPS, EXPERTS_PER_GROUP), 2)[0].sum(-1)
    gidx = jnp.argmax(grp_score, axis=-1)
    gmask = jnp.arange(N_EXPERT_GROUPS)[None, :] == gidx[:, None]
    emask = jnp.repeat(gmask, EXPERTS_PER_GROUP, axis=1)
    _, eidx = lax.top_k(jnp.where(emask, sel, -jnp.inf), TOP_K)
    gate = jnp.take_along_axis(probs, eidx, axis=-1)
    gate = gate / jnp.sum(gate, axis=-1, keepdims=True)

    m = t * TOP_K
    flat_e = eidx.reshape(m).astype(jnp.int32)
    flat_tok = jnp.arange(m, dtype=jnp.int32) // TOP_K
    order = jnp.argsort(flat_e, stable=True)
    e_sorted = flat_e[order]
    counts = jnp.bincount(flat_e, length=N_EXPERTS)
    starts = jnp.cumsum(counts) - counts
    padded = ((counts + EXPERT_BLOCK - 1) // EXPERT_BLOCK) * EXPERT_BLOCK
    pend = jnp.cumsum(padded)
    pstarts = pend - padded
    rank = jnp.arange(m, dtype=jnp.int32) - starts[e_sorted]
    dest_sorted = (pstarts[e_sorted] + rank).astype(jnp.int32)
    dest = jnp.zeros((m,), jnp.int32).at[order].set(dest_sorted)
    n_rows = ((m + EXPERT_BLOCK - 1) // EXPERT_BLOCK + N_EXPERTS) * EXPERT_BLOCK
    n_blocks = n_rows // EXPERT_BLOCK
    xbuf = jnp.zeros((n_rows, d), x.dtype).at[dest].set(x[flat_tok])
    blk_start = jnp.arange(n_blocks, dtype=jnp.int32) * EXPERT_BLOCK
    blk_e = jnp.minimum(jnp.searchsorted(pend, blk_start, side="right"), N_EXPERTS - 1)

    def expert_block(args):
        xb, e = args
        hid = jax.nn.silu(xb @ w1[e]) * (xb @ w3[e])
        return hid @ w2[e]

    ybuf = lax.map(expert_block, (xbuf.reshape(n_blocks, EXPERT_BLOCK, d), blk_e))
    ybuf = ybuf.reshape(n_rows, d)
    y = ybuf[dest].reshape(t, TOP_K, d) * gate[..., None].astype(x.dtype)
    return jnp.sum(y, axis=1)


def setup_inputs(seed: int = 0) -> dict:
    key = jax.random.key(seed)
    ks = jax.random.split(key, 20)
    f32 = jnp.float32
    nrm = lambda k, shape: jax.random.normal(k, shape, f32)
    return {
        "x": nrm(ks[0], (BATCH, SEQ, D_MODEL)),
        "ln_in_g": 1.0 + 0.02 * nrm(ks[1], (D_MODEL,)),
        "ln_in_b": 0.02 * nrm(ks[2], (D_MODEL,)),
        "w_in": nrm(ks[3], (DEPTH, D_MODEL, D_PROJ)) * D_MODEL ** -0.5,
        "pool_w": nrm(ks[4], (DEPTH, N_POOL_GROUPS, POOL_GROUP, POOL_GROUP)) * POOL_GROUP ** -0.5,
        "pool_scale": 1.0 + 0.1 * nrm(ks[5], (DEPTH, D_POOL)),
        "conv_w": nrm(ks[6], (DEPTH, CONV_WIDTH, D_CONV)) * CONV_WIDTH ** -0.5,
        "conv_b": 0.02 * nrm(ks[7], (DEPTH, D_CONV)),
        "w_out": nrm(ks[8], (DEPTH, D_MIX, D_MODEL)) * (D_MIX ** -0.5 * BETA),
        "ln1_g": 1.0 + 0.02 * nrm(ks[9], (DEPTH, D_MODEL)),
        "ln1_b": 0.02 * nrm(ks[10], (DEPTH, D_MODEL)),
        "w_router": nrm(ks[11], (D_MODEL, N_EXPERTS)) * D_MODEL ** -0.5,
        "router_bias": 0.01 * nrm(ks[12], (N_EXPERTS,)),
        "exp_w1": nrm(ks[13], (DEPTH, N_EXPERTS, D_MODEL, D_EXPERT)) * D_MODEL ** -0.5,
        "exp_w3": nrm(ks[14], (DEPTH, N_EXPERTS, D_MODEL, D_EXPERT)) * D_MODEL ** -0.5,
        "exp_w2": nrm(ks[15], (DEPTH, N_EXPERTS, D_EXPERT, D_MODEL)) * (D_EXPERT ** -0.5 * BETA),
        "ln2_g": 1.0 + 0.02 * nrm(ks[16], (DEPTH, D_MODEL)),
        "ln2_b": 0.02 * nrm(ks[17], (DEPTH, D_MODEL)),
    }


def reference(x, ln_in_g, ln_in_b, w_in, pool_w, pool_scale, conv_w, conv_b, w_out,
              ln1_g, ln1_b, w_router, router_bias, exp_w1, exp_w3, exp_w2, ln2_g, ln2_b):
    bsz, s, d = x.shape
    h = layer_norm(x, ln_in_g, ln_in_b)
    for l in range(DEPTH):
        mix = hybrid_mixer(h, w_in[l], pool_w[l], pool_scale[l], conv_w[l], conv_b[l], w_out[l])
        h = layer_norm(ALPHA * h + mix, ln1_g[l], ln1_b[l])
        ffn = grouped_moe(h.reshape(bsz * s, d), w_router, router_bias,
                          exp_w1[l], exp_w3[l], exp_w2[l]).reshape(bsz, s, d)
        h = layer_norm(ALPHA * h + ffn, ln2_g[l], ln2_b[l])
    return h
```

```python
import functools

import jax
import jax.numpy as jnp
from jax import lax
from jax.experimental import pallas as pl
from jax.experimental.pallas import tpu as pltpu

D_MODEL = 1024
DEPTH = 4
D_POOL = 512
POOL_WINDOWS = (2, 4, 8, 16)
POOL_GROUP = 128
D_CONV = 512
CONV_WIDTH = 3
D_PROJ = D_POOL + 3 * D_CONV
N_EXPERTS = 16
N_EXPERT_GROUPS = 4
EXPERTS_PER_GROUP = 4
TOP_K = 2
D_EXPERT = 512
ALPHA = float((2 * DEPTH) ** 0.25)
LN_EPS = 1e-5

F32 = jnp.float32
BF16 = jnp.bfloat16

TS = 512
V_HALO = 16
Z_HALO = 8
TB = 512
BM = 256
VMEM_LIMIT = 56 * 1024 * 1024


def _layer_norm(x, g, b):
    mu = jnp.mean(x, axis=-1, keepdims=True)
    xc = x - mu
    var = jnp.mean(xc * xc, axis=-1, keepdims=True)
    return xc * lax.rsqrt(var + LN_EPS) * g + b


def _dot(a, b):
    return jnp.dot(a, b, preferred_element_type=F32)


def _ln_kernel(x_ref, g_ref, b_ref, o_ref):
    o_ref[...] = _layer_norm(x_ref[...], g_ref[...], b_ref[...])


def _input_layer_norm(x, g, b):
    t = x.shape[0]
    return pl.pallas_call(
        _ln_kernel,
        grid=(t // TS,),
        in_specs=[
            pl.BlockSpec((TS, D_MODEL), lambda i: (i, 0)),
            pl.BlockSpec((1, D_MODEL), lambda i: (0, 0)),
            pl.BlockSpec((1, D_MODEL), lambda i: (0, 0)),
        ],
        out_specs=pl.BlockSpec((TS, D_MODEL), lambda i: (i, 0)),
        out_shape=jax.ShapeDtypeStruct((t, D_MODEL), F32),
        compiler_params=pltpu.CompilerParams(dimension_semantics=("arbitrary",)),
        name="input_ln",
    )(x, g, b)


def _route(sel, probs):
    rows = [sel[e:e + 1, :] for e in range(N_EXPERTS)]
    prow = [probs[e:e + 1, :] for e in range(N_EXPERTS)]
    gscore = []
    for g in range(N_EXPERT_GROUPS):
        a, b, c, d = rows[4 * g:4 * g + 4]
        hi1, lo1 = jnp.maximum(a, b), jnp.minimum(a, b)
        hi2, lo2 = jnp.maximum(c, d), jnp.minimum(c, d)
        m1 = jnp.maximum(hi1, hi2)
        m2 = jnp.maximum(jnp.minimum(hi1, hi2), jnp.maximum(lo1, lo2))
        gscore.append(m1 + m2)
    best = gscore[0]
    gidx = jnp.zeros(best.shape, jnp.int32)
    for g in range(1, N_EXPERT_GROUPS):
        upd = gscore[g] > best
        best = jnp.where(upd, gscore[g], best)
        gidx = jnp.where(upd, g, gidx)
    neg = jnp.full(best.shape, -jnp.inf, F32)
    masked = [jnp.where(gidx == (e // EXPERTS_PER_GROUP), rows[e], neg) for e in range(N_EXPERTS)]

    def top1(vals):
        bv = neg
        bi = jnp.zeros(best.shape, jnp.int32)
        bp = jnp.zeros(best.shape, F32)
        for e in range(N_EXPERTS):
            upd = vals[e] > bv
            bv = jnp.where(upd, vals[e], bv)
            bi = jnp.where(upd, e, bi)
            bp = jnp.where(upd, prow[e], bp)
        return bi, bp

    i1, p1 = top1(masked)
    masked2 = [jnp.where(i1 == e, neg, masked[e]) for e in range(N_EXPERTS)]
    i2, p2 = top1(masked2)
    den = p1 + p2
    return i1, i2, p1 / den, p2 / den


def _mixer_kernel(h_ref, win_ref, pw_ref, ps_ref, cw_ref, cb_ref, wout_ref, g_ref, b_ref,
                  wr_ref, rb_ref, tri_ref,
                  h1_ref, eidx_ref, gate_ref, rank_ref, cnt_ref,
                  vbuf, zbuf, mixbuf, carry):
    bi = pl.program_id(0)
    si = pl.program_id(1)

    @pl.when(si == 0)
    def _():
        vbuf[0:V_HALO, :] = jnp.zeros((V_HALO, D_POOL), F32)
        zbuf[0:Z_HALO, :] = jnp.zeros((Z_HALO, D_CONV), F32)

    @pl.when((bi == 0) & (si == 0))
    def _():
        carry[...] = jnp.zeros(carry.shape, F32)

    h = h_ref[...]
    hb = h.astype(BF16)
    vbuf[V_HALO:V_HALO + TS, :] = _dot(hb, win_ref[:, 0:D_POOL])
    gate_b = _dot(hb, win_ref[:, D_POOL:D_POOL + D_CONV])
    gate_c = _dot(hb, win_ref[:, D_POOL + D_CONV:D_POOL + 2 * D_CONV])
    u = _dot(hb, win_ref[:, D_POOL + 2 * D_CONV:D_PROJ])
    z = gate_c * u
    zbuf[Z_HALO:Z_HALO + TS, :] = z

    tpos = si * TS + lax.broadcasted_iota(jnp.int32, (TS, 1), 0)
    for g, win in enumerate(POOL_WINDOWS):
        cols = pl.ds(g * POOL_GROUP, POOL_GROUP)
        cur = vbuf[pl.ds(V_HALO, TS), cols]
        acc = cur
        for j in range(1, win):
            acc = acc + vbuf[pl.ds(V_HALO - j, TS), cols]
        denom = jnp.minimum(tpos + 1, win).astype(F32)
        pooled = acc / denom - cur
        mixed = _dot(pooled.astype(BF16), pw_ref[g]) * ps_ref[:, cols]
        mixbuf[:, cols] = mixed.astype(BF16)

    y = (cw_ref[2:3, :] * z
         + cw_ref[1:2, :] * zbuf[pl.ds(Z_HALO - 1, TS), :]
         + cw_ref[0:1, :] * zbuf[pl.ds(Z_HALO - 2, TS), :])
    mixbuf[:, D_POOL:D_POOL + D_CONV] = (gate_b * (y + cb_ref[...])).astype(BF16)

    vbuf[0:V_HALO, :] = vbuf[TS:TS + V_HALO, :]
    zbuf[0:Z_HALO, :] = zbuf[TS:TS + Z_HALO, :]

    mix = _dot(mixbuf[...], wout_ref[...])
    h1 = _layer_norm(ALPHA * h + mix, g_ref[...], b_ref[...])
    h1_ref[...] = h1

    logits = lax.dot_general(wr_ref[...], h1.astype(BF16), (((1,), (1,)), ((), ())),
                             preferred_element_type=F32)
    mx = jnp.max(logits, axis=0, keepdims=True)
    ex = jnp.exp(logits - mx)
    probs = ex / jnp.sum(ex, axis=0, keepdims=True)
    sel = probs + rb_ref[...]
    i1, i2, g1, g2 = _route(sel, probs)
    eidx_ref[0:1, :] = i1
    eidx_ref[1:2, :] = i2
    gate_ref[0:1, :] = g1
    gate_ref[1:2, :] = g2

    eiota = lax.broadcasted_iota(jnp.int32, (N_EXPERTS, TS), 0)
    oh0 = jnp.where(eiota == i1, 1.0, 0.0).astype(F32)
    oh1 = jnp.where(eiota == i2, 1.0, 0.0).astype(F32)
    pre0 = _dot(oh0.astype(BF16), tri_ref[...])
    pre1 = _dot(oh1.astype(BF16), tri_ref[...])
    tot0 = jnp.sum(oh0, axis=1, keepdims=True)
    tot1 = jnp.sum(oh1, axis=1, keepdims=True)
    c = carry[:, 0:1]
    rank0 = jnp.sum(oh0 * (c + pre0), axis=0, keepdims=True)
    rank1 = jnp.sum(oh1 * (c + tot0 + pre1), axis=0, keepdims=True)
    rank_ref[0:1, :] = rank0.astype(jnp.int32)
    rank_ref[1:2, :] = rank1.astype(jnp.int32)
    newc = jnp.broadcast_to(c + tot0 + tot1, carry.shape)
    carry[...] = newc
    cnt_ref[...] = newc


def _mixer(h, win, pw, ps, cw, cb, wout, g, b, wr_t, rbias, tri, bsz, seq):
    t = h.shape[0]
    ns = seq // TS
    const2 = lambda bi, si: (0, 0)
    tile = lambda bi, si: (bi * ns + si, 0)
    lane_tile = lambda bi, si: (0, bi * ns + si)
    return pl.pallas_call(
        _mixer_kernel,
        grid=(bsz, ns),
        in_specs=[
            pl.BlockSpec((TS, D_MODEL), tile),
            pl.BlockSpec((D_MODEL, D_PROJ), const2),
            pl.BlockSpec((len(POOL_WINDOWS), POOL_GROUP, POOL_GROUP), lambda bi, si: (0, 0, 0)),
            pl.BlockSpec((1, D_POOL), const2),
            pl.BlockSpec((CONV_WIDTH, D_CONV), const2),
            pl.BlockSpec((1, D_CONV), const2),
            pl.BlockSpec((D_MODEL, D_MODEL), const2),
            pl.BlockSpec((1, D_MODEL), const2),
            pl.BlockSpec((1, D_MODEL), const2),
            pl.BlockSpec((N_EXPERTS, D_MODEL), const2),
            pl.BlockSpec((N_EXPERTS, 1), const2),
            pl.BlockSpec((TS, TS), const2),
        ],
        out_specs=[
            pl.BlockSpec((TS, D_MODEL), tile),
            pl.BlockSpec((TOP_K, TS), lane_tile),
            pl.BlockSpec((TOP_K, TS), lane_tile),
            pl.BlockSpec((TOP_K, TS), lane_tile),
            pl.BlockSpec((N_EXPERTS, 128), const2),
        ],
        out_shape=[
            jax.ShapeDtypeStruct((t, D_MODEL), F32),
            jax.ShapeDtypeStruct((TOP_K, t), jnp.int32),
            jax.ShapeDtypeStruct((TOP_K, t), F32),
            jax.ShapeDtypeStruct((TOP_K, t), jnp.int32),
            jax.ShapeDtypeStruct((N_EXPERTS, 128), F32),
        ],
        scratch_shapes=[
            pltpu.VMEM((TS + V_HALO, D_POOL), F32),
            pltpu.VMEM((TS + Z_HALO, D_CONV), F32),
            pltpu.VMEM((TS, D_MODEL), BF16),
            pltpu.VMEM((N_EXPERTS, 128), F32),
        ],
        compiler_params=pltpu.CompilerParams(
            dimension_semantics=("arbitrary", "arbitrary"), vmem_limit_bytes=VMEM_LIMIT),
        name="mixer_route",
    )(h, win, pw, ps, cw, cb, wout, g, b, wr_t, rbias, tri)


def _scatter_kernel(dest_ref, h1_ref, xin_ref, xout_ref, sem):
    del xin_ref

    def body(i, carry):
        d0 = dest_ref[0, 0, i]
        d1 = dest_ref[0, 0, TB + i]
        pltpu.make_async_copy(h1_ref.at[pl.ds(i, 1)], xout_ref.at[pl.ds(d0, 1)], sem).start()
        pltpu.make_async_copy(h1_ref.at[pl.ds(i, 1)], xout_ref.at[pl.ds(d1, 1)], sem).start()
        return carry

    lax.fori_loop(0, TB, body, 0, unroll=8)
    for _ in range(TOP_K):
        pltpu.make_async_copy(h1_ref, xout_ref.at[pl.ds(0, TB)], sem).wait()


def _scatter_rows(dest3, h1, xbuf):
    t = h1.shape[0]
    return pl.pallas_call(
        _scatter_kernel,
        grid=(t // TB,),
        in_specs=[
            pl.BlockSpec((1, 1, TOP_K * TB), lambda i: (i, 0, 0), memory_space=pltpu.SMEM),
            pl.BlockSpec((TB, D_MODEL), lambda i: (i, 0)),
            pl.BlockSpec(memory_space=pl.ANY),
        ],
        out_specs=pl.BlockSpec(memory_space=pl.ANY),
        out_shape=jax.ShapeDtypeStruct(xbuf.shape, xbuf.dtype),
        scratch_shapes=[pltpu.SemaphoreType.DMA(())],
        input_output_aliases={2: 0},
        compiler_params=pltpu.CompilerParams(dimension_semantics=("arbitrary",)),
        name="scatter_rows",
    )(dest3, h1, xbuf)


def _expert_kernel(be_ref, nu_ref, x_ref, w1_ref, w3_ref, w2_ref, yin_ref, y_ref):
    del be_ref, yin_ref

    @pl.when(pl.program_id(0) < nu_ref[0])
    def _():
        xb = x_ref[...].astype(BF16)
        a = _dot(xb, w1_ref[0])
        b = _dot(xb, w3_ref[0])
        hid = a * jax.nn.sigmoid(a) * b
        y_ref[...] = _dot(hid.astype(BF16), w2_ref[0])


def _expert_ffn(blk_e, nused, xbuf, w1, w3, w2, ybuf):
    n_rows = xbuf.shape[0]
    nb = n_rows // BM
    row_blk = lambda i, be, nu: (jnp.minimum(i, nu[0] - 1), 0)
    wsel = lambda i, be, nu: (be[jnp.minimum(i, nu[0] - 1)], 0, 0)
    grid_spec = pltpu.PrefetchScalarGridSpec(
        num_scalar_prefetch=2,
        grid=(nb,),
        in_specs=[
            pl.BlockSpec((BM, D_MODEL), row_blk),
            pl.BlockSpec((1, D_MODEL, D_EXPERT), wsel),
            pl.BlockSpec((1, D_MODEL, D_EXPERT), wsel),
            pl.BlockSpec((1, D_EXPERT, D_MODEL), wsel),
            pl.BlockSpec(memory_space=pl.ANY),
        ],
        out_specs=pl.BlockSpec((BM, D_MODEL), row_blk),
    )
    return pl.pallas_call(
        _expert_kernel,
        grid_spec=grid_spec,
        out_shape=jax.ShapeDtypeStruct(ybuf.shape, ybuf.dtype),
        input_output_aliases={6: 0},
        compiler_params=pltpu.CompilerParams(
            dimension_semantics=("arbitrary",), vmem_limit_bytes=VMEM_LIMIT),
        name="expert_ffn",
    )(blk_e, nused, xbuf, w1, w3, w2, ybuf)


def _combine_kernel(dest_ref, h1_ref, gate_ref, y_hbm, g_ref, b_ref, o_ref, ybuf, sem):
    def body(i, carry):
        d0 = dest_ref[0, 0, i]
        d1 = dest_ref[0, 0, TB + i]
        pltpu.make_async_copy(y_hbm.at[pl.ds(d0, 1)], ybuf.at[0, pl.ds(i, 1)], sem).start()
        pltpu.make_async_copy(y_hbm.at[pl.ds(d1, 1)], ybuf.at[1, pl.ds(i, 1)], sem).start()
        return carry

    lax.fori_loop(0, TB, body, 0, unroll=8)
    for k in range(TOP_K):
        pltpu.make_async_copy(y_hbm.at[pl.ds(0, TB)], ybuf.at[k], sem).wait()
    gate = gate_ref[...]
    x2 = ALPHA * h1_ref[...] + gate[:, 0:1] * ybuf[0] + gate[:, 1:2] * ybuf[1]
    o_ref[...] = _layer_norm(x2, g_ref[...], b_ref[...])


def _combine(dest3, h1, gate_t, ybuf, g, b):
    t = h1.shape[0]
    return pl.pallas_call(
        _combine_kernel,
        grid=(t // TB,),
        in_specs=[
            pl.BlockSpec((1, 1, TOP_K * TB), lambda i: (i, 0, 0), memory_space=pltpu.SMEM),
            pl.BlockSpec((TB, D_MODEL), lambda i: (i, 0)),
            pl.BlockSpec((TB, TOP_K), lambda i: (i, 0)),
            pl.BlockSpec(memory_space=pl.ANY),
            pl.BlockSpec((1, D_MODEL), lambda i: (0, 0)),
            pl.BlockSpec((1, D_MODEL), lambda i: (0, 0)),
        ],
        out_specs=pl.BlockSpec((TB, D_MODEL), lambda i: (i, 0)),
        out_shape=jax.ShapeDtypeStruct((t, D_MODEL), F32),
        scratch_shapes=[pltpu.VMEM((TOP_K, TB, D_MODEL), F32), pltpu.SemaphoreType.DMA(())],
        compiler_params=pltpu.CompilerParams(
            dimension_semantics=("arbitrary",), vmem_limit_bytes=VMEM_LIMIT),
        name="combine_ln",
    )(dest3, h1, gate_t, ybuf, g, b)


def kernel(x, ln_in_g, ln_in_b, w_in, pool_w, pool_scale, conv_w, conv_b, w_out, ln1_g, ln1_b,
           w_router, router_bias, exp_w1, exp_w3, exp_w2, ln2_g, ln2_b):
    bsz, seq, d = x.shape
    t = bsz * seq
    m = t * TOP_K
    n_rows = m + N_EXPERTS * BM
    nb = n_rows // BM
    nt = t // TB

    row = lambda v: v.reshape(1, -1)
    tri = (lax.broadcasted_iota(jnp.int32, (TS, TS), 0)
           < lax.broadcasted_iota(jnp.int32, (TS, TS), 1)).astype(BF16)
    wr_t = w_router.T.astype(BF16)
    rbias = router_bias.reshape(N_EXPERTS, 1).astype(F32)

    h = _input_layer_norm(x.reshape(t, d), row(ln_in_g), row(ln_in_b))
    xbuf = jnp.zeros((n_rows, d), F32)
    ybuf = jnp.zeros((n_rows, d), F32)
    for l in range(DEPTH):
        h1, eidx, gate, rank, cnt = _mixer(
            h, w_in[l].astype(BF16), pool_w[l].astype(BF16), row(pool_scale[l]), conv_w[l],
            row(conv_b[l]), w_out[l].astype(BF16), row(ln1_g[l]), row(ln1_b[l]),
            wr_t, rbias, tri, bsz, seq)
        counts = cnt[:, 0].astype(jnp.int32)
        padded = ((counts + BM - 1) // BM) * BM
        pend = jnp.cumsum(padded)
        pstart = pend - padded
        dest = pstart[eidx] + rank
        dest3 = dest.reshape(TOP_K, nt, TB).transpose(1, 0, 2).reshape(nt, 1, TOP_K * TB)
        nused = (pend[-1:] // BM).astype(jnp.int32)
        blk_start = jnp.arange(nb, dtype=jnp.int32) * BM
        blk_e = jnp.minimum(jnp.searchsorted(pend, blk_start, side="right"),
                            N_EXPERTS - 1).astype(jnp.int32)

        xbuf = _scatter_rows(dest3, h1, xbuf)
        ybuf = _expert_ffn(blk_e, nused, xbuf, exp_w1[l].astype(BF16), exp_w3[l].astype(BF16),
                           exp_w2[l].astype(BF16), ybuf)
        h = _combine(dest3, h1, gate.T, ybuf, row(ln2_g[l]), row(ln2_b[l]))
    return h.reshape(bsz, seq, d)
```

```python
import jax
import jax.numpy as jnp
from jax import lax
from jax.experimental import pallas as pl
from jax.experimental.pallas import tpu as pltpu

D_MODEL = 1024
DEPTH = 4
D_POOL = 512
POOL_WINDOWS = (2, 4, 8, 16)
POOL_GROUP = 128
D_CONV = 512
CONV_WIDTH = 3
D_PROJ = D_POOL + 3 * D_CONV
N_EXPERTS = 16
N_EXPERT_GROUPS = 4
EXPERTS_PER_GROUP = 4
TOP_K = 2
D_EXPERT = 512
ALPHA = float((2 * DEPTH) ** 0.25)
LN_EPS = 1e-5

F32 = jnp.float32
BF16 = jnp.bfloat16

TS = 512
V_HALO = 16
Z_HALO = 8
BM = 256
W_STAGE_ROWS = 256
VMEM_LIMIT = 56 * 1024 * 1024


def _layer_norm(x, g, b):
    mu = jnp.mean(x, axis=-1, keepdims=True)
    xc = x - mu
    var = jnp.mean(xc * xc, axis=-1, keepdims=True)
    return xc * lax.rsqrt(var + LN_EPS) * g + b


def _dot(a, b):
    return jnp.dot(a, b, preferred_element_type=F32)


def _ln_kernel(x_ref, g_ref, b_ref, o_ref):
    o_ref[...] = _layer_norm(x_ref[...], g_ref[...], b_ref[...])


def _input_layer_norm(x, g, b):
    t = x.shape[0]
    return pl.pallas_call(
        _ln_kernel,
        grid=(t // TS,),
        in_specs=[
            pl.BlockSpec((TS, D_MODEL), lambda i: (i, 0)),
            pl.BlockSpec((1, D_MODEL), lambda i: (0, 0)),
            pl.BlockSpec((1, D_MODEL), lambda i: (0, 0)),
        ],
        out_specs=pl.BlockSpec((TS, D_MODEL), lambda i: (i, 0)),
        out_shape=jax.ShapeDtypeStruct((t, D_MODEL), F32),
        compiler_params=pltpu.CompilerParams(dimension_semantics=("arbitrary",)),
        name="input_ln",
    )(x, g, b)


def _route(sel, probs):
    rows = [sel[e:e + 1, :] for e in range(N_EXPERTS)]
    prow = [probs[e:e + 1, :] for e in range(N_EXPERTS)]
    gscore = []
    for g in range(N_EXPERT_GROUPS):
        a, b, c, d = rows[4 * g:4 * g + 4]
        hi1, lo1 = jnp.maximum(a, b), jnp.minimum(a, b)
        hi2, lo2 = jnp.maximum(c, d), jnp.minimum(c, d)
        m1 = jnp.maximum(hi1, hi2)
        m2 = jnp.maximum(jnp.minimum(hi1, hi2), jnp.maximum(lo1, lo2))
        gscore.append(m1 + m2)
    best = gscore[0]
    gidx = jnp.zeros(best.shape, jnp.int32)
    for g in range(1, N_EXPERT_GROUPS):
        upd = gscore[g] > best
        best = jnp.where(upd, gscore[g], best)
        gidx = jnp.where(upd, g, gidx)
    neg = jnp.full(best.shape, -jnp.inf, F32)
    masked = [jnp.where(gidx == (e // EXPERTS_PER_GROUP), rows[e], neg) for e in range(N_EXPERTS)]

    def top1(vals):
        bv = neg
        bi = jnp.zeros(best.shape, jnp.int32)
        bp = jnp.zeros(best.shape, F32)
        for e in range(N_EXPERTS):
            upd = vals[e] > bv
            bv = jnp.where(upd, vals[e], bv)
            bi = jnp.where(upd, e, bi)
            bp = jnp.where(upd, prow[e], bp)
        return bi, bp

    i1, p1 = top1(masked)
    masked2 = [jnp.where(i1 == e, neg, masked[e]) for e in range(N_EXPERTS)]
    i2, p2 = top1(masked2)
    den = p1 + p2
    return i1, i2, p1 / den, p2 / den


def _mixer_kernel(h_ref, win_hbm, pw_ref, ps_ref, cw_ref, cb_ref, wout_hbm, g_ref, b_ref,
                  wr_ref, rb_ref, tri_ref,
                  h1_ref, eidx_ref, gate_ref, rank_ref, cnt_ref,
                  vbuf, zbuf, mixbuf, carry, win_ref, wout_ref, stage, sem):
    bi = pl.program_id(0)
    si = pl.program_id(1)

    @pl.when((bi == 0) & (si == 0))
    def _():
        for c in range(D_MODEL // W_STAGE_ROWS):
            rows = pl.ds(c * W_STAGE_ROWS, W_STAGE_ROWS)
            cp = pltpu.make_async_copy(win_hbm.at[rows], stage, sem)
            cp.start()
            cp.wait()
            win_ref[rows, :] = stage[...].astype(BF16)
        for c in range(D_MODEL // W_STAGE_ROWS):
            rows = pl.ds(c * W_STAGE_ROWS, W_STAGE_ROWS)
            cp = pltpu.make_async_copy(wout_hbm.at[rows], stage.at[:, pl.ds(0, D_MODEL)], sem)
            cp.start()
            cp.wait()
            wout_ref[rows, :] = stage[:, 0:D_MODEL].astype(BF16)
        carry[...] = jnp.zeros(carry.shape, F32)

    @pl.when(si == 0)
    def _():
        vbuf[0:V_HALO, :] = jnp.zeros((V_HALO, D_POOL), F32)
        zbuf[0:Z_HALO, :] = jnp.zeros((Z_HALO, D_CONV), F32)

    h = h_ref[...]
    hb = h.astype(BF16)
    vbuf[V_HALO:V_HALO + TS, :] = _dot(hb, win_ref[:, 0:D_POOL])
    gate_b = _dot(hb, win_ref[:, D_POOL:D_POOL + D_CONV])
    gate_c = _dot(hb, win_ref[:, D_POOL + D_CONV:D_POOL + 2 * D_CONV])
    u = _dot(hb, win_ref[:, D_POOL + 2 * D_CONV:D_PROJ])
    z = gate_c * u
    zbuf[Z_HALO:Z_HALO + TS, :] = z

    tpos = si * TS + lax.broadcasted_iota(jnp.int32, (TS, 1), 0)
    for g, win in enumerate(POOL_WINDOWS):
        cols = pl.ds(g * POOL_GROUP, POOL_GROUP)
        cur = vbuf[pl.ds(V_HALO, TS), cols]
        acc = cur
        for j in range(1, win):
            acc = acc + vbuf[pl.ds(V_HALO - j, TS), cols]
        denom = jnp.minimum(tpos + 1, win).astype(F32)
        pooled = acc / denom - cur
        mixed = _dot(pooled.astype(BF16), pw_ref[g].astype(BF16)) * ps_ref[:, cols]
        mixbuf[:, cols] = mixed.astype(BF16)

    y = (cw_ref[2:3, :] * z
         + cw_ref[1:2, :] * zbuf[pl.ds(Z_HALO - 1, TS), :]
         + cw_ref[0:1, :] * zbuf[pl.ds(Z_HALO - 2, TS), :])
    mixbuf[:, D_POOL:D_POOL + D_CONV] = (gate_b * (y + cb_ref[...])).astype(BF16)

    vbuf[0:V_HALO, :] = vbuf[TS:TS + V_HALO, :]
    zbuf[0:Z_HALO, :] = zbuf[TS:TS + Z_HALO, :]

    mix = _dot(mixbuf[...], wout_ref[...])
    h1 = _layer_norm(ALPHA * h + mix, g_ref[...], b_ref[...])
    h1_ref[...] = h1

    logits = lax.dot_general(wr_ref[...].astype(BF16), h1.astype(BF16), (((1,), (1,)), ((), ())),
                             preferred_element_type=F32)
    mx = jnp.max(logits, axis=0, keepdims=True)
    ex = jnp.exp(logits - mx)
    probs = ex / jnp.sum(ex, axis=0, keepdims=True)
    sel = probs + rb_ref[...]
    i1, i2, g1, g2 = _route(sel, probs)
    eidx_ref[0, 0:1, :] = i1
    eidx_ref[0, 1:2, :] = i2
    gate_ref[0:1, :] = g1
    gate_ref[1:2, :] = g2

    eiota = lax.broadcasted_iota(jnp.int32, (N_EXPERTS, TS), 0)
    oh0 = jnp.where(eiota == i1, 1.0, 0.0).astype(F32)
    oh1 = jnp.where(eiota == i2, 1.0, 0.0).astype(F32)
    pre0 = _dot(oh0.astype(BF16), tri_ref[...])
    pre1 = _dot(oh1.astype(BF16), tri_ref[...])
    tot0 = jnp.sum(oh0, axis=1, keepdims=True)
    tot1 = jnp.sum(oh1, axis=1, keepdims=True)
    c = carry[:, 0:1]
    rank0 = jnp.sum(oh0 * (c + pre0), axis=0, keepdims=True)
    rank1 = jnp.sum(oh1 * (c + tot0 + pre1), axis=0, keepdims=True)
    rank_ref[0, 0:1, :] = rank0.astype(jnp.int32)
    rank_ref[0, 1:2, :] = rank1.astype(jnp.int32)
    newc = jnp.broadcast_to(c + tot0 + tot1, carry.shape)
    carry[...] = newc
    cnt_ref[...] = newc


def _mixer(h, win, pw, ps, cw, cb, wout, g, b, wr_t, rbias, tri, bsz, seq):
    t = h.shape[0]
    ns = seq // TS
    nt = t // TS
    const2 = lambda bi, si: (0, 0)
    const3 = lambda bi, si: (0, 0, 0)
    tile = lambda bi, si: (bi * ns + si, 0)
    tile3 = lambda bi, si: (bi * ns + si, 0, 0)
    lane_tile = lambda bi, si: (0, bi * ns + si)
    return pl.pallas_call(
        _mixer_kernel,
        grid=(bsz, ns),
        in_specs=[
            pl.BlockSpec((TS, D_MODEL), tile),
            pl.BlockSpec(memory_space=pl.ANY),
            pl.BlockSpec((len(POOL_WINDOWS), POOL_GROUP, POOL_GROUP), const3),
            pl.BlockSpec((1, D_POOL), const2),
            pl.BlockSpec((CONV_WIDTH, D_CONV), const2),
            pl.BlockSpec((1, D_CONV), const2),
            pl.BlockSpec(memory_space=pl.ANY),
            pl.BlockSpec((1, D_MODEL), const2),
            pl.BlockSpec((1, D_MODEL), const2),
            pl.BlockSpec((N_EXPERTS, D_MODEL), const2),
            pl.BlockSpec((N_EXPERTS, 1), const2),
            pl.BlockSpec((TS, TS), const2),
        ],
        out_specs=[
            pl.BlockSpec((TS, D_MODEL), tile),
            pl.BlockSpec((1, TOP_K, TS), tile3),
            pl.BlockSpec((TOP_K, TS), lane_tile),
            pl.BlockSpec((1, TOP_K, TS), tile3),
            pl.BlockSpec((N_EXPERTS, 128), const2),
        ],
        out_shape=[
            jax.ShapeDtypeStruct((t, D_MODEL), F32),
            jax.ShapeDtypeStruct((nt, TOP_K, TS), jnp.int32),
            jax.ShapeDtypeStruct((TOP_K, t), F32),
            jax.ShapeDtypeStruct((nt, TOP_K, TS), jnp.int32),
            jax.ShapeDtypeStruct((N_EXPERTS, 128), F32),
        ],
        scratch_shapes=[
            pltpu.VMEM((TS + V_HALO, D_POOL), F32),
            pltpu.VMEM((TS + Z_HALO, D_CONV), F32),
            pltpu.VMEM((TS, D_MODEL), BF16),
            pltpu.VMEM((N_EXPERTS, 128), F32),
            pltpu.VMEM((D_MODEL, D_PROJ), BF16),
            pltpu.VMEM((D_MODEL, D_MODEL), BF16),
            pltpu.VMEM((W_STAGE_ROWS, D_PROJ), F32),
            pltpu.SemaphoreType.DMA(()),
        ],
        compiler_params=pltpu.CompilerParams(
            dimension_semantics=("arbitrary", "arbitrary"), vmem_limit_bytes=VMEM_LIMIT),
        name="mixer_route",
    )(h, win, pw, ps, cw, cb, wout, g, b, wr_t, rbias, tri)


def _scatter_kernel(pstart_ref, eidx_ref, rank_ref, h1_ref, xin_ref, xout_ref, sem):
    del xin_ref

    def body(i, carry):
        for k in range(TOP_K):
            d = pstart_ref[eidx_ref[0, k, i]] + rank_ref[0, k, i]
            pltpu.make_async_copy(h1_ref.at[pl.ds(i, 1)], xout_ref.at[pl.ds(d, 1)], sem).start()
        return carry

    lax.fori_loop(0, TS, body, 0, unroll=8)
    for _ in range(TOP_K):
        pltpu.make_async_copy(h1_ref, xout_ref.at[pl.ds(0, TS)], sem).wait()


def _scatter_rows(pstart, eidx3, rank3, h1, xbuf):
    t = h1.shape[0]
    smem_tile = pl.BlockSpec((1, TOP_K, TS), lambda i, ps: (i, 0, 0), memory_space=pltpu.SMEM)
    grid_spec = pltpu.PrefetchScalarGridSpec(
        num_scalar_prefetch=1,
        grid=(t // TS,),
        in_specs=[
            smem_tile,
            smem_tile,
            pl.BlockSpec((TS, D_MODEL), lambda i, ps: (i, 0)),
            pl.BlockSpec(memory_space=pl.ANY),
        ],
        out_specs=pl.BlockSpec(memory_space=pl.ANY),
        scratch_shapes=[pltpu.SemaphoreType.DMA(())],
    )
    return pl.pallas_call(
        _scatter_kernel,
        grid_spec=grid_spec,
        out_shape=jax.ShapeDtypeStruct(xbuf.shape, xbuf.dtype),
        input_output_aliases={4: 0},
        compiler_params=pltpu.CompilerParams(dimension_semantics=("arbitrary",)),
        name="scatter_rows",
    )(pstart, eidx3, rank3, h1, xbuf)


def _block_expert(i, pend_ref):
    r = i * BM
    e = jnp.int32(0)
    for j in range(N_EXPERTS - 1):
        e = e + (r >= pend_ref[j]).astype(jnp.int32)
    return e


def _used_block(i, pend_ref):
    return jnp.minimum(i, pend_ref[N_EXPERTS - 1] // BM - 1)


def _expert_kernel(pend_ref, x_ref, w1_ref, w3_ref, w2_ref, yin_ref, y_ref, w1b, w3b, w2b):
    del yin_ref
    i = pl.program_id(0)
    used = i < pend_ref[N_EXPERTS - 1] // BM
    e_here = _block_expert(i, pend_ref)
    e_prev = _block_expert(jnp.maximum(i - 1, 0), pend_ref)

    @pl.when(used & ((i == 0) | (e_here != e_prev)))
    def _():
        w1b[...] = w1_ref[0].astype(BF16)
        w3b[...] = w3_ref[0].astype(BF16)
        w2b[...] = w2_ref[0].astype(BF16)

    @pl.when(used)
    def _():
        xb = x_ref[...].astype(BF16)
        a = _dot(xb, w1b[...])
        b = _dot(xb, w3b[...])
        hid = a * jax.nn.sigmoid(a) * b
        y_ref[...] = _dot(hid.astype(BF16), w2b[...])


def _expert_ffn(pend, xbuf, w1, w3, w2, ybuf):
    n_rows = xbuf.shape[0]
    row_blk = lambda i, pe: (_used_block(i, pe), 0)
    wsel = lambda i, pe: (_block_expert(_used_block(i, pe), pe), 0, 0)
    grid_spec = pltpu.PrefetchScalarGridSpec(
        num_scalar_prefetch=1,
        grid=(n_rows // BM,),
        in_specs=[
            pl.BlockSpec((BM, D_MODEL), row_blk),
            pl.BlockSpec((1, D_MODEL, D_EXPERT), wsel),
            pl.BlockSpec((1, D_MODEL, D_EXPERT), wsel),
            pl.BlockSpec((1, D_EXPERT, D_MODEL), wsel),
            pl.BlockSpec(memory_space=pl.ANY),
        ],
        out_specs=pl.BlockSpec((BM, D_MODEL), row_blk),
        scratch_shapes=[
            pltpu.VMEM((D_MODEL, D_EXPERT), BF16),
            pltpu.VMEM((D_MODEL, D_EXPERT), BF16),
            pltpu.VMEM((D_EXPERT, D_MODEL), BF16),
        ],
    )
    return pl.pallas_call(
        _expert_kernel,
        grid_spec=grid_spec,
        out_shape=jax.ShapeDtypeStruct(ybuf.shape, ybuf.dtype),
        input_output_aliases={5: 0},
        compiler_params=pltpu.CompilerParams(
            dimension_semantics=("arbitrary",), vmem_limit_bytes=VMEM_LIMIT),
        name="expert_ffn",
    )(pend, xbuf, w1, w3, w2, ybuf)


def _combine_kernel(pstart_ref, eidx_ref, rank_ref, h1_ref, gate_ref, y_hbm, g_ref, b_ref, o_ref,
                    ybuf, sem):
    def body(i, carry):
        for k in range(TOP_K):
            d = pstart_ref[eidx_ref[0, k, i]] + rank_ref[0, k, i]
            pltpu.make_async_copy(y_hbm.at[pl.ds(d, 1)], ybuf.at[k, pl.ds(i, 1)], sem).start()
        return carry

    lax.fori_loop(0, TS, body, 0, unroll=8)
    for k in range(TOP_K):
        pltpu.make_async_copy(y_hbm.at[pl.ds(0, TS)], ybuf.at[k], sem).wait()
    gate = gate_ref[...]
    x2 = ALPHA * h1_ref[...] + gate[:, 0:1] * ybuf[0] + gate[:, 1:2] * ybuf[1]
    o_ref[...] = _layer_norm(x2, g_ref[...], b_ref[...])


def _combine(pstart, eidx3, rank3, h1, gate_t, ybuf, g, b):
    t = h1.shape[0]
    smem_tile = pl.BlockSpec((1, TOP_K, TS), lambda i, ps: (i, 0, 0), memory_space=pltpu.SMEM)
    grid_spec = pltpu.PrefetchScalarGridSpec(
        num_scalar_prefetch=1,
        grid=(t // TS,),
        in_specs=[
            smem_tile,
            smem_tile,
            pl.BlockSpec((TS, D_MODEL), lambda i, ps: (i, 0)),
            pl.BlockSpec((TS, TOP_K), lambda i, ps: (i, 0)),
            pl.BlockSpec(memory_space=pl.ANY),
            pl.BlockSpec((1, D_MODEL), lambda i, ps: (0, 0)),
            pl.BlockSpec((1, D_MODEL), lambda i, ps: (0, 0)),
        ],
        out_specs=pl.BlockSpec((TS, D_MODEL), lambda i, ps: (i, 0)),
        scratch_shapes=[pltpu.VMEM((TOP_K, TS, D_MODEL), F32), pltpu.SemaphoreType.DMA(())],
    )
    return pl.pallas_call(
        _combine_kernel,
        grid_spec=grid_spec,
        out_shape=jax.ShapeDtypeStruct((t, D_MODEL), F32),
        compiler_params=pltpu.CompilerParams(
            dimension_semantics=("arbitrary",), vmem_limit_bytes=VMEM_LIMIT),
        name="combine_ln",
    )(pstart, eidx3, rank3, h1, gate_t, ybuf, g, b)


def kernel(x, ln_in_g, ln_in_b, w_in, pool_w, pool_scale, conv_w, conv_b, w_out, ln1_g, ln1_b,
           w_router, router_bias, exp_w1, exp_w3, exp_w2, ln2_g, ln2_b):
    bsz, seq, d = x.shape
    t = bsz * seq
    n_rows = t * TOP_K + N_EXPERTS * BM

    row = lambda v: v.reshape(1, -1)
    tri = (lax.broadcasted_iota(jnp.int32, (TS, TS), 0)
           < lax.broadcasted_iota(jnp.int32, (TS, TS), 1)).astype(BF16)
    wr_t = w_router.T
    rbias = router_bias.reshape(N_EXPERTS, 1).astype(F32)

    h = _input_layer_norm(x.reshape(t, d), row(ln_in_g), row(ln_in_b))
    xbuf = jnp.zeros((n_rows, d), F32)
    ybuf = jnp.zeros((n_rows, d), F32)
    for l in range(DEPTH):
        h1, eidx3, gate, rank3, cnt = _mixer(
            h, w_in[l], pool_w[l], row(pool_scale[l]), conv_w[l], row(conv_b[l]), w_out[l],
            row(ln1_g[l]), row(ln1_b[l]), wr_t, rbias, tri, bsz, seq)
        counts = cnt[:, 0].astype(jnp.int32)
        padded = ((counts + BM - 1) // BM) * BM
        pend = jnp.cumsum(padded).astype(jnp.int32)
        pstart = pend - padded

        xbuf = _scatter_rows(pstart, eidx3, rank3, h1, xbuf)
        ybuf = _expert_ffn(pend, xbuf, exp_w1[l], exp_w3[l], exp_w2[l], ybuf)
        h = _combine(pstart, eidx3, rank3, h1, gate.T, ybuf, row(ln2_g[l]), row(ln2_b[l]))
    return h.reshape(bsz, seq, d)
```

```python
import functools

import jax
import jax.numpy as jnp
from jax import lax
from jax.experimental import pallas as pl
from jax.experimental.pallas import tpu as pltpu

D_MODEL = 1024
DEPTH = 4
D_POOL = 512
POOL_WINDOWS = (2, 4, 8, 16)
POOL_GROUP = 128
D_CONV = 512
CONV_WIDTH = 3
D_PROJ = D_POOL + 3 * D_CONV
N_EXPERTS = 16
N_EXPERT_GROUPS = 4
EXPERTS_PER_GROUP = 4
PAIRS_PER_GROUP = 6
N_CLASSES = N_EXPERT_GROUPS * PAIRS_PER_GROUP
CLASS_ROWS = 32
D_EXPERT = 512
ALPHA = float((2 * DEPTH) ** 0.25)
LN_EPS = 1e-5

F32 = jnp.float32
BF16 = jnp.bfloat16

LANES = 128
TS = 512
V_HALO = 16
Z_HALO = 8
BM = 256
D_ROW = D_MODEL + LANES
W_STAGE_ROWS = 256
VMEM_LIMIT = 56 * 1024 * 1024


def _layer_norm(x, g, b):
    mu = jnp.mean(x, axis=-1, keepdims=True)
    xc = x - mu
    var = jnp.mean(xc * xc, axis=-1, keepdims=True)
    return xc * lax.rsqrt(var + LN_EPS) * g + b


def _dot(a, b):
    return jnp.dot(a, b, preferred_element_type=F32)


def _ln_kernel(x_ref, g_ref, b_ref, o_ref):
    o_ref[...] = _layer_norm(x_ref[...], g_ref[...], b_ref[...])


def _input_layer_norm(x, g, b):
    t = x.shape[0]
    return pl.pallas_call(
        _ln_kernel,
        grid=(t // TS,),
        in_specs=[
            pl.BlockSpec((TS, D_MODEL), lambda i: (i, 0)),
            pl.BlockSpec((1, D_MODEL), lambda i: (0, 0)),
            pl.BlockSpec((1, D_MODEL), lambda i: (0, 0)),
        ],
        out_specs=pl.BlockSpec((TS, D_MODEL), lambda i: (i, 0)),
        out_shape=jax.ShapeDtypeStruct((t, D_MODEL), F32),
        compiler_params=pltpu.CompilerParams(dimension_semantics=("arbitrary",)),
        name="input_ln",
    )(x, g, b)


def _route(sel, probs):
    rows = [sel[e:e + 1, :] for e in range(N_EXPERTS)]
    prow = [probs[e:e + 1, :] for e in range(N_EXPERTS)]
    gscore = []
    for g in range(N_EXPERT_GROUPS):
        a, b, c, d = rows[4 * g:4 * g + 4]
        hi1, lo1 = jnp.maximum(a, b), jnp.minimum(a, b)
        hi2, lo2 = jnp.maximum(c, d), jnp.minimum(c, d)
        m1 = jnp.maximum(hi1, hi2)
        m2 = jnp.maximum(jnp.minimum(hi1, hi2), jnp.maximum(lo1, lo2))
        gscore.append(m1 + m2)
    best = gscore[0]
    gidx = jnp.zeros(best.shape, jnp.int32)
    for g in range(1, N_EXPERT_GROUPS):
        upd = gscore[g] > best
        best = jnp.where(upd, gscore[g], best)
        gidx = jnp.where(upd, g, gidx)
    neg = jnp.full(best.shape, -jnp.inf, F32)
    masked = [jnp.where(gidx == (e // EXPERTS_PER_GROUP), rows[e], neg) for e in range(N_EXPERTS)]

    def top1(vals):
        bv = neg
        bi = jnp.zeros(best.shape, jnp.int32)
        bp = jnp.zeros(best.shape, F32)
        for e in range(N_EXPERTS):
            upd = vals[e] > bv
            bv = jnp.where(upd, vals[e], bv)
            bi = jnp.where(upd, e, bi)
            bp = jnp.where(upd, prow[e], bp)
        return bi, bp

    i1, p1 = top1(masked)
    masked2 = [jnp.where(i1 == e, neg, masked[e]) for e in range(N_EXPERTS)]
    i2, p2 = top1(masked2)
    den = p1 + p2
    return gidx, i1, i2, p1 / den, p2 / den


def _pair_class(gidx, i1, i2):
    a = jnp.minimum(i1, i2) - gidx * EXPERTS_PER_GROUP
    b = jnp.maximum(i1, i2) - gidx * EXPERTS_PER_GROUP
    base = jnp.where(a == 0, 0, jnp.where(a == 1, 2, 3))
    return gidx * PAIRS_PER_GROUP + base + b - 1


def _mixer_kernel(h_ref, win_hbm, pw_ref, ps_ref, cw_ref, cb_ref, wout_hbm, g_ref, b_ref,
                  wr_ref, rb_ref, tri_ref,
                  h1x_ref, route_ref, cnt_ref,
                  vbuf, zbuf, mixbuf, carry, win_ref, wout_ref, stage, sem, *, layer):
    bi = pl.program_id(0)
    si = pl.program_id(1)

    @pl.when((bi == 0) & (si == 0))
    def _():
        for c in range(D_MODEL // W_STAGE_ROWS):
            rows = pl.ds(c * W_STAGE_ROWS, W_STAGE_ROWS)
            cp = pltpu.make_async_copy(win_hbm.at[layer, rows], stage, sem)
            cp.start()
            cp.wait()
            win_ref[rows, :] = stage[...].astype(BF16)
        for c in range(D_MODEL // W_STAGE_ROWS):
            rows = pl.ds(c * W_STAGE_ROWS, W_STAGE_ROWS)
            cp = pltpu.make_async_copy(wout_hbm.at[layer, rows], stage.at[:, pl.ds(0, D_MODEL)], sem)
            cp.start()
            cp.wait()
            wout_ref[rows, :] = stage[:, 0:D_MODEL].astype(BF16)
        carry[...] = jnp.zeros(carry.shape, F32)

    @pl.when(si == 0)
    def _():
        vbuf[0:V_HALO, :] = jnp.zeros((V_HALO, D_POOL), F32)
        zbuf[0:Z_HALO, :] = jnp.zeros((Z_HALO, D_CONV), F32)

    h = h_ref[...]
    hb = h.astype(BF16)
    vbuf[V_HALO:V_HALO + TS, :] = _dot(hb, win_ref[:, 0:D_POOL])
    gate_b = _dot(hb, win_ref[:, D_POOL:D_POOL + D_CONV])
    gate_c = _dot(hb, win_ref[:, D_POOL + D_CONV:D_POOL + 2 * D_CONV])
    u = _dot(hb, win_ref[:, D_POOL + 2 * D_CONV:D_PROJ])
    z = gate_c * u
    zbuf[Z_HALO:Z_HALO + TS, :] = z

    tpos = si * TS + lax.broadcasted_iota(jnp.int32, (TS, 1), 0)
    for g, win in enumerate(POOL_WINDOWS):
        cols = pl.ds(g * POOL_GROUP, POOL_GROUP)
        cur = vbuf[pl.ds(V_HALO, TS), cols]
        acc = cur
        for j in range(1, win):
            acc = acc + vbuf[pl.ds(V_HALO - j, TS), cols]
        denom = jnp.minimum(tpos + 1, win).astype(F32)
        pooled = acc / denom - cur
        mixed = _dot(pooled.astype(BF16), pw_ref[0, g].astype(BF16)) * ps_ref[0, :, cols]
        mixbuf[:, cols] = mixed.astype(BF16)

    y = (cw_ref[0, 2:3, :] * z
         + cw_ref[0, 1:2, :] * zbuf[pl.ds(Z_HALO - 1, TS), :]
         + cw_ref[0, 0:1, :] * zbuf[pl.ds(Z_HALO - 2, TS), :])
    mixbuf[:, D_POOL:D_POOL + D_CONV] = (gate_b * (y + cb_ref[0])).astype(BF16)

    vbuf[0:V_HALO, :] = vbuf[TS:TS + V_HALO, :]
    zbuf[0:Z_HALO, :] = zbuf[TS:TS + Z_HALO, :]

    mix = _dot(mixbuf[...], wout_ref[...])
    h1 = _layer_norm(ALPHA * h + mix, g_ref[0], b_ref[0])
    h1x_ref[:, 0:D_MODEL] = h1

    logits = lax.dot_general(wr_ref[...].astype(BF16), h1.astype(BF16), (((1,), (1,)), ((), ())),
                             preferred_element_type=F32)
    mx = jnp.max(logits, axis=0, keepdims=True)
    ex = jnp.exp(logits - mx)
    probs = ex / jnp.sum(ex, axis=0, keepdims=True)
    sel = probs + rb_ref[...]
    gidx, i1, i2, g1, g2 = _route(sel, probs)
    cls = _pair_class(gidx, i1, i2)
    first_low = i1 < i2
    gate_lo = jnp.where(first_low, g1, g2)
    gate_hi = jnp.where(first_low, g2, g1)

    riota = lax.broadcasted_iota(jnp.int32, (LANES, TS), 0)
    gate_rows = jnp.where(riota == 0, gate_lo, jnp.where(riota == 1, gate_hi, 0.0))
    h1x_ref[:, D_MODEL:D_ROW] = gate_rows.T

    ciota = lax.broadcasted_iota(jnp.int32, (CLASS_ROWS, TS), 0)
    onehot = jnp.where(ciota == cls, 1.0, 0.0).astype(F32)
    before = _dot(onehot.astype(BF16), tri_ref[...])
    total = jnp.sum(onehot, axis=1, keepdims=True)
    c = carry[:, 0:1]
    rank = jnp.sum(onehot * (c + before), axis=0, keepdims=True)
    route_ref[0, 0:1, :] = cls
    route_ref[0, 1:2, :] = rank.astype(jnp.int32)
    newc = jnp.broadcast_to(c + total, carry.shape)
    carry[...] = newc
    cnt_ref[...] = newc


def _mixer(layer, h, w_in, pool_w, pool_scale, conv_w, conv_b, w_out, ln_g, ln_b, wr_t, rbias, tri,
           bsz, seq):
    t = h.shape[0]
    ns = seq // TS
    nt = t // TS
    const2 = lambda bi, si: (0, 0)
    lsel3 = lambda bi, si: (layer, 0, 0)
    lsel4 = lambda bi, si: (layer, 0, 0, 0)
    tile = lambda bi, si: (bi * ns + si, 0)
    tile3 = lambda bi, si: (bi * ns + si, 0, 0)
    return pl.pallas_call(
        functools.partial(_mixer_kernel, layer=layer),
        grid=(bsz, ns),
        in_specs=[
            pl.BlockSpec((TS, D_MODEL), tile),
            pl.BlockSpec(memory_space=pl.ANY),
            pl.BlockSpec((1, len(POOL_WINDOWS), POOL_GROUP, POOL_GROUP), lsel4),
            pl.BlockSpec((1, 1, D_POOL), lsel3),
            pl.BlockSpec((1, CONV_WIDTH, D_CONV), lsel3),
            pl.BlockSpec((1, 1, D_CONV), lsel3),
            pl.BlockSpec(memory_space=pl.ANY),
            pl.BlockSpec((1, 1, D_MODEL), lsel3),
            pl.BlockSpec((1, 1, D_MODEL), lsel3),
            pl.BlockSpec((N_EXPERTS, D_MODEL), const2),
            pl.BlockSpec((N_EXPERTS, 1), const2),
            pl.BlockSpec((TS, TS), const2),
        ],
        out_specs=[
            pl.BlockSpec((TS, D_ROW), tile),
            pl.BlockSpec((1, 2, TS), tile3),
            pl.BlockSpec((CLASS_ROWS, LANES), const2),
        ],
        out_shape=[
            jax.ShapeDtypeStruct((t, D_ROW), F32),
            jax.ShapeDtypeStruct((nt, 2, TS), jnp.int32),
            jax.ShapeDtypeStruct((CLASS_ROWS, LANES), F32),
        ],
        scratch_shapes=[
            pltpu.VMEM((TS + V_HALO, D_POOL), F32),
            pltpu.VMEM((TS + Z_HALO, D_CONV), F32),
            pltpu.VMEM((TS, D_MODEL), BF16),
            pltpu.VMEM((CLASS_ROWS, LANES), F32),
            pltpu.VMEM((D_MODEL, D_PROJ), BF16),
            pltpu.VMEM((D_MODEL, D_MODEL), BF16),
            pltpu.VMEM((W_STAGE_ROWS, D_PROJ), F32),
            pltpu.SemaphoreType.DMA(()),
        ],
        compiler_params=pltpu.CompilerParams(
            dimension_semantics=("arbitrary", "arbitrary"), vmem_limit_bytes=VMEM_LIMIT),
        name="mixer_route",
    )(h, w_in, pool_w, pool_scale, conv_w, conv_b, w_out, ln_g, ln_b, wr_t, rbias, tri)


def _scatter_kernel(pstart_ref, route_ref, h1x_ref, xin_ref, xout_ref, sem):
    del xin_ref

    def body(i, carry):
        d = pstart_ref[route_ref[0, 0, i]] + route_ref[0, 1, i]
        pltpu.make_async_copy(h1x_ref.at[pl.ds(i, 1)], xout_ref.at[pl.ds(d, 1)], sem).start()
        return carry

    lax.fori_loop(0, TS, body, 0, unroll=8)
    pltpu.make_async_copy(h1x_ref, xout_ref.at[pl.ds(0, TS)], sem).wait()


def _scatter_rows(pstart, route3, h1x, xbuf):
    t = h1x.shape[0]
    grid_spec = pltpu.PrefetchScalarGridSpec(
        num_scalar_prefetch=1,
        grid=(t // TS,),
        in_specs=[
            pl.BlockSpec((1, 2, TS), lambda i, ps: (i, 0, 0), memory_space=pltpu.SMEM),
            pl.BlockSpec((TS, D_ROW), lambda i, ps: (i, 0)),
            pl.BlockSpec(memory_space=pl.ANY),
        ],
        out_specs=pl.BlockSpec(memory_space=pl.ANY),
        scratch_shapes=[pltpu.SemaphoreType.DMA(())],
    )
    return pl.pallas_call(
        _scatter_kernel,
        grid_spec=grid_spec,
        out_shape=jax.ShapeDtypeStruct(xbuf.shape, xbuf.dtype),
        input_output_aliases={3: 0},
        compiler_params=pltpu.CompilerParams(dimension_semantics=("arbitrary",)),
        name="scatter_rows",
    )(pstart, route3, h1x, xbuf)


def _block_class(i, pend_ref):
    r = i * BM
    c = jnp.int32(0)
    for j in range(N_CLASSES - 1):
        c = c + (r >= pend_ref[j]).astype(jnp.int32)
    return c


def _class_experts(c):
    g = c // PAIRS_PER_GROUP
    p = c - g * PAIRS_PER_GROUP
    ge3 = (p >= 3).astype(jnp.int32)
    ge5 = (p >= 5).astype(jnp.int32)
    a = ge3 + ge5
    b = p + 1 - 2 * ge3 - ge5
    return g * EXPERTS_PER_GROUP + a, g * EXPERTS_PER_GROUP + b


def _used_block(i, pend_ref):
    return jnp.minimum(i, pend_ref[N_CLASSES - 1] // BM - 1)


def _expert_kernel(pend_ref, x_ref, w1a_ref, w1b_ref, w3a_ref, w3b_ref, w2a_ref, w2b_ref, yin_ref,
                   y_ref, w1a, w1b, w3a, w3b, w2a, w2b):
    del yin_ref
    i = pl.program_id(0)
    used = i < pend_ref[N_CLASSES - 1] // BM
    ea, eb = _class_experts(_block_class(i, pend_ref))
    pa, pb = _class_experts(_block_class(jnp.maximum(i - 1, 0), pend_ref))

    @pl.when(used & ((i == 0) | (ea != pa)))
    def _():
        w1a[...] = w1a_ref[0, 0].astype(BF16)
        w3a[...] = w3a_ref[0, 0].astype(BF16)
        w2a[...] = w2a_ref[0, 0].astype(BF16)

    @pl.when(used & ((i == 0) | (eb != pb)))
    def _():
        w1b[...] = w1b_ref[0, 0].astype(BF16)
        w3b[...] = w3b_ref[0, 0].astype(BF16)
        w2b[...] = w2b_ref[0, 0].astype(BF16)

    @pl.when(used)
    def _():
        xb = x_ref[:, 0:D_MODEL].astype(BF16)
        gates = x_ref[:, D_MODEL:D_ROW]

        def ffn(w1, w3, w2):
            a = _dot(xb, w1[...])
            b = _dot(xb, w3[...])
            hid = a * jax.nn.sigmoid(a) * b
            return _dot(hid.astype(BF16), w2[...])

        y_ref[...] = gates[:, 0:1] * ffn(w1a, w3a, w2a) + gates[:, 1:2] * ffn(w1b, w3b, w2b)


def _expert_ffn(layer, pend, xbuf, w1, w3, w2, ybuf):
    n_rows = xbuf.shape[0]
    row_blk = lambda i, pe: (_used_block(i, pe), 0)

    def wsel(which):
        def index_map(i, pe):
            return (layer, _class_experts(_block_class(_used_block(i, pe), pe))[which], 0, 0)
        return index_map

    up_spec = lambda which: pl.BlockSpec((1, 1, D_MODEL, D_EXPERT), wsel(which))
    down_spec = lambda which: pl.BlockSpec((1, 1, D_EXPERT, D_MODEL), wsel(which))
    grid_spec = pltpu.PrefetchScalarGridSpec(
        num_scalar_prefetch=1,
        grid=(n_rows // BM,),
        in_specs=[
            pl.BlockSpec((BM, D_ROW), row_blk),
            up_spec(0), up_spec(1), up_spec(0), up_spec(1), down_spec(0), down_spec(1),
            pl.BlockSpec(memory_space=pl.ANY),
        ],
        out_specs=pl.BlockSpec((BM, D_MODEL), row_blk),
        scratch_shapes=[
            pltpu.VMEM((D_MODEL, D_EXPERT), BF16),
            pltpu.VMEM((D_MODEL, D_EXPERT), BF16),
            pltpu.VMEM((D_MODEL, D_EXPERT), BF16),
            pltpu.VMEM((D_MODEL, D_EXPERT), BF16),
            pltpu.VMEM((D_EXPERT, D_MODEL), BF16),
            pltpu.VMEM((D_EXPERT, D_MODEL), BF16),
        ],
    )
    return pl.pallas_call(
        _expert_kernel,
        grid_spec=grid_spec,
        out_shape=jax.ShapeDtypeStruct(ybuf.shape, ybuf.dtype),
        input_output_aliases={8: 0},
        compiler_params=pltpu.CompilerParams(
            dimension_semantics=("arbitrary",), vmem_limit_bytes=VMEM_LIMIT),
        name="expert_ffn",
    )(pend, xbuf, w1, w1, w3, w3, w2, w2, ybuf)


def _combine_kernel(pstart_ref, route_ref, h1x_ref, y_hbm, g_ref, b_ref, o_ref, ybuf, sem):
    def body(i, carry):
        d = pstart_ref[route_ref[0, 0, i]] + route_ref[0, 1, i]
        pltpu.make_async_copy(y_hbm.at[pl.ds(d, 1)], ybuf.at[pl.ds(i, 1)], sem).start()
        return carry

    lax.fori_loop(0, TS, body, 0, unroll=8)
    pltpu.make_async_copy(y_hbm.at[pl.ds(0, TS)], ybuf, sem).wait()
    x2 = ALPHA * h1x_ref[:, 0:D_MODEL] + ybuf[...]
    o_ref[...] = _layer_norm(x2, g_ref[0], b_ref[0])


def _combine(layer, pstart, route3, h1x, ybuf, ln_g, ln_b):
    t = h1x.shape[0]
    lsel3 = lambda i, ps: (layer, 0, 0)
    grid_spec = pltpu.PrefetchScalarGridSpec(
        num_scalar_prefetch=1,
        grid=(t // TS,),
        in_specs=[
            pl.BlockSpec((1, 2, TS), lambda i, ps: (i, 0, 0), memory_space=pltpu.SMEM),
            pl.BlockSpec((TS, D_ROW), lambda i, ps: (i, 0)),
            pl.BlockSpec(memory_space=pl.ANY),
            pl.BlockSpec((1, 1, D_MODEL), lsel3),
            pl.BlockSpec((1, 1, D_MODEL), lsel3),
        ],
        out_specs=pl.BlockSpec((TS, D_MODEL), lambda i, ps: (i, 0)),
        scratch_shapes=[pltpu.VMEM((TS, D_MODEL), F32), pltpu.SemaphoreType.DMA(())],
    )
    return pl.pallas_call(
        _combine_kernel,
        grid_spec=grid_spec,
        out_shape=jax.ShapeDtypeStruct((t, D_MODEL), F32),
        compiler_params=pltpu.CompilerParams(
            dimension_semantics=("arbitrary",), vmem_limit_bytes=VMEM_LIMIT),
        name="combine_ln",
    )(pstart, route3, h1x, ybuf, ln_g, ln_b)


def kernel(x, ln_in_g, ln_in_b, w_in, pool_w, pool_scale, conv_w, conv_b, w_out, ln1_g, ln1_b,
           w_router, router_bias, exp_w1, exp_w3, exp_w2, ln2_g, ln2_b):
    bsz, seq, d = x.shape
    t = bsz * seq
    n_rows = t + N_CLASSES * BM

    per_layer_row = lambda v: v.reshape(DEPTH, 1, -1)
    tri = (lax.broadcasted_iota(jnp.int32, (TS, TS), 0)
           < lax.broadcasted_iota(jnp.int32, (TS, TS), 1)).astype(BF16)
    wr_t = w_router.T
    rbias = router_bias.reshape(N_EXPERTS, 1).astype(F32)
    pool_scale3, conv_b3 = per_layer_row(pool_scale), per_layer_row(conv_b)
    ln1_g3, ln1_b3 = per_layer_row(ln1_g), per_layer_row(ln1_b)
    ln2_g3, ln2_b3 = per_layer_row(ln2_g), per_layer_row(ln2_b)

    h = _input_layer_norm(x.reshape(t, d), ln_in_g.reshape(1, -1), ln_in_b.reshape(1, -1))
    xbuf = jnp.zeros((n_rows, D_ROW), F32)
    ybuf = jnp.zeros((n_rows, d), F32)
    for l in range(DEPTH):
        h1x, route3, cnt = _mixer(l, h, w_in, pool_w, pool_scale3, conv_w, conv_b3, w_out,
                                  ln1_g3, ln1_b3, wr_t, rbias, tri, bsz, seq)
        counts = cnt[:N_CLASSES, 0].astype(jnp.int32)
        padded = ((counts + BM - 1) // BM) * BM
        pend = jnp.cumsum(padded).astype(jnp.int32)
        pstart = pend - padded

        xbuf = _scatter_rows(pstart, route3, h1x, xbuf)
        ybuf = _expert_ffn(l, pend, xbuf, exp_w1, exp_w3, exp_w2, ybuf)
        h = _combine(l, pstart, route3, h1x, ybuf, ln2_g3, ln2_b3)
    return h.reshape(bsz, seq, d)
```

```python
import functools

import jax
import jax.numpy as jnp
from jax import lax
from jax.experimental import pallas as pl
from jax.experimental.pallas import tpu as pltpu

D_MODEL = 1024
DEPTH = 4
D_POOL = 512
POOL_WINDOWS = (2, 4, 8, 16)
POOL_GROUP = 128
D_CONV = 512
CONV_WIDTH = 3
D_PROJ = D_POOL + 3 * D_CONV
N_EXPERTS = 16
N_EXPERT_GROUPS = 4
EXPERTS_PER_GROUP = 4
PAIRS_PER_GROUP = 6
N_CLASSES = N_EXPERT_GROUPS * PAIRS_PER_GROUP
CLASS_ROWS = 32
D_EXPERT = 512
ALPHA = float((2 * DEPTH) ** 0.25)
LN_EPS = 1e-5

F32 = jnp.float32
BF16 = jnp.bfloat16
I32 = jnp.int32

LANES = 128
TS = 512
V_HALO = 16
Z_HALO = 8
BM = 256
W_STAGE_ROWS = 256
VMEM_LIMIT = 56 * 1024 * 1024


def _layer_norm(x, g, b):
    mu = jnp.mean(x, axis=-1, keepdims=True)
    xc = x - mu
    var = jnp.mean(xc * xc, axis=-1, keepdims=True)
    return xc * lax.rsqrt(var + LN_EPS) * g + b


def _dot(a, b):
    return jnp.dot(a, b, preferred_element_type=F32)


def _route(sel):
    rows = [sel[e:e + 1, :] for e in range(N_EXPERTS)]
    gscore = []
    for g in range(N_EXPERT_GROUPS):
        a, b, c, d = rows[4 * g:4 * g + 4]
        hi1, lo1 = jnp.maximum(a, b), jnp.minimum(a, b)
        hi2, lo2 = jnp.maximum(c, d), jnp.minimum(c, d)
        m1 = jnp.maximum(hi1, hi2)
        m2 = jnp.maximum(jnp.minimum(hi1, hi2), jnp.maximum(lo1, lo2))
        gscore.append(m1 + m2)
    best = gscore[0]
    gidx = jnp.zeros(best.shape, I32)
    for g in range(1, N_EXPERT_GROUPS):
        upd = gscore[g] > best
        best = jnp.where(upd, gscore[g], best)
        gidx = jnp.where(upd, g, gidx)
    neg = jnp.full(best.shape, -jnp.inf, F32)
    masked = [jnp.where(gidx == (e // EXPERTS_PER_GROUP), rows[e], neg) for e in range(N_EXPERTS)]

    def top1(vals):
        bv = neg
        bi = jnp.zeros(best.shape, I32)
        for e in range(N_EXPERTS):
            upd = vals[e] > bv
            bv = jnp.where(upd, vals[e], bv)
            bi = jnp.where(upd, e, bi)
        return bi

    i1 = top1(masked)
    i2 = top1([jnp.where(i1 == e, neg, masked[e]) for e in range(N_EXPERTS)])
    return gidx, i1, i2


def _pair_class(gidx, i1, i2):
    a = jnp.minimum(i1, i2) - gidx * EXPERTS_PER_GROUP
    b = jnp.maximum(i1, i2) - gidx * EXPERTS_PER_GROUP
    base = jnp.where(a == 0, 0, jnp.where(a == 1, 2, 3))
    return gidx * PAIRS_PER_GROUP + base + b - 1


def _sorted_row(pstart_ref, route_ref, t):
    return pstart_ref[route_ref[0, 0, t]] + route_ref[0, 1, t]


def _prefetch_expert_rows(i, nt, pstart_ref, rcur_ref, rnext_ref, y_hbm, ybuf, gsem):
    slot = i % 2

    def gather(route_ref, to_slot, t):
        return pltpu.make_async_copy(y_hbm.at[pl.ds(_sorted_row(pstart_ref, route_ref, t), 1)],
                                     ybuf.at[to_slot, pl.ds(t, 1)], gsem.at[to_slot])

    @pl.when(i == 0)
    def _():
        def body(t, carry):
            gather(rcur_ref, 0, t).start()
            return carry
        lax.fori_loop(0, TS, body, 0, unroll=8)

    @pl.when(i + 1 < nt)
    def _():
        for t in range(TS):
            gather(rnext_ref, 1 - slot, t).start()

    pltpu.make_async_copy(y_hbm.at[pl.ds(0, TS)], ybuf.at[slot], gsem.at[slot]).wait()
    return ybuf[slot]


def _layer_kernel(*refs, layer, tiles_per_seq):
    if layer == 0:
        x_ref, gin_ref, bin_ref = refs[:3]
        rest = refs[3:]
    else:
        pstart_ref, rcur_ref, rnext_ref, h1p_ref, y_hbm, g2_ref, b2_ref = refs[:7]
        rest = refs[7:]
    (win_hbm, pw_ref, ps_ref, cw_ref, cb_ref, wout_hbm, g_ref, b_ref, wr_ref, rb_ref, tri_ref,
     h1_ref, route_ref, cnt_ref,
     vbuf, zbuf, mixbuf, carry, win_ref, wout_ref, stage, wsem, ybuf, gsem) = rest

    i = pl.program_id(0)
    nt = pl.num_programs(0)

    @pl.when(i == 0)
    def _():
        for c in range(D_MODEL // W_STAGE_ROWS):
            rows = pl.ds(c * W_STAGE_ROWS, W_STAGE_ROWS)
            cp = pltpu.make_async_copy(win_hbm.at[layer, rows], stage, wsem)
            cp.start()
            cp.wait()
            win_ref[rows, :] = stage[...].astype(BF16)
        for c in range(D_MODEL // W_STAGE_ROWS):
            rows = pl.ds(c * W_STAGE_ROWS, W_STAGE_ROWS)
            cp = pltpu.make_async_copy(wout_hbm.at[layer, rows], stage.at[:, pl.ds(0, D_MODEL)], wsem)
            cp.start()
            cp.wait()
            wout_ref[rows, :] = stage[:, 0:D_MODEL].astype(BF16)
        carry[...] = jnp.zeros(carry.shape, F32)

    if layer == 0:
        h = _layer_norm(x_ref[...], gin_ref[...], bin_ref[...])
    else:
        y = _prefetch_expert_rows(i, nt, pstart_ref, rcur_ref, rnext_ref, y_hbm, ybuf, gsem)
        h = _layer_norm(ALPHA * h1p_ref[...] + y, g2_ref[0], b2_ref[0])

    si = i % tiles_per_seq

    @pl.when(si == 0)
    def _():
        vbuf[0:V_HALO, :] = jnp.zeros((V_HALO, D_POOL), F32)
        zbuf[0:Z_HALO, :] = jnp.zeros((Z_HALO, D_CONV), F32)

    hb = h.astype(BF16)
    vbuf[V_HALO:V_HALO + TS, :] = _dot(hb, win_ref[:, 0:D_POOL])
    gate_b = _dot(hb, win_ref[:, D_POOL:D_POOL + D_CONV])
    gate_c = _dot(hb, win_ref[:, D_POOL + D_CONV:D_POOL + 2 * D_CONV])
    u = _dot(hb, win_ref[:, D_POOL + 2 * D_CONV:D_PROJ])
    z = gate_c * u
    zbuf[Z_HALO:Z_HALO + TS, :] = z

    tpos = si * TS + lax.broadcasted_iota(I32, (TS, 1), 0)
    for g, win in enumerate(POOL_WINDOWS):
        cols = pl.ds(g * POOL_GROUP, POOL_GROUP)
        cur = vbuf[pl.ds(V_HALO, TS), cols]
        acc = cur
        for j in range(1, win):
            acc = acc + vbuf[pl.ds(V_HALO - j, TS), cols]
        denom = jnp.minimum(tpos + 1, win).astype(F32)
        pooled = acc / denom - cur
        mixed = _dot(pooled.astype(BF16), pw_ref[0, g].astype(BF16)) * ps_ref[0, :, cols]
        mixbuf[:, cols] = mixed.astype(BF16)

    yc = (cw_ref[0, 2:3, :] * z
          + cw_ref[0, 1:2, :] * zbuf[pl.ds(Z_HALO - 1, TS), :]
          + cw_ref[0, 0:1, :] * zbuf[pl.ds(Z_HALO - 2, TS), :])
    mixbuf[:, D_POOL:D_POOL + D_CONV] = (gate_b * (yc + cb_ref[0])).astype(BF16)

    vbuf[0:V_HALO, :] = vbuf[TS:TS + V_HALO, :]
    zbuf[0:Z_HALO, :] = zbuf[TS:TS + Z_HALO, :]

    mix = _dot(mixbuf[...], wout_ref[...])
    h1 = _layer_norm(ALPHA * h + mix, g_ref[0], b_ref[0])
    h1_ref[...] = h1

    logits = lax.dot_general(wr_ref[...].astype(BF16), h1.astype(BF16), (((1,), (1,)), ((), ())),
                             preferred_element_type=F32)
    mx = jnp.max(logits, axis=0, keepdims=True)
    ex = jnp.exp(logits - mx)
    probs = ex / jnp.sum(ex, axis=0, keepdims=True)
    gidx, i1, i2 = _route(probs + rb_ref[...])
    cls = _pair_class(gidx, i1, i2)

    ciota = lax.broadcasted_iota(I32, (CLASS_ROWS, TS), 0)
    onehot = jnp.where(ciota == cls, 1.0, 0.0).astype(F32)
    before = _dot(onehot.astype(BF16), tri_ref[...])
    total = jnp.sum(onehot, axis=1, keepdims=True)
    c = carry[:, 0:1]
    rank = jnp.sum(onehot * (c + before), axis=0, keepdims=True)
    route_ref[0, 0:1, :] = cls
    route_ref[0, 1:2, :] = rank.astype(I32)
    newc = jnp.broadcast_to(c + total, carry.shape)
    carry[...] = newc
    cnt_ref[...] = newc


def _layer_call(layer, bsz, seq, prev, w_in, pool_w, pool_scale, conv_w, conv_b, w_out, ln_g, ln_b,
                wr_t, rbias, tri):
    t = bsz * seq
    nt = t // TS
    const2 = lambda i, *_: (0, 0)
    lsel3 = lambda i, *_: (layer, 0, 0)
    lsel4 = lambda i, *_: (layer, 0, 0, 0)
    tile = lambda i, *_: (i, 0)
    tile3 = lambda i, *_: (i, 0, 0)
    if layer == 0:
        x, gin, bin_ = prev
        prefetch = ()
        head_args = (x, gin, bin_)
        head_specs = [
            pl.BlockSpec((TS, D_MODEL), tile),
            pl.BlockSpec((1, D_MODEL), const2),
            pl.BlockSpec((1, D_MODEL), const2),
        ]
    else:
        pstart_prev, route_prev, h1_prev, y_prev, g2, b2 = prev
        prefetch = (pstart_prev,)
        head_args = (route_prev, route_prev, h1_prev, y_prev, g2, b2)
        head_specs = [
            pl.BlockSpec((1, 2, TS), tile3, memory_space=pltpu.SMEM),
            pl.BlockSpec((1, 2, TS), lambda i, *_: (jnp.minimum(i + 1, nt - 1), 0, 0),
                         memory_space=pltpu.SMEM),
            pl.BlockSpec((TS, D_MODEL), tile),
            pl.BlockSpec(memory_space=pl.ANY),
            pl.BlockSpec((1, 1, D_MODEL), lambda i, *_: (layer - 1, 0, 0)),
            pl.BlockSpec((1, 1, D_MODEL), lambda i, *_: (layer - 1, 0, 0)),
        ]
    grid_spec = pltpu.PrefetchScalarGridSpec(
        num_scalar_prefetch=len(prefetch),
        grid=(nt,),
        in_specs=head_specs + [
            pl.BlockSpec(memory_space=pl.ANY),
            pl.BlockSpec((1, len(POOL_WINDOWS), POOL_GROUP, POOL_GROUP), lsel4),
            pl.BlockSpec((1, 1, D_POOL), lsel3),
            pl.BlockSpec((1, CONV_WIDTH, D_CONV), lsel3),
            pl.BlockSpec((1, 1, D_CONV), lsel3),
            pl.BlockSpec(memory_space=pl.ANY),
            pl.BlockSpec((1, 1, D_MODEL), lsel3),
            pl.BlockSpec((1, 1, D_MODEL), lsel3),
            pl.BlockSpec((N_EXPERTS, D_MODEL), const2),
            pl.BlockSpec((N_EXPERTS, 1), const2),
            pl.BlockSpec((TS, TS), const2),
        ],
        out_specs=[
            pl.BlockSpec((TS, D_MODEL), tile),
            pl.BlockSpec((1, 2, TS), tile3),
            pl.BlockSpec((CLASS_ROWS, LANES), const2),
        ],
        scratch_shapes=[
            pltpu.VMEM((TS + V_HALO, D_POOL), F32),
            pltpu.VMEM((TS + Z_HALO, D_CONV), F32),
            pltpu.VMEM((TS, D_MODEL), BF16),
            pltpu.VMEM((CLASS_ROWS, LANES), F32),
            pltpu.VMEM((D_MODEL, D_PROJ), BF16),
            pltpu.VMEM((D_MODEL, D_MODEL), BF16),
            pltpu.VMEM((W_STAGE_ROWS, D_PROJ), F32),
            pltpu.SemaphoreType.DMA(()),
            pltpu.VMEM((2, TS, D_MODEL), F32),
            pltpu.SemaphoreType.DMA((2,)),
        ],
    )
    return pl.pallas_call(
        functools.partial(_layer_kernel, layer=layer, tiles_per_seq=seq // TS),
        grid_spec=grid_spec,
        out_shape=[
            jax.ShapeDtypeStruct((t, D_MODEL), F32),
            jax.ShapeDtypeStruct((nt, 2, TS), I32),
            jax.ShapeDtypeStruct((CLASS_ROWS, LANES), F32),
        ],
        compiler_params=pltpu.CompilerParams(
            dimension_semantics=("arbitrary",), vmem_limit_bytes=VMEM_LIMIT),
        name="layer_mix_route",
    )(*prefetch, *head_args, w_in, pool_w, pool_scale, conv_w, conv_b, w_out, ln_g, ln_b, wr_t, rbias,
      tri)


def _scatter_kernel(pstart_ref, route_ref, h1_ref, xin_ref, xout_ref, sem):
    del xin_ref
    for t in range(TS):
        pltpu.make_async_copy(h1_ref.at[pl.ds(t, 1)],
                              xout_ref.at[pl.ds(_sorted_row(pstart_ref, route_ref, t), 1)], sem).start()
    pltpu.make_async_copy(h1_ref, xout_ref.at[pl.ds(0, TS)], sem).wait()


def _scatter_rows(pstart, route3, h1, xbuf):
    t = h1.shape[0]
    grid_spec = pltpu.PrefetchScalarGridSpec(
        num_scalar_prefetch=1,
        grid=(t // TS,),
        in_specs=[
            pl.BlockSpec((1, 2, TS), lambda i, ps: (i, 0, 0), memory_space=pltpu.SMEM),
            pl.BlockSpec((TS, D_MODEL), lambda i, ps: (i, 0)),
            pl.BlockSpec(memory_space=pl.ANY),
        ],
        out_specs=pl.BlockSpec(memory_space=pl.ANY),
        scratch_shapes=[pltpu.SemaphoreType.DMA(())],
    )
    return pl.pallas_call(
        _scatter_kernel,
        grid_spec=grid_spec,
        out_shape=jax.ShapeDtypeStruct(xbuf.shape, xbuf.dtype),
        input_output_aliases={3: 0},
        compiler_params=pltpu.CompilerParams(dimension_semantics=("arbitrary",)),
        name="scatter_rows",
    )(pstart, route3, h1, xbuf)


def _expert_kernel(ea_ref, eb_ref, nu_ref, x_ref, w1a_ref, w1b_ref, w3a_ref, w3b_ref,
                   w2a_ref, w2b_ref, wr_ref, y_ref, w1a, w1b, w3a, w3b, w2a, w2b):
    i = pl.program_id(0)
    used = i < nu_ref[0]
    ea = ea_ref[i]
    eb = eb_ref[i]
    prev = jnp.maximum(i - 1, 0)

    @pl.when(used & ((i == 0) | (ea != ea_ref[prev])))
    def _():
        w1a[...] = w1a_ref[0, 0].astype(BF16)
        w3a[...] = w3a_ref[0, 0].astype(BF16)
        w2a[...] = w2a_ref[0, 0].astype(BF16)

    @pl.when(used & ((i == 0) | (eb != eb_ref[prev])))
    def _():
        w1b[...] = w1b_ref[0, 0].astype(BF16)
        w3b[...] = w3b_ref[0, 0].astype(BF16)
        w2b[...] = w2b_ref[0, 0].astype(BF16)

    @pl.when(used)
    def _():
        xb = x_ref[...].astype(BF16)

        logits = _dot(xb, wr_ref[...].astype(BF16))
        lane = lax.broadcasted_iota(I32, logits.shape, 1)
        logits = jnp.where(lane < N_EXPERTS, logits, -jnp.inf)
        ex = jnp.exp(logits - jnp.max(logits, axis=1, keepdims=True))
        probs = ex / jnp.sum(ex, axis=1, keepdims=True)
        pa = jnp.sum(jnp.where(lane == ea, probs, 0.0), axis=1, keepdims=True)
        pb = jnp.sum(jnp.where(lane == eb, probs, 0.0), axis=1, keepdims=True)
        den = pa + pb

        def ffn(w1, w3, w2):
            a = _dot(xb, w1[...])
            b = _dot(xb, w3[...])
            hid = a * jax.nn.sigmoid(a) * b
            return _dot(hid.astype(BF16), w2[...])

        y_ref[...] = (pa / den) * ffn(w1a, w3a, w2a) + (pb / den) * ffn(w1b, w3b, w2b)

    @pl.when(jnp.logical_not(used))
    def _():
        y_ref[...] = jnp.zeros(y_ref.shape, F32)


def _expert_ffn(layer, blk_ea, blk_eb, nused, xbuf, w1, w3, w2, wr_pad):
    n_rows = xbuf.shape[0]
    used_blk = lambda i, ea, eb, nu: (jnp.minimum(i, nu[0] - 1), 0)
    wa = lambda i, ea, eb, nu: (layer, ea[i], 0, 0)
    wb = lambda i, ea, eb, nu: (layer, eb[i], 0, 0)
    up = lambda im: pl.BlockSpec((1, 1, D_MODEL, D_EXPERT), im)
    down = lambda im: pl.BlockSpec((1, 1, D_EXPERT, D_MODEL), im)
    grid_spec = pltpu.PrefetchScalarGridSpec(
        num_scalar_prefetch=3,
        grid=(n_rows // BM,),
        in_specs=[
            pl.BlockSpec((BM, D_MODEL), used_blk),
            up(wa), up(wb), up(wa), up(wb), down(wa), down(wb),
            pl.BlockSpec((D_MODEL, LANES), lambda i, ea, eb, nu: (0, 0)),
        ],
        out_specs=pl.BlockSpec((BM, D_MODEL), lambda i, ea, eb, nu: (i, 0)),
        scratch_shapes=[
            pltpu.VMEM((D_MODEL, D_EXPERT), BF16),
            pltpu.VMEM((D_MODEL, D_EXPERT), BF16),
            pltpu.VMEM((D_MODEL, D_EXPERT), BF16),
            pltpu.VMEM((D_MODEL, D_EXPERT), BF16),
            pltpu.VMEM((D_EXPERT, D_MODEL), BF16),
            pltpu.VMEM((D_EXPERT, D_MODEL), BF16),
        ],
    )
    return pl.pallas_call(
        _expert_kernel,
        grid_spec=grid_spec,
        out_shape=jax.ShapeDtypeStruct((n_rows, D_MODEL), F32),
        compiler_params=pltpu.CompilerParams(
            dimension_semantics=("arbitrary",), vmem_limit_bytes=VMEM_LIMIT),
        name="expert_ffn",
    )(blk_ea, blk_eb, nused, xbuf, w1, w1, w3, w3, w2, w2, wr_pad)


def _dispatch_tables(cnt, nb):
    counts = cnt[:N_CLASSES, 0].astype(I32)
    padded = ((counts + BM - 1) // BM) * BM
    pend = jnp.cumsum(padded).astype(I32)
    pstart = pend - padded
    nused = pend[-1:] // BM
    step = jnp.minimum(jnp.arange(nb, dtype=I32), nused - 1)
    c = jnp.minimum(jnp.searchsorted(pend, step * BM, side="right"), N_CLASSES - 1).astype(I32)
    g = c // PAIRS_PER_GROUP
    p = c - g * PAIRS_PER_GROUP
    ge3 = (p >= 3).astype(I32)
    ge5 = (p >= 5).astype(I32)
    blk_ea = g * EXPERTS_PER_GROUP + ge3 + ge5
    blk_eb = g * EXPERTS_PER_GROUP + p + 1 - 2 * ge3 - ge5
    return pstart, blk_ea, blk_eb, nused


def _final_kernel(pstart_ref, rcur_ref, rnext_ref, h1p_ref, y_hbm, g2_ref, b2_ref, o_ref, ybuf, gsem):
    i = pl.program_id(0)
    y = _prefetch_expert_rows(i, pl.num_programs(0), pstart_ref, rcur_ref, rnext_ref, y_hbm, ybuf, gsem)
    o_ref[...] = _layer_norm(ALPHA * h1p_ref[...] + y, g2_ref[0], b2_ref[0])


def _final_call(pstart_prev, route_prev, h1_prev, y_prev, g2, b2):
    t = h1_prev.shape[0]
    nt = t // TS
    last = DEPTH - 1
    grid_spec = pltpu.PrefetchScalarGridSpec(
        num_scalar_prefetch=1,
        grid=(nt,),
        in_specs=[
            pl.BlockSpec((1, 2, TS), lambda i, ps: (i, 0, 0), memory_space=pltpu.SMEM),
            pl.BlockSpec((1, 2, TS), lambda i, ps: (jnp.minimum(i + 1, nt - 1), 0, 0),
                         memory_space=pltpu.SMEM),
            pl.BlockSpec((TS, D_MODEL), lambda i, ps: (i, 0)),
            pl.BlockSpec(memory_space=pl.ANY),
            pl.BlockSpec((1, 1, D_MODEL), lambda i, ps: (last, 0, 0)),
            pl.BlockSpec((1, 1, D_MODEL), lambda i, ps: (last, 0, 0)),
        ],
        out_specs=pl.BlockSpec((TS, D_MODEL), lambda i, ps: (i, 0)),
        scratch_shapes=[pltpu.VMEM((2, TS, D_MODEL), F32), pltpu.SemaphoreType.DMA((2,))],
    )
    return pl.pallas_call(
        _final_kernel,
        grid_spec=grid_spec,
        out_shape=jax.ShapeDtypeStruct((t, D_MODEL), F32),
        compiler_params=pltpu.CompilerParams(
            dimension_semantics=("arbitrary",), vmem_limit_bytes=VMEM_LIMIT),
        name="final_combine_ln",
    )(pstart_prev, route_prev, route_prev, h1_prev, y_prev, g2, b2)


def kernel(x, ln_in_g, ln_in_b, w_in, pool_w, pool_scale, conv_w, conv_b, w_out, ln1_g, ln1_b,
           w_router, router_bias, exp_w1, exp_w3, exp_w2, ln2_g, ln2_b):
    bsz, seq, d = x.shape
    t = bsz * seq
    n_rows = t + N_CLASSES * BM
    nb = n_rows // BM

    per_layer_row = lambda v: v.reshape(DEPTH, 1, -1)
    tri = (lax.broadcasted_iota(I32, (TS, TS), 0)
           < lax.broadcasted_iota(I32, (TS, TS), 1)).astype(BF16)
    wr_t = w_router.T
    wr_pad = jnp.pad(w_router, ((0, 0), (0, LANES - N_EXPERTS)))
    rbias = router_bias.reshape(N_EXPERTS, 1).astype(F32)
    pool_scale3, conv_b3 = per_layer_row(pool_scale), per_layer_row(conv_b)
    ln1_g3, ln1_b3 = per_layer_row(ln1_g), per_layer_row(ln1_b)
    ln2_g3, ln2_b3 = per_layer_row(ln2_g), per_layer_row(ln2_b)

    prev = (x.reshape(t, d), ln_in_g.reshape(1, -1), ln_in_b.reshape(1, -1))
    xbuf = jnp.zeros((n_rows, d), F32)
    for l in range(DEPTH):
        h1, route3, cnt = _layer_call(l, bsz, seq, prev, w_in, pool_w, pool_scale3, conv_w, conv_b3,
                                      w_out, ln1_g3, ln1_b3, wr_t, rbias, tri)
        pstart, blk_ea, blk_eb, nused = _dispatch_tables(cnt, nb)
        xbuf = _scatter_rows(pstart, route3, h1, xbuf)
        ybuf = _expert_ffn(l, blk_ea, blk_eb, nused, xbuf, exp_w1, exp_w3, exp_w2, wr_pad)
        prev = (pstart, route3, h1, ybuf, ln2_g3, ln2_b3)
    out = _final_call(*prev)
    return out.reshape(bsz, seq, d)
```

```python
import functools

import jax
import jax.numpy as jnp
from jax import lax
from jax.experimental import pallas as pl
from jax.experimental.pallas import tpu as pltpu

D_MODEL = 1024
DEPTH = 4
D_POOL = 512
POOL_WINDOWS = (2, 4, 8, 16)
POOL_GROUP = 128
D_CONV = 512
CONV_WIDTH = 3
D_PROJ = D_POOL + 3 * D_CONV
N_EXPERTS = 16
N_EXPERT_GROUPS = 4
EXPERTS_PER_GROUP = 4
PAIRS_PER_GROUP = 6
N_CLASSES = N_EXPERT_GROUPS * PAIRS_PER_GROUP
CLASS_ROWS = 32
D_EXPERT = 512
ALPHA = float((2 * DEPTH) ** 0.25)
LN_EPS = 1e-5

F32 = jnp.float32
BF16 = jnp.bfloat16
I32 = jnp.int32

LANES = 128
TS = 512
V_HALO = 16
Z_HALO = 8
BM = 256
W_STAGE_ROWS = 256
DMA_PRIORITIES = 2
VMEM_LIMIT = 56 * 1024 * 1024


def _layer_norm(x, g, b):
    mu = jnp.mean(x, axis=-1, keepdims=True)
    xc = x - mu
    var = jnp.mean(xc * xc, axis=-1, keepdims=True)
    return xc * lax.rsqrt(var + LN_EPS) * g + b


def _dot(a, b):
    return jnp.dot(a, b, preferred_element_type=F32)


def _route(sel):
    rows = [sel[e:e + 1, :] for e in range(N_EXPERTS)]
    gscore = []
    for g in range(N_EXPERT_GROUPS):
        a, b, c, d = rows[4 * g:4 * g + 4]
        hi1, lo1 = jnp.maximum(a, b), jnp.minimum(a, b)
        hi2, lo2 = jnp.maximum(c, d), jnp.minimum(c, d)
        m1 = jnp.maximum(hi1, hi2)
        m2 = jnp.maximum(jnp.minimum(hi1, hi2), jnp.maximum(lo1, lo2))
        gscore.append(m1 + m2)
    best = gscore[0]
    gidx = jnp.zeros(best.shape, I32)
    for g in range(1, N_EXPERT_GROUPS):
        upd = gscore[g] > best
        best = jnp.where(upd, gscore[g], best)
        gidx = jnp.where(upd, g, gidx)
    neg = jnp.full(best.shape, -jnp.inf, F32)
    masked = [jnp.where(gidx == (e // EXPERTS_PER_GROUP), rows[e], neg) for e in range(N_EXPERTS)]

    def top1(vals):
        bv = neg
        bi = jnp.zeros(best.shape, I32)
        for e in range(N_EXPERTS):
            upd = vals[e] > bv
            bv = jnp.where(upd, vals[e], bv)
            bi = jnp.where(upd, e, bi)
        return bi

    i1 = top1(masked)
    i2 = top1([jnp.where(i1 == e, neg, masked[e]) for e in range(N_EXPERTS)])
    return gidx, i1, i2


def _pair_class(gidx, i1, i2):
    a = jnp.minimum(i1, i2) - gidx * EXPERTS_PER_GROUP
    b = jnp.maximum(i1, i2) - gidx * EXPERTS_PER_GROUP
    base = jnp.where(a == 0, 0, jnp.where(a == 1, 2, 3))
    return gidx * PAIRS_PER_GROUP + base + b - 1


def _sorted_row(pstart_ref, route_ref, t):
    return pstart_ref[route_ref[0, 0, t]] + route_ref[0, 1, t]


def _prefetch_expert_rows(i, nt, pstart_ref, rcur_ref, rnext_ref, y_hbm, ybuf, gsem):
    slot = i % 2

    def gather(route_ref, to_slot, t):
        return pltpu.make_async_copy(y_hbm.at[pl.ds(_sorted_row(pstart_ref, route_ref, t), 1)],
                                     ybuf.at[to_slot, pl.ds(t, 1)], gsem.at[to_slot])

    @pl.when(i == 0)
    def _():
        def body(t, carry):
            gather(rcur_ref, 0, t).start()
            return carry
        lax.fori_loop(0, TS, body, 0, unroll=8)

    @pl.when(i + 1 < nt)
    def _():
        for t in range(TS):
            gather(rnext_ref, 1 - slot, t).start(priority=t % DMA_PRIORITIES)

    pltpu.make_async_copy(y_hbm.at[pl.ds(0, TS)], ybuf.at[slot], gsem.at[slot]).wait()
    return ybuf[slot]


def _layer_kernel(*refs, layer, tiles_per_seq):
    if layer == 0:
        x_ref, gin_ref, bin_ref = refs[:3]
        rest = refs[3:]
    else:
        pstart_ref, rcur_ref, rnext_ref, h1p_ref, y_hbm, g2_ref, b2_ref = refs[:7]
        rest = refs[7:]
    (win_hbm, pw_ref, ps_ref, cw_ref, cb_ref, wout_hbm, g_ref, b_ref, wr_ref, rb_ref, tri_ref,
     h1_ref, route_ref, cnt_ref,
     vbuf, zbuf, mixbuf, carry, win_ref, wout_ref, stage, wsem, ybuf, gsem) = rest

    i = pl.program_id(0)
    nt = pl.num_programs(0)

    @pl.when(i == 0)
    def _():
        for c in range(D_MODEL // W_STAGE_ROWS):
            rows = pl.ds(c * W_STAGE_ROWS, W_STAGE_ROWS)
            cp = pltpu.make_async_copy(win_hbm.at[layer, rows], stage, wsem)
            cp.start()
            cp.wait()
            win_ref[rows, :] = stage[...].astype(BF16)
        for c in range(D_MODEL // W_STAGE_ROWS):
            rows = pl.ds(c * W_STAGE_ROWS, W_STAGE_ROWS)
            cp = pltpu.make_async_copy(wout_hbm.at[layer, rows], stage.at[:, pl.ds(0, D_MODEL)], wsem)
            cp.start()
            cp.wait()
            wout_ref[rows, :] = stage[:, 0:D_MODEL].astype(BF16)
        carry[...] = jnp.zeros(carry.shape, F32)

    if layer == 0:
        h = _layer_norm(x_ref[...], gin_ref[...], bin_ref[...])
    else:
        y = _prefetch_expert_rows(i, nt, pstart_ref, rcur_ref, rnext_ref, y_hbm, ybuf, gsem)
        h = _layer_norm(ALPHA * h1p_ref[...] + y, g2_ref[0], b2_ref[0])

    si = i % tiles_per_seq

    @pl.when(si == 0)
    def _():
        vbuf[0:V_HALO, :] = jnp.zeros((V_HALO, D_POOL), F32)
        zbuf[0:Z_HALO, :] = jnp.zeros((Z_HALO, D_CONV), F32)

    hb = h.astype(BF16)
    vbuf[V_HALO:V_HALO + TS, :] = _dot(hb, win_ref[:, 0:D_POOL])
    gate_b = _dot(hb, win_ref[:, D_POOL:D_POOL + D_CONV])
    gate_c = _dot(hb, win_ref[:, D_POOL + D_CONV:D_POOL + 2 * D_CONV])
    u = _dot(hb, win_ref[:, D_POOL + 2 * D_CONV:D_PROJ])
    z = gate_c * u
    zbuf[Z_HALO:Z_HALO + TS, :] = z

    tpos = si * TS + lax.broadcasted_iota(I32, (TS, 1), 0)
    for g, win in enumerate(POOL_WINDOWS):
        cols = pl.ds(g * POOL_GROUP, POOL_GROUP)
        cur = vbuf[pl.ds(V_HALO, TS), cols]
        acc = cur
        for j in range(1, win):
            acc = acc + vbuf[pl.ds(V_HALO - j, TS), cols]
        denom = jnp.minimum(tpos + 1, win).astype(F32)
        pooled = acc / denom - cur
        mixed = _dot(pooled.astype(BF16), pw_ref[0, g].astype(BF16)) * ps_ref[0, :, cols]
        mixbuf[:, cols] = mixed.astype(BF16)

    yc = (cw_ref[0, 2:3, :] * z
          + cw_ref[0, 1:2, :] * zbuf[pl.ds(Z_HALO - 1, TS), :]
          + cw_ref[0, 0:1, :] * zbuf[pl.ds(Z_HALO - 2, TS), :])
    mixbuf[:, D_POOL:D_POOL + D_CONV] = (gate_b * (yc + cb_ref[0])).astype(BF16)

    vbuf[0:V_HALO, :] = vbuf[TS:TS + V_HALO, :]
    zbuf[0:Z_HALO, :] = zbuf[TS:TS + Z_HALO, :]

    mix = _dot(mixbuf[...], wout_ref[...])
    h1 = _layer_norm(ALPHA * h + mix, g_ref[0], b_ref[0])
    h1_ref[...] = h1

    logits = lax.dot_general(wr_ref[...].astype(BF16), h1.astype(BF16), (((1,), (1,)), ((), ())),
                             preferred_element_type=F32)
    mx = jnp.max(logits, axis=0, keepdims=True)
    ex = jnp.exp(logits - mx)
    probs = ex / jnp.sum(ex, axis=0, keepdims=True)
    gidx, i1, i2 = _route(probs + rb_ref[...])
    cls = _pair_class(gidx, i1, i2)

    ciota = lax.broadcasted_iota(I32, (CLASS_ROWS, TS), 0)
    onehot = jnp.where(ciota == cls, 1.0, 0.0).astype(F32)
    before = _dot(onehot.astype(BF16), tri_ref[...])
    total = jnp.sum(onehot, axis=1, keepdims=True)
    c = carry[:, 0:1]
    rank = jnp.sum(onehot * (c + before), axis=0, keepdims=True)
    route_ref[0, 0:1, :] = cls
    route_ref[0, 1:2, :] = rank.astype(I32)
    newc = jnp.broadcast_to(c + total, carry.shape)
    carry[...] = newc
    cnt_ref[...] = newc


def _layer_call(layer, bsz, seq, prev, w_in, pool_w, pool_scale, conv_w, conv_b, w_out, ln_g, ln_b,
                wr_t, rbias, tri):
    t = bsz * seq
    nt = t // TS
    const2 = lambda i, *_: (0, 0)
    lsel3 = lambda i, *_: (layer, 0, 0)
    lsel4 = lambda i, *_: (layer, 0, 0, 0)
    tile = lambda i, *_: (i, 0)
    tile3 = lambda i, *_: (i, 0, 0)
    if layer == 0:
        x, gin, bin_ = prev
        prefetch = ()
        head_args = (x, gin, bin_)
        head_specs = [
            pl.BlockSpec((TS, D_MODEL), tile),
            pl.BlockSpec((1, D_MODEL), const2),
            pl.BlockSpec((1, D_MODEL), const2),
        ]
    else:
        pstart_prev, route_prev, h1_prev, y_prev, g2, b2 = prev
        prefetch = (pstart_prev,)
        head_args = (route_prev, route_prev, h1_prev, y_prev, g2, b2)
        head_specs = [
            pl.BlockSpec((1, 2, TS), tile3, memory_space=pltpu.SMEM),
            pl.BlockSpec((1, 2, TS), lambda i, *_: (jnp.minimum(i + 1, nt - 1), 0, 0),
                         memory_space=pltpu.SMEM),
            pl.BlockSpec((TS, D_MODEL), tile),
            pl.BlockSpec(memory_space=pl.ANY),
            pl.BlockSpec((1, 1, D_MODEL), lambda i, *_: (layer - 1, 0, 0)),
            pl.BlockSpec((1, 1, D_MODEL), lambda i, *_: (layer - 1, 0, 0)),
        ]
    grid_spec = pltpu.PrefetchScalarGridSpec(
        num_scalar_prefetch=len(prefetch),
        grid=(nt,),
        in_specs=head_specs + [
            pl.BlockSpec(memory_space=pl.ANY),
            pl.BlockSpec((1, len(POOL_WINDOWS), POOL_GROUP, POOL_GROUP), lsel4),
            pl.BlockSpec((1, 1, D_POOL), lsel3),
            pl.BlockSpec((1, CONV_WIDTH, D_CONV), lsel3),
            pl.BlockSpec((1, 1, D_CONV), lsel3),
            pl.BlockSpec(memory_space=pl.ANY),
            pl.BlockSpec((1, 1, D_MODEL), lsel3),
            pl.BlockSpec((1, 1, D_MODEL), lsel3),
            pl.BlockSpec((N_EXPERTS, D_MODEL), const2),
            pl.BlockSpec((N_EXPERTS, 1), const2),
            pl.BlockSpec((TS, TS), const2),
        ],
        out_specs=[
            pl.BlockSpec((TS, D_MODEL), tile),
            pl.BlockSpec((1, 2, TS), tile3),
            pl.BlockSpec((CLASS_ROWS, LANES), const2),
        ],
        scratch_shapes=[
            pltpu.VMEM((TS + V_HALO, D_POOL), F32),
            pltpu.VMEM((TS + Z_HALO, D_CONV), F32),
            pltpu.VMEM((TS, D_MODEL), BF16),
            pltpu.VMEM((CLASS_ROWS, LANES), F32),
            pltpu.VMEM((D_MODEL, D_PROJ), BF16),
            pltpu.VMEM((D_MODEL, D_MODEL), BF16),
            pltpu.VMEM((W_STAGE_ROWS, D_PROJ), F32),
            pltpu.SemaphoreType.DMA(()),
            pltpu.VMEM((2, TS, D_MODEL), F32),
            pltpu.SemaphoreType.DMA((2,)),
        ],
    )
    return pl.pallas_call(
        functools.partial(_layer_kernel, layer=layer, tiles_per_seq=seq // TS),
        grid_spec=grid_spec,
        out_shape=[
            jax.ShapeDtypeStruct((t, D_MODEL), F32),
            jax.ShapeDtypeStruct((nt, 2, TS), I32),
            jax.ShapeDtypeStruct((CLASS_ROWS, LANES), F32),
        ],
        compiler_params=pltpu.CompilerParams(
            dimension_semantics=("arbitrary",), vmem_limit_bytes=VMEM_LIMIT),
        name="layer_mix_route",
    )(*prefetch, *head_args, w_in, pool_w, pool_scale, conv_w, conv_b, w_out, ln_g, ln_b, wr_t, rbias,
      tri)


def _scatter_kernel(pstart_ref, route_ref, h1_ref, xin_ref, xout_ref, sem):
    del xin_ref
    for t in range(TS):
        pltpu.make_async_copy(h1_ref.at[pl.ds(t, 1)],
                              xout_ref.at[pl.ds(_sorted_row(pstart_ref, route_ref, t), 1)],
                              sem).start(priority=t % DMA_PRIORITIES)
    pltpu.make_async_copy(h1_ref, xout_ref.at[pl.ds(0, TS)], sem).wait()


def _scatter_rows(pstart, route3, h1, xbuf):
    t = h1.shape[0]
    grid_spec = pltpu.PrefetchScalarGridSpec(
        num_scalar_prefetch=1,
        grid=(t // TS,),
        in_specs=[
            pl.BlockSpec((1, 2, TS), lambda i, ps: (i, 0, 0), memory_space=pltpu.SMEM),
            pl.BlockSpec((TS, D_MODEL), lambda i, ps: (i, 0)),
            pl.BlockSpec(memory_space=pl.ANY),
        ],
        out_specs=pl.BlockSpec(memory_space=pl.ANY),
        scratch_shapes=[pltpu.SemaphoreType.DMA(())],
    )
    return pl.pallas_call(
        _scatter_kernel,
        grid_spec=grid_spec,
        out_shape=jax.ShapeDtypeStruct(xbuf.shape, xbuf.dtype),
        input_output_aliases={3: 0},
        compiler_params=pltpu.CompilerParams(dimension_semantics=("arbitrary",)),
        name="scatter_rows",
    )(pstart, route3, h1, xbuf)


def _expert_kernel(ea_ref, eb_ref, nu_ref, x_ref, w1a_ref, w1b_ref, w3a_ref, w3b_ref,
                   w2a_ref, w2b_ref, wr_ref, y_ref, w1a, w1b, w3a, w3b, w2a, w2b):
    i = pl.program_id(0)
    used = i < nu_ref[0]
    ea = ea_ref[i]
    eb = eb_ref[i]
    prev = jnp.maximum(i - 1, 0)

    @pl.when(used & ((i == 0) | (ea != ea_ref[prev])))
    def _():
        w1a[...] = w1a_ref[0, 0].astype(BF16)
        w3a[...] = w3a_ref[0, 0].astype(BF16)
        w2a[...] = w2a_ref[0, 0].astype(BF16)

    @pl.when(used & ((i == 0) | (eb != eb_ref[prev])))
    def _():
        w1b[...] = w1b_ref[0, 0].astype(BF16)
        w3b[...] = w3b_ref[0, 0].astype(BF16)
        w2b[...] = w2b_ref[0, 0].astype(BF16)

    @pl.when(used)
    def _():
        xb = x_ref[...].astype(BF16)

        logits = _dot(xb, wr_ref[...].astype(BF16))
        lane = lax.broadcasted_iota(I32, logits.shape, 1)
        logits = jnp.where(lane < N_EXPERTS, logits, -jnp.inf)
        ex = jnp.exp(logits - jnp.max(logits, axis=1, keepdims=True))
        probs = ex / jnp.sum(ex, axis=1, keepdims=True)
        pa = jnp.sum(jnp.where(lane == ea, probs, 0.0), axis=1, keepdims=True)
        pb = jnp.sum(jnp.where(lane == eb, probs, 0.0), axis=1, keepdims=True)
        den = pa + pb

        def ffn(w1, w3, w2):
            a = _dot(xb, w1[...])
            b = _dot(xb, w3[...])
            hid = a * jax.nn.sigmoid(a) * b
            return _dot(hid.astype(BF16), w2[...])

        y_ref[...] = (pa / den) * ffn(w1a, w3a, w2a) + (pb / den) * ffn(w1b, w3b, w2b)

    @pl.when(jnp.logical_not(used))
    def _():
        y_ref[...] = jnp.zeros(y_ref.shape, F32)


def _expert_ffn(layer, blk_ea, blk_eb, nused, xbuf, w1, w3, w2, wr_pad):
    n_rows = xbuf.shape[0]
    used_blk = lambda i, ea, eb, nu: (jnp.minimum(i, nu[0] - 1), 0)
    wa = lambda i, ea, eb, nu: (layer, ea[i], 0, 0)
    wb = lambda i, ea, eb, nu: (layer, eb[i], 0, 0)
    up = lambda im: pl.BlockSpec((1, 1, D_MODEL, D_EXPERT), im)
    down = lambda im: pl.BlockSpec((1, 1, D_EXPERT, D_MODEL), im)
    grid_spec = pltpu.PrefetchScalarGridSpec(
        num_scalar_prefetch=3,
        grid=(n_rows // BM,),
        in_specs=[
            pl.BlockSpec((BM, D_MODEL), used_blk),
            up(wa), up(wb), up(wa), up(wb), down(wa), down(wb),
            pl.BlockSpec((D_MODEL, LANES), lambda i, ea, eb, nu: (0, 0)),
        ],
        out_specs=pl.BlockSpec((BM, D_MODEL), lambda i, ea, eb, nu: (i, 0)),
        scratch_shapes=[
            pltpu.VMEM((D_MODEL, D_EXPERT), BF16),
            pltpu.VMEM((D_MODEL, D_EXPERT), BF16),
            pltpu.VMEM((D_MODEL, D_EXPERT), BF16),
            pltpu.VMEM((D_MODEL, D_EXPERT), BF16),
            pltpu.VMEM((D_EXPERT, D_MODEL), BF16),
            pltpu.VMEM((D_EXPERT, D_MODEL), BF16),
        ],
    )
    return pl.pallas_call(
        _expert_kernel,
        grid_spec=grid_spec,
        out_shape=jax.ShapeDtypeStruct((n_rows, D_MODEL), F32),
        compiler_params=pltpu.CompilerParams(
            dimension_semantics=("arbitrary",), vmem_limit_bytes=VMEM_LIMIT),
        name="expert_ffn",
    )(blk_ea, blk_eb, nused, xbuf, w1, w1, w3, w3, w2, w2, wr_pad)


def _dispatch_tables(cnt, nb):
    counts = cnt[:N_CLASSES, 0].astype(I32)
    padded = ((counts + BM - 1) // BM) * BM
    pend = jnp.cumsum(padded).astype(I32)
    pstart = pend - padded
    nused = pend[-1:] // BM
    step = jnp.minimum(jnp.arange(nb, dtype=I32), nused - 1)
    c = jnp.minimum(jnp.searchsorted(pend, step * BM, side="right"), N_CLASSES - 1).astype(I32)
    g = c // PAIRS_PER_GROUP
    p = c - g * PAIRS_PER_GROUP
    ge3 = (p >= 3).astype(I32)
    ge5 = (p >= 5).astype(I32)
    blk_ea = g * EXPERTS_PER_GROUP + ge3 + ge5
    blk_eb = g * EXPERTS_PER_GROUP + p + 1 - 2 * ge3 - ge5
    return pstart, blk_ea, blk_eb, nused


def _final_kernel(pstart_ref, rcur_ref, rnext_ref, h1p_ref, y_hbm, g2_ref, b2_ref, o_ref, ybuf, gsem):
    i = pl.program_id(0)
    y = _prefetch_expert_rows(i, pl.num_programs(0), pstart_ref, rcur_ref, rnext_ref, y_hbm, ybuf, gsem)
    o_ref[...] = _layer_norm(ALPHA * h1p_ref[...] + y, g2_ref[0], b2_ref[0])


def _final_call(pstart_prev, route_prev, h1_prev, y_prev, g2, b2):
    t = h1_prev.shape[0]
    nt = t // TS
    last = DEPTH - 1
    grid_spec = pltpu.PrefetchScalarGridSpec(
        num_scalar_prefetch=1,
        grid=(nt,),
        in_specs=[
            pl.BlockSpec((1, 2, TS), lambda i, ps: (i, 0, 0), memory_space=pltpu.SMEM),
            pl.BlockSpec((1, 2, TS), lambda i, ps: (jnp.minimum(i + 1, nt - 1), 0, 0),
                         memory_space=pltpu.SMEM),
            pl.BlockSpec((TS, D_MODEL), lambda i, ps: (i, 0)),
            pl.BlockSpec(memory_space=pl.ANY),
            pl.BlockSpec((1, 1, D_MODEL), lambda i, ps: (last, 0, 0)),
            pl.BlockSpec((1, 1, D_MODEL), lambda i, ps: (last, 0, 0)),
        ],
        out_specs=pl.BlockSpec((TS, D_MODEL), lambda i, ps: (i, 0)),
        scratch_shapes=[pltpu.VMEM((2, TS, D_MODEL), F32), pltpu.SemaphoreType.DMA((2,))],
    )
    return pl.pallas_call(
        _final_kernel,
        grid_spec=grid_spec,
        out_shape=jax.ShapeDtypeStruct((t, D_MODEL), F32),
        compiler_params=pltpu.CompilerParams(
            dimension_semantics=("arbitrary",), vmem_limit_bytes=VMEM_LIMIT),
        name="final_combine_ln",
    )(pstart_prev, route_prev, route_prev, h1_prev, y_prev, g2, b2)


def kernel(x, ln_in_g, ln_in_b, w_in, pool_w, pool_scale, conv_w, conv_b, w_out, ln1_g, ln1_b,
           w_router, router_bias, exp_w1, exp_w3, exp_w2, ln2_g, ln2_b):
    bsz, seq, d = x.shape
    t = bsz * seq
    n_rows = t + N_CLASSES * BM
    nb = n_rows // BM

    per_layer_row = lambda v: v.reshape(DEPTH, 1, -1)
    tri = (lax.broadcasted_iota(I32, (TS, TS), 0)
           < lax.broadcasted_iota(I32, (TS, TS), 1)).astype(BF16)
    wr_t = w_router.T
    wr_pad = jnp.pad(w_router, ((0, 0), (0, LANES - N_EXPERTS)))
    rbias = router_bias.reshape(N_EXPERTS, 1).astype(F32)
    pool_scale3, conv_b3 = per_layer_row(pool_scale), per_layer_row(conv_b)
    ln1_g3, ln1_b3 = per_layer_row(ln1_g), per_layer_row(ln1_b)
    ln2_g3, ln2_b3 = per_layer_row(ln2_g), per_layer_row(ln2_b)

    prev = (x.reshape(t, d), ln_in_g.reshape(1, -1), ln_in_b.reshape(1, -1))
    xbuf = jnp.zeros((n_rows, d), F32)
    for l in range(DEPTH):
        h1, route3, cnt = _layer_call(l, bsz, seq, prev, w_in, pool_w, pool_scale3, conv_w, conv_b3,
                                      w_out, ln1_g3, ln1_b3, wr_t, rbias, tri)
        pstart, blk_ea, blk_eb, nused = _dispatch_tables(cnt, nb)
        xbuf = _scatter_rows(pstart, route3, h1, xbuf)
        ybuf = _expert_ffn(l, blk_ea, blk_eb, nused, xbuf, exp_w1, exp_w3, exp_w2, wr_pad)
        prev = (pstart, route3, h1, ybuf, ln2_g3, ln2_b3)
    out = _final_call(*prev)
    return out.reshape(bsz, seq, d)
```

```python
import functools

import jax
import jax.numpy as jnp
from jax import lax
from jax.experimental import pallas as pl
from jax.experimental.pallas import tpu as pltpu

D_MODEL = 1024
DEPTH = 4
D_POOL = 512
POOL_WINDOWS = (2, 4, 8, 16)
POOL_GROUP = 128
D_CONV = 512
CONV_WIDTH = 3
D_PROJ = D_POOL + 3 * D_CONV
N_EXPERTS = 16
N_EXPERT_GROUPS = 4
EXPERTS_PER_GROUP = 4
PAIRS_PER_GROUP = 6
N_CLASSES = N_EXPERT_GROUPS * PAIRS_PER_GROUP
CLASS_ROWS = 32
D_EXPERT = 512
ALPHA = float((2 * DEPTH) ** 0.25)
LN_EPS = 1e-5

F32 = jnp.float32
BF16 = jnp.bfloat16
I32 = jnp.int32

LANES = 128
ROW_TILE = D_MODEL // LANES
TS = 512
V_HALO = 16
Z_HALO = 8
BM = 256
W_STAGE_ROWS = 256
VMEM_LIMIT = 56 * 1024 * 1024


def _layer_norm(x, g, b):
    mu = jnp.mean(x, axis=-1, keepdims=True)
    xc = x - mu
    var = jnp.mean(xc * xc, axis=-1, keepdims=True)
    return xc * lax.rsqrt(var + LN_EPS) * g + b


def _dot(a, b):
    return jnp.dot(a, b, preferred_element_type=F32)


def _store_row_tiled(ref, value):
    n = value.shape[0]
    for j in range(ROW_TILE):
        ref[pl.ds(j, n, stride=ROW_TILE), :] = value[:, j * LANES:(j + 1) * LANES]


def _load_row_tiled(ref, n):
    return jnp.concatenate([ref[pl.ds(j, n, stride=ROW_TILE), :] for j in range(ROW_TILE)], axis=1)


def _row(ref, first_sublane):
    return ref.at[pl.ds(first_sublane, ROW_TILE)]


def _route(sel):
    rows = [sel[e:e + 1, :] for e in range(N_EXPERTS)]
    gscore = []
    for g in range(N_EXPERT_GROUPS):
        a, b, c, d = rows[4 * g:4 * g + 4]
        hi1, lo1 = jnp.maximum(a, b), jnp.minimum(a, b)
        hi2, lo2 = jnp.maximum(c, d), jnp.minimum(c, d)
        m1 = jnp.maximum(hi1, hi2)
        m2 = jnp.maximum(jnp.minimum(hi1, hi2), jnp.maximum(lo1, lo2))
        gscore.append(m1 + m2)
    best = gscore[0]
    gidx = jnp.zeros(best.shape, I32)
    for g in range(1, N_EXPERT_GROUPS):
        upd = gscore[g] > best
        best = jnp.where(upd, gscore[g], best)
        gidx = jnp.where(upd, g, gidx)
    neg = jnp.full(best.shape, -jnp.inf, F32)
    masked = [jnp.where(gidx == (e // EXPERTS_PER_GROUP), rows[e], neg) for e in range(N_EXPERTS)]

    def top1(vals):
        bv = neg
        bi = jnp.zeros(best.shape, I32)
        for e in range(N_EXPERTS):
            upd = vals[e] > bv
            bv = jnp.where(upd, vals[e], bv)
            bi = jnp.where(upd, e, bi)
        return bi

    i1 = top1(masked)
    i2 = top1([jnp.where(i1 == e, neg, masked[e]) for e in range(N_EXPERTS)])
    return gidx, i1, i2


def _pair_class(gidx, i1, i2):
    a = jnp.minimum(i1, i2) - gidx * EXPERTS_PER_GROUP
    b = jnp.maximum(i1, i2) - gidx * EXPERTS_PER_GROUP
    base = jnp.where(a == 0, 0, jnp.where(a == 1, 2, 3))
    return gidx * PAIRS_PER_GROUP + base + b - 1


def _prefetch_expert_rows(i, nt, scur_ref, snext_ref, y_hbm, ybuf, gsem):
    slot = i % 2

    def gather(slot_ref, to_slot, t, t_sublane):
        src = _row(y_hbm, pl.multiple_of(slot_ref[0, 0, t], ROW_TILE))
        return pltpu.make_async_copy(src, _row(ybuf.at[to_slot], t_sublane), gsem.at[to_slot])

    @pl.when(i == 0)
    def _():
        def body(t, carry):
            gather(scur_ref, 0, t, pl.multiple_of(t * ROW_TILE, ROW_TILE)).start()
            return carry
        lax.fori_loop(0, TS, body, 0, unroll=8)

    @pl.when(i + 1 < nt)
    def _():
        for t in range(TS):
            gather(snext_ref, 1 - slot, t, t * ROW_TILE).start()

    pltpu.make_async_copy(y_hbm.at[pl.ds(0, TS * ROW_TILE)], ybuf.at[slot], gsem.at[slot]).wait()
    return _load_row_tiled(ybuf.at[slot], TS)


def _layer_kernel(*refs, layer, tiles_per_seq):
    if layer == 0:
        x_ref, gin_ref, bin_ref = refs[:3]
        rest = refs[3:]
    else:
        scur_ref, snext_ref, h1p_ref, y_hbm, g2_ref, b2_ref = refs[:6]
        rest = refs[6:]
    (win_hbm, pw_ref, ps_ref, cw_ref, cb_ref, wout_hbm, g_ref, b_ref, wr_ref, rb_ref, tri_ref,
     h1_ref, h1t_ref, route_ref, cnt_ref,
     vbuf, zbuf, mixbuf, carry, win_ref, wout_ref, stage, wsem, ybuf, gsem) = rest

    i = pl.program_id(0)
    nt = pl.num_programs(0)

    @pl.when(i == 0)
    def _():
        for c in range(D_MODEL // W_STAGE_ROWS):
            rows = pl.ds(c * W_STAGE_ROWS, W_STAGE_ROWS)
            cp = pltpu.make_async_copy(win_hbm.at[layer, rows], stage, wsem)
            cp.start()
            cp.wait()
            win_ref[rows, :] = stage[...].astype(BF16)
        for c in range(D_MODEL // W_STAGE_ROWS):
            rows = pl.ds(c * W_STAGE_ROWS, W_STAGE_ROWS)
            cp = pltpu.make_async_copy(wout_hbm.at[layer, rows], stage.at[:, pl.ds(0, D_MODEL)], wsem)
            cp.start()
            cp.wait()
            wout_ref[rows, :] = stage[:, 0:D_MODEL].astype(BF16)
        carry[...] = jnp.zeros(carry.shape, F32)

    if layer == 0:
        h = _layer_norm(x_ref[...], gin_ref[...], bin_ref[...])
    else:
        y = _prefetch_expert_rows(i, nt, scur_ref, snext_ref, y_hbm, ybuf, gsem)
        h = _layer_norm(ALPHA * h1p_ref[...] + y, g2_ref[0], b2_ref[0])

    si = i % tiles_per_seq

    @pl.when(si == 0)
    def _():
        vbuf[0:V_HALO, :] = jnp.zeros((V_HALO, D_POOL), F32)
        zbuf[0:Z_HALO, :] = jnp.zeros((Z_HALO, D_CONV), F32)

    hb = h.astype(BF16)
    vbuf[V_HALO:V_HALO + TS, :] = _dot(hb, win_ref[:, 0:D_POOL])
    gate_b = _dot(hb, win_ref[:, D_POOL:D_POOL + D_CONV])
    gate_c = _dot(hb, win_ref[:, D_POOL + D_CONV:D_POOL + 2 * D_CONV])
    u = _dot(hb, win_ref[:, D_POOL + 2 * D_CONV:D_PROJ])
    z = gate_c * u
    zbuf[Z_HALO:Z_HALO + TS, :] = z

    tpos = si * TS + lax.broadcasted_iota(I32, (TS, 1), 0)
    for g, win in enumerate(POOL_WINDOWS):
        cols = pl.ds(g * POOL_GROUP, POOL_GROUP)
        cur = vbuf[pl.ds(V_HALO, TS), cols]
        acc = cur
        for j in range(1, win):
            acc = acc + vbuf[pl.ds(V_HALO - j, TS), cols]
        denom = jnp.minimum(tpos + 1, win).astype(F32)
        pooled = acc / denom - cur
        mixed = _dot(pooled.astype(BF16), pw_ref[0, g].astype(BF16)) * ps_ref[0, :, cols]
        mixbuf[:, cols] = mixed.astype(BF16)

    yc = (cw_ref[0, 2:3, :] * z
          + cw_ref[0, 1:2, :] * zbuf[pl.ds(Z_HALO - 1, TS), :]
          + cw_ref[0, 0:1, :] * zbuf[pl.ds(Z_HALO - 2, TS), :])
    mixbuf[:, D_POOL:D_POOL + D_CONV] = (gate_b * (yc + cb_ref[0])).astype(BF16)

    vbuf[0:V_HALO, :] = vbuf[TS:TS + V_HALO, :]
    zbuf[0:Z_HALO, :] = zbuf[TS:TS + Z_HALO, :]

    mix = _dot(mixbuf[...], wout_ref[...])
    h1 = _layer_norm(ALPHA * h + mix, g_ref[0], b_ref[0])
    h1_ref[...] = h1
    _store_row_tiled(h1t_ref, h1)

    logits = lax.dot_general(wr_ref[...].astype(BF16), h1.astype(BF16), (((1,), (1,)), ((), ())),
                             preferred_element_type=F32)
    mx = jnp.max(logits, axis=0, keepdims=True)
    ex = jnp.exp(logits - mx)
    probs = ex / jnp.sum(ex, axis=0, keepdims=True)
    gidx, i1, i2 = _route(probs + rb_ref[...])
    cls = _pair_class(gidx, i1, i2)

    ciota = lax.broadcasted_iota(I32, (CLASS_ROWS, TS), 0)
    onehot = jnp.where(ciota == cls, 1.0, 0.0).astype(F32)
    before = _dot(onehot.astype(BF16), tri_ref[...])
    total = jnp.sum(onehot, axis=1, keepdims=True)
    c = carry[:, 0:1]
    rank = jnp.sum(onehot * (c + before), axis=0, keepdims=True)
    route_ref[0:1, :] = cls
    route_ref[1:2, :] = rank.astype(I32)
    newc = jnp.broadcast_to(c + total, carry.shape)
    carry[...] = newc
    cnt_ref[...] = newc


def _layer_call(layer, bsz, seq, prev, w_in, pool_w, pool_scale, conv_w, conv_b, w_out, ln_g, ln_b,
                wr_t, rbias, tri):
    t = bsz * seq
    nt = t // TS
    const2 = lambda i: (0, 0)
    lsel3 = lambda i: (layer, 0, 0)
    lsel4 = lambda i: (layer, 0, 0, 0)
    tile = lambda i: (i, 0)
    lane_tile = lambda i: (0, i)
    if layer == 0:
        x, gin, bin_ = prev
        head_args = (x, gin, bin_)
        head_specs = [
            pl.BlockSpec((TS, D_MODEL), tile),
            pl.BlockSpec((1, D_MODEL), const2),
            pl.BlockSpec((1, D_MODEL), const2),
        ]
    else:
        slot_prev, h1_prev, y_prev, g2, b2 = prev
        head_args = (slot_prev, slot_prev, h1_prev, y_prev, g2, b2)
        head_specs = [
            pl.BlockSpec((1, 1, TS), lambda i: (i, 0, 0), memory_space=pltpu.SMEM),
            pl.BlockSpec((1, 1, TS), lambda i: (jnp.minimum(i + 1, nt - 1), 0, 0),
                         memory_space=pltpu.SMEM),
            pl.BlockSpec((TS, D_MODEL), tile),
            pl.BlockSpec(memory_space=pl.ANY),
            pl.BlockSpec((1, 1, D_MODEL), lambda i: (layer - 1, 0, 0)),
            pl.BlockSpec((1, 1, D_MODEL), lambda i: (layer - 1, 0, 0)),
        ]
    return pl.pallas_call(
        functools.partial(_layer_kernel, layer=layer, tiles_per_seq=seq // TS),
        grid=(nt,),
        in_specs=head_specs + [
            pl.BlockSpec(memory_space=pl.ANY),
            pl.BlockSpec((1, len(POOL_WINDOWS), POOL_GROUP, POOL_GROUP), lsel4),
            pl.BlockSpec((1, 1, D_POOL), lsel3),
            pl.BlockSpec((1, CONV_WIDTH, D_CONV), lsel3),
            pl.BlockSpec((1, 1, D_CONV), lsel3),
            pl.BlockSpec(memory_space=pl.ANY),
            pl.BlockSpec((1, 1, D_MODEL), lsel3),
            pl.BlockSpec((1, 1, D_MODEL), lsel3),
            pl.BlockSpec((N_EXPERTS, D_MODEL), const2),
            pl.BlockSpec((N_EXPERTS, 1), const2),
            pl.BlockSpec((TS, TS), const2),
        ],
        out_specs=[
            pl.BlockSpec((TS, D_MODEL), tile),
            pl.BlockSpec((TS * ROW_TILE, LANES), tile),
            pl.BlockSpec((2, TS), lane_tile),
            pl.BlockSpec((CLASS_ROWS, LANES), const2),
        ],
        out_shape=[
            jax.ShapeDtypeStruct((t, D_MODEL), F32),
            jax.ShapeDtypeStruct((t * ROW_TILE, LANES), F32),
            jax.ShapeDtypeStruct((2, t), I32),
            jax.ShapeDtypeStruct((CLASS_ROWS, LANES), F32),
        ],
        scratch_shapes=[
            pltpu.VMEM((TS + V_HALO, D_POOL), F32),
            pltpu.VMEM((TS + Z_HALO, D_CONV), F32),
            pltpu.VMEM((TS, D_MODEL), BF16),
            pltpu.VMEM((CLASS_ROWS, LANES), F32),
            pltpu.VMEM((D_MODEL, D_PROJ), BF16),
            pltpu.VMEM((D_MODEL, D_MODEL), BF16),
            pltpu.VMEM((W_STAGE_ROWS, D_PROJ), F32),
            pltpu.SemaphoreType.DMA(()),
            pltpu.VMEM((2, TS * ROW_TILE, LANES), F32),
            pltpu.SemaphoreType.DMA((2,)),
        ],
        compiler_params=pltpu.CompilerParams(
            dimension_semantics=("arbitrary",), vmem_limit_bytes=VMEM_LIMIT),
        name="layer_mix_route",
    )(*head_args, w_in, pool_w, pool_scale, conv_w, conv_b, w_out, ln_g, ln_b, wr_t, rbias, tri)


def _slot_kernel(pstart_ref, route_ref, slot_ref):
    cls = route_ref[0:1, :]
    start = jnp.zeros(cls.shape, I32)
    for k in range(N_CLASSES):
        start = jnp.where(cls == k, pstart_ref[k], start)
    slot_ref[...] = (start + route_ref[1:2, :]) * ROW_TILE


def _slots(pstart, route):
    t = route.shape[1]
    grid_spec = pltpu.PrefetchScalarGridSpec(
        num_scalar_prefetch=1,
        grid=(1,),
        in_specs=[pl.BlockSpec((2, t), lambda i, ps: (0, 0))],
        out_specs=pl.BlockSpec((1, t), lambda i, ps: (0, 0)),
    )
    return pl.pallas_call(
        _slot_kernel,
        grid_spec=grid_spec,
        out_shape=jax.ShapeDtypeStruct((1, t), I32),
        name="token_slots",
    )(pstart, route)


def _scatter_kernel(slot_ref, h1t_ref, xin_ref, xout_ref, sem):
    del xin_ref
    for t in range(TS):
        dst = _row(xout_ref, pl.multiple_of(slot_ref[0, 0, t], ROW_TILE))
        pltpu.make_async_copy(_row(h1t_ref, t * ROW_TILE), dst, sem).start()
    pltpu.make_async_copy(h1t_ref, xout_ref.at[pl.ds(0, TS * ROW_TILE)], sem).wait()


def _scatter_rows(slot3, h1t, xbuf):
    nt = slot3.shape[0]
    return pl.pallas_call(
        _scatter_kernel,
        grid=(nt,),
        in_specs=[
            pl.BlockSpec((1, 1, TS), lambda i: (i, 0, 0), memory_space=pltpu.SMEM),
            pl.BlockSpec((TS * ROW_TILE, LANES), lambda i: (i, 0)),
            pl.BlockSpec(memory_space=pl.ANY),
        ],
        out_specs=pl.BlockSpec(memory_space=pl.ANY),
        out_shape=jax.ShapeDtypeStruct(xbuf.shape, xbuf.dtype),
        scratch_shapes=[pltpu.SemaphoreType.DMA(())],
        input_output_aliases={2: 0},
        compiler_params=pltpu.CompilerParams(dimension_semantics=("arbitrary",)),
        name="scatter_rows",
    )(slot3, h1t, xbuf)


def _expert_kernel(ea_ref, eb_ref, nu_ref, x_ref, w1a_ref, w1b_ref, w3a_ref, w3b_ref,
                   w2a_ref, w2b_ref, wr_ref, y_ref, w1a, w1b, w3a, w3b, w2a, w2b):
    i = pl.program_id(0)
    used = i < nu_ref[0]
    ea = ea_ref[i]
    eb = eb_ref[i]
    prev = jnp.maximum(i - 1, 0)

    @pl.when(used & ((i == 0) | (ea != ea_ref[prev])))
    def _():
        w1a[...] = w1a_ref[0, 0].astype(BF16)
        w3a[...] = w3a_ref[0, 0].astype(BF16)
        w2a[...] = w2a_ref[0, 0].astype(BF16)

    @pl.when(used & ((i == 0) | (eb != eb_ref[prev])))
    def _():
        w1b[...] = w1b_ref[0, 0].astype(BF16)
        w3b[...] = w3b_ref[0, 0].astype(BF16)
        w2b[...] = w2b_ref[0, 0].astype(BF16)

    @pl.when(used)
    def _():
        xb = _load_row_tiled(x_ref, BM).astype(BF16)

        logits = _dot(xb, wr_ref[...].astype(BF16))
        lane = lax.broadcasted_iota(I32, logits.shape, 1)
        logits = jnp.where(lane < N_EXPERTS, logits, -jnp.inf)
        ex = jnp.exp(logits - jnp.max(logits, axis=1, keepdims=True))
        probs = ex / jnp.sum(ex, axis=1, keepdims=True)
        pa = jnp.sum(jnp.where(lane == ea, probs, 0.0), axis=1, keepdims=True)
        pb = jnp.sum(jnp.where(lane == eb, probs, 0.0), axis=1, keepdims=True)
        den = pa + pb

        def ffn(w1, w3, w2):
            a = _dot(xb, w1[...])
            b = _dot(xb, w3[...])
            hid = a * jax.nn.sigmoid(a) * b
            return _dot(hid.astype(BF16), w2[...])

        y = (pa / den) * ffn(w1a, w3a, w2a) + (pb / den) * ffn(w1b, w3b, w2b)
        _store_row_tiled(y_ref, y)

    @pl.when(jnp.logical_not(used))
    def _():
        y_ref[...] = jnp.zeros(y_ref.shape, F32)


def _expert_ffn(layer, blk_ea, blk_eb, nused, xbuf, w1, w3, w2, wr_pad):
    n_blocks = xbuf.shape[0] // (BM * ROW_TILE)
    used_blk = lambda i, ea, eb, nu: (jnp.minimum(i, nu[0] - 1), 0)
    wa = lambda i, ea, eb, nu: (layer, ea[i], 0, 0)
    wb = lambda i, ea, eb, nu: (layer, eb[i], 0, 0)
    up = lambda im: pl.BlockSpec((1, 1, D_MODEL, D_EXPERT), im)
    down = lambda im: pl.BlockSpec((1, 1, D_EXPERT, D_MODEL), im)
    grid_spec = pltpu.PrefetchScalarGridSpec(
        num_scalar_prefetch=3,
        grid=(n_blocks,),
        in_specs=[
            pl.BlockSpec((BM * ROW_TILE, LANES), used_blk),
            up(wa), up(wb), up(wa), up(wb), down(wa), down(wb),
            pl.BlockSpec((D_MODEL, LANES), lambda i, ea, eb, nu: (0, 0)),
        ],
        out_specs=pl.BlockSpec((BM * ROW_TILE, LANES), lambda i, ea, eb, nu: (i, 0)),
        scratch_shapes=[
            pltpu.VMEM((D_MODEL, D_EXPERT), BF16),
            pltpu.VMEM((D_MODEL, D_EXPERT), BF16),
            pltpu.VMEM((D_MODEL, D_EXPERT), BF16),
            pltpu.VMEM((D_MODEL, D_EXPERT), BF16),
            pltpu.VMEM((D_EXPERT, D_MODEL), BF16),
            pltpu.VMEM((D_EXPERT, D_MODEL), BF16),
        ],
    )
    return pl.pallas_call(
        _expert_kernel,
        grid_spec=grid_spec,
        out_shape=jax.ShapeDtypeStruct(xbuf.shape, F32),
        compiler_params=pltpu.CompilerParams(
            dimension_semantics=("arbitrary",), vmem_limit_bytes=VMEM_LIMIT),
        name="expert_ffn",
    )(blk_ea, blk_eb, nused, xbuf, w1, w1, w3, w3, w2, w2, wr_pad)


def _dispatch_tables(cnt, nb):
    counts = cnt[:N_CLASSES, 0].astype(I32)
    padded = ((counts + BM - 1) // BM) * BM
    pend = jnp.cumsum(padded).astype(I32)
    pstart = pend - padded
    nused = pend[-1:] // BM
    step = jnp.minimum(jnp.arange(nb, dtype=I32), nused - 1)
    c = jnp.minimum(jnp.searchsorted(pend, step * BM, side="right"), N_CLASSES - 1).astype(I32)
    g = c // PAIRS_PER_GROUP
    p = c - g * PAIRS_PER_GROUP
    ge3 = (p >= 3).astype(I32)
    ge5 = (p >= 5).astype(I32)
    blk_ea = g * EXPERTS_PER_GROUP + ge3 + ge5
    blk_eb = g * EXPERTS_PER_GROUP + p + 1 - 2 * ge3 - ge5
    return pstart, blk_ea, blk_eb, nused


def _final_kernel(scur_ref, snext_ref, h1p_ref, y_hbm, g2_ref, b2_ref, o_ref, ybuf, gsem):
    i = pl.program_id(0)
    y = _prefetch_expert_rows(i, pl.num_programs(0), scur_ref, snext_ref, y_hbm, ybuf, gsem)
    o_ref[...] = _layer_norm(ALPHA * h1p_ref[...] + y, g2_ref[0], b2_ref[0])


def _final_call(slot_prev, h1_prev, y_prev, g2, b2):
    t = h1_prev.shape[0]
    nt = t // TS
    last = DEPTH - 1
    return pl.pallas_call(
        _final_kernel,
        grid=(nt,),
        in_specs=[
            pl.BlockSpec((1, 1, TS), lambda i: (i, 0, 0), memory_space=pltpu.SMEM),
            pl.BlockSpec((1, 1, TS), lambda i: (jnp.minimum(i + 1, nt - 1), 0, 0),
                         memory_space=pltpu.SMEM),
            pl.BlockSpec((TS, D_MODEL), lambda i: (i, 0)),
            pl.BlockSpec(memory_space=pl.ANY),
            pl.BlockSpec((1, 1, D_MODEL), lambda i: (last, 0, 0)),
            pl.BlockSpec((1, 1, D_MODEL), lambda i: (last, 0, 0)),
        ],
        out_specs=pl.BlockSpec((TS, D_MODEL), lambda i: (i, 0)),
        out_shape=jax.ShapeDtypeStruct((t, D_MODEL), F32),
        scratch_shapes=[pltpu.VMEM((2, TS * ROW_TILE, LANES), F32), pltpu.SemaphoreType.DMA((2,))],
        compiler_params=pltpu.CompilerParams(
            dimension_semantics=("arbitrary",), vmem_limit_bytes=VMEM_LIMIT),
        name="final_combine_ln",
    )(slot_prev, slot_prev, h1_prev, y_prev, g2, b2)


def kernel(x, ln_in_g, ln_in_b, w_in, pool_w, pool_scale, conv_w, conv_b, w_out, ln1_g, ln1_b,
           w_router, router_bias, exp_w1, exp_w3, exp_w2, ln2_g, ln2_b):
    bsz, seq, d = x.shape
    t = bsz * seq
    nt = t // TS
    n_rows = t + N_CLASSES * BM
    nb = n_rows // BM

    per_layer_row = lambda v: v.reshape(DEPTH, 1, -1)
    tri = (lax.broadcasted_iota(I32, (TS, TS), 0)
           < lax.broadcasted_iota(I32, (TS, TS), 1)).astype(BF16)
    wr_t = w_router.T
    wr_pad = jnp.pad(w_router, ((0, 0), (0, LANES - N_EXPERTS)))
    rbias = router_bias.reshape(N_EXPERTS, 1).astype(F32)
    pool_scale3, conv_b3 = per_layer_row(pool_scale), per_layer_row(conv_b)
    ln1_g3, ln1_b3 = per_layer_row(ln1_g), per_layer_row(ln1_b)
    ln2_g3, ln2_b3 = per_layer_row(ln2_g), per_layer_row(ln2_b)

    prev = (x.reshape(t, d), ln_in_g.reshape(1, -1), ln_in_b.reshape(1, -1))
    xbuf = jnp.zeros((n_rows * ROW_TILE, LANES), F32)
    for l in range(DEPTH):
        h1, h1t, route, cnt = _layer_call(l, bsz, seq, prev, w_in, pool_w, pool_scale3, conv_w, conv_b3,
                                          w_out, ln1_g3, ln1_b3, wr_t, rbias, tri)
        pstart, blk_ea, blk_eb, nused = _dispatch_tables(cnt, nb)
        slot3 = _slots(pstart, route).reshape(nt, 1, TS)
        xbuf = _scatter_rows(slot3, h1t, xbuf)
        ybuf = _expert_ffn(l, blk_ea, blk_eb, nused, xbuf, exp_w1, exp_w3, exp_w2, wr_pad)
        prev = (slot3, h1, ybuf, ln2_g3, ln2_b3)
    out = _final_call(*prev)
    return out.reshape(bsz, seq, d)
```

```python
import functools

import jax
import jax.numpy as jnp
from jax import lax
from jax.experimental import pallas as pl
from jax.experimental.pallas import tpu as pltpu

D_MODEL = 1024
DEPTH = 4
D_POOL = 512
POOL_WINDOWS = (2, 4, 8, 16)
POOL_GROUP = 128
D_CONV = 512
CONV_WIDTH = 3
D_PROJ = D_POOL + 3 * D_CONV
N_EXPERTS = 16
N_EXPERT_GROUPS = 4
EXPERTS_PER_GROUP = 4
PAIRS_PER_GROUP = 6
N_CLASSES = N_EXPERT_GROUPS * PAIRS_PER_GROUP
CLASS_ROWS = 32
D_EXPERT = 512
ALPHA = float((2 * DEPTH) ** 0.25)
LN_EPS = 1e-5

F32 = jnp.float32
BF16 = jnp.bfloat16
I32 = jnp.int32

LANES = 128
ROW_TILE = D_MODEL // LANES
TS = 512
V_HALO = 16
Z_HALO = 8
BM = 256
W_STAGE_ROWS = 256
DMA_QUEUES = 2
VMEM_LIMIT = 56 * 1024 * 1024


def _layer_norm(x, g, b):
    mu = jnp.mean(x, axis=-1, keepdims=True)
    xc = x - mu
    var = jnp.mean(xc * xc, axis=-1, keepdims=True)
    return xc * lax.rsqrt(var + LN_EPS) * g + b


def _dot(a, b):
    return jnp.dot(a, b, preferred_element_type=F32)


def _store_row_tiled(ref, value):
    n = value.shape[0]
    for j in range(ROW_TILE):
        ref[pl.ds(j, n, stride=ROW_TILE), :] = value[:, j * LANES:(j + 1) * LANES]


def _load_row_tiled(ref, n):
    return jnp.concatenate([ref[pl.ds(j, n, stride=ROW_TILE), :] for j in range(ROW_TILE)], axis=1)


def _row(ref, first_sublane):
    return ref.at[pl.ds(first_sublane, ROW_TILE)]


def _route(sel):
    rows = [sel[e:e + 1, :] for e in range(N_EXPERTS)]
    gscore = []
    for g in range(N_EXPERT_GROUPS):
        a, b, c, d = rows[4 * g:4 * g + 4]
        hi1, lo1 = jnp.maximum(a, b), jnp.minimum(a, b)
        hi2, lo2 = jnp.maximum(c, d), jnp.minimum(c, d)
        m1 = jnp.maximum(hi1, hi2)
        m2 = jnp.maximum(jnp.minimum(hi1, hi2), jnp.maximum(lo1, lo2))
        gscore.append(m1 + m2)
    best = gscore[0]
    gidx = jnp.zeros(best.shape, I32)
    for g in range(1, N_EXPERT_GROUPS):
        upd = gscore[g] > best
        best = jnp.where(upd, gscore[g], best)
        gidx = jnp.where(upd, g, gidx)
    neg = jnp.full(best.shape, -jnp.inf, F32)
    masked = [jnp.where(gidx == (e // EXPERTS_PER_GROUP), rows[e], neg) for e in range(N_EXPERTS)]

    def top1(vals):
        bv = neg
        bi = jnp.zeros(best.shape, I32)
        for e in range(N_EXPERTS):
            upd = vals[e] > bv
            bv = jnp.where(upd, vals[e], bv)
            bi = jnp.where(upd, e, bi)
        return bi

    i1 = top1(masked)
    i2 = top1([jnp.where(i1 == e, neg, masked[e]) for e in range(N_EXPERTS)])
    return gidx, i1, i2


def _pair_class(gidx, i1, i2):
    a = jnp.minimum(i1, i2) - gidx * EXPERTS_PER_GROUP
    b = jnp.maximum(i1, i2) - gidx * EXPERTS_PER_GROUP
    base = jnp.where(a == 0, 0, jnp.where(a == 1, 2, 3))
    return gidx * PAIRS_PER_GROUP + base + b - 1


def _prefetch_expert_rows(i, nt, scur_ref, snext_ref, y_hbm, ybuf, gsem):
    slot = i % 2

    def gather(slot_ref, to_slot, t, t_sublane):
        src = _row(y_hbm, pl.multiple_of(slot_ref[0, 0, t], ROW_TILE))
        return pltpu.make_async_copy(src, _row(ybuf.at[to_slot], t_sublane), gsem.at[to_slot])

    @pl.when(i == 0)
    def _():
        def body(t, carry):
            gather(scur_ref, 0, t, pl.multiple_of(t * ROW_TILE, ROW_TILE)).start()
            return carry
        lax.fori_loop(0, TS, body, 0, unroll=8)

    @pl.when(i + 1 < nt)
    def _():
        for t in range(TS):
            gather(snext_ref, 1 - slot, t, t * ROW_TILE).start()

    pltpu.make_async_copy(y_hbm.at[pl.ds(0, TS * ROW_TILE)], ybuf.at[slot], gsem.at[slot]).wait()
    return _load_row_tiled(ybuf.at[slot], TS)


def _layer_kernel(*refs, layer, tiles_per_seq):
    if layer == 0:
        x_ref, gin_ref, bin_ref = refs[:3]
        rest = refs[3:]
    else:
        scur_ref, snext_ref, h1p_ref, y_hbm, g2_ref, b2_ref = refs[:6]
        rest = refs[6:]
    (win_hbm, pw_ref, ps_ref, cw_ref, cb_ref, wout_hbm, g_ref, b_ref, wr_ref, rb_ref, tri_ref,
     h1_ref, h1t_ref, route_ref, cnt_ref,
     vbuf, zbuf, mixbuf, carry, win_ref, wout_ref, stage, wsem, ybuf, gsem) = rest

    i = pl.program_id(0)
    nt = pl.num_programs(0)

    @pl.when(i == 0)
    def _():
        for c in range(D_MODEL // W_STAGE_ROWS):
            rows = pl.ds(c * W_STAGE_ROWS, W_STAGE_ROWS)
            cp = pltpu.make_async_copy(win_hbm.at[layer, rows], stage, wsem)
            cp.start()
            cp.wait()
            win_ref[rows, :] = stage[...].astype(BF16)
        for c in range(D_MODEL // W_STAGE_ROWS):
            rows = pl.ds(c * W_STAGE_ROWS, W_STAGE_ROWS)
            cp = pltpu.make_async_copy(wout_hbm.at[layer, rows], stage.at[:, pl.ds(0, D_MODEL)], wsem)
            cp.start()
            cp.wait()
            wout_ref[rows, :] = stage[:, 0:D_MODEL].astype(BF16)
        carry[...] = jnp.zeros(carry.shape, F32)

    if layer == 0:
        h = _layer_norm(x_ref[...], gin_ref[...], bin_ref[...])
    else:
        y = _prefetch_expert_rows(i, nt, scur_ref, snext_ref, y_hbm, ybuf, gsem)
        h = _layer_norm(ALPHA * h1p_ref[...] + y, g2_ref[0], b2_ref[0])

    si = i % tiles_per_seq

    @pl.when(si == 0)
    def _():
        vbuf[0:V_HALO, :] = jnp.zeros((V_HALO, D_POOL), F32)
        zbuf[0:Z_HALO, :] = jnp.zeros((Z_HALO, D_CONV), F32)

    hb = h.astype(BF16)
    vbuf[V_HALO:V_HALO + TS, :] = _dot(hb, win_ref[:, 0:D_POOL])
    gate_b = _dot(hb, win_ref[:, D_POOL:D_POOL + D_CONV])
    gate_c = _dot(hb, win_ref[:, D_POOL + D_CONV:D_POOL + 2 * D_CONV])
    u = _dot(hb, win_ref[:, D_POOL + 2 * D_CONV:D_PROJ])
    z = gate_c * u
    zbuf[Z_HALO:Z_HALO + TS, :] = z

    tpos = si * TS + lax.broadcasted_iota(I32, (TS, 1), 0)
    for g, win in enumerate(POOL_WINDOWS):
        cols = pl.ds(g * POOL_GROUP, POOL_GROUP)
        cur = vbuf[pl.ds(V_HALO, TS), cols]
        acc = cur
        for j in range(1, win):
            acc = acc + vbuf[pl.ds(V_HALO - j, TS), cols]
        denom = jnp.minimum(tpos + 1, win).astype(F32)
        pooled = acc / denom - cur
        mixed = _dot(pooled.astype(BF16), pw_ref[0, g].astype(BF16)) * ps_ref[0, :, cols]
        mixbuf[:, cols] = mixed.astype(BF16)

    yc = (cw_ref[0, 2:3, :] * z
          + cw_ref[0, 1:2, :] * zbuf[pl.ds(Z_HALO - 1, TS), :]
          + cw_ref[0, 0:1, :] * zbuf[pl.ds(Z_HALO - 2, TS), :])
    mixbuf[:, D_POOL:D_POOL + D_CONV] = (gate_b * (yc + cb_ref[0])).astype(BF16)

    vbuf[0:V_HALO, :] = vbuf[TS:TS + V_HALO, :]
    zbuf[0:Z_HALO, :] = zbuf[TS:TS + Z_HALO, :]

    mix = _dot(mixbuf[...], wout_ref[...])
    h1 = _layer_norm(ALPHA * h + mix, g_ref[0], b_ref[0])
    h1_ref[...] = h1
    _store_row_tiled(h1t_ref, h1)

    logits = lax.dot_general(wr_ref[...].astype(BF16), h1.astype(BF16), (((1,), (1,)), ((), ())),
                             preferred_element_type=F32)
    mx = jnp.max(logits, axis=0, keepdims=True)
    ex = jnp.exp(logits - mx)
    probs = ex / jnp.sum(ex, axis=0, keepdims=True)
    gidx, i1, i2 = _route(probs + rb_ref[...])
    cls = _pair_class(gidx, i1, i2)

    ciota = lax.broadcasted_iota(I32, (CLASS_ROWS, TS), 0)
    onehot = jnp.where(ciota == cls, 1.0, 0.0).astype(F32)
    before = _dot(onehot.astype(BF16), tri_ref[...])
    total = jnp.sum(onehot, axis=1, keepdims=True)
    c = carry[:, 0:1]
    rank = jnp.sum(onehot * (c + before), axis=0, keepdims=True)
    route_ref[0:1, :] = cls
    route_ref[1:2, :] = rank.astype(I32)
    newc = jnp.broadcast_to(c + total, carry.shape)
    carry[...] = newc
    cnt_ref[...] = newc


def _layer_call(layer, bsz, seq, prev, w_in, pool_w, pool_scale, conv_w, conv_b, w_out, ln_g, ln_b,
                wr_t, rbias, tri):
    t = bsz * seq
    nt = t // TS
    const2 = lambda i: (0, 0)
    lsel3 = lambda i: (layer, 0, 0)
    lsel4 = lambda i: (layer, 0, 0, 0)
    tile = lambda i: (i, 0)
    lane_tile = lambda i: (0, i)
    if layer == 0:
        x, gin, bin_ = prev
        head_args = (x, gin, bin_)
        head_specs = [
            pl.BlockSpec((TS, D_MODEL), tile),
            pl.BlockSpec((1, D_MODEL), const2),
            pl.BlockSpec((1, D_MODEL), const2),
        ]
    else:
        slot_prev, h1_prev, y_prev, g2, b2 = prev
        head_args = (slot_prev, slot_prev, h1_prev, y_prev, g2, b2)
        head_specs = [
            pl.BlockSpec((1, 1, TS), lambda i: (i, 0, 0), memory_space=pltpu.SMEM),
            pl.BlockSpec((1, 1, TS), lambda i: (jnp.minimum(i + 1, nt - 1), 0, 0),
                         memory_space=pltpu.SMEM),
            pl.BlockSpec((TS, D_MODEL), tile),
            pl.BlockSpec(memory_space=pl.ANY),
            pl.BlockSpec((1, 1, D_MODEL), lambda i: (layer - 1, 0, 0)),
            pl.BlockSpec((1, 1, D_MODEL), lambda i: (layer - 1, 0, 0)),
        ]
    return pl.pallas_call(
        functools.partial(_layer_kernel, layer=layer, tiles_per_seq=seq // TS),
        grid=(nt,),
        in_specs=head_specs + [
            pl.BlockSpec(memory_space=pl.ANY),
            pl.BlockSpec((1, len(POOL_WINDOWS), POOL_GROUP, POOL_GROUP), lsel4),
            pl.BlockSpec((1, 1, D_POOL), lsel3),
            pl.BlockSpec((1, CONV_WIDTH, D_CONV), lsel3),
            pl.BlockSpec((1, 1, D_CONV), lsel3),
            pl.BlockSpec(memory_space=pl.ANY),
            pl.BlockSpec((1, 1, D_MODEL), lsel3),
            pl.BlockSpec((1, 1, D_MODEL), lsel3),
            pl.BlockSpec((N_EXPERTS, D_MODEL), const2),
            pl.BlockSpec((N_EXPERTS, 1), const2),
            pl.BlockSpec((TS, TS), const2),
        ],
        out_specs=[
            pl.BlockSpec((TS, D_MODEL), tile),
            pl.BlockSpec((TS * ROW_TILE, LANES), tile),
            pl.BlockSpec((2, TS), lane_tile),
            pl.BlockSpec((CLASS_ROWS, LANES), const2),
        ],
        out_shape=[
            jax.ShapeDtypeStruct((t, D_MODEL), F32),
            jax.ShapeDtypeStruct((t * ROW_TILE, LANES), F32),
            jax.ShapeDtypeStruct((2, t), I32),
            jax.ShapeDtypeStruct((CLASS_ROWS, LANES), F32),
        ],
        scratch_shapes=[
            pltpu.VMEM((TS + V_HALO, D_POOL), F32),
            pltpu.VMEM((TS + Z_HALO, D_CONV), F32),
            pltpu.VMEM((TS, D_MODEL), BF16),
            pltpu.VMEM((CLASS_ROWS, LANES), F32),
            pltpu.VMEM((D_MODEL, D_PROJ), BF16),
            pltpu.VMEM((D_MODEL, D_MODEL), BF16),
            pltpu.VMEM((W_STAGE_ROWS, D_PROJ), F32),
            pltpu.SemaphoreType.DMA(()),
            pltpu.VMEM((2, TS * ROW_TILE, LANES), F32),
            pltpu.SemaphoreType.DMA((2,)),
        ],
        compiler_params=pltpu.CompilerParams(
            dimension_semantics=("arbitrary",), vmem_limit_bytes=VMEM_LIMIT),
        name="layer_mix_route",
    )(*head_args, w_in, pool_w, pool_scale, conv_w, conv_b, w_out, ln_g, ln_b, wr_t, rbias, tri)


def _slot_kernel(pstart_ref, route_ref, slot_ref):
    cls = route_ref[0:1, :]
    start = jnp.zeros(cls.shape, I32)
    for k in range(N_CLASSES):
        start = jnp.where(cls == k, pstart_ref[k], start)
    slot_ref[...] = (start + route_ref[1:2, :]) * ROW_TILE


def _slots(pstart, route):
    t = route.shape[1]
    grid_spec = pltpu.PrefetchScalarGridSpec(
        num_scalar_prefetch=1,
        grid=(1,),
        in_specs=[pl.BlockSpec((2, t), lambda i, ps: (0, 0))],
        out_specs=pl.BlockSpec((1, t), lambda i, ps: (0, 0)),
    )
    return pl.pallas_call(
        _slot_kernel,
        grid_spec=grid_spec,
        out_shape=jax.ShapeDtypeStruct((1, t), I32),
        name="token_slots",
    )(pstart, route)


def _scatter_kernel(slot_ref, h1t_ref, xin_ref, xout_ref, sem):
    del xin_ref
    for t in range(TS):
        dst = _row(xout_ref, pl.multiple_of(slot_ref[0, 0, t], ROW_TILE))
        pltpu.make_async_copy(_row(h1t_ref, t * ROW_TILE), dst, sem).start(priority=t % DMA_QUEUES)
    pltpu.make_async_copy(h1t_ref, xout_ref.at[pl.ds(0, TS * ROW_TILE)], sem).wait()


def _scatter_rows(slot3, h1t, xbuf):
    nt = slot3.shape[0]
    return pl.pallas_call(
        _scatter_kernel,
        grid=(nt,),
        in_specs=[
            pl.BlockSpec((1, 1, TS), lambda i: (i, 0, 0), memory_space=pltpu.SMEM),
            pl.BlockSpec((TS * ROW_TILE, LANES), lambda i: (i, 0)),
            pl.BlockSpec(memory_space=pl.ANY),
        ],
        out_specs=pl.BlockSpec(memory_space=pl.ANY),
        out_shape=jax.ShapeDtypeStruct(xbuf.shape, xbuf.dtype),
        scratch_shapes=[pltpu.SemaphoreType.DMA(())],
        input_output_aliases={2: 0},
        compiler_params=pltpu.CompilerParams(dimension_semantics=("arbitrary",)),
        name="scatter_rows",
    )(slot3, h1t, xbuf)


def _expert_kernel(ea_ref, eb_ref, nu_ref, x_ref, w1a_ref, w1b_ref, w3a_ref, w3b_ref,
                   w2a_ref, w2b_ref, wr_ref, y_ref, w1a, w1b, w3a, w3b, w2a, w2b):
    i = pl.program_id(0)
    used = i < nu_ref[0]
    ea = ea_ref[i]
    eb = eb_ref[i]
    prev = jnp.maximum(i - 1, 0)

    @pl.when(used & ((i == 0) | (ea != ea_ref[prev])))
    def _():
        w1a[...] = w1a_ref[0, 0].astype(BF16)
        w3a[...] = w3a_ref[0, 0].astype(BF16)
        w2a[...] = w2a_ref[0, 0].astype(BF16)

    @pl.when(used & ((i == 0) | (eb != eb_ref[prev])))
    def _():
        w1b[...] = w1b_ref[0, 0].astype(BF16)
        w3b[...] = w3b_ref[0, 0].astype(BF16)
        w2b[...] = w2b_ref[0, 0].astype(BF16)

    @pl.when(used)
    def _():
        xb = _load_row_tiled(x_ref, BM).astype(BF16)

        logits = _dot(xb, wr_ref[...].astype(BF16))
        lane = lax.broadcasted_iota(I32, logits.shape, 1)
        logits = jnp.where(lane < N_EXPERTS, logits, -jnp.inf)
        ex = jnp.exp(logits - jnp.max(logits, axis=1, keepdims=True))
        probs = ex / jnp.sum(ex, axis=1, keepdims=True)
        pa = jnp.sum(jnp.where(lane == ea, probs, 0.0), axis=1, keepdims=True)
        pb = jnp.sum(jnp.where(lane == eb, probs, 0.0), axis=1, keepdims=True)
        den = pa + pb

        def ffn(w1, w3, w2):
            a = _dot(xb, w1[...])
            b = _dot(xb, w3[...])
            hid = a * jax.nn.sigmoid(a) * b
            return _dot(hid.astype(BF16), w2[...])

        y = (pa / den) * ffn(w1a, w3a, w2a) + (pb / den) * ffn(w1b, w3b, w2b)
        _store_row_tiled(y_ref, y)

    @pl.when(jnp.logical_not(used))
    def _():
        y_ref[...] = jnp.zeros(y_ref.shape, F32)


def _expert_ffn(layer, blk_ea, blk_eb, nused, xbuf, w1, w3, w2, wr_pad):
    n_blocks = xbuf.shape[0] // (BM * ROW_TILE)
    used_blk = lambda i, ea, eb, nu: (jnp.minimum(i, nu[0] - 1), 0)
    wa = lambda i, ea, eb, nu: (layer, ea[i], 0, 0)
    wb = lambda i, ea, eb, nu: (layer, eb[i], 0, 0)
    up = lambda im: pl.BlockSpec((1, 1, D_MODEL, D_EXPERT), im)
    down = lambda im: pl.BlockSpec((1, 1, D_EXPERT, D_MODEL), im)
    grid_spec = pltpu.PrefetchScalarGridSpec(
        num_scalar_prefetch=3,
        grid=(n_blocks,),
        in_specs=[
            pl.BlockSpec((BM * ROW_TILE, LANES), used_blk),
            up(wa), up(wb), up(wa), up(wb), down(wa), down(wb),
            pl.BlockSpec((D_MODEL, LANES), lambda i, ea, eb, nu: (0, 0)),
        ],
        out_specs=pl.BlockSpec((BM * ROW_TILE, LANES), lambda i, ea, eb, nu: (i, 0)),
        scratch_shapes=[
            pltpu.VMEM((D_MODEL, D_EXPERT), BF16),
            pltpu.VMEM((D_MODEL, D_EXPERT), BF16),
            pltpu.VMEM((D_MODEL, D_EXPERT), BF16),
            pltpu.VMEM((D_MODEL, D_EXPERT), BF16),
            pltpu.VMEM((D_EXPERT, D_MODEL), BF16),
            pltpu.VMEM((D_EXPERT, D_MODEL), BF16),
        ],
    )
    return pl.pallas_call(
        _expert_kernel,
        grid_spec=grid_spec,
        out_shape=jax.ShapeDtypeStruct(xbuf.shape, F32),
        compiler_params=pltpu.CompilerParams(
            dimension_semantics=("arbitrary",), vmem_limit_bytes=VMEM_LIMIT),
        name="expert_ffn",
    )(blk_ea, blk_eb, nused, xbuf, w1, w1, w3, w3, w2, w2, wr_pad)


def _dispatch_tables(cnt, nb):
    counts = cnt[:N_CLASSES, 0].astype(I32)
    padded = ((counts + BM - 1) // BM) * BM
    pend = jnp.cumsum(padded).astype(I32)
    pstart = pend - padded
    nused = pend[-1:] // BM
    step = jnp.minimum(jnp.arange(nb, dtype=I32), nused - 1)
    c = jnp.minimum(jnp.searchsorted(pend, step * BM, side="right"), N_CLASSES - 1).astype(I32)
    g = c // PAIRS_PER_GROUP
    p = c - g * PAIRS_PER_GROUP
    ge3 = (p >= 3).astype(I32)
    ge5 = (p >= 5).astype(I32)
    blk_ea = g * EXPERTS_PER_GROUP + ge3 + ge5
    blk_eb = g * EXPERTS_PER_GROUP + p + 1 - 2 * ge3 - ge5
    return pstart, blk_ea, blk_eb, nused


def _final_kernel(scur_ref, snext_ref, h1p_ref, y_hbm, g2_ref, b2_ref, o_ref, ybuf, gsem):
    i = pl.program_id(0)
    y = _prefetch_expert_rows(i, pl.num_programs(0), scur_ref, snext_ref, y_hbm, ybuf, gsem)
    o_ref[...] = _layer_norm(ALPHA * h1p_ref[...] + y, g2_ref[0], b2_ref[0])


def _final_call(slot_prev, h1_prev, y_prev, g2, b2):
    t = h1_prev.shape[0]
    nt = t // TS
    last = DEPTH - 1
    return pl.pallas_call(
        _final_kernel,
        grid=(nt,),
        in_specs=[
            pl.BlockSpec((1, 1, TS), lambda i: (i, 0, 0), memory_space=pltpu.SMEM),
            pl.BlockSpec((1, 1, TS), lambda i: (jnp.minimum(i + 1, nt - 1), 0, 0),
                         memory_space=pltpu.SMEM),
            pl.BlockSpec((TS, D_MODEL), lambda i: (i, 0)),
            pl.BlockSpec(memory_space=pl.ANY),
            pl.BlockSpec((1, 1, D_MODEL), lambda i: (last, 0, 0)),
            pl.BlockSpec((1, 1, D_MODEL), lambda i: (last, 0, 0)),
        ],
        out_specs=pl.BlockSpec((TS, D_MODEL), lambda i: (i, 0)),
        out_shape=jax.ShapeDtypeStruct((t, D_MODEL), F32),
        scratch_shapes=[pltpu.VMEM((2, TS * ROW_TILE, LANES), F32), pltpu.SemaphoreType.DMA((2,))],
        compiler_params=pltpu.CompilerParams(
            dimension_semantics=("arbitrary",), vmem_limit_bytes=VMEM_LIMIT),
        name="final_combine_ln",
    )(slot_prev, slot_prev, h1_prev, y_prev, g2, b2)


def kernel(x, ln_in_g, ln_in_b, w_in, pool_w, pool_scale, conv_w, conv_b, w_out, ln1_g, ln1_b,
           w_router, router_bias, exp_w1, exp_w3, exp_w2, ln2_g, ln2_b):
    bsz, seq, d = x.shape
    t = bsz * seq
    nt = t // TS
    n_rows = t + N_CLASSES * BM
    nb = n_rows // BM

    per_layer_row = lambda v: v.reshape(DEPTH, 1, -1)
    tri = (lax.broadcasted_iota(I32, (TS, TS), 0)
           < lax.broadcasted_iota(I32, (TS, TS), 1)).astype(BF16)
    wr_t = w_router.T
    wr_pad = jnp.pad(w_router, ((0, 0), (0, LANES - N_EXPERTS)))
    rbias = router_bias.reshape(N_EXPERTS, 1).astype(F32)
    pool_scale3, conv_b3 = per_layer_row(pool_scale), per_layer_row(conv_b)
    ln1_g3, ln1_b3 = per_layer_row(ln1_g), per_layer_row(ln1_b)
    ln2_g3, ln2_b3 = per_layer_row(ln2_g), per_layer_row(ln2_b)

    prev = (x.reshape(t, d), ln_in_g.reshape(1, -1), ln_in_b.reshape(1, -1))
    xbuf = jnp.zeros((n_rows * ROW_TILE, LANES), F32)
    for l in range(DEPTH):
        h1, h1t, route, cnt = _layer_call(l, bsz, seq, prev, w_in, pool_w, pool_scale3, conv_w, conv_b3,
                                          w_out, ln1_g3, ln1_b3, wr_t, rbias, tri)
        pstart, blk_ea, blk_eb, nused = _dispatch_tables(cnt, nb)
        slot3 = _slots(pstart, route).reshape(nt, 1, TS)
        xbuf = _scatter_rows(slot3, h1t, xbuf)
        ybuf = _expert_ffn(l, blk_ea, blk_eb, nused, xbuf, exp_w1, exp_w3, exp_w2, wr_pad)
        prev = (slot3, h1, ybuf, ln2_g3, ln2_b3)
    out = _final_call(*prev)
    return out.reshape(bsz, seq, d)
```

```python
import functools

import jax
import jax.numpy as jnp
from jax import lax
from jax.experimental import pallas as pl
from jax.experimental.pallas import tpu as pltpu

D_MODEL = 1024
DEPTH = 4
D_POOL = 512
POOL_WINDOWS = (2, 4, 8, 16)
POOL_GROUP = 128
D_CONV = 512
CONV_WIDTH = 3
D_PROJ = D_POOL + 3 * D_CONV
N_EXPERTS = 16
N_EXPERT_GROUPS = 4
EXPERTS_PER_GROUP = 4
PAIRS_PER_GROUP = 6
N_CLASSES = N_EXPERT_GROUPS * PAIRS_PER_GROUP
CLASS_ROWS = 32
D_EXPERT = 512
ALPHA = float((2 * DEPTH) ** 0.25)
LN_EPS = 1e-5

F32 = jnp.float32
BF16 = jnp.bfloat16
I32 = jnp.int32

LANES = 128
ROW_TILE = D_MODEL // LANES
TS = 512
V_HALO = 16
Z_HALO = 8
BM = 256
W_STAGE_ROWS = 256
DMA_QUEUES = 2
VMEM_LIMIT = 56 * 1024 * 1024


def _layer_norm(x, g, b):
    mu = jnp.mean(x, axis=-1, keepdims=True)
    xc = x - mu
    var = jnp.mean(xc * xc, axis=-1, keepdims=True)
    return xc * lax.rsqrt(var + LN_EPS) * g + b


def _dot(a, b):
    return jnp.dot(a, b, preferred_element_type=F32)


def _store_row_tiled(ref, value):
    n = value.shape[0]
    for j in range(ROW_TILE):
        ref[pl.ds(j, n, stride=ROW_TILE), :] = value[:, j * LANES:(j + 1) * LANES]


def _load_row_tiled(ref, n):
    return jnp.concatenate([ref[pl.ds(j, n, stride=ROW_TILE), :] for j in range(ROW_TILE)], axis=1)


def _row(ref, first_sublane):
    return ref.at[pl.ds(first_sublane, ROW_TILE)]


def _route(sel):
    rows = [sel[e:e + 1, :] for e in range(N_EXPERTS)]
    gscore = []
    for g in range(N_EXPERT_GROUPS):
        a, b, c, d = rows[4 * g:4 * g + 4]
        hi1, lo1 = jnp.maximum(a, b), jnp.minimum(a, b)
        hi2, lo2 = jnp.maximum(c, d), jnp.minimum(c, d)
        m1 = jnp.maximum(hi1, hi2)
        m2 = jnp.maximum(jnp.minimum(hi1, hi2), jnp.maximum(lo1, lo2))
        gscore.append(m1 + m2)
    best = gscore[0]
    gidx = jnp.zeros(best.shape, I32)
    for g in range(1, N_EXPERT_GROUPS):
        upd = gscore[g] > best
        best = jnp.where(upd, gscore[g], best)
        gidx = jnp.where(upd, g, gidx)
    neg = jnp.full(best.shape, -jnp.inf, F32)
    masked = [jnp.where(gidx == (e // EXPERTS_PER_GROUP), rows[e], neg) for e in range(N_EXPERTS)]

    def top1(vals):
        bv = neg
        bi = jnp.zeros(best.shape, I32)
        for e in range(N_EXPERTS):
            upd = vals[e] > bv
            bv = jnp.where(upd, vals[e], bv)
            bi = jnp.where(upd, e, bi)
        return bi

    i1 = top1(masked)
    i2 = top1([jnp.where(i1 == e, neg, masked[e]) for e in range(N_EXPERTS)])
    return gidx, i1, i2


def _pair_class(gidx, i1, i2):
    a = jnp.minimum(i1, i2) - gidx * EXPERTS_PER_GROUP
    b = jnp.maximum(i1, i2) - gidx * EXPERTS_PER_GROUP
    base = jnp.where(a == 0, 0, jnp.where(a == 1, 2, 3))
    return gidx * PAIRS_PER_GROUP + base + b - 1


def _prefetch_expert_rows(i, nt, scur_ref, snext_ref, y_hbm, ybuf, gsem):
    slot = i % 2

    def gather(slot_ref, to_slot, t, t_sublane):
        src = _row(y_hbm, pl.multiple_of(slot_ref[0, 0, t], ROW_TILE))
        return pltpu.make_async_copy(src, _row(ybuf.at[to_slot], t_sublane), gsem.at[to_slot])

    @pl.when(i == 0)
    def _():
        def body(t, carry):
            gather(scur_ref, 0, t, pl.multiple_of(t * ROW_TILE, ROW_TILE)).start()
            return carry
        lax.fori_loop(0, TS, body, 0, unroll=8)

    @pl.when(i + 1 < nt)
    def _():
        for t in range(TS):
            gather(snext_ref, 1 - slot, t, t * ROW_TILE).start(priority=t % DMA_QUEUES)

    pltpu.make_async_copy(y_hbm.at[pl.ds(0, TS * ROW_TILE)], ybuf.at[slot], gsem.at[slot]).wait()
    return _load_row_tiled(ybuf.at[slot], TS)


def _layer_kernel(*refs, layer, tiles_per_seq):
    if layer == 0:
        x_ref, gin_ref, bin_ref = refs[:3]
        rest = refs[3:]
    else:
        scur_ref, snext_ref, h1p_ref, y_hbm, g2_ref, b2_ref = refs[:6]
        rest = refs[6:]
    (win_hbm, pw_ref, ps_ref, cw_ref, cb_ref, wout_hbm, g_ref, b_ref, wr_ref, rb_ref, tri_ref,
     h1t_ref, route_ref, cnt_ref,
     vbuf, zbuf, mixbuf, carry, win_ref, wout_ref, stage, wsem, ybuf, gsem) = rest

    i = pl.program_id(0)
    nt = pl.num_programs(0)

    @pl.when(i == 0)
    def _():
        for c in range(D_MODEL // W_STAGE_ROWS):
            rows = pl.ds(c * W_STAGE_ROWS, W_STAGE_ROWS)
            cp = pltpu.make_async_copy(win_hbm.at[layer, rows], stage, wsem)
            cp.start()
            cp.wait()
            win_ref[rows, :] = stage[...].astype(BF16)
        for c in range(D_MODEL // W_STAGE_ROWS):
            rows = pl.ds(c * W_STAGE_ROWS, W_STAGE_ROWS)
            cp = pltpu.make_async_copy(wout_hbm.at[layer, rows], stage.at[:, pl.ds(0, D_MODEL)], wsem)
            cp.start()
            cp.wait()
            wout_ref[rows, :] = stage[:, 0:D_MODEL].astype(BF16)
        carry[...] = jnp.zeros(carry.shape, F32)

    if layer == 0:
        h = _layer_norm(x_ref[...], gin_ref[...], bin_ref[...])
    else:
        y = _prefetch_expert_rows(i, nt, scur_ref, snext_ref, y_hbm, ybuf, gsem)
        h = _layer_norm(ALPHA * _load_row_tiled(h1p_ref, TS) + y, g2_ref[0], b2_ref[0])

    si = i % tiles_per_seq

    @pl.when(si == 0)
    def _():
        vbuf[0:V_HALO, :] = jnp.zeros((V_HALO, D_POOL), F32)
        zbuf[0:Z_HALO, :] = jnp.zeros((Z_HALO, D_CONV), F32)

    hb = h.astype(BF16)
    vbuf[V_HALO:V_HALO + TS, :] = _dot(hb, win_ref[:, 0:D_POOL])
    gate_b = _dot(hb, win_ref[:, D_POOL:D_POOL + D_CONV])
    gate_c = _dot(hb, win_ref[:, D_POOL + D_CONV:D_POOL + 2 * D_CONV])
    u = _dot(hb, win_ref[:, D_POOL + 2 * D_CONV:D_PROJ])
    z = gate_c * u
    zbuf[Z_HALO:Z_HALO + TS, :] = z

    tpos = si * TS + lax.broadcasted_iota(I32, (TS, 1), 0)
    for g, win in enumerate(POOL_WINDOWS):
        cols = pl.ds(g * POOL_GROUP, POOL_GROUP)
        cur = vbuf[pl.ds(V_HALO, TS), cols]
        acc = cur
        for j in range(1, win):
            acc = acc + vbuf[pl.ds(V_HALO - j, TS), cols]
        denom = jnp.minimum(tpos + 1, win).astype(F32)
        pooled = acc / denom - cur
        mixed = _dot(pooled.astype(BF16), pw_ref[0, g].astype(BF16)) * ps_ref[0, :, cols]
        mixbuf[:, cols] = mixed.astype(BF16)

    yc = (cw_ref[0, 2:3, :] * z
          + cw_ref[0, 1:2, :] * zbuf[pl.ds(Z_HALO - 1, TS), :]
          + cw_ref[0, 0:1, :] * zbuf[pl.ds(Z_HALO - 2, TS), :])
    mixbuf[:, D_POOL:D_POOL + D_CONV] = (gate_b * (yc + cb_ref[0])).astype(BF16)

    vbuf[0:V_HALO, :] = vbuf[TS:TS + V_HALO, :]
    zbuf[0:Z_HALO, :] = zbuf[TS:TS + Z_HALO, :]

    mix = _dot(mixbuf[...], wout_ref[...])
    h1 = _layer_norm(ALPHA * h + mix, g_ref[0], b_ref[0])
    _store_row_tiled(h1t_ref, h1)

    logits = lax.dot_general(wr_ref[...].astype(BF16), h1.astype(BF16), (((1,), (1,)), ((), ())),
                             preferred_element_type=F32)
    mx = jnp.max(logits, axis=0, keepdims=True)
    ex = jnp.exp(logits - mx)
    probs = ex / jnp.sum(ex, axis=0, keepdims=True)
    gidx, i1, i2 = _route(probs + rb_ref[...])
    cls = _pair_class(gidx, i1, i2)

    ciota = lax.broadcasted_iota(I32, (CLASS_ROWS, TS), 0)
    onehot = jnp.where(ciota == cls, 1.0, 0.0).astype(F32)
    before = _dot(onehot.astype(BF16), tri_ref[...])
    total = jnp.sum(onehot, axis=1, keepdims=True)
    c = carry[:, 0:1]
    rank = jnp.sum(onehot * (c + before), axis=0, keepdims=True)
    route_ref[0:1, :] = cls
    route_ref[1:2, :] = rank.astype(I32)
    newc = jnp.broadcast_to(c + total, carry.shape)
    carry[...] = newc
    cnt_ref[...] = newc


def _layer_call(layer, bsz, seq, prev, w_in, pool_w, pool_scale, conv_w, conv_b, w_out, ln_g, ln_b,
                wr_t, rbias, tri):
    t = bsz * seq
    nt = t // TS
    const2 = lambda i: (0, 0)
    lsel3 = lambda i: (layer, 0, 0)
    lsel4 = lambda i: (layer, 0, 0, 0)
    tile = lambda i: (i, 0)
    lane_tile = lambda i: (0, i)
    if layer == 0:
        x, gin, bin_ = prev
        head_args = (x, gin, bin_)
        head_specs = [
            pl.BlockSpec((TS, D_MODEL), tile),
            pl.BlockSpec((1, D_MODEL), const2),
            pl.BlockSpec((1, D_MODEL), const2),
        ]
    else:
        slot_prev, h1t_prev, y_prev, g2, b2 = prev
        head_args = (slot_prev, slot_prev, h1t_prev, y_prev, g2, b2)
        head_specs = [
            pl.BlockSpec((1, 1, TS), lambda i: (i, 0, 0), memory_space=pltpu.SMEM),
            pl.BlockSpec((1, 1, TS), lambda i: (jnp.minimum(i + 1, nt - 1), 0, 0),
                         memory_space=pltpu.SMEM),
            pl.BlockSpec((TS * ROW_TILE, LANES), tile),
            pl.BlockSpec(memory_space=pl.ANY),
            pl.BlockSpec((1, 1, D_MODEL), lambda i: (layer - 1, 0, 0)),
            pl.BlockSpec((1, 1, D_MODEL), lambda i: (layer - 1, 0, 0)),
        ]
    return pl.pallas_call(
        functools.partial(_layer_kernel, layer=layer, tiles_per_seq=seq // TS),
        grid=(nt,),
        in_specs=head_specs + [
            pl.BlockSpec(memory_space=pl.ANY),
            pl.BlockSpec((1, len(POOL_WINDOWS), POOL_GROUP, POOL_GROUP), lsel4),
            pl.BlockSpec((1, 1, D_POOL), lsel3),
            pl.BlockSpec((1, CONV_WIDTH, D_CONV), lsel3),
            pl.BlockSpec((1, 1, D_CONV), lsel3),
            pl.BlockSpec(memory_space=pl.ANY),
            pl.BlockSpec((1, 1, D_MODEL), lsel3),
            pl.BlockSpec((1, 1, D_MODEL), lsel3),
            pl.BlockSpec((N_EXPERTS, D_MODEL), const2),
            pl.BlockSpec((N_EXPERTS, 1), const2),
            pl.BlockSpec((TS, TS), const2),
        ],
        out_specs=[
            pl.BlockSpec((TS * ROW_TILE, LANES), tile),
            pl.BlockSpec((2, TS), lane_tile),
            pl.BlockSpec((CLASS_ROWS, LANES), const2),
        ],
        out_shape=[
            jax.ShapeDtypeStruct((t * ROW_TILE, LANES), F32),
            jax.ShapeDtypeStruct((2, t), I32),
            jax.ShapeDtypeStruct((CLASS_ROWS, LANES), F32),
        ],
        scratch_shapes=[
            pltpu.VMEM((TS + V_HALO, D_POOL), F32),
            pltpu.VMEM((TS + Z_HALO, D_CONV), F32),
            pltpu.VMEM((TS, D_MODEL), BF16),
            pltpu.VMEM((CLASS_ROWS, LANES), F32),
            pltpu.VMEM((D_MODEL, D_PROJ), BF16),
            pltpu.VMEM((D_MODEL, D_MODEL), BF16),
            pltpu.VMEM((W_STAGE_ROWS, D_PROJ), F32),
            pltpu.SemaphoreType.DMA(()),
            pltpu.VMEM((2, TS * ROW_TILE, LANES), F32),
            pltpu.SemaphoreType.DMA((2,)),
        ],
        compiler_params=pltpu.CompilerParams(
            dimension_semantics=("arbitrary",), vmem_limit_bytes=VMEM_LIMIT),
        name="layer_mix_route",
    )(*head_args, w_in, pool_w, pool_scale, conv_w, conv_b, w_out, ln_g, ln_b, wr_t, rbias, tri)


def _slot_kernel(pstart_ref, route_ref, slot_ref):
    cls = route_ref[0:1, :]
    start = jnp.zeros(cls.shape, I32)
    for k in range(N_CLASSES):
        start = jnp.where(cls == k, pstart_ref[k], start)
    slot_ref[...] = (start + route_ref[1:2, :]) * ROW_TILE


def _slots(pstart, route):
    t = route.shape[1]
    grid_spec = pltpu.PrefetchScalarGridSpec(
        num_scalar_prefetch=1,
        grid=(1,),
        in_specs=[pl.BlockSpec((2, t), lambda i, ps: (0, 0))],
        out_specs=pl.BlockSpec((1, t), lambda i, ps: (0, 0)),
    )
    return pl.pallas_call(
        _slot_kernel,
        grid_spec=grid_spec,
        out_shape=jax.ShapeDtypeStruct((1, t), I32),
        name="token_slots",
    )(pstart, route)


def _scatter_kernel(slot_ref, h1t_ref, xin_ref, xout_ref, sem):
    del xin_ref
    for t in range(TS):
        dst = _row(xout_ref, pl.multiple_of(slot_ref[0, 0, t], ROW_TILE))
        pltpu.make_async_copy(_row(h1t_ref, t * ROW_TILE), dst, sem).start(priority=t % DMA_QUEUES)
    pltpu.make_async_copy(h1t_ref, xout_ref.at[pl.ds(0, TS * ROW_TILE)], sem).wait()


def _scatter_rows(slot3, h1t, xbuf):
    nt = slot3.shape[0]
    return pl.pallas_call(
        _scatter_kernel,
        grid=(nt,),
        in_specs=[
            pl.BlockSpec((1, 1, TS), lambda i: (i, 0, 0), memory_space=pltpu.SMEM),
            pl.BlockSpec((TS * ROW_TILE, LANES), lambda i: (i, 0)),
            pl.BlockSpec(memory_space=pl.ANY),
        ],
        out_specs=pl.BlockSpec(memory_space=pl.ANY),
        out_shape=jax.ShapeDtypeStruct(xbuf.shape, xbuf.dtype),
        scratch_shapes=[pltpu.SemaphoreType.DMA(())],
        input_output_aliases={2: 0},
        compiler_params=pltpu.CompilerParams(dimension_semantics=("arbitrary",)),
        name="scatter_rows",
    )(slot3, h1t, xbuf)


def _expert_kernel(ea_ref, eb_ref, nu_ref, x_ref, w1a_ref, w1b_ref, w3a_ref, w3b_ref,
                   w2a_ref, w2b_ref, wr_ref, y_ref, w1a, w1b, w3a, w3b, w2a, w2b):
    i = pl.program_id(0)
    used = i < nu_ref[0]
    ea = ea_ref[i]
    eb = eb_ref[i]
    prev = jnp.maximum(i - 1, 0)

    @pl.when(used & ((i == 0) | (ea != ea_ref[prev])))
    def _():
        w1a[...] = w1a_ref[0, 0].astype(BF16)
        w3a[...] = w3a_ref[0, 0].astype(BF16)
        w2a[...] = w2a_ref[0, 0].astype(BF16)

    @pl.when(used & ((i == 0) | (eb != eb_ref[prev])))
    def _():
        w1b[...] = w1b_ref[0, 0].astype(BF16)
        w3b[...] = w3b_ref[0, 0].astype(BF16)
        w2b[...] = w2b_ref[0, 0].astype(BF16)

    @pl.when(used)
    def _():
        xb = _load_row_tiled(x_ref, BM).astype(BF16)

        logits = _dot(xb, wr_ref[...].astype(BF16))
        lane = lax.broadcasted_iota(I32, logits.shape, 1)
        logits = jnp.where(lane < N_EXPERTS, logits, -jnp.inf)
        ex = jnp.exp(logits - jnp.max(logits, axis=1, keepdims=True))
        probs = ex / jnp.sum(ex, axis=1, keepdims=True)
        pa = jnp.sum(jnp.where(lane == ea, probs, 0.0), axis=1, keepdims=True)
        pb = jnp.sum(jnp.where(lane == eb, probs, 0.0), axis=1, keepdims=True)
        den = pa + pb

        def ffn(w1, w3, w2):
            a = _dot(xb, w1[...])
            b = _dot(xb, w3[...])
            hid = a * jax.nn.sigmoid(a) * b
            return _dot(hid.astype(BF16), w2[...])

        y = (pa / den) * ffn(w1a, w3a, w2a) + (pb / den) * ffn(w1b, w3b, w2b)
        _store_row_tiled(y_ref, y)

    @pl.when(jnp.logical_not(used))
    def _():
        y_ref[...] = jnp.zeros(y_ref.shape, F32)


def _expert_ffn(layer, blk_ea, blk_eb, nused, xbuf, w1, w3, w2, wr_pad):
    n_blocks = xbuf.shape[0] // (BM * ROW_TILE)
    used_blk = lambda i, ea, eb, nu: (jnp.minimum(i, nu[0] - 1), 0)
    wa = lambda i, ea, eb, nu: (layer, ea[i], 0, 0)
    wb = lambda i, ea, eb, nu: (layer, eb[i], 0, 0)
    up = lambda im: pl.BlockSpec((1, 1, D_MODEL, D_EXPERT), im)
    down = lambda im: pl.BlockSpec((1, 1, D_EXPERT, D_MODEL), im)
    grid_spec = pltpu.PrefetchScalarGridSpec(
        num_scalar_prefetch=3,
        grid=(n_blocks,),
        in_specs=[
            pl.BlockSpec((BM * ROW_TILE, LANES), used_blk),
            up(wa), up(wb), up(wa), up(wb), down(wa), down(wb),
            pl.BlockSpec((D_MODEL, LANES), lambda i, ea, eb, nu: (0, 0)),
        ],
        out_specs=pl.BlockSpec((BM * ROW_TILE, LANES), lambda i, ea, eb, nu: (i, 0)),
        scratch_shapes=[
            pltpu.VMEM((D_MODEL, D_EXPERT), BF16),
            pltpu.VMEM((D_MODEL, D_EXPERT), BF16),
            pltpu.VMEM((D_MODEL, D_EXPERT), BF16),
            pltpu.VMEM((D_MODEL, D_EXPERT), BF16),
            pltpu.VMEM((D_EXPERT, D_MODEL), BF16),
            pltpu.VMEM((D_EXPERT, D_MODEL), BF16),
        ],
    )
    return pl.pallas_call(
        _expert_kernel,
        grid_spec=grid_spec,
        out_shape=jax.ShapeDtypeStruct(xbuf.shape, F32),
        compiler_params=pltpu.CompilerParams(
            dimension_semantics=("arbitrary",), vmem_limit_bytes=VMEM_LIMIT),
        name="expert_ffn",
    )(blk_ea, blk_eb, nused, xbuf, w1, w1, w3, w3, w2, w2, wr_pad)


def _dispatch_tables(cnt, nb):
    counts = cnt[:N_CLASSES, 0].astype(I32)
    padded = ((counts + BM - 1) // BM) * BM
    pend = jnp.cumsum(padded).astype(I32)
    pstart = pend - padded
    nused = pend[-1:] // BM
    step = jnp.minimum(jnp.arange(nb, dtype=I32), nused - 1)
    c = jnp.minimum(jnp.searchsorted(pend, step * BM, side="right"), N_CLASSES - 1).astype(I32)
    g = c // PAIRS_PER_GROUP
    p = c - g * PAIRS_PER_GROUP
    ge3 = (p >= 3).astype(I32)
    ge5 = (p >= 5).astype(I32)
    blk_ea = g * EXPERTS_PER_GROUP + ge3 + ge5
    blk_eb = g * EXPERTS_PER_GROUP + p + 1 - 2 * ge3 - ge5
    return pstart, blk_ea, blk_eb, nused


def _final_kernel(scur_ref, snext_ref, h1p_ref, y_hbm, g2_ref, b2_ref, o_ref, ybuf, gsem):
    i = pl.program_id(0)
    y = _prefetch_expert_rows(i, pl.num_programs(0), scur_ref, snext_ref, y_hbm, ybuf, gsem)
    o_ref[...] = _layer_norm(ALPHA * _load_row_tiled(h1p_ref, TS) + y, g2_ref[0], b2_ref[0])


def _final_call(slot_prev, h1t_prev, y_prev, g2, b2):
    t = h1t_prev.shape[0] // ROW_TILE
    nt = t // TS
    last = DEPTH - 1
    return pl.pallas_call(
        _final_kernel,
        grid=(nt,),
        in_specs=[
            pl.BlockSpec((1, 1, TS), lambda i: (i, 0, 0), memory_space=pltpu.SMEM),
            pl.BlockSpec((1, 1, TS), lambda i: (jnp.minimum(i + 1, nt - 1), 0, 0),
                         memory_space=pltpu.SMEM),
            pl.BlockSpec((TS * ROW_TILE, LANES), lambda i: (i, 0)),
            pl.BlockSpec(memory_space=pl.ANY),
            pl.BlockSpec((1, 1, D_MODEL), lambda i: (last, 0, 0)),
            pl.BlockSpec((1, 1, D_MODEL), lambda i: (last, 0, 0)),
        ],
        out_specs=pl.BlockSpec((TS, D_MODEL), lambda i: (i, 0)),
        out_shape=jax.ShapeDtypeStruct((t, D_MODEL), F32),
        scratch_shapes=[pltpu.VMEM((2, TS * ROW_TILE, LANES), F32), pltpu.SemaphoreType.DMA((2,))],
        compiler_params=pltpu.CompilerParams(
            dimension_semantics=("arbitrary",), vmem_limit_bytes=VMEM_LIMIT),
        name="final_combine_ln",
    )(slot_prev, slot_prev, h1t_prev, y_prev, g2, b2)


def kernel(x, ln_in_g, ln_in_b, w_in, pool_w, pool_scale, conv_w, conv_b, w_out, ln1_g, ln1_b,
           w_router, router_bias, exp_w1, exp_w3, exp_w2, ln2_g, ln2_b):
    bsz, seq, d = x.shape
    t = bsz * seq
    nt = t // TS
    n_rows = t + N_CLASSES * BM
    nb = n_rows // BM

    per_layer_row = lambda v: v.reshape(DEPTH, 1, -1)
    tri = (lax.broadcasted_iota(I32, (TS, TS), 0)
           < lax.broadcasted_iota(I32, (TS, TS), 1)).astype(BF16)
    wr_t = w_router.T
    wr_pad = jnp.pad(w_router, ((0, 0), (0, LANES - N_EXPERTS)))
    rbias = router_bias.reshape(N_EXPERTS, 1).astype(F32)
    pool_scale3, conv_b3 = per_layer_row(pool_scale), per_layer_row(conv_b)
    ln1_g3, ln1_b3 = per_layer_row(ln1_g), per_layer_row(ln1_b)
    ln2_g3, ln2_b3 = per_layer_row(ln2_g), per_layer_row(ln2_b)

    prev = (x.reshape(t, d), ln_in_g.reshape(1, -1), ln_in_b.reshape(1, -1))
    xbuf = jnp.zeros((n_rows * ROW_TILE, LANES), F32)
    for l in range(DEPTH):
        h1t, route, cnt = _layer_call(l, bsz, seq, prev, w_in, pool_w, pool_scale3, conv_w, conv_b3,
                                          w_out, ln1_g3, ln1_b3, wr_t, rbias, tri)
        pstart, blk_ea, blk_eb, nused = _dispatch_tables(cnt, nb)
        slot3 = _slots(pstart, route).reshape(nt, 1, TS)
        xbuf = _scatter_rows(slot3, h1t, xbuf)
        ybuf = _expert_ffn(l, blk_ea, blk_eb, nused, xbuf, exp_w1, exp_w3, exp_w2, wr_pad)
        prev = (slot3, h1t, ybuf, ln2_g3, ln2_b3)
    out = _final_call(*prev)
    return out.reshape(bsz, seq, d)
```

```python
import functools

import jax
import jax.numpy as jnp
from jax import lax
from jax.experimental import pallas as pl
from jax.experimental.pallas import tpu as pltpu

D_MODEL = 1024
DEPTH = 4
D_POOL = 512
POOL_WINDOWS = (2, 4, 8, 16)
POOL_GROUP = 128
D_CONV = 512
CONV_WIDTH = 3
D_PROJ = D_POOL + 3 * D_CONV
N_EXPERTS = 16
N_EXPERT_GROUPS = 4
EXPERTS_PER_GROUP = 4
PAIRS_PER_GROUP = 6
N_CLASSES = N_EXPERT_GROUPS * PAIRS_PER_GROUP
CLASS_ROWS = 32
D_EXPERT = 512
ALPHA = float((2 * DEPTH) ** 0.25)
LN_EPS = 1e-5

F32 = jnp.float32
BF16 = jnp.bfloat16
I32 = jnp.int32

LANES = 128
ROW_TILE = D_MODEL // LANES
TS = 512
V_HALO = 16
Z_HALO = 8
BM = 256
W_STAGE_ROWS = 256
DMA_QUEUES = 2
VMEM_LIMIT = 56 * 1024 * 1024


def _layer_norm(x, g, b):
    mu = jnp.mean(x, axis=-1, keepdims=True)
    xc = x - mu
    var = jnp.mean(xc * xc, axis=-1, keepdims=True)
    return xc * lax.rsqrt(var + LN_EPS) * g + b


def _dot(a, b):
    return jnp.dot(a, b, preferred_element_type=F32)


def _store_row_tiled(ref, value):
    n = value.shape[0]
    for j in range(ROW_TILE):
        ref[pl.ds(j, n, stride=ROW_TILE), :] = value[:, j * LANES:(j + 1) * LANES]


def _load_row_tiled(ref, n):
    return jnp.concatenate([ref[pl.ds(j, n, stride=ROW_TILE), :] for j in range(ROW_TILE)], axis=1)


def _row(ref, first_sublane):
    return ref.at[pl.ds(first_sublane, ROW_TILE)]


def _route(sel):
    rows = [sel[e:e + 1, :] for e in range(N_EXPERTS)]
    gscore = []
    for g in range(N_EXPERT_GROUPS):
        a, b, c, d = rows[4 * g:4 * g + 4]
        hi1, lo1 = jnp.maximum(a, b), jnp.minimum(a, b)
        hi2, lo2 = jnp.maximum(c, d), jnp.minimum(c, d)
        m1 = jnp.maximum(hi1, hi2)
        m2 = jnp.maximum(jnp.minimum(hi1, hi2), jnp.maximum(lo1, lo2))
        gscore.append(m1 + m2)
    best = gscore[0]
    gidx = jnp.zeros(best.shape, I32)
    for g in range(1, N_EXPERT_GROUPS):
        upd = gscore[g] > best
        best = jnp.where(upd, gscore[g], best)
        gidx = jnp.where(upd, g, gidx)
    neg = jnp.full(best.shape, -jnp.inf, F32)
    masked = [jnp.where(gidx == (e // EXPERTS_PER_GROUP), rows[e], neg) for e in range(N_EXPERTS)]

    def top1(vals):
        bv = neg
        bi = jnp.zeros(best.shape, I32)
        for e in range(N_EXPERTS):
            upd = vals[e] > bv
            bv = jnp.where(upd, vals[e], bv)
            bi = jnp.where(upd, e, bi)
        return bi

    i1 = top1(masked)
    i2 = top1([jnp.where(i1 == e, neg, masked[e]) for e in range(N_EXPERTS)])
    return gidx, i1, i2


def _pair_class(gidx, i1, i2):
    a = jnp.minimum(i1, i2) - gidx * EXPERTS_PER_GROUP
    b = jnp.maximum(i1, i2) - gidx * EXPERTS_PER_GROUP
    base = jnp.where(a == 0, 0, jnp.where(a == 1, 2, 3))
    return gidx * PAIRS_PER_GROUP + base + b - 1


def _prefetch_expert_rows(i, nt, scur_ref, snext_ref, y_hbm, ybuf, gsem):
    slot = i % 2

    def gather(slot_ref, to_slot, t, t_sublane):
        src = _row(y_hbm, pl.multiple_of(slot_ref[0, 0, t], ROW_TILE))
        return pltpu.make_async_copy(src, _row(ybuf.at[to_slot], t_sublane), gsem.at[to_slot])

    @pl.when(i == 0)
    def _():
        def body(t, carry):
            gather(scur_ref, 0, t, pl.multiple_of(t * ROW_TILE, ROW_TILE)).start()
            return carry
        lax.fori_loop(0, TS, body, 0, unroll=8)

    @pl.when(i + 1 < nt)
    def _():
        for t in range(TS):
            gather(snext_ref, 1 - slot, t, t * ROW_TILE).start(priority=t % DMA_QUEUES)

    pltpu.make_async_copy(y_hbm.at[pl.ds(0, TS * ROW_TILE)], ybuf.at[slot], gsem.at[slot]).wait()
    return _load_row_tiled(ybuf.at[slot], TS)


def _layer_kernel(*refs, layer, tiles_per_seq):
    if layer == 0:
        x_ref, gin_ref, bin_ref = refs[:3]
        rest = refs[3:]
    else:
        scur_ref, snext_ref, h1p_ref, y_hbm, g2_ref, b2_ref = refs[:6]
        rest = refs[6:]
    (win_hbm, pw_ref, ps_ref, cw_ref, cb_ref, wout_hbm, g_ref, b_ref, wr_ref, rb_ref, tri_ref,
     h1t_ref, route_ref, cnt_ref,
     vbuf, zbuf, mixbuf, carry, win_ref, wout_ref, stage, wsem, ybuf, gsem) = rest

    i = pl.program_id(0)
    nt = pl.num_programs(0)

    @pl.when(i == 0)
    def _():
        for c in range(D_MODEL // W_STAGE_ROWS):
            rows = pl.ds(c * W_STAGE_ROWS, W_STAGE_ROWS)
            cp = pltpu.make_async_copy(win_hbm.at[layer, rows], stage, wsem)
            cp.start()
            cp.wait()
            win_ref[rows, :] = stage[...].astype(BF16)
        for c in range(D_MODEL // W_STAGE_ROWS):
            rows = pl.ds(c * W_STAGE_ROWS, W_STAGE_ROWS)
            cp = pltpu.make_async_copy(wout_hbm.at[layer, rows], stage.at[:, pl.ds(0, D_MODEL)], wsem)
            cp.start()
            cp.wait()
            wout_ref[rows, :] = stage[:, 0:D_MODEL].astype(BF16)
        carry[...] = jnp.zeros(carry.shape, F32)

    if layer == 0:
        h = _layer_norm(x_ref[...], gin_ref[...], bin_ref[...])
    else:
        y = _prefetch_expert_rows(i, nt, scur_ref, snext_ref, y_hbm, ybuf, gsem)
        h = _layer_norm(ALPHA * _load_row_tiled(h1p_ref, TS) + y, g2_ref[0], b2_ref[0])

    si = i % tiles_per_seq

    @pl.when(si == 0)
    def _():
        vbuf[0:V_HALO, :] = jnp.zeros((V_HALO, D_POOL), F32)
        zbuf[0:Z_HALO, :] = jnp.zeros((Z_HALO, D_CONV), F32)

    hb = h.astype(BF16)
    vbuf[V_HALO:V_HALO + TS, :] = _dot(hb, win_ref[:, 0:D_POOL])
    gate_b = _dot(hb, win_ref[:, D_POOL:D_POOL + D_CONV])
    gate_c = _dot(hb, win_ref[:, D_POOL + D_CONV:D_POOL + 2 * D_CONV])
    u = _dot(hb, win_ref[:, D_POOL + 2 * D_CONV:D_PROJ])
    z = gate_c * u
    zbuf[Z_HALO:Z_HALO + TS, :] = z

    tpos = si * TS + lax.broadcasted_iota(I32, (TS, 1), 0)
    for g, win in enumerate(POOL_WINDOWS):
        cols = pl.ds(g * POOL_GROUP, POOL_GROUP)
        cur = vbuf[pl.ds(V_HALO, TS), cols]
        acc = cur
        for j in range(1, win):
            acc = acc + vbuf[pl.ds(V_HALO - j, TS), cols]
        denom = jnp.minimum(tpos + 1, win).astype(F32)
        pooled = acc / denom - cur
        mixed = _dot(pooled.astype(BF16), pw_ref[0, g].astype(BF16)) * ps_ref[0, :, cols]
        mixbuf[:, cols] = mixed.astype(BF16)

    yc = (cw_ref[0, 2:3, :] * z
          + cw_ref[0, 1:2, :] * zbuf[pl.ds(Z_HALO - 1, TS), :]
          + cw_ref[0, 0:1, :] * zbuf[pl.ds(Z_HALO - 2, TS), :])
    mixbuf[:, D_POOL:D_POOL + D_CONV] = (gate_b * (yc + cb_ref[0])).astype(BF16)

    vbuf[0:V_HALO, :] = vbuf[TS:TS + V_HALO, :]
    zbuf[0:Z_HALO, :] = zbuf[TS:TS + Z_HALO, :]

    mix = _dot(mixbuf[...], wout_ref[...])
    h1 = _layer_norm(ALPHA * h + mix, g_ref[0], b_ref[0])
    _store_row_tiled(h1t_ref, h1)

    logits = lax.dot_general(wr_ref[...].astype(BF16), h1.astype(BF16), (((1,), (1,)), ((), ())),
                             preferred_element_type=F32)
    mx = jnp.max(logits, axis=0, keepdims=True)
    ex = jnp.exp(logits - mx)
    probs = ex / jnp.sum(ex, axis=0, keepdims=True)
    gidx, i1, i2 = _route(probs + rb_ref[...])
    cls = _pair_class(gidx, i1, i2)

    ciota = lax.broadcasted_iota(I32, (CLASS_ROWS, TS), 0)
    onehot = jnp.where(ciota == cls, 1.0, 0.0).astype(F32)
    before = _dot(onehot.astype(BF16), tri_ref[...])
    total = jnp.sum(onehot, axis=1, keepdims=True)
    c = carry[:, 0:1]
    rank = jnp.sum(onehot * (c + before), axis=0, keepdims=True)
    route_ref[0:1, :] = cls
    route_ref[1:2, :] = rank.astype(I32)
    newc = jnp.broadcast_to(c + total, carry.shape)
    carry[...] = newc
    cnt_ref[...] = newc


def _layer_call(layer, bsz, seq, prev, w_in, pool_w, pool_scale, conv_w, conv_b, w_out, ln_g, ln_b,
                wr_t, rbias, tri):
    t = bsz * seq
    nt = t // TS
    const2 = lambda i: (0, 0)
    lsel3 = lambda i: (layer, 0, 0)
    lsel4 = lambda i: (layer, 0, 0, 0)
    tile = lambda i: (i, 0)
    lane_tile = lambda i: (0, i)
    if layer == 0:
        x, gin, bin_ = prev
        head_args = (x, gin, bin_)
        head_specs = [
            pl.BlockSpec((TS, D_MODEL), tile),
            pl.BlockSpec((1, D_MODEL), const2),
            pl.BlockSpec((1, D_MODEL), const2),
        ]
    else:
        slot_prev, h1t_prev, y_prev, g2, b2 = prev
        head_args = (slot_prev, slot_prev, h1t_prev, y_prev, g2, b2)
        head_specs = [
            pl.BlockSpec((1, 1, TS), lambda i: (i, 0, 0), memory_space=pltpu.SMEM),
            pl.BlockSpec((1, 1, TS), lambda i: (jnp.minimum(i + 1, nt - 1), 0, 0),
                         memory_space=pltpu.SMEM),
            pl.BlockSpec((TS * ROW_TILE, LANES), tile),
            pl.BlockSpec(memory_space=pl.ANY),
            pl.BlockSpec((1, 1, D_MODEL), lambda i: (layer - 1, 0, 0)),
            pl.BlockSpec((1, 1, D_MODEL), lambda i: (layer - 1, 0, 0)),
        ]
    return pl.pallas_call(
        functools.partial(_layer_kernel, layer=layer, tiles_per_seq=seq // TS),
        grid=(nt,),
        in_specs=head_specs + [
            pl.BlockSpec(memory_space=pl.ANY),
            pl.BlockSpec((1, len(POOL_WINDOWS), POOL_GROUP, POOL_GROUP), lsel4),
            pl.BlockSpec((1, 1, D_POOL), lsel3),
            pl.BlockSpec((1, CONV_WIDTH, D_CONV), lsel3),
            pl.BlockSpec((1, 1, D_CONV), lsel3),
            pl.BlockSpec(memory_space=pl.ANY),
            pl.BlockSpec((1, 1, D_MODEL), lsel3),
            pl.BlockSpec((1, 1, D_MODEL), lsel3),
            pl.BlockSpec((N_EXPERTS, D_MODEL), const2),
            pl.BlockSpec((N_EXPERTS, 1), const2),
            pl.BlockSpec((TS, TS), const2),
        ],
        out_specs=[
            pl.BlockSpec((TS * ROW_TILE, LANES), tile),
            pl.BlockSpec((2, TS), lane_tile),
            pl.BlockSpec((CLASS_ROWS, LANES), const2),
        ],
        out_shape=[
            jax.ShapeDtypeStruct((t * ROW_TILE, LANES), F32),
            jax.ShapeDtypeStruct((2, t), I32),
            jax.ShapeDtypeStruct((CLASS_ROWS, LANES), F32),
        ],
        scratch_shapes=[
            pltpu.VMEM((TS + V_HALO, D_POOL), F32),
            pltpu.VMEM((TS + Z_HALO, D_CONV), F32),
            pltpu.VMEM((TS, D_MODEL), BF16),
            pltpu.VMEM((CLASS_ROWS, LANES), F32),
            pltpu.VMEM((D_MODEL, D_PROJ), BF16),
            pltpu.VMEM((D_MODEL, D_MODEL), BF16),
            pltpu.VMEM((W_STAGE_ROWS, D_PROJ), F32),
            pltpu.SemaphoreType.DMA(()),
            pltpu.VMEM((2, TS * ROW_TILE, LANES), F32),
            pltpu.SemaphoreType.DMA((2,)),
        ],
        compiler_params=pltpu.CompilerParams(
            dimension_semantics=("arbitrary",), vmem_limit_bytes=VMEM_LIMIT),
        name="layer_mix_route",
    )(*head_args, w_in, pool_w, pool_scale, conv_w, conv_b, w_out, ln_g, ln_b, wr_t, rbias, tri)


def _slot_kernel(pstart_ref, route_ref, slot_ref):
    cls = route_ref[0:1, :]
    start = jnp.zeros(cls.shape, I32)
    for k in range(N_CLASSES):
        start = jnp.where(cls == k, pstart_ref[k], start)
    slot_ref[...] = (start + route_ref[1:2, :]) * ROW_TILE


def _slots(pstart, route):
    t = route.shape[1]
    grid_spec = pltpu.PrefetchScalarGridSpec(
        num_scalar_prefetch=1,
        grid=(1,),
        in_specs=[pl.BlockSpec((2, t), lambda i, ps: (0, 0))],
        out_specs=pl.BlockSpec((1, t), lambda i, ps: (0, 0)),
    )
    return pl.pallas_call(
        _slot_kernel,
        grid_spec=grid_spec,
        out_shape=jax.ShapeDtypeStruct((1, t), I32),
        name="token_slots",
    )(pstart, route)


def _scatter_kernel(slot_ref, h1t_ref, xin_ref, xout_ref, sem):
    del xin_ref
    for t in range(TS):
        dst = _row(xout_ref, pl.multiple_of(slot_ref[0, 0, t], ROW_TILE))
        pltpu.make_async_copy(_row(h1t_ref, t * ROW_TILE), dst, sem).start(priority=t % DMA_QUEUES)
    pltpu.make_async_copy(h1t_ref, xout_ref.at[pl.ds(0, TS * ROW_TILE)], sem).wait()


def _scatter_rows(slot3, h1t, xbuf):
    nt = slot3.shape[0]
    return pl.pallas_call(
        _scatter_kernel,
        grid=(nt,),
        in_specs=[
            pl.BlockSpec((1, 1, TS), lambda i: (i, 0, 0), memory_space=pltpu.SMEM),
            pl.BlockSpec((TS * ROW_TILE, LANES), lambda i: (i, 0)),
            pl.BlockSpec(memory_space=pl.ANY),
        ],
        out_specs=pl.BlockSpec(memory_space=pl.ANY),
        out_shape=jax.ShapeDtypeStruct(xbuf.shape, xbuf.dtype),
        scratch_shapes=[pltpu.SemaphoreType.DMA(())],
        input_output_aliases={2: 0},
        compiler_params=pltpu.CompilerParams(dimension_semantics=("arbitrary",)),
        name="scatter_rows",
    )(slot3, h1t, xbuf)


def _expert_kernel(ea_ref, eb_ref, nu_ref, x_ref, w1_hbm, w3_hbm, w2_hbm, wr_ref, y_ref,
                   w1a, w3a, w2a, w1b, w3b, w2b, s1a, s3a, s2a, s1b, s3b, s2b, wsem, *, layer):
    i = pl.program_id(0)
    nb = pl.num_programs(0)
    nu = nu_ref[0]
    used = i < nu
    ea = ea_ref[i]
    eb = eb_ref[i]

    def swap_weights(e_ref, slot, live, staged):
        e = e_ref[i]

        def fetch(expert):
            return [pltpu.make_async_copy(src.at[layer, expert], dst, wsem.at[slot])
                    for src, dst in zip((w1_hbm, w3_hbm, w2_hbm), staged)]

        @pl.when(used & (i == 0))
        def _():
            for cp in fetch(e):
                cp.start()

        @pl.when(used & ((i == 0) | (e != e_ref[jnp.maximum(i - 1, 0)])))
        def _():
            for cp in fetch(e):
                cp.wait()
            for dst, src in zip(live, staged):
                dst[...] = src[...].astype(BF16)
            nxt = lax.while_loop(lambda j: (j < nu) & (e_ref[jnp.minimum(j, nb - 1)] == e),
                                 lambda j: j + 1, i + 1)

            @pl.when(nxt < nu)
            def _():
                for cp in fetch(e_ref[jnp.minimum(nxt, nb - 1)]):
                    cp.start()

    swap_weights(ea_ref, 0, (w1a, w3a, w2a), (s1a, s3a, s2a))
    swap_weights(eb_ref, 1, (w1b, w3b, w2b), (s1b, s3b, s2b))

    @pl.when(used)
    def _():
        xb = _load_row_tiled(x_ref, BM).astype(BF16)

        logits = _dot(xb, wr_ref[...].astype(BF16))
        lane = lax.broadcasted_iota(I32, logits.shape, 1)
        logits = jnp.where(lane < N_EXPERTS, logits, -jnp.inf)
        ex = jnp.exp(logits - jnp.max(logits, axis=1, keepdims=True))
        probs = ex / jnp.sum(ex, axis=1, keepdims=True)
        pa = jnp.sum(jnp.where(lane == ea, probs, 0.0), axis=1, keepdims=True)
        pb = jnp.sum(jnp.where(lane == eb, probs, 0.0), axis=1, keepdims=True)
        den = pa + pb

        def ffn(w1, w3, w2):
            a = _dot(xb, w1[...])
            b = _dot(xb, w3[...])
            hid = a * jax.nn.sigmoid(a) * b
            return _dot(hid.astype(BF16), w2[...])

        y = (pa / den) * ffn(w1a, w3a, w2a) + (pb / den) * ffn(w1b, w3b, w2b)
        _store_row_tiled(y_ref, y)

    @pl.when(jnp.logical_not(used))
    def _():
        y_ref[...] = jnp.zeros(y_ref.shape, F32)


def _expert_ffn(layer, blk_ea, blk_eb, nused, xbuf, w1, w3, w2, wr_pad):
    n_blocks = xbuf.shape[0] // (BM * ROW_TILE)
    used_blk = lambda i, ea, eb, nu: (jnp.minimum(i, nu[0] - 1), 0)
    up = (D_MODEL, D_EXPERT)
    down = (D_EXPERT, D_MODEL)
    expert_slot = lambda dtype: [pltpu.VMEM(up, dtype), pltpu.VMEM(up, dtype), pltpu.VMEM(down, dtype)]
    grid_spec = pltpu.PrefetchScalarGridSpec(
        num_scalar_prefetch=3,
        grid=(n_blocks,),
        in_specs=[
            pl.BlockSpec((BM * ROW_TILE, LANES), used_blk),
            pl.BlockSpec(memory_space=pl.ANY),
            pl.BlockSpec(memory_space=pl.ANY),
            pl.BlockSpec(memory_space=pl.ANY),
            pl.BlockSpec((D_MODEL, LANES), lambda i, ea, eb, nu: (0, 0)),
        ],
        out_specs=pl.BlockSpec((BM * ROW_TILE, LANES), lambda i, ea, eb, nu: (i, 0)),
        scratch_shapes=(expert_slot(BF16) + expert_slot(BF16) + expert_slot(F32) + expert_slot(F32)
                        + [pltpu.SemaphoreType.DMA((2,))]),
    )
    return pl.pallas_call(
        functools.partial(_expert_kernel, layer=layer),
        grid_spec=grid_spec,
        out_shape=jax.ShapeDtypeStruct(xbuf.shape, F32),
        compiler_params=pltpu.CompilerParams(
            dimension_semantics=("arbitrary",), vmem_limit_bytes=VMEM_LIMIT),
        name="expert_ffn",
    )(blk_ea, blk_eb, nused, xbuf, w1, w3, w2, wr_pad)


def _dispatch_tables(cnt, nb):
    counts = cnt[:N_CLASSES, 0].astype(I32)
    padded = ((counts + BM - 1) // BM) * BM
    pend = jnp.cumsum(padded).astype(I32)
    pstart = pend - padded
    nused = pend[-1:] // BM
    step = jnp.minimum(jnp.arange(nb, dtype=I32), nused - 1)
    c = jnp.minimum(jnp.searchsorted(pend, step * BM, side="right"), N_CLASSES - 1).astype(I32)
    g = c // PAIRS_PER_GROUP
    p = c - g * PAIRS_PER_GROUP
    ge3 = (p >= 3).astype(I32)
    ge5 = (p >= 5).astype(I32)
    blk_ea = g * EXPERTS_PER_GROUP + ge3 + ge5
    blk_eb = g * EXPERTS_PER_GROUP + p + 1 - 2 * ge3 - ge5
    return pstart, blk_ea, blk_eb, nused


def _final_kernel(scur_ref, snext_ref, h1p_ref, y_hbm, g2_ref, b2_ref, o_ref, ybuf, gsem):
    i = pl.program_id(0)
    y = _prefetch_expert_rows(i, pl.num_programs(0), scur_ref, snext_ref, y_hbm, ybuf, gsem)
    o_ref[...] = _layer_norm(ALPHA * _load_row_tiled(h1p_ref, TS) + y, g2_ref[0], b2_ref[0])


def _final_call(slot_prev, h1t_prev, y_prev, g2, b2):
    t = h1t_prev.shape[0] // ROW_TILE
    nt = t // TS
    last = DEPTH - 1
    return pl.pallas_call(
        _final_kernel,
        grid=(nt,),
        in_specs=[
            pl.BlockSpec((1, 1, TS), lambda i: (i, 0, 0), memory_space=pltpu.SMEM),
            pl.BlockSpec((1, 1, TS), lambda i: (jnp.minimum(i + 1, nt - 1), 0, 0),
                         memory_space=pltpu.SMEM),
            pl.BlockSpec((TS * ROW_TILE, LANES), lambda i: (i, 0)),
            pl.BlockSpec(memory_space=pl.ANY),
            pl.BlockSpec((1, 1, D_MODEL), lambda i: (last, 0, 0)),
            pl.BlockSpec((1, 1, D_MODEL), lambda i: (last, 0, 0)),
        ],
        out_specs=pl.BlockSpec((TS, D_MODEL), lambda i: (i, 0)),
        out_shape=jax.ShapeDtypeStruct((t, D_MODEL), F32),
        scratch_shapes=[pltpu.VMEM((2, TS * ROW_TILE, LANES), F32), pltpu.SemaphoreType.DMA((2,))],
        compiler_params=pltpu.CompilerParams(
            dimension_semantics=("arbitrary",), vmem_limit_bytes=VMEM_LIMIT),
        name="final_combine_ln",
    )(slot_prev, slot_prev, h1t_prev, y_prev, g2, b2)


def kernel(x, ln_in_g, ln_in_b, w_in, pool_w, pool_scale, conv_w, conv_b, w_out, ln1_g, ln1_b,
           w_router, router_bias, exp_w1, exp_w3, exp_w2, ln2_g, ln2_b):
    bsz, seq, d = x.shape
    t = bsz * seq
    nt = t // TS
    n_rows = t + N_CLASSES * BM
    nb = n_rows // BM

    per_layer_row = lambda v: v.reshape(DEPTH, 1, -1)
    tri = (lax.broadcasted_iota(I32, (TS, TS), 0)
           < lax.broadcasted_iota(I32, (TS, TS), 1)).astype(BF16)
    wr_t = w_router.T
    wr_pad = jnp.pad(w_router, ((0, 0), (0, LANES - N_EXPERTS)))
    rbias = router_bias.reshape(N_EXPERTS, 1).astype(F32)
    pool_scale3, conv_b3 = per_layer_row(pool_scale), per_layer_row(conv_b)
    ln1_g3, ln1_b3 = per_layer_row(ln1_g), per_layer_row(ln1_b)
    ln2_g3, ln2_b3 = per_layer_row(ln2_g), per_layer_row(ln2_b)

    prev = (x.reshape(t, d), ln_in_g.reshape(1, -1), ln_in_b.reshape(1, -1))
    xbuf = jnp.zeros((n_rows * ROW_TILE, LANES), F32)
    for l in range(DEPTH):
        h1t, route, cnt = _layer_call(l, bsz, seq, prev, w_in, pool_w, pool_scale3, conv_w, conv_b3,
                                          w_out, ln1_g3, ln1_b3, wr_t, rbias, tri)
        pstart, blk_ea, blk_eb, nused = _dispatch_tables(cnt, nb)
        slot3 = _slots(pstart, route).reshape(nt, 1, TS)
        xbuf = _scatter_rows(slot3, h1t, xbuf)
        ybuf = _expert_ffn(l, blk_ea, blk_eb, nused, xbuf, exp_w1, exp_w3, exp_w2, wr_pad)
        prev = (slot3, h1t, ybuf, ln2_g3, ln2_b3)
    out = _final_call(*prev)
    return out.reshape(bsz, seq, d)
```

```python
import functools

import jax
import jax.numpy as jnp
from jax import lax
from jax.experimental import pallas as pl
from jax.experimental.pallas import tpu as pltpu

D_MODEL = 1024
DEPTH = 4
D_POOL = 512
POOL_WINDOWS = (2, 4, 8, 16)
POOL_GROUP = 128
D_CONV = 512
CONV_WIDTH = 3
D_PROJ = D_POOL + 3 * D_CONV
N_EXPERTS = 16
N_EXPERT_GROUPS = 4
EXPERTS_PER_GROUP = 4
PAIRS_PER_GROUP = 6
N_CLASSES = N_EXPERT_GROUPS * PAIRS_PER_GROUP
CLASS_ROWS = 32
D_EXPERT = 512
ALPHA = float((2 * DEPTH) ** 0.25)
LN_EPS = 1e-5

F32 = jnp.float32
BF16 = jnp.bfloat16
I32 = jnp.int32

LANES = 128
ROW_TILE = D_MODEL // LANES
TS = 512
SUBLANES = 8
V_HALO = 32
Z_HALO = 8
BM = 256
W_STAGE_ROWS = 256
DMA_QUEUES = 2
VMEM_LIMIT = 56 * 1024 * 1024


def _layer_norm(x, g, b):
    mu = jnp.mean(x, axis=-1, keepdims=True)
    xc = x - mu
    var = jnp.mean(xc * xc, axis=-1, keepdims=True)
    return xc * lax.rsqrt(var + LN_EPS) * g + b


def _dot(a, b):
    return jnp.dot(a, b, preferred_element_type=F32)


def _store_row_tiled(ref, value):
    n = value.shape[0]
    for j in range(ROW_TILE):
        ref[pl.ds(j, n, stride=ROW_TILE), :] = value[:, j * LANES:(j + 1) * LANES]


def _load_row_tiled(ref, n):
    return jnp.concatenate([ref[pl.ds(j, n, stride=ROW_TILE), :] for j in range(ROW_TILE)], axis=1)


def _row(ref, first_sublane):
    return ref.at[pl.ds(first_sublane, ROW_TILE)]


def _route(sel):
    rows = [sel[e:e + 1, :] for e in range(N_EXPERTS)]
    gscore = []
    for g in range(N_EXPERT_GROUPS):
        a, b, c, d = rows[4 * g:4 * g + 4]
        hi1, lo1 = jnp.maximum(a, b), jnp.minimum(a, b)
        hi2, lo2 = jnp.maximum(c, d), jnp.minimum(c, d)
        m1 = jnp.maximum(hi1, hi2)
        m2 = jnp.maximum(jnp.minimum(hi1, hi2), jnp.maximum(lo1, lo2))
        gscore.append(m1 + m2)
    best = gscore[0]
    gidx = jnp.zeros(best.shape, I32)
    for g in range(1, N_EXPERT_GROUPS):
        upd = gscore[g] > best
        best = jnp.where(upd, gscore[g], best)
        gidx = jnp.where(upd, g, gidx)
    neg = jnp.full(best.shape, -jnp.inf, F32)
    masked = [jnp.where(gidx == (e // EXPERTS_PER_GROUP), rows[e], neg) for e in range(N_EXPERTS)]

    def top1(vals):
        bv = neg
        bi = jnp.zeros(best.shape, I32)
        for e in range(N_EXPERTS):
            upd = vals[e] > bv
            bv = jnp.where(upd, vals[e], bv)
            bi = jnp.where(upd, e, bi)
        return bi

    i1 = top1(masked)
    i2 = top1([jnp.where(i1 == e, neg, masked[e]) for e in range(N_EXPERTS)])
    return gidx, i1, i2


def _pair_class(gidx, i1, i2):
    a = jnp.minimum(i1, i2) - gidx * EXPERTS_PER_GROUP
    b = jnp.maximum(i1, i2) - gidx * EXPERTS_PER_GROUP
    base = jnp.where(a == 0, 0, jnp.where(a == 1, 2, 3))
    return gidx * PAIRS_PER_GROUP + base + b - 1


def _prefetch_expert_rows(i, nt, scur_ref, snext_ref, y_hbm, ybuf, gsem):
    slot = i % 2

    def gather(slot_ref, to_slot, t, t_sublane):
        src = _row(y_hbm, pl.multiple_of(slot_ref[0, 0, t], ROW_TILE))
        return pltpu.make_async_copy(src, _row(ybuf.at[to_slot], t_sublane), gsem.at[to_slot])

    @pl.when(i == 0)
    def _():
        def body(t, carry):
            gather(scur_ref, 0, t, pl.multiple_of(t * ROW_TILE, ROW_TILE)).start()
            return carry
        lax.fori_loop(0, TS, body, 0, unroll=8)

    @pl.when(i + 1 < nt)
    def _():
        for t in range(TS):
            gather(snext_ref, 1 - slot, t, t * ROW_TILE).start(priority=t % DMA_QUEUES)

    pltpu.make_async_copy(y_hbm.at[pl.ds(0, TS * ROW_TILE)], ybuf.at[slot], gsem.at[slot]).wait()
    return _load_row_tiled(ybuf.at[slot], TS)


def _layer_kernel(*refs, layer, tiles_per_seq):
    if layer == 0:
        x_ref, gin_ref, bin_ref = refs[:3]
        rest = refs[3:]
    else:
        scur_ref, snext_ref, h1p_ref, y_hbm, g2_ref, b2_ref = refs[:6]
        rest = refs[6:]
    (win_hbm, pw_ref, ps_ref, cw_ref, cb_ref, wout_hbm, g_ref, b_ref, wr_ref, rb_ref, tri_ref,
     h1t_ref, route_ref, cnt_ref,
     vbuf, pwork, zbuf, mixbuf, carry, win_ref, wout_ref, stage, wsem, ybuf, gsem) = rest

    i = pl.program_id(0)
    nt = pl.num_programs(0)

    @pl.when(i == 0)
    def _():
        chunks = []
        for src, dst, width in ((win_hbm, win_ref, D_PROJ), (wout_hbm, wout_ref, D_MODEL)):
            for c in range(D_MODEL // W_STAGE_ROWS):
                chunks.append((src, dst, width, pl.ds(c * W_STAGE_ROWS, W_STAGE_ROWS)))

        def fetch(k):
            src, _, width, rows = chunks[k]
            return pltpu.make_async_copy(src.at[layer, rows], stage.at[k % 2, :, pl.ds(0, width)],
                                         wsem.at[k % 2])

        fetch(0).start()
        for k, (_, dst, width, rows) in enumerate(chunks):
            if k + 1 < len(chunks):
                fetch(k + 1).start()
            fetch(k).wait()
            dst[rows, :] = stage[k % 2, :, 0:width].astype(BF16)
        carry[...] = jnp.zeros(carry.shape, F32)

    if layer == 0:
        h = _layer_norm(x_ref[...], gin_ref[...], bin_ref[...])
    else:
        y = _prefetch_expert_rows(i, nt, scur_ref, snext_ref, y_hbm, ybuf, gsem)
        h = _layer_norm(ALPHA * _load_row_tiled(h1p_ref, TS) + y, g2_ref[0], b2_ref[0])

    si = i % tiles_per_seq

    @pl.when(si == 0)
    def _():
        vbuf[0:V_HALO, :] = jnp.zeros((V_HALO, D_POOL), F32)
        zbuf[0:Z_HALO, :] = jnp.zeros((Z_HALO, D_CONV), F32)

    hb = h.astype(BF16)
    vbuf[V_HALO:V_HALO + TS, :] = _dot(hb, win_ref[:, 0:D_POOL])
    gate_b = _dot(hb, win_ref[:, D_POOL:D_POOL + D_CONV])
    gate_c = _dot(hb, win_ref[:, D_POOL + D_CONV:D_POOL + 2 * D_CONV])
    u = _dot(hb, win_ref[:, D_POOL + 2 * D_CONV:D_PROJ])
    z = gate_c * u
    zbuf[Z_HALO:Z_HALO + TS, :] = z

    src = vbuf
    for level in range(1, len(POOL_WINDOWS)):
        dst = pwork.at[(level - 1) % 2]
        lo = SUBLANES * level
        shift = 2 ** (level - 1)
        n = TS + V_HALO - lo
        cols = pl.ds(level * POOL_GROUP, D_POOL - level * POOL_GROUP)
        dst[pl.ds(lo, n), cols] = src[pl.ds(lo, n), cols] + src[pl.ds(lo - shift, n), cols]
        src = dst
    tpos = si * TS + lax.broadcasted_iota(I32, (TS, 1), 0)
    for g, win in enumerate(POOL_WINDOWS):
        cols = pl.ds(g * POOL_GROUP, POOL_GROUP)
        cur = vbuf[pl.ds(V_HALO, TS), cols]
        half = vbuf if g == 0 else pwork.at[(g - 1) % 2]
        acc = half[pl.ds(V_HALO, TS), cols] + half[pl.ds(V_HALO - win // 2, TS), cols]
        denom = jnp.minimum(tpos + 1, win).astype(F32)
        pooled = acc / denom - cur
        mixed = _dot(pooled.astype(BF16), pw_ref[0, g].astype(BF16)) * ps_ref[0, :, cols]
        mixbuf[:, cols] = mixed.astype(BF16)

    yc = (cw_ref[0, 2:3, :] * z
          + cw_ref[0, 1:2, :] * zbuf[pl.ds(Z_HALO - 1, TS), :]
          + cw_ref[0, 0:1, :] * zbuf[pl.ds(Z_HALO - 2, TS), :])
    mixbuf[:, D_POOL:D_POOL + D_CONV] = (gate_b * (yc + cb_ref[0])).astype(BF16)

    vbuf[0:V_HALO, :] = vbuf[TS:TS + V_HALO, :]
    zbuf[0:Z_HALO, :] = zbuf[TS:TS + Z_HALO, :]

    mix = _dot(mixbuf[...], wout_ref[...])
    h1 = _layer_norm(ALPHA * h + mix, g_ref[0], b_ref[0])
    _store_row_tiled(h1t_ref, h1)

    logits = lax.dot_general(wr_ref[...].astype(BF16), h1.astype(BF16), (((1,), (1,)), ((), ())),
                             preferred_element_type=F32)
    mx = jnp.max(logits, axis=0, keepdims=True)
    ex = jnp.exp(logits - mx)
    probs = ex / jnp.sum(ex, axis=0, keepdims=True)
    gidx, i1, i2 = _route(probs + rb_ref[...])
    cls = _pair_class(gidx, i1, i2)

    ciota = lax.broadcasted_iota(I32, (CLASS_ROWS, TS), 0)
    onehot = jnp.where(ciota == cls, 1.0, 0.0).astype(F32)
    before = _dot(onehot.astype(BF16), tri_ref[...])
    total = jnp.sum(onehot, axis=1, keepdims=True)
    c = carry[:, 0:1]
    rank = jnp.sum(onehot * (c + before), axis=0, keepdims=True)
    route_ref[0:1, :] = cls
    route_ref[1:2, :] = rank.astype(I32)
    newc = jnp.broadcast_to(c + total, carry.shape)
    carry[...] = newc
    cnt_ref[...] = newc


def _layer_call(layer, bsz, seq, prev, w_in, pool_w, pool_scale, conv_w, conv_b, w_out, ln_g, ln_b,
                wr_t, rbias, tri):
    t = bsz * seq
    nt = t // TS
    const2 = lambda i: (0, 0)
    lsel3 = lambda i: (layer, 0, 0)
    lsel4 = lambda i: (layer, 0, 0, 0)
    tile = lambda i: (i, 0)
    lane_tile = lambda i: (0, i)
    if layer == 0:
        x, gin, bin_ = prev
        head_args = (x, gin, bin_)
        head_specs = [
            pl.BlockSpec((TS, D_MODEL), tile),
            pl.BlockSpec((1, D_MODEL), const2),
            pl.BlockSpec((1, D_MODEL), const2),
        ]
    else:
        slot_prev, h1t_prev, y_prev, g2, b2 = prev
        head_args = (slot_prev, slot_prev, h1t_prev, y_prev, g2, b2)
        head_specs = [
            pl.BlockSpec((1, 1, TS), lambda i: (i, 0, 0), memory_space=pltpu.SMEM),
            pl.BlockSpec((1, 1, TS), lambda i: (jnp.minimum(i + 1, nt - 1), 0, 0),
                         memory_space=pltpu.SMEM),
            pl.BlockSpec((TS * ROW_TILE, LANES), tile),
            pl.BlockSpec(memory_space=pl.ANY),
            pl.BlockSpec((1, 1, D_MODEL), lambda i: (layer - 1, 0, 0)),
            pl.BlockSpec((1, 1, D_MODEL), lambda i: (layer - 1, 0, 0)),
        ]
    return pl.pallas_call(
        functools.partial(_layer_kernel, layer=layer, tiles_per_seq=seq // TS),
        grid=(nt,),
        in_specs=head_specs + [
            pl.BlockSpec(memory_space=pl.ANY),
            pl.BlockSpec((1, len(POOL_WINDOWS), POOL_GROUP, POOL_GROUP), lsel4),
            pl.BlockSpec((1, 1, D_POOL), lsel3),
            pl.BlockSpec((1, CONV_WIDTH, D_CONV), lsel3),
            pl.BlockSpec((1, 1, D_CONV), lsel3),
            pl.BlockSpec(memory_space=pl.ANY),
            pl.BlockSpec((1, 1, D_MODEL), lsel3),
            pl.BlockSpec((1, 1, D_MODEL), lsel3),
            pl.BlockSpec((N_EXPERTS, D_MODEL), const2),
            pl.BlockSpec((N_EXPERTS, 1), const2),
            pl.BlockSpec((TS, TS), const2),
        ],
        out_specs=[
            pl.BlockSpec((TS * ROW_TILE, LANES), tile),
            pl.BlockSpec((2, TS), lane_tile),
            pl.BlockSpec((CLASS_ROWS, LANES), const2),
        ],
        out_shape=[
            jax.ShapeDtypeStruct((t * ROW_TILE, LANES), F32),
            jax.ShapeDtypeStruct((2, t), I32),
            jax.ShapeDtypeStruct((CLASS_ROWS, LANES), F32),
        ],
        scratch_shapes=[
            pltpu.VMEM((TS + V_HALO, D_POOL), F32),
            pltpu.VMEM((2, TS + V_HALO, D_POOL), F32),
            pltpu.VMEM((TS + Z_HALO, D_CONV), F32),
            pltpu.VMEM((TS, D_MODEL), BF16),
            pltpu.VMEM((CLASS_ROWS, LANES), F32),
            pltpu.VMEM((D_MODEL, D_PROJ), BF16),
            pltpu.VMEM((D_MODEL, D_MODEL), BF16),
            pltpu.VMEM((2, W_STAGE_ROWS, D_PROJ), F32),
            pltpu.SemaphoreType.DMA((2,)),
            pltpu.VMEM((2, TS * ROW_TILE, LANES), F32),
            pltpu.SemaphoreType.DMA((2,)),
        ],
        compiler_params=pltpu.CompilerParams(
            dimension_semantics=("arbitrary",), vmem_limit_bytes=VMEM_LIMIT),
        name="layer_mix_route",
    )(*head_args, w_in, pool_w, pool_scale, conv_w, conv_b, w_out, ln_g, ln_b, wr_t, rbias, tri)


def _slot_kernel(pstart_ref, route_ref, slot_ref):
    cls = route_ref[0:1, :]
    start = jnp.zeros(cls.shape, I32)
    for k in range(N_CLASSES):
        start = jnp.where(cls == k, pstart_ref[k], start)
    slot_ref[...] = (start + route_ref[1:2, :]) * ROW_TILE


def _slots(pstart, route):
    t = route.shape[1]
    grid_spec = pltpu.PrefetchScalarGridSpec(
        num_scalar_prefetch=1,
        grid=(1,),
        in_specs=[pl.BlockSpec((2, t), lambda i, ps: (0, 0))],
        out_specs=pl.BlockSpec((1, t), lambda i, ps: (0, 0)),
    )
    return pl.pallas_call(
        _slot_kernel,
        grid_spec=grid_spec,
        out_shape=jax.ShapeDtypeStruct((1, t), I32),
        name="token_slots",
    )(pstart, route)


def _scatter_kernel(slot_ref, h1t_ref, xin_ref, xout_ref, sem):
    del xin_ref
    for t in range(TS):
        dst = _row(xout_ref, pl.multiple_of(slot_ref[0, 0, t], ROW_TILE))
        pltpu.make_async_copy(_row(h1t_ref, t * ROW_TILE), dst, sem).start(priority=t % DMA_QUEUES)
    pltpu.make_async_copy(h1t_ref, xout_ref.at[pl.ds(0, TS * ROW_TILE)], sem).wait()


def _scatter_rows(slot3, h1t, xbuf):
    nt = slot3.shape[0]
    return pl.pallas_call(
        _scatter_kernel,
        grid=(nt,),
        in_specs=[
            pl.BlockSpec((1, 1, TS), lambda i: (i, 0, 0), memory_space=pltpu.SMEM),
            pl.BlockSpec((TS * ROW_TILE, LANES), lambda i: (i, 0)),
            pl.BlockSpec(memory_space=pl.ANY),
        ],
        out_specs=pl.BlockSpec(memory_space=pl.ANY),
        out_shape=jax.ShapeDtypeStruct(xbuf.shape, xbuf.dtype),
        scratch_shapes=[pltpu.SemaphoreType.DMA(())],
        input_output_aliases={2: 0},
        compiler_params=pltpu.CompilerParams(dimension_semantics=("arbitrary",)),
        name="scatter_rows",
    )(slot3, h1t, xbuf)


def _expert_kernel(ea_ref, eb_ref, nu_ref, x_ref, w1_hbm, w3_hbm, w2_hbm, wr_ref, y_ref,
                   w1a, w3a, w2a, w1b, w3b, w2b, s1a, s3a, s2a, s1b, s3b, s2b, wsem, *, layer):
    i = pl.program_id(0)
    nb = pl.num_programs(0)
    nu = nu_ref[0]
    used = i < nu
    ea = ea_ref[i]
    eb = eb_ref[i]

    def swap_weights(e_ref, slot, live, staged):
        e = e_ref[i]

        def fetch(expert):
            return [pltpu.make_async_copy(src.at[layer, expert], dst, wsem.at[slot])
                    for src, dst in zip((w1_hbm, w3_hbm, w2_hbm), staged)]

        @pl.when(used & (i == 0))
        def _():
            for cp in fetch(e):
                cp.start()

        @pl.when(used & ((i == 0) | (e != e_ref[jnp.maximum(i - 1, 0)])))
        def _():
            for cp in fetch(e):
                cp.wait()
            for dst, src in zip(live, staged):
                dst[...] = src[...].astype(BF16)
            nxt = lax.while_loop(lambda j: (j < nu) & (e_ref[jnp.minimum(j, nb - 1)] == e),
                                 lambda j: j + 1, i + 1)

            @pl.when(nxt < nu)
            def _():
                for cp in fetch(e_ref[jnp.minimum(nxt, nb - 1)]):
                    cp.start()

    swap_weights(ea_ref, 0, (w1a, w3a, w2a), (s1a, s3a, s2a))
    swap_weights(eb_ref, 1, (w1b, w3b, w2b), (s1b, s3b, s2b))

    @pl.when(used)
    def _():
        xb = _load_row_tiled(x_ref, BM).astype(BF16)

        logits = _dot(xb, wr_ref[...].astype(BF16))
        lane = lax.broadcasted_iota(I32, logits.shape, 1)
        logits = jnp.where(lane < N_EXPERTS, logits, -jnp.inf)
        ex = jnp.exp(logits - jnp.max(logits, axis=1, keepdims=True))
        probs = ex / jnp.sum(ex, axis=1, keepdims=True)
        pa = jnp.sum(jnp.where(lane == ea, probs, 0.0), axis=1, keepdims=True)
        pb = jnp.sum(jnp.where(lane == eb, probs, 0.0), axis=1, keepdims=True)
        den = pa + pb

        def ffn(w1, w3, w2):
            a = _dot(xb, w1[...])
            b = _dot(xb, w3[...])
            hid = a * jax.nn.sigmoid(a) * b
            return _dot(hid.astype(BF16), w2[...])

        y = (pa / den) * ffn(w1a, w3a, w2a) + (pb / den) * ffn(w1b, w3b, w2b)
        _store_row_tiled(y_ref, y)

    @pl.when(jnp.logical_not(used))
    def _():
        y_ref[...] = jnp.zeros(y_ref.shape, F32)


def _expert_ffn(layer, blk_ea, blk_eb, nused, xbuf, w1, w3, w2, wr_pad):
    n_blocks = xbuf.shape[0] // (BM * ROW_TILE)
    used_blk = lambda i, ea, eb, nu: (jnp.minimum(i, nu[0] - 1), 0)
    up = (D_MODEL, D_EXPERT)
    down = (D_EXPERT, D_MODEL)
    expert_slot = lambda dtype: [pltpu.VMEM(up, dtype), pltpu.VMEM(up, dtype), pltpu.VMEM(down, dtype)]
    grid_spec = pltpu.PrefetchScalarGridSpec(
        num_scalar_prefetch=3,
        grid=(n_blocks,),
        in_specs=[
            pl.BlockSpec((BM * ROW_TILE, LANES), used_blk),
            pl.BlockSpec(memory_space=pl.ANY),
            pl.BlockSpec(memory_space=pl.ANY),
            pl.BlockSpec(memory_space=pl.ANY),
            pl.BlockSpec((D_MODEL, LANES), lambda i, ea, eb, nu: (0, 0)),
        ],
        out_specs=pl.BlockSpec((BM * ROW_TILE, LANES), lambda i, ea, eb, nu: (i, 0)),
        scratch_shapes=(expert_slot(BF16) + expert_slot(BF16) + expert_slot(F32) + expert_slot(F32)
                        + [pltpu.SemaphoreType.DMA((2,))]),
    )
    return pl.pallas_call(
        functools.partial(_expert_kernel, layer=layer),
        grid_spec=grid_spec,
        out_shape=jax.ShapeDtypeStruct(xbuf.shape, F32),
        compiler_params=pltpu.CompilerParams(
            dimension_semantics=("arbitrary",), vmem_limit_bytes=VMEM_LIMIT),
        name="expert_ffn",
    )(blk_ea, blk_eb, nused, xbuf, w1, w3, w2, wr_pad)


def _dispatch_tables(cnt, nb):
    counts = cnt[:N_CLASSES, 0].astype(I32)
    padded = ((counts + BM - 1) // BM) * BM
    pend = jnp.cumsum(padded).astype(I32)
    pstart = pend - padded
    nused = pend[-1:] // BM
    step = jnp.minimum(jnp.arange(nb, dtype=I32), nused - 1)
    c = jnp.sum((step[:, None] * BM >= pend[None, :N_CLASSES - 1]).astype(I32), axis=1)
    g = c // PAIRS_PER_GROUP
    p = c - g * PAIRS_PER_GROUP
    ge3 = (p >= 3).astype(I32)
    ge5 = (p >= 5).astype(I32)
    blk_ea = g * EXPERTS_PER_GROUP + ge3 + ge5
    blk_eb = g * EXPERTS_PER_GROUP + p + 1 - 2 * ge3 - ge5
    return pstart, blk_ea, blk_eb, nused


def _final_kernel(scur_ref, snext_ref, h1p_ref, y_hbm, g2_ref, b2_ref, o_ref, ybuf, gsem):
    i = pl.program_id(0)
    y = _prefetch_expert_rows(i, pl.num_programs(0), scur_ref, snext_ref, y_hbm, ybuf, gsem)
    o_ref[...] = _layer_norm(ALPHA * _load_row_tiled(h1p_ref, TS) + y, g2_ref[0], b2_ref[0])


def _final_call(slot_prev, h1t_prev, y_prev, g2, b2):
    t = h1t_prev.shape[0] // ROW_TILE
    nt = t // TS
    last = DEPTH - 1
    return pl.pallas_call(
        _final_kernel,
        grid=(nt,),
        in_specs=[
            pl.BlockSpec((1, 1, TS), lambda i: (i, 0, 0), memory_space=pltpu.SMEM),
            pl.BlockSpec((1, 1, TS), lambda i: (jnp.minimum(i + 1, nt - 1), 0, 0),
                         memory_space=pltpu.SMEM),
            pl.BlockSpec((TS * ROW_TILE, LANES), lambda i: (i, 0)),
            pl.BlockSpec(memory_space=pl.ANY),
            pl.BlockSpec((1, 1, D_MODEL), lambda i: (last, 0, 0)),
            pl.BlockSpec((1, 1, D_MODEL), lambda i: (last, 0, 0)),
        ],
        out_specs=pl.BlockSpec((TS, D_MODEL), lambda i: (i, 0)),
        out_shape=jax.ShapeDtypeStruct((t, D_MODEL), F32),
        scratch_shapes=[pltpu.VMEM((2, TS * ROW_TILE, LANES), F32), pltpu.SemaphoreType.DMA((2,))],
        compiler_params=pltpu.CompilerParams(
            dimension_semantics=("arbitrary",), vmem_limit_bytes=VMEM_LIMIT),
        name="final_combine_ln",
    )(slot_prev, slot_prev, h1t_prev, y_prev, g2, b2)


def kernel(x, ln_in_g, ln_in_b, w_in, pool_w, pool_scale, conv_w, conv_b, w_out, ln1_g, ln1_b,
           w_router, router_bias, exp_w1, exp_w3, exp_w2, ln2_g, ln2_b):
    bsz, seq, d = x.shape
    t = bsz * seq
    nt = t // TS
    n_rows = t + N_CLASSES * BM
    nb = n_rows // BM

    per_layer_row = lambda v: v.reshape(DEPTH, 1, -1)
    tri = (lax.broadcasted_iota(I32, (TS, TS), 0)
           < lax.broadcasted_iota(I32, (TS, TS), 1)).astype(BF16)
    wr_t = w_router.T
    wr_pad = jnp.pad(w_router, ((0, 0), (0, LANES - N_EXPERTS)))
    rbias = router_bias.reshape(N_EXPERTS, 1).astype(F32)
    pool_scale3, conv_b3 = per_layer_row(pool_scale), per_layer_row(conv_b)
    ln1_g3, ln1_b3 = per_layer_row(ln1_g), per_layer_row(ln1_b)
    ln2_g3, ln2_b3 = per_layer_row(ln2_g), per_layer_row(ln2_b)

    prev = (x.reshape(t, d), ln_in_g.reshape(1, -1), ln_in_b.reshape(1, -1))
    xbuf = jnp.zeros((n_rows * ROW_TILE, LANES), F32)
    for l in range(DEPTH):
        h1t, route, cnt = _layer_call(l, bsz, seq, prev, w_in, pool_w, pool_scale3, conv_w, conv_b3,
                                          w_out, ln1_g3, ln1_b3, wr_t, rbias, tri)
        pstart, blk_ea, blk_eb, nused = _dispatch_tables(cnt, nb)
        slot3 = _slots(pstart, route).reshape(nt, 1, TS)
        xbuf = _scatter_rows(slot3, h1t, xbuf)
        ybuf = _expert_ffn(l, blk_ea, blk_eb, nused, xbuf, exp_w1, exp_w3, exp_w2, wr_pad)
        prev = (slot3, h1t, ybuf, ln2_g3, ln2_b3)
    out = _final_call(*prev)
    return out.reshape(bsz, seq, d)
```

```python
import functools

import jax
import jax.numpy as jnp
from jax import lax
from jax.experimental import pallas as pl
from jax.experimental.pallas import tpu as pltpu

D_MODEL = 1024
DEPTH = 4
D_POOL = 512
POOL_WINDOWS = (2, 4, 8, 16)
POOL_GROUP = 128
D_CONV = 512
CONV_WIDTH = 3
D_PROJ = D_POOL + 3 * D_CONV
N_EXPERTS = 16
N_EXPERT_GROUPS = 4
EXPERTS_PER_GROUP = 4
PAIRS_PER_GROUP = 6
N_CLASSES = N_EXPERT_GROUPS * PAIRS_PER_GROUP
CLASS_ROWS = 32
D_EXPERT = 512
ALPHA = float((2 * DEPTH) ** 0.25)
LN_EPS = 1e-5

F32 = jnp.float32
BF16 = jnp.bfloat16
I32 = jnp.int32

LANES = 128
ROW_TILE = D_MODEL // LANES
TS = 512
SUBLANES = 8
V_HALO = 32
Z_HALO = 8
BM = 256
W_STAGE_ROWS = 256
DMA_QUEUES = 2
VMEM_LIMIT = 56 * 1024 * 1024


def _layer_norm(x, g, b):
    mu = jnp.mean(x, axis=-1, keepdims=True)
    xc = x - mu
    var = jnp.mean(xc * xc, axis=-1, keepdims=True)
    return xc * lax.rsqrt(var + LN_EPS) * g + b


def _dot(a, b):
    return jnp.dot(a, b, preferred_element_type=F32)


def _store_row_tiled(ref, value):
    n = value.shape[0]
    for j in range(ROW_TILE):
        ref[pl.ds(j, n, stride=ROW_TILE), :] = value[:, j * LANES:(j + 1) * LANES]


def _load_row_tiled(ref, n):
    return jnp.concatenate([ref[pl.ds(j, n, stride=ROW_TILE), :] for j in range(ROW_TILE)], axis=1)


def _row(ref, first_sublane):
    return ref.at[pl.ds(first_sublane, ROW_TILE)]


def _route(sel):
    rows = [sel[e:e + 1, :] for e in range(N_EXPERTS)]
    gscore = []
    for g in range(N_EXPERT_GROUPS):
        a, b, c, d = rows[4 * g:4 * g + 4]
        hi1, lo1 = jnp.maximum(a, b), jnp.minimum(a, b)
        hi2, lo2 = jnp.maximum(c, d), jnp.minimum(c, d)
        m1 = jnp.maximum(hi1, hi2)
        m2 = jnp.maximum(jnp.minimum(hi1, hi2), jnp.maximum(lo1, lo2))
        gscore.append(m1 + m2)
    best = gscore[0]
    gidx = jnp.zeros(best.shape, I32)
    for g in range(1, N_EXPERT_GROUPS):
        upd = gscore[g] > best
        best = jnp.where(upd, gscore[g], best)
        gidx = jnp.where(upd, g, gidx)
    neg = jnp.full(best.shape, -jnp.inf, F32)
    masked = [jnp.where(gidx == (e // EXPERTS_PER_GROUP), rows[e], neg) for e in range(N_EXPERTS)]

    def top1(vals):
        bv = neg
        bi = jnp.zeros(best.shape, I32)
        for e in range(N_EXPERTS):
            upd = vals[e] > bv
            bv = jnp.where(upd, vals[e], bv)
            bi = jnp.where(upd, e, bi)
        return bi

    i1 = top1(masked)
    i2 = top1([jnp.where(i1 == e, neg, masked[e]) for e in range(N_EXPERTS)])
    return gidx, i1, i2


def _pair_class(gidx, i1, i2):
    a = jnp.minimum(i1, i2) - gidx * EXPERTS_PER_GROUP
    b = jnp.maximum(i1, i2) - gidx * EXPERTS_PER_GROUP
    base = jnp.where(a == 0, 0, jnp.where(a == 1, 2, 3))
    return gidx * PAIRS_PER_GROUP + base + b - 1


def _prefetch_expert_rows(i, nt, scur_ref, snext_ref, y_hbm, ybuf, gsem):
    slot = i % 2

    def gather(slot_ref, to_slot, t, t_sublane):
        src = _row(y_hbm, pl.multiple_of(slot_ref[0, 0, t], ROW_TILE))
        return pltpu.make_async_copy(src, _row(ybuf.at[to_slot], t_sublane), gsem.at[to_slot])

    @pl.when(i == 0)
    def _():
        def body(t, carry):
            gather(scur_ref, 0, t, pl.multiple_of(t * ROW_TILE, ROW_TILE)).start()
            return carry
        lax.fori_loop(0, TS, body, 0, unroll=8)

    @pl.when(i + 1 < nt)
    def _():
        for t in range(TS):
            gather(snext_ref, 1 - slot, t, t * ROW_TILE).start(priority=t % DMA_QUEUES)

    pltpu.make_async_copy(y_hbm.at[pl.ds(0, TS * ROW_TILE)], ybuf.at[slot], gsem.at[slot]).wait()
    return _load_row_tiled(ybuf.at[slot], TS)


def _layer_kernel(*refs, layer, tiles_per_seq):
    if layer == 0:
        x_ref, gin_ref, bin_ref = refs[:3]
        rest = refs[3:]
    else:
        scur_ref, snext_ref, h1p_ref, y_hbm, g2_ref, b2_ref = refs[:6]
        rest = refs[6:]
    (win_hbm, pw_ref, ps_ref, cw_ref, cb_ref, wout_hbm, g_ref, b_ref, wr_ref, rb_ref, tri_ref,
     h1t_ref, route_ref, cnt_ref,
     vbuf, pwork, zbuf, mixbuf, carry, win_ref, wout_ref, stage, wsem, ybuf, gsem) = rest

    i = pl.program_id(0)
    nt = pl.num_programs(0)

    @pl.when(i == 0)
    def _():
        chunks = []
        for src, dst, width in ((win_hbm, win_ref, D_PROJ), (wout_hbm, wout_ref, D_MODEL)):
            for c in range(D_MODEL // W_STAGE_ROWS):
                chunks.append((src, dst, width, pl.ds(c * W_STAGE_ROWS, W_STAGE_ROWS)))

        def fetch(k):
            src, _, width, rows = chunks[k]
            return pltpu.make_async_copy(src.at[layer, rows], stage.at[k % 2, :, pl.ds(0, width)],
                                         wsem.at[k % 2])

        fetch(0).start()
        for k, (_, dst, width, rows) in enumerate(chunks):
            if k + 1 < len(chunks):
                fetch(k + 1).start()
            fetch(k).wait()
            dst[rows, :] = stage[k % 2, :, 0:width].astype(BF16)
        carry[...] = jnp.zeros(carry.shape, F32)

    if layer == 0:
        h = _layer_norm(x_ref[...], gin_ref[...], bin_ref[...])
    else:
        y = _prefetch_expert_rows(i, nt, scur_ref, snext_ref, y_hbm, ybuf, gsem)
        h = _layer_norm(ALPHA * _load_row_tiled(h1p_ref, TS) + y, g2_ref[0], b2_ref[0])

    si = i % tiles_per_seq

    @pl.when(si == 0)
    def _():
        vbuf[0:V_HALO, :] = jnp.zeros((V_HALO, D_POOL), F32)
        zbuf[0:Z_HALO, :] = jnp.zeros((Z_HALO, D_CONV), F32)

    hb = h.astype(BF16)
    vbuf[V_HALO:V_HALO + TS, :] = _dot(hb, win_ref[:, 0:D_POOL])
    gate_b = _dot(hb, win_ref[:, D_POOL:D_POOL + D_CONV])
    gate_c = _dot(hb, win_ref[:, D_POOL + D_CONV:D_POOL + 2 * D_CONV])
    u = _dot(hb, win_ref[:, D_POOL + 2 * D_CONV:D_PROJ])
    z = gate_c * u
    zbuf[Z_HALO:Z_HALO + TS, :] = z

    src = vbuf
    for level in range(1, len(POOL_WINDOWS)):
        dst = pwork.at[(level - 1) % 2]
        lo = SUBLANES * level
        shift = 2 ** (level - 1)
        n = TS + V_HALO - lo
        cols = pl.ds(level * POOL_GROUP, D_POOL - level * POOL_GROUP)
        dst[pl.ds(lo, n), cols] = src[pl.ds(lo, n), cols] + src[pl.ds(lo - shift, n), cols]
        src = dst
    tpos = si * TS + lax.broadcasted_iota(I32, (TS, 1), 0)
    for g, win in enumerate(POOL_WINDOWS):
        cols = pl.ds(g * POOL_GROUP, POOL_GROUP)
        cur = vbuf[pl.ds(V_HALO, TS), cols]
        half = vbuf if g == 0 else pwork.at[(g - 1) % 2]
        acc = half[pl.ds(V_HALO, TS), cols] + half[pl.ds(V_HALO - win // 2, TS), cols]
        denom = jnp.minimum(tpos + 1, win).astype(F32)
        pooled = acc / denom - cur
        mixed = _dot(pooled.astype(BF16), pw_ref[0, g].astype(BF16)) * ps_ref[0, :, cols]
        mixbuf[:, cols] = mixed.astype(BF16)

    yc = (cw_ref[0, 2:3, :] * z
          + cw_ref[0, 1:2, :] * zbuf[pl.ds(Z_HALO - 1, TS), :]
          + cw_ref[0, 0:1, :] * zbuf[pl.ds(Z_HALO - 2, TS), :])
    mixbuf[:, D_POOL:D_POOL + D_CONV] = (gate_b * (yc + cb_ref[0])).astype(BF16)

    vbuf[0:V_HALO, :] = vbuf[TS:TS + V_HALO, :]
    zbuf[0:Z_HALO, :] = zbuf[TS:TS + Z_HALO, :]

    mix = _dot(mixbuf[...], wout_ref[...])
    h1 = _layer_norm(ALPHA * h + mix, g_ref[0], b_ref[0])
    _store_row_tiled(h1t_ref, h1)

    logits = lax.dot_general(wr_ref[...].astype(BF16), h1.astype(BF16), (((1,), (1,)), ((), ())),
                             preferred_element_type=F32)
    mx = jnp.max(logits, axis=0, keepdims=True)
    ex = jnp.exp(logits - mx)
    probs = ex / jnp.sum(ex, axis=0, keepdims=True)
    gidx, i1, i2 = _route(probs + rb_ref[...])
    cls = _pair_class(gidx, i1, i2)

    ciota = lax.broadcasted_iota(I32, (CLASS_ROWS, TS), 0)
    onehot = jnp.where(ciota == cls, 1.0, 0.0).astype(F32)
    before = _dot(onehot.astype(BF16), tri_ref[...])
    total = jnp.sum(onehot, axis=1, keepdims=True)
    c = carry[:, 0:1]
    rank = jnp.sum(onehot * (c + before), axis=0, keepdims=True)
    route_ref[0:1, :] = cls
    route_ref[1:2, :] = rank.astype(I32)
    newc = jnp.broadcast_to(c + total, carry.shape)
    carry[...] = newc
    cnt_ref[...] = newc


def _layer_call(layer, bsz, seq, prev, w_in, pool_w, pool_scale, conv_w, conv_b, w_out, ln_g, ln_b,
                wr_t, rbias, tri):
    t = bsz * seq
    nt = t // TS
    const2 = lambda i: (0, 0)
    lsel3 = lambda i: (layer, 0, 0)
    lsel4 = lambda i: (layer, 0, 0, 0)
    tile = lambda i: (i, 0)
    lane_tile = lambda i: (0, i)
    if layer == 0:
        x, gin, bin_ = prev
        head_args = (x, gin, bin_)
        head_specs = [
            pl.BlockSpec((TS, D_MODEL), tile),
            pl.BlockSpec((1, D_MODEL), const2),
            pl.BlockSpec((1, D_MODEL), const2),
        ]
    else:
        slot_prev, h1t_prev, y_prev, g2, b2 = prev
        head_args = (slot_prev, slot_prev, h1t_prev, y_prev, g2, b2)
        head_specs = [
            pl.BlockSpec((1, 1, TS), lambda i: (i, 0, 0), memory_space=pltpu.SMEM),
            pl.BlockSpec((1, 1, TS), lambda i: (jnp.minimum(i + 1, nt - 1), 0, 0),
                         memory_space=pltpu.SMEM),
            pl.BlockSpec((TS * ROW_TILE, LANES), tile),
            pl.BlockSpec(memory_space=pl.ANY),
            pl.BlockSpec((1, 1, D_MODEL), lambda i: (layer - 1, 0, 0)),
            pl.BlockSpec((1, 1, D_MODEL), lambda i: (layer - 1, 0, 0)),
        ]
    return pl.pallas_call(
        functools.partial(_layer_kernel, layer=layer, tiles_per_seq=seq // TS),
        grid=(nt,),
        in_specs=head_specs + [
            pl.BlockSpec(memory_space=pl.ANY),
            pl.BlockSpec((1, len(POOL_WINDOWS), POOL_GROUP, POOL_GROUP), lsel4),
            pl.BlockSpec((1, 1, D_POOL), lsel3),
            pl.BlockSpec((1, CONV_WIDTH, D_CONV), lsel3),
            pl.BlockSpec((1, 1, D_CONV), lsel3),
            pl.BlockSpec(memory_space=pl.ANY),
            pl.BlockSpec((1, 1, D_MODEL), lsel3),
            pl.BlockSpec((1, 1, D_MODEL), lsel3),
            pl.BlockSpec((N_EXPERTS, D_MODEL), const2),
            pl.BlockSpec((N_EXPERTS, 1), const2),
            pl.BlockSpec((TS, TS), const2),
        ],
        out_specs=[
            pl.BlockSpec((TS * ROW_TILE, LANES), tile),
            pl.BlockSpec((2, TS), lane_tile),
            pl.BlockSpec((CLASS_ROWS, LANES), const2),
        ],
        out_shape=[
            jax.ShapeDtypeStruct((t * ROW_TILE, LANES), F32),
            jax.ShapeDtypeStruct((2, t), I32),
            jax.ShapeDtypeStruct((CLASS_ROWS, LANES), F32),
        ],
        scratch_shapes=[
            pltpu.VMEM((TS + V_HALO, D_POOL), F32),
            pltpu.VMEM((2, TS + V_HALO, D_POOL), F32),
            pltpu.VMEM((TS + Z_HALO, D_CONV), F32),
            pltpu.VMEM((TS, D_MODEL), BF16),
            pltpu.VMEM((CLASS_ROWS, LANES), F32),
            pltpu.VMEM((D_MODEL, D_PROJ), BF16),
            pltpu.VMEM((D_MODEL, D_MODEL), BF16),
            pltpu.VMEM((2, W_STAGE_ROWS, D_PROJ), F32),
            pltpu.SemaphoreType.DMA((2,)),
            pltpu.VMEM((2, TS * ROW_TILE, LANES), F32),
            pltpu.SemaphoreType.DMA((2,)),
        ],
        compiler_params=pltpu.CompilerParams(
            dimension_semantics=("arbitrary",), vmem_limit_bytes=VMEM_LIMIT),
        name="layer_mix_route",
    )(*head_args, w_in, pool_w, pool_scale, conv_w, conv_b, w_out, ln_g, ln_b, wr_t, rbias, tri)


def _slot_kernel(pstart_ref, route_ref, slot_ref):
    cls = route_ref[0:1, :]
    start = jnp.zeros(cls.shape, I32)
    for k in range(N_CLASSES):
        start = jnp.where(cls == k, pstart_ref[k], start)
    slot_ref[...] = (start + route_ref[1:2, :]) * ROW_TILE


def _slots(pstart, route):
    t = route.shape[1]
    grid_spec = pltpu.PrefetchScalarGridSpec(
        num_scalar_prefetch=1,
        grid=(1,),
        in_specs=[pl.BlockSpec((2, t), lambda i, ps: (0, 0))],
        out_specs=pl.BlockSpec((1, t), lambda i, ps: (0, 0)),
    )
    return pl.pallas_call(
        _slot_kernel,
        grid_spec=grid_spec,
        out_shape=jax.ShapeDtypeStruct((1, t), I32),
        name="token_slots",
    )(pstart, route)


def _scatter_kernel(slot_ref, h1t_hbm, xin_ref, xout_ref, sem):
    del xin_ref
    i = pl.program_id(0)
    tile_base = i * (TS * ROW_TILE)
    for t in range(TS):
        src = _row(h1t_hbm, pl.multiple_of(tile_base + t * ROW_TILE, ROW_TILE))
        dst = _row(xout_ref, pl.multiple_of(slot_ref[0, 0, t], ROW_TILE))
        pltpu.make_async_copy(src, dst, sem).start(priority=t % DMA_QUEUES)
    tile_done = pltpu.make_async_copy(h1t_hbm.at[pl.ds(0, TS * ROW_TILE)],
                                      xout_ref.at[pl.ds(0, TS * ROW_TILE)], sem)

    @pl.when(i > 0)
    def _():
        tile_done.wait()

    @pl.when(i == pl.num_programs(0) - 1)
    def _():
        tile_done.wait()


def _scatter_rows(slot3, h1t, xbuf):
    nt = slot3.shape[0]
    return pl.pallas_call(
        _scatter_kernel,
        grid=(nt,),
        in_specs=[
            pl.BlockSpec((1, 1, TS), lambda i: (i, 0, 0), memory_space=pltpu.SMEM),
            pl.BlockSpec(memory_space=pl.ANY),
            pl.BlockSpec(memory_space=pl.ANY),
        ],
        out_specs=pl.BlockSpec(memory_space=pl.ANY),
        out_shape=jax.ShapeDtypeStruct(xbuf.shape, xbuf.dtype),
        scratch_shapes=[pltpu.SemaphoreType.DMA(())],
        input_output_aliases={2: 0},
        compiler_params=pltpu.CompilerParams(dimension_semantics=("arbitrary",)),
        name="scatter_rows",
    )(slot3, h1t, xbuf)


def _expert_kernel(ea_ref, eb_ref, nu_ref, x_ref, w1_hbm, w3_hbm, w2_hbm, wr_ref, y_ref,
                   w1a, w3a, w2a, w1b, w3b, w2b, s1a, s3a, s2a, s1b, s3b, s2b, wsem, *, layer):
    i = pl.program_id(0)
    nb = pl.num_programs(0)
    nu = nu_ref[0]
    used = i < nu
    ea = ea_ref[i]
    eb = eb_ref[i]

    def swap_weights(e_ref, slot, live, staged):
        e = e_ref[i]

        def fetch(expert):
            return [pltpu.make_async_copy(src.at[layer, expert], dst, wsem.at[slot])
                    for src, dst in zip((w1_hbm, w3_hbm, w2_hbm), staged)]

        @pl.when(used & (i == 0))
        def _():
            for cp in fetch(e):
                cp.start()

        @pl.when(used & ((i == 0) | (e != e_ref[jnp.maximum(i - 1, 0)])))
        def _():
            for cp in fetch(e):
                cp.wait()
            for dst, src in zip(live, staged):
                dst[...] = src[...].astype(BF16)
            nxt = lax.while_loop(lambda j: (j < nu) & (e_ref[jnp.minimum(j, nb - 1)] == e),
                                 lambda j: j + 1, i + 1)

            @pl.when(nxt < nu)
            def _():
                for cp in fetch(e_ref[jnp.minimum(nxt, nb - 1)]):
                    cp.start()

    swap_weights(ea_ref, 0, (w1a, w3a, w2a), (s1a, s3a, s2a))
    swap_weights(eb_ref, 1, (w1b, w3b, w2b), (s1b, s3b, s2b))

    @pl.when(used)
    def _():
        xb = _load_row_tiled(x_ref, BM).astype(BF16)

        logits = _dot(xb, wr_ref[...].astype(BF16))
        lane = lax.broadcasted_iota(I32, logits.shape, 1)
        logits = jnp.where(lane < N_EXPERTS, logits, -jnp.inf)
        ex = jnp.exp(logits - jnp.max(logits, axis=1, keepdims=True))
        probs = ex / jnp.sum(ex, axis=1, keepdims=True)
        pa = jnp.sum(jnp.where(lane == ea, probs, 0.0), axis=1, keepdims=True)
        pb = jnp.sum(jnp.where(lane == eb, probs, 0.0), axis=1, keepdims=True)
        den = pa + pb

        def ffn(w1, w3, w2):
            a = _dot(xb, w1[...])
            b = _dot(xb, w3[...])
            hid = a * jax.nn.sigmoid(a) * b
            return _dot(hid.astype(BF16), w2[...])

        y = (pa / den) * ffn(w1a, w3a, w2a) + (pb / den) * ffn(w1b, w3b, w2b)
        _store_row_tiled(y_ref, y)

    @pl.when(jnp.logical_not(used))
    def _():
        y_ref[...] = jnp.zeros(y_ref.shape, F32)


def _expert_ffn(layer, blk_ea, blk_eb, nused, xbuf, w1, w3, w2, wr_pad):
    n_blocks = xbuf.shape[0] // (BM * ROW_TILE)
    used_blk = lambda i, ea, eb, nu: (jnp.minimum(i, nu[0] - 1), 0)
    up = (D_MODEL, D_EXPERT)
    down = (D_EXPERT, D_MODEL)
    expert_slot = lambda dtype: [pltpu.VMEM(up, dtype), pltpu.VMEM(up, dtype), pltpu.VMEM(down, dtype)]
    grid_spec = pltpu.PrefetchScalarGridSpec(
        num_scalar_prefetch=3,
        grid=(n_blocks,),
        in_specs=[
            pl.BlockSpec((BM * ROW_TILE, LANES), used_blk),
            pl.BlockSpec(memory_space=pl.ANY),
            pl.BlockSpec(memory_space=pl.ANY),
            pl.BlockSpec(memory_space=pl.ANY),
            pl.BlockSpec((D_MODEL, LANES), lambda i, ea, eb, nu: (0, 0)),
        ],
        out_specs=pl.BlockSpec((BM * ROW_TILE, LANES), lambda i, ea, eb, nu: (i, 0)),
        scratch_shapes=(expert_slot(BF16) + expert_slot(BF16) + expert_slot(F32) + expert_slot(F32)
                        + [pltpu.SemaphoreType.DMA((2,))]),
    )
    return pl.pallas_call(
        functools.partial(_expert_kernel, layer=layer),
        grid_spec=grid_spec,
        out_shape=jax.ShapeDtypeStruct(xbuf.shape, F32),
        compiler_params=pltpu.CompilerParams(
            dimension_semantics=("arbitrary",), vmem_limit_bytes=VMEM_LIMIT),
        name="expert_ffn",
    )(blk_ea, blk_eb, nused, xbuf, w1, w3, w2, wr_pad)


def _dispatch_tables(cnt, nb):
    counts = cnt[:N_CLASSES, 0].astype(I32)
    padded = ((counts + BM - 1) // BM) * BM
    pend = jnp.cumsum(padded).astype(I32)
    pstart = pend - padded
    nused = pend[-1:] // BM
    step = jnp.minimum(jnp.arange(nb, dtype=I32), nused - 1)
    c = jnp.sum((step[:, None] * BM >= pend[None, :N_CLASSES - 1]).astype(I32), axis=1)
    g = c // PAIRS_PER_GROUP
    p = c - g * PAIRS_PER_GROUP
    ge3 = (p >= 3).astype(I32)
    ge5 = (p >= 5).astype(I32)
    blk_ea = g * EXPERTS_PER_GROUP + ge3 + ge5
    blk_eb = g * EXPERTS_PER_GROUP + p + 1 - 2 * ge3 - ge5
    return pstart, blk_ea, blk_eb, nused


def _final_kernel(scur_ref, snext_ref, h1p_ref, y_hbm, g2_ref, b2_ref, o_ref, ybuf, gsem):
    i = pl.program_id(0)
    y = _prefetch_expert_rows(i, pl.num_programs(0), scur_ref, snext_ref, y_hbm, ybuf, gsem)
    o_ref[...] = _layer_norm(ALPHA * _load_row_tiled(h1p_ref, TS) + y, g2_ref[0], b2_ref[0])


def _final_call(slot_prev, h1t_prev, y_prev, g2, b2):
    t = h1t_prev.shape[0] // ROW_TILE
    nt = t // TS
    last = DEPTH - 1
    return pl.pallas_call(
        _final_kernel,
        grid=(nt,),
        in_specs=[
            pl.BlockSpec((1, 1, TS), lambda i: (i, 0, 0), memory_space=pltpu.SMEM),
            pl.BlockSpec((1, 1, TS), lambda i: (jnp.minimum(i + 1, nt - 1), 0, 0),
                         memory_space=pltpu.SMEM),
            pl.BlockSpec((TS * ROW_TILE, LANES), lambda i: (i, 0)),
            pl.BlockSpec(memory_space=pl.ANY),
            pl.BlockSpec((1, 1, D_MODEL), lambda i: (last, 0, 0)),
            pl.BlockSpec((1, 1, D_MODEL), lambda i: (last, 0, 0)),
        ],
        out_specs=pl.BlockSpec((TS, D_MODEL), lambda i: (i, 0)),
        out_shape=jax.ShapeDtypeStruct((t, D_MODEL), F32),
        scratch_shapes=[pltpu.VMEM((2, TS * ROW_TILE, LANES), F32), pltpu.SemaphoreType.DMA((2,))],
        compiler_params=pltpu.CompilerParams(
            dimension_semantics=("arbitrary",), vmem_limit_bytes=VMEM_LIMIT),
        name="final_combine_ln",
    )(slot_prev, slot_prev, h1t_prev, y_prev, g2, b2)


def kernel(x, ln_in_g, ln_in_b, w_in, pool_w, pool_scale, conv_w, conv_b, w_out, ln1_g, ln1_b,
           w_router, router_bias, exp_w1, exp_w3, exp_w2, ln2_g, ln2_b):
    bsz, seq, d = x.shape
    t = bsz * seq
    nt = t // TS
    n_rows = t + N_CLASSES * BM
    nb = n_rows // BM

    per_layer_row = lambda v: v.reshape(DEPTH, 1, -1)
    tri = (lax.broadcasted_iota(I32, (TS, TS), 0)
           < lax.broadcasted_iota(I32, (TS, TS), 1)).astype(BF16)
    wr_t = w_router.T
    wr_pad = jnp.pad(w_router, ((0, 0), (0, LANES - N_EXPERTS)))
    rbias = router_bias.reshape(N_EXPERTS, 1).astype(F32)
    pool_scale3, conv_b3 = per_layer_row(pool_scale), per_layer_row(conv_b)
    ln1_g3, ln1_b3 = per_layer_row(ln1_g), per_layer_row(ln1_b)
    ln2_g3, ln2_b3 = per_layer_row(ln2_g), per_layer_row(ln2_b)

    prev = (x.reshape(t, d), ln_in_g.reshape(1, -1), ln_in_b.reshape(1, -1))
    xbuf = jnp.zeros((n_rows * ROW_TILE, LANES), F32)
    for l in range(DEPTH):
        h1t, route, cnt = _layer_call(l, bsz, seq, prev, w_in, pool_w, pool_scale3, conv_w, conv_b3,
                                          w_out, ln1_g3, ln1_b3, wr_t, rbias, tri)
        pstart, blk_ea, blk_eb, nused = _dispatch_tables(cnt, nb)
        slot3 = _slots(pstart, route).reshape(nt, 1, TS)
        xbuf = _scatter_rows(slot3, h1t, xbuf)
        ybuf = _expert_ffn(l, blk_ea, blk_eb, nused, xbuf, exp_w1, exp_w3, exp_w2, wr_pad)
        prev = (slot3, h1t, ybuf, ln2_g3, ln2_b3)
    out = _final_call(*prev)
    return out.reshape(bsz, seq, d)
```

```python
import functools

import jax
import jax.numpy as jnp
from jax import lax
from jax.experimental import pallas as pl
from jax.experimental.pallas import tpu as pltpu

D_MODEL = 1024
DEPTH = 4
D_POOL = 512
POOL_WINDOWS = (2, 4, 8, 16)
POOL_GROUP = 128
D_CONV = 512
CONV_WIDTH = 3
D_PROJ = D_POOL + 3 * D_CONV
N_EXPERTS = 16
N_EXPERT_GROUPS = 4
EXPERTS_PER_GROUP = 4
PAIRS_PER_GROUP = 6
N_CLASSES = N_EXPERT_GROUPS * PAIRS_PER_GROUP
CLASS_ROWS = 32
D_EXPERT = 512
ALPHA = float((2 * DEPTH) ** 0.25)
LN_EPS = 1e-5

F32 = jnp.float32
BF16 = jnp.bfloat16
I32 = jnp.int32

LANES = 128
ROW_TILE = D_MODEL // LANES
TS = 512
SUBLANES = 8
V_HALO = 32
Z_HALO = 8
BM = 256
W_STAGE_ROWS = 256
DMA_QUEUES = 2
VMEM_LIMIT = 56 * 1024 * 1024


def _layer_norm(x, g, b):
    mu = jnp.mean(x, axis=-1, keepdims=True)
    xc = x - mu
    var = jnp.mean(xc * xc, axis=-1, keepdims=True)
    return xc * lax.rsqrt(var + LN_EPS) * g + b


def _dot(a, b):
    return jnp.dot(a, b, preferred_element_type=F32)


def _store_row_tiled(ref, value):
    n = value.shape[0]
    for j in range(ROW_TILE):
        ref[pl.ds(j, n, stride=ROW_TILE), :] = value[:, j * LANES:(j + 1) * LANES]


def _load_row_tiled(ref, n):
    return jnp.concatenate([ref[pl.ds(j, n, stride=ROW_TILE), :] for j in range(ROW_TILE)], axis=1)


def _row(ref, first_sublane):
    return ref.at[pl.ds(first_sublane, ROW_TILE)]


def _route(sel):
    rows = [sel[e:e + 1, :] for e in range(N_EXPERTS)]
    gscore = []
    for g in range(N_EXPERT_GROUPS):
        a, b, c, d = rows[4 * g:4 * g + 4]
        hi1, lo1 = jnp.maximum(a, b), jnp.minimum(a, b)
        hi2, lo2 = jnp.maximum(c, d), jnp.minimum(c, d)
        m1 = jnp.maximum(hi1, hi2)
        m2 = jnp.maximum(jnp.minimum(hi1, hi2), jnp.maximum(lo1, lo2))
        gscore.append(m1 + m2)
    best = gscore[0]
    gidx = jnp.zeros(best.shape, I32)
    for g in range(1, N_EXPERT_GROUPS):
        upd = gscore[g] > best
        best = jnp.where(upd, gscore[g], best)
        gidx = jnp.where(upd, g, gidx)
    neg = jnp.full(best.shape, -jnp.inf, F32)
    masked = [jnp.where(gidx == (e // EXPERTS_PER_GROUP), rows[e], neg) for e in range(N_EXPERTS)]

    def top1(vals):
        bv = neg
        bi = jnp.zeros(best.shape, I32)
        for e in range(N_EXPERTS):
            upd = vals[e] > bv
            bv = jnp.where(upd, vals[e], bv)
            bi = jnp.where(upd, e, bi)
        return bi

    i1 = top1(masked)
    i2 = top1([jnp.where(i1 == e, neg, masked[e]) for e in range(N_EXPERTS)])
    return gidx, i1, i2


def _pair_class(gidx, i1, i2):
    a = jnp.minimum(i1, i2) - gidx * EXPERTS_PER_GROUP
    b = jnp.maximum(i1, i2) - gidx * EXPERTS_PER_GROUP
    base = jnp.where(a == 0, 0, jnp.where(a == 1, 2, 3))
    return gidx * PAIRS_PER_GROUP + base + b - 1


def _prefetch_expert_rows(i, nt, scur_ref, snext_ref, y_hbm, ybuf, gsem):
    slot = i % 2

    def gather(slot_ref, to_slot, t, t_sublane):
        src = _row(y_hbm, pl.multiple_of(slot_ref[0, 0, t], ROW_TILE))
        return pltpu.make_async_copy(src, _row(ybuf.at[to_slot], t_sublane), gsem.at[to_slot])

    @pl.when(i == 0)
    def _():
        def body(t, carry):
            gather(scur_ref, 0, t, pl.multiple_of(t * ROW_TILE, ROW_TILE)).start()
            return carry
        lax.fori_loop(0, TS, body, 0, unroll=8)

    @pl.when(i + 1 < nt)
    def _():
        for t in range(TS):
            gather(snext_ref, 1 - slot, t, t * ROW_TILE).start(priority=t % DMA_QUEUES)

    pltpu.make_async_copy(y_hbm.at[pl.ds(0, TS * ROW_TILE)], ybuf.at[slot], gsem.at[slot]).wait()
    return _load_row_tiled(ybuf.at[slot], TS)


def _layer_kernel(*refs, layer, tiles_per_seq):
    if layer == 0:
        x_ref, gin_ref, bin_ref = refs[:3]
        rest = refs[3:]
    else:
        scur_ref, snext_ref, h1p_ref, y_hbm, g2_ref, b2_ref = refs[:6]
        rest = refs[6:]
    (win_hbm, pw_ref, ps_ref, cw_ref, cb_ref, wout_hbm, g_ref, b_ref, wr_ref, rb_ref, tri_ref,
     h1t_ref, route_ref, cnt_ref,
     vbuf, pwork, zbuf, mixbuf, carry, win_ref, wout_ref, stage, wsem, ybuf, gsem) = rest

    i = pl.program_id(0)
    nt = pl.num_programs(0)

    @pl.when(i == 0)
    def _():
        chunks = []
        for src, dst, width in ((win_hbm, win_ref, D_PROJ), (wout_hbm, wout_ref, D_MODEL)):
            for c in range(D_MODEL // W_STAGE_ROWS):
                chunks.append((src, dst, width, pl.ds(c * W_STAGE_ROWS, W_STAGE_ROWS)))

        def fetch(k):
            src, _, width, rows = chunks[k]
            return pltpu.make_async_copy(src.at[layer, rows], stage.at[k % 2, :, pl.ds(0, width)],
                                         wsem.at[k % 2])

        fetch(0).start()
        for k, (_, dst, width, rows) in enumerate(chunks):
            if k + 1 < len(chunks):
                fetch(k + 1).start()
            fetch(k).wait()
            dst[rows, :] = stage[k % 2, :, 0:width].astype(BF16)
        carry[...] = jnp.zeros(carry.shape, F32)

    if layer == 0:
        h = _layer_norm(x_ref[...], gin_ref[...], bin_ref[...])
    else:
        y = _prefetch_expert_rows(i, nt, scur_ref, snext_ref, y_hbm, ybuf, gsem)
        h = _layer_norm(ALPHA * _load_row_tiled(h1p_ref, TS) + y, g2_ref[0], b2_ref[0])

    si = i % tiles_per_seq

    @pl.when(si == 0)
    def _():
        vbuf[0:V_HALO, :] = jnp.zeros((V_HALO, D_POOL), F32)
        zbuf[0:Z_HALO, :] = jnp.zeros((Z_HALO, D_CONV), F32)

    hb = h.astype(BF16)
    vbuf[V_HALO:V_HALO + TS, :] = _dot(hb, win_ref[:, 0:D_POOL])
    gate_b = _dot(hb, win_ref[:, D_POOL:D_POOL + D_CONV])
    gate_c = _dot(hb, win_ref[:, D_POOL + D_CONV:D_POOL + 2 * D_CONV])
    u = _dot(hb, win_ref[:, D_POOL + 2 * D_CONV:D_PROJ])
    z = gate_c * u
    zbuf[Z_HALO:Z_HALO + TS, :] = z

    src = vbuf
    for level in range(1, len(POOL_WINDOWS)):
        dst = pwork.at[(level - 1) % 2]
        lo = SUBLANES * level
        shift = 2 ** (level - 1)
        n = TS + V_HALO - lo
        cols = pl.ds(level * POOL_GROUP, D_POOL - level * POOL_GROUP)
        dst[pl.ds(lo, n), cols] = src[pl.ds(lo, n), cols] + src[pl.ds(lo - shift, n), cols]
        src = dst
    tpos = si * TS + lax.broadcasted_iota(I32, (TS, 1), 0)
    for g, win in enumerate(POOL_WINDOWS):
        cols = pl.ds(g * POOL_GROUP, POOL_GROUP)
        cur = vbuf[pl.ds(V_HALO, TS), cols]
        half = vbuf if g == 0 else pwork.at[(g - 1) % 2]
        acc = half[pl.ds(V_HALO, TS), cols] + half[pl.ds(V_HALO - win // 2, TS), cols]
        denom = jnp.minimum(tpos + 1, win).astype(F32)
        pooled = acc / denom - cur
        mixed = _dot(pooled.astype(BF16), pw_ref[0, g].astype(BF16)) * ps_ref[0, :, cols]
        mixbuf[:, cols] = mixed.astype(BF16)

    yc = (cw_ref[0, 2:3, :] * z
          + cw_ref[0, 1:2, :] * zbuf[pl.ds(Z_HALO - 1, TS), :]
          + cw_ref[0, 0:1, :] * zbuf[pl.ds(Z_HALO - 2, TS), :])
    mixbuf[:, D_POOL:D_POOL + D_CONV] = (gate_b * (yc + cb_ref[0])).astype(BF16)

    vbuf[0:V_HALO, :] = vbuf[TS:TS + V_HALO, :]
    zbuf[0:Z_HALO, :] = zbuf[TS:TS + Z_HALO, :]

    mix = _dot(mixbuf[...], wout_ref[...])
    h1 = _layer_norm(ALPHA * h + mix, g_ref[0], b_ref[0])
    _store_row_tiled(h1t_ref, h1)

    logits = lax.dot_general(wr_ref[...].astype(BF16), h1.astype(BF16), (((1,), (1,)), ((), ())),
                             preferred_element_type=F32)
    mx = jnp.max(logits, axis=0, keepdims=True)
    ex = jnp.exp(logits - mx)
    probs = ex / jnp.sum(ex, axis=0, keepdims=True)
    gidx, i1, i2 = _route(probs + rb_ref[...])
    cls = _pair_class(gidx, i1, i2)

    ciota = lax.broadcasted_iota(I32, (CLASS_ROWS, TS), 0)
    onehot = jnp.where(ciota == cls, 1.0, 0.0).astype(F32)
    before = _dot(onehot.astype(BF16), tri_ref[...])
    total = jnp.sum(onehot, axis=1, keepdims=True)
    c = carry[:, 0:1]
    rank = jnp.sum(onehot * (c + before), axis=0, keepdims=True)
    route_ref[0:1, :] = cls
    route_ref[1:2, :] = rank.astype(I32)
    newc = jnp.broadcast_to(c + total, carry.shape)
    carry[...] = newc
    cnt_ref[...] = newc


def _layer_call(layer, bsz, seq, prev, w_in, pool_w, pool_scale, conv_w, conv_b, w_out, ln_g, ln_b,
                wr_t, rbias, tri):
    t = bsz * seq
    nt = t // TS
    const2 = lambda i: (0, 0)
    lsel3 = lambda i: (layer, 0, 0)
    lsel4 = lambda i: (layer, 0, 0, 0)
    tile = lambda i: (i, 0)
    lane_tile = lambda i: (0, i)
    if layer == 0:
        x, gin, bin_ = prev
        head_args = (x, gin, bin_)
        head_specs = [
            pl.BlockSpec((TS, D_MODEL), tile),
            pl.BlockSpec((1, D_MODEL), const2),
            pl.BlockSpec((1, D_MODEL), const2),
        ]
    else:
        slot_prev, h1t_prev, y_prev, g2, b2 = prev
        head_args = (slot_prev, slot_prev, h1t_prev, y_prev, g2, b2)
        head_specs = [
            pl.BlockSpec((1, 1, TS), lambda i: (i, 0, 0), memory_space=pltpu.SMEM),
            pl.BlockSpec((1, 1, TS), lambda i: (jnp.minimum(i + 1, nt - 1), 0, 0),
                         memory_space=pltpu.SMEM),
            pl.BlockSpec((TS * ROW_TILE, LANES), tile),
            pl.BlockSpec(memory_space=pl.ANY),
            pl.BlockSpec((1, 1, D_MODEL), lambda i: (layer - 1, 0, 0)),
            pl.BlockSpec((1, 1, D_MODEL), lambda i: (layer - 1, 0, 0)),
        ]
    return pl.pallas_call(
        functools.partial(_layer_kernel, layer=layer, tiles_per_seq=seq // TS),
        grid=(nt,),
        in_specs=head_specs + [
            pl.BlockSpec(memory_space=pl.ANY),
            pl.BlockSpec((1, len(POOL_WINDOWS), POOL_GROUP, POOL_GROUP), lsel4),
            pl.BlockSpec((1, 1, D_POOL), lsel3),
            pl.BlockSpec((1, CONV_WIDTH, D_CONV), lsel3),
            pl.BlockSpec((1, 1, D_CONV), lsel3),
            pl.BlockSpec(memory_space=pl.ANY),
            pl.BlockSpec((1, 1, D_MODEL), lsel3),
            pl.BlockSpec((1, 1, D_MODEL), lsel3),
            pl.BlockSpec((N_EXPERTS, D_MODEL), const2),
            pl.BlockSpec((N_EXPERTS, 1), const2),
            pl.BlockSpec((TS, TS), const2),
        ],
        out_specs=[
            pl.BlockSpec((TS * ROW_TILE, LANES), tile),
            pl.BlockSpec((2, TS), lane_tile),
            pl.BlockSpec((CLASS_ROWS, LANES), const2),
        ],
        out_shape=[
            jax.ShapeDtypeStruct((t * ROW_TILE, LANES), F32),
            jax.ShapeDtypeStruct((2, t), I32),
            jax.ShapeDtypeStruct((CLASS_ROWS, LANES), F32),
        ],
        scratch_shapes=[
            pltpu.VMEM((TS + V_HALO, D_POOL), F32),
            pltpu.VMEM((2, TS + V_HALO, D_POOL), F32),
            pltpu.VMEM((TS + Z_HALO, D_CONV), F32),
            pltpu.VMEM((TS, D_MODEL), BF16),
            pltpu.VMEM((CLASS_ROWS, LANES), F32),
            pltpu.VMEM((D_MODEL, D_PROJ), BF16),
            pltpu.VMEM((D_MODEL, D_MODEL), BF16),
            pltpu.VMEM((2, W_STAGE_ROWS, D_PROJ), F32),
            pltpu.SemaphoreType.DMA((2,)),
            pltpu.VMEM((2, TS * ROW_TILE, LANES), F32),
            pltpu.SemaphoreType.DMA((2,)),
        ],
        compiler_params=pltpu.CompilerParams(
            dimension_semantics=("arbitrary",), vmem_limit_bytes=VMEM_LIMIT),
        name="layer_mix_route",
    )(*head_args, w_in, pool_w, pool_scale, conv_w, conv_b, w_out, ln_g, ln_b, wr_t, rbias, tri)


def _slot_kernel(pstart_ref, route_ref, slot_ref):
    cls = route_ref[0:1, :]
    start = jnp.zeros(cls.shape, I32)
    for k in range(N_CLASSES):
        start = jnp.where(cls == k, pstart_ref[k], start)
    slot_ref[...] = (start + route_ref[1:2, :]) * ROW_TILE


def _slots(pstart, route):
    t = route.shape[1]
    grid_spec = pltpu.PrefetchScalarGridSpec(
        num_scalar_prefetch=1,
        grid=(1,),
        in_specs=[pl.BlockSpec((2, t), lambda i, ps: (0, 0))],
        out_specs=pl.BlockSpec((1, t), lambda i, ps: (0, 0)),
    )
    return pl.pallas_call(
        _slot_kernel,
        grid_spec=grid_spec,
        out_shape=jax.ShapeDtypeStruct((1, t), I32),
        name="token_slots",
    )(pstart, route)


def _scatter_kernel(slot_ref, h1t_ref, xin_ref, xout_ref, sem):
    del xin_ref
    for t in range(TS):
        dst = _row(xout_ref, pl.multiple_of(slot_ref[0, 0, t], ROW_TILE))
        pltpu.make_async_copy(_row(h1t_ref, t * ROW_TILE), dst, sem).start(priority=t % DMA_QUEUES)
    pltpu.make_async_copy(h1t_ref, xout_ref.at[pl.ds(0, TS * ROW_TILE)], sem).wait()


def _scatter_rows(slot3, h1t, xbuf):
    nt = slot3.shape[0]
    return pl.pallas_call(
        _scatter_kernel,
        grid=(nt,),
        in_specs=[
            pl.BlockSpec((1, 1, TS), lambda i: (i, 0, 0), memory_space=pltpu.SMEM),
            pl.BlockSpec((TS * ROW_TILE, LANES), lambda i: (i, 0)),
            pl.BlockSpec(memory_space=pl.ANY),
        ],
        out_specs=pl.BlockSpec(memory_space=pl.ANY),
        out_shape=jax.ShapeDtypeStruct(xbuf.shape, xbuf.dtype),
        scratch_shapes=[pltpu.SemaphoreType.DMA(())],
        input_output_aliases={2: 0},
        compiler_params=pltpu.CompilerParams(dimension_semantics=("arbitrary",)),
        name="scatter_rows",
    )(slot3, h1t, xbuf)


def _expert_kernel(ea_ref, eb_ref, nu_ref, x_ref, w1_hbm, w3_hbm, w2_hbm, wr_ref, y_ref,
                   wup, w2a, w2b, s1a, s3a, s2a, s1b, s3b, s2b, wsem, *, layer):
    i = pl.program_id(0)
    nb = pl.num_programs(0)
    nu = nu_ref[0]
    used = i < nu
    ea = ea_ref[i]
    eb = eb_ref[i]

    def swap_weights(e_ref, slot, live, staged):
        e = e_ref[i]

        def fetch(expert):
            return [pltpu.make_async_copy(src.at[layer, expert], dst, wsem.at[slot])
                    for src, dst in zip((w1_hbm, w3_hbm, w2_hbm), staged)]

        @pl.when(used & (i == 0))
        def _():
            for cp in fetch(e):
                cp.start()

        @pl.when(used & ((i == 0) | (e != e_ref[jnp.maximum(i - 1, 0)])))
        def _():
            for cp in fetch(e):
                cp.wait()
            for dst, src in zip(live, staged):
                dst[...] = src[...].astype(BF16)
            nxt = lax.while_loop(lambda j: (j < nu) & (e_ref[jnp.minimum(j, nb - 1)] == e),
                                 lambda j: j + 1, i + 1)

            @pl.when(nxt < nu)
            def _():
                for cp in fetch(e_ref[jnp.minimum(nxt, nb - 1)]):
                    cp.start()

    def up_cols(k):
        return wup.at[:, pl.ds(k * D_EXPERT, D_EXPERT)]

    swap_weights(ea_ref, 0, (up_cols(0), up_cols(2), w2a), (s1a, s3a, s2a))
    swap_weights(eb_ref, 1, (up_cols(1), up_cols(3), w2b), (s1b, s3b, s2b))

    @pl.when(used)
    def _():
        xb = _load_row_tiled(x_ref, BM).astype(BF16)

        logits = _dot(xb, wr_ref[...].astype(BF16))
        lane = lax.broadcasted_iota(I32, logits.shape, 1)
        logits = jnp.where(lane < N_EXPERTS, logits, -jnp.inf)
        ex = jnp.exp(logits - jnp.max(logits, axis=1, keepdims=True))
        probs = ex / jnp.sum(ex, axis=1, keepdims=True)
        pa = jnp.sum(jnp.where(lane == ea, probs, 0.0), axis=1, keepdims=True)
        pb = jnp.sum(jnp.where(lane == eb, probs, 0.0), axis=1, keepdims=True)
        den = pa + pb

        up = _dot(xb, wup[...])
        a = up[:, 0:2 * D_EXPERT]
        hid = (a * jax.nn.sigmoid(a) * up[:, 2 * D_EXPERT:4 * D_EXPERT]).astype(BF16)
        y = ((pa / den) * _dot(hid[:, 0:D_EXPERT], w2a[...])
             + (pb / den) * _dot(hid[:, D_EXPERT:2 * D_EXPERT], w2b[...]))
        _store_row_tiled(y_ref, y)

    @pl.when(jnp.logical_not(used))
    def _():
        y_ref[...] = jnp.zeros(y_ref.shape, F32)


def _expert_ffn(layer, blk_ea, blk_eb, nused, xbuf, w1, w3, w2, wr_pad):
    n_blocks = xbuf.shape[0] // (BM * ROW_TILE)
    used_blk = lambda i, ea, eb, nu: (jnp.minimum(i, nu[0] - 1), 0)
    up = (D_MODEL, D_EXPERT)
    down = (D_EXPERT, D_MODEL)
    expert_slot = lambda dtype: [pltpu.VMEM(up, dtype), pltpu.VMEM(up, dtype), pltpu.VMEM(down, dtype)]
    grid_spec = pltpu.PrefetchScalarGridSpec(
        num_scalar_prefetch=3,
        grid=(n_blocks,),
        in_specs=[
            pl.BlockSpec((BM * ROW_TILE, LANES), used_blk),
            pl.BlockSpec(memory_space=pl.ANY),
            pl.BlockSpec(memory_space=pl.ANY),
            pl.BlockSpec(memory_space=pl.ANY),
            pl.BlockSpec((D_MODEL, LANES), lambda i, ea, eb, nu: (0, 0)),
        ],
        out_specs=pl.BlockSpec((BM * ROW_TILE, LANES), lambda i, ea, eb, nu: (i, 0)),
        scratch_shapes=([pltpu.VMEM((D_MODEL, 4 * D_EXPERT), BF16), pltpu.VMEM(down, BF16),
                         pltpu.VMEM(down, BF16)] + expert_slot(F32) + expert_slot(F32)
                        + [pltpu.SemaphoreType.DMA((2,))]),
    )
    return pl.pallas_call(
        functools.partial(_expert_kernel, layer=layer),
        grid_spec=grid_spec,
        out_shape=jax.ShapeDtypeStruct(xbuf.shape, F32),
        compiler_params=pltpu.CompilerParams(
            dimension_semantics=("arbitrary",), vmem_limit_bytes=VMEM_LIMIT),
        name="expert_ffn",
    )(blk_ea, blk_eb, nused, xbuf, w1, w3, w2, wr_pad)


def _dispatch_tables(cnt, nb):
    counts = cnt[:N_CLASSES, 0].astype(I32)
    padded = ((counts + BM - 1) // BM) * BM
    pend = jnp.cumsum(padded).astype(I32)
    pstart = pend - padded
    nused = pend[-1:] // BM
    step = jnp.minimum(jnp.arange(nb, dtype=I32), nused - 1)
    c = jnp.sum((step[:, None] * BM >= pend[None, :N_CLASSES - 1]).astype(I32), axis=1)
    g = c // PAIRS_PER_GROUP
    p = c - g * PAIRS_PER_GROUP
    ge3 = (p >= 3).astype(I32)
    ge5 = (p >= 5).astype(I32)
    blk_ea = g * EXPERTS_PER_GROUP + ge3 + ge5
    blk_eb = g * EXPERTS_PER_GROUP + p + 1 - 2 * ge3 - ge5
    return pstart, blk_ea, blk_eb, nused


def _final_kernel(scur_ref, snext_ref, h1p_ref, y_hbm, g2_ref, b2_ref, o_ref, ybuf, gsem):
    i = pl.program_id(0)
    y = _prefetch_expert_rows(i, pl.num_programs(0), scur_ref, snext_ref, y_hbm, ybuf, gsem)
    o_ref[...] = _layer_norm(ALPHA * _load_row_tiled(h1p_ref, TS) + y, g2_ref[0], b2_ref[0])


def _final_call(slot_prev, h1t_prev, y_prev, g2, b2):
    t = h1t_prev.shape[0] // ROW_TILE
    nt = t // TS
    last = DEPTH - 1
    return pl.pallas_call(
        _final_kernel,
        grid=(nt,),
        in_specs=[
            pl.BlockSpec((1, 1, TS), lambda i: (i, 0, 0), memory_space=pltpu.SMEM),
            pl.BlockSpec((1, 1, TS), lambda i: (jnp.minimum(i + 1, nt - 1), 0, 0),
                         memory_space=pltpu.SMEM),
            pl.BlockSpec((TS * ROW_TILE, LANES), lambda i: (i, 0)),
            pl.BlockSpec(memory_space=pl.ANY),
            pl.BlockSpec((1, 1, D_MODEL), lambda i: (last, 0, 0)),
            pl.BlockSpec((1, 1, D_MODEL), lambda i: (last, 0, 0)),
        ],
        out_specs=pl.BlockSpec((TS, D_MODEL), lambda i: (i, 0)),
        out_shape=jax.ShapeDtypeStruct((t, D_MODEL), F32),
        scratch_shapes=[pltpu.VMEM((2, TS * ROW_TILE, LANES), F32), pltpu.SemaphoreType.DMA((2,))],
        compiler_params=pltpu.CompilerParams(
            dimension_semantics=("arbitrary",), vmem_limit_bytes=VMEM_LIMIT),
        name="final_combine_ln",
    )(slot_prev, slot_prev, h1t_prev, y_prev, g2, b2)


def kernel(x, ln_in_g, ln_in_b, w_in, pool_w, pool_scale, conv_w, conv_b, w_out, ln1_g, ln1_b,
           w_router, router_bias, exp_w1, exp_w3, exp_w2, ln2_g, ln2_b):
    bsz, seq, d = x.shape
    t = bsz * seq
    nt = t // TS
    n_rows = t + N_CLASSES * BM
    nb = n_rows // BM

    per_layer_row = lambda v: v.reshape(DEPTH, 1, -1)
    tri = (lax.broadcasted_iota(I32, (TS, TS), 0)
           < lax.broadcasted_iota(I32, (TS, TS), 1)).astype(BF16)
    wr_t = w_router.T
    wr_pad = jnp.pad(w_router, ((0, 0), (0, LANES - N_EXPERTS)))
    rbias = router_bias.reshape(N_EXPERTS, 1).astype(F32)
    pool_scale3, conv_b3 = per_layer_row(pool_scale), per_layer_row(conv_b)
    ln1_g3, ln1_b3 = per_layer_row(ln1_g), per_layer_row(ln1_b)
    ln2_g3, ln2_b3 = per_layer_row(ln2_g), per_layer_row(ln2_b)

    prev = (x.reshape(t, d), ln_in_g.reshape(1, -1), ln_in_b.reshape(1, -1))
    xbuf = jnp.zeros((n_rows * ROW_TILE, LANES), F32)
    for l in range(DEPTH):
        h1t, route, cnt = _layer_call(l, bsz, seq, prev, w_in, pool_w, pool_scale3, conv_w, conv_b3,
                                          w_out, ln1_g3, ln1_b3, wr_t, rbias, tri)
        pstart, blk_ea, blk_eb, nused = _dispatch_tables(cnt, nb)
        slot3 = _slots(pstart, route).reshape(nt, 1, TS)
        xbuf = _scatter_rows(slot3, h1t, xbuf)
        ybuf = _expert_ffn(l, blk_ea, blk_eb, nused, xbuf, exp_w1, exp_w3, exp_w2, wr_pad)
        prev = (slot3, h1t, ybuf, ln2_g3, ln2_b3)
    out = _final_call(*prev)
    return out.reshape(bsz, seq, d)
```

```python
import functools

import jax
import jax.numpy as jnp
from jax import lax
from jax.experimental import pallas as pl
from jax.experimental.pallas import tpu as pltpu

D_MODEL = 1024
DEPTH = 4
D_POOL = 512
POOL_WINDOWS = (2, 4, 8, 16)
POOL_GROUP = 128
D_CONV = 512
CONV_WIDTH = 3
D_PROJ = D_POOL + 3 * D_CONV
N_EXPERTS = 16
N_EXPERT_GROUPS = 4
EXPERTS_PER_GROUP = 4
PAIRS_PER_GROUP = 6
N_CLASSES = N_EXPERT_GROUPS * PAIRS_PER_GROUP
CLASS_ROWS = 32
D_EXPERT = 512
ALPHA = float((2 * DEPTH) ** 0.25)
LN_EPS = 1e-5

F32 = jnp.float32
BF16 = jnp.bfloat16
I32 = jnp.int32

LANES = 128
ROW_TILE = D_MODEL // LANES
TS = 512
SUBLANES = 8
V_HALO = 32
Z_HALO = 8
BM = 256
W_STAGE_ROWS = 256
DMA_QUEUES = 2
VMEM_LIMIT = 56 * 1024 * 1024


def _layer_norm(x, g, b):
    mu = jnp.mean(x, axis=-1, keepdims=True)
    xc = x - mu
    var = jnp.mean(xc * xc, axis=-1, keepdims=True)
    return xc * lax.rsqrt(var + LN_EPS) * g + b


def _dot(a, b):
    return jnp.dot(a, b, preferred_element_type=F32)


def _store_row_tiled(ref, value):
    n = value.shape[0]
    for j in range(ROW_TILE):
        ref[pl.ds(j, n, stride=ROW_TILE), :] = value[:, j * LANES:(j + 1) * LANES]


def _load_row_tiled(ref, n):
    return jnp.concatenate([ref[pl.ds(j, n, stride=ROW_TILE), :] for j in range(ROW_TILE)], axis=1)


def _row(ref, first_sublane):
    return ref.at[pl.ds(first_sublane, ROW_TILE)]


def _route(sel):
    rows = [sel[e:e + 1, :] for e in range(N_EXPERTS)]
    gscore = []
    for g in range(N_EXPERT_GROUPS):
        a, b, c, d = rows[4 * g:4 * g + 4]
        hi1, lo1 = jnp.maximum(a, b), jnp.minimum(a, b)
        hi2, lo2 = jnp.maximum(c, d), jnp.minimum(c, d)
        m1 = jnp.maximum(hi1, hi2)
        m2 = jnp.maximum(jnp.minimum(hi1, hi2), jnp.maximum(lo1, lo2))
        gscore.append(m1 + m2)
    best = gscore[0]
    gidx = jnp.zeros(best.shape, I32)
    for g in range(1, N_EXPERT_GROUPS):
        upd = gscore[g] > best
        best = jnp.where(upd, gscore[g], best)
        gidx = jnp.where(upd, g, gidx)
    neg = jnp.full(best.shape, -jnp.inf, F32)
    masked = [jnp.where(gidx == (e // EXPERTS_PER_GROUP), rows[e], neg) for e in range(N_EXPERTS)]

    def top1(vals):
        bv = neg
        bi = jnp.zeros(best.shape, I32)
        for e in range(N_EXPERTS):
            upd = vals[e] > bv
            bv = jnp.where(upd, vals[e], bv)
            bi = jnp.where(upd, e, bi)
        return bi

    i1 = top1(masked)
    i2 = top1([jnp.where(i1 == e, neg, masked[e]) for e in range(N_EXPERTS)])
    return gidx, i1, i2


def _pair_class(gidx, i1, i2):
    a = jnp.minimum(i1, i2) - gidx * EXPERTS_PER_GROUP
    b = jnp.maximum(i1, i2) - gidx * EXPERTS_PER_GROUP
    base = jnp.where(a == 0, 0, jnp.where(a == 1, 2, 3))
    return gidx * PAIRS_PER_GROUP + base + b - 1


def _prefetch_expert_rows(i, nt, scur_ref, snext_ref, y_hbm, ybuf, gsem):
    slot = i % 2

    def gather(slot_ref, to_slot, t, t_sublane):
        src = _row(y_hbm, pl.multiple_of(slot_ref[0, 0, t], ROW_TILE))
        return pltpu.make_async_copy(src, _row(ybuf.at[to_slot], t_sublane), gsem.at[to_slot])

    @pl.when(i == 0)
    def _():
        def body(t, carry):
            gather(scur_ref, 0, t, pl.multiple_of(t * ROW_TILE, ROW_TILE)).start()
            return carry
        lax.fori_loop(0, TS, body, 0, unroll=8)

    @pl.when(i + 1 < nt)
    def _():
        for t in range(TS):
            gather(snext_ref, 1 - slot, t, t * ROW_TILE).start(priority=t % DMA_QUEUES)

    pltpu.make_async_copy(y_hbm.at[pl.ds(0, TS * ROW_TILE)], ybuf.at[slot], gsem.at[slot]).wait()
    return _load_row_tiled(ybuf.at[slot], TS)


def _layer_kernel(*refs, layer, tiles_per_seq):
    if layer == 0:
        x_ref, gin_ref, bin_ref = refs[:3]
        rest = refs[3:]
    else:
        scur_ref, snext_ref, h1p_ref, y_hbm, g2_ref, b2_ref = refs[:6]
        rest = refs[6:]
    (win_hbm, pw_ref, ps_ref, cw_ref, cb_ref, wout_hbm, g_ref, b_ref, wr_ref, rb_ref, tri_ref,
     h1t_ref, route_ref, cnt_ref,
     vbuf, pwork, zbuf, mixbuf, carry, win_ref, wout_ref, stage, wsem, ybuf, gsem) = rest

    i = pl.program_id(0)
    nt = pl.num_programs(0)

    @pl.when(i == 0)
    def _():
        chunks = []
        for src, dst, width in ((win_hbm, win_ref, D_PROJ), (wout_hbm, wout_ref, D_MODEL)):
            for c in range(D_MODEL // W_STAGE_ROWS):
                chunks.append((src, dst, width, pl.ds(c * W_STAGE_ROWS, W_STAGE_ROWS)))

        def fetch(k):
            src, _, width, rows = chunks[k]
            return pltpu.make_async_copy(src.at[layer, rows], stage.at[k % 2, :, pl.ds(0, width)],
                                         wsem.at[k % 2])

        fetch(0).start()
        for k, (_, dst, width, rows) in enumerate(chunks):
            if k + 1 < len(chunks):
                fetch(k + 1).start()
            fetch(k).wait()
            dst[rows, :] = stage[k % 2, :, 0:width].astype(BF16)
        carry[...] = jnp.zeros(carry.shape, F32)

    if layer == 0:
        h = _layer_norm(x_ref[...], gin_ref[...], bin_ref[...])
    else:
        y = _prefetch_expert_rows(i, nt, scur_ref, snext_ref, y_hbm, ybuf, gsem)
        h = _layer_norm(ALPHA * _load_row_tiled(h1p_ref, TS) + y, g2_ref[0], b2_ref[0])

    si = i % tiles_per_seq

    @pl.when(si == 0)
    def _():
        vbuf[0:V_HALO, :] = jnp.zeros((V_HALO, D_POOL), F32)
        zbuf[0:Z_HALO, :] = jnp.zeros((Z_HALO, D_CONV), F32)

    hb = h.astype(BF16)
    vbuf[V_HALO:V_HALO + TS, :] = _dot(hb, win_ref[:, 0:D_POOL])
    gate_b = _dot(hb, win_ref[:, D_POOL:D_POOL + D_CONV])
    gate_c = _dot(hb, win_ref[:, D_POOL + D_CONV:D_POOL + 2 * D_CONV])
    u = _dot(hb, win_ref[:, D_POOL + 2 * D_CONV:D_PROJ])
    z = gate_c * u
    zbuf[Z_HALO:Z_HALO + TS, :] = z

    src = vbuf
    for level in range(1, len(POOL_WINDOWS)):
        dst = pwork.at[(level - 1) % 2]
        lo = SUBLANES * level
        shift = 2 ** (level - 1)
        n = TS + V_HALO - lo
        cols = pl.ds(level * POOL_GROUP, D_POOL - level * POOL_GROUP)
        dst[pl.ds(lo, n), cols] = src[pl.ds(lo, n), cols] + src[pl.ds(lo - shift, n), cols]
        src = dst
    tpos = si * TS + lax.broadcasted_iota(I32, (TS, 1), 0)
    for g, win in enumerate(POOL_WINDOWS):
        cols = pl.ds(g * POOL_GROUP, POOL_GROUP)
        cur = vbuf[pl.ds(V_HALO, TS), cols]
        half = vbuf if g == 0 else pwork.at[(g - 1) % 2]
        acc = half[pl.ds(V_HALO, TS), cols] + half[pl.ds(V_HALO - win // 2, TS), cols]
        denom = jnp.minimum(tpos + 1, win).astype(F32)
        pooled = acc / denom - cur
        mixed = _dot(pooled.astype(BF16), pw_ref[0, g].astype(BF16)) * ps_ref[0, :, cols]
        mixbuf[:, cols] = mixed.astype(BF16)

    yc = (cw_ref[0, 2:3, :] * z
          + cw_ref[0, 1:2, :] * zbuf[pl.ds(Z_HALO - 1, TS), :]
          + cw_ref[0, 0:1, :] * zbuf[pl.ds(Z_HALO - 2, TS), :])
    mixbuf[:, D_POOL:D_POOL + D_CONV] = (gate_b * (yc + cb_ref[0])).astype(BF16)

    vbuf[0:V_HALO, :] = vbuf[TS:TS + V_HALO, :]
    zbuf[0:Z_HALO, :] = zbuf[TS:TS + Z_HALO, :]

    mix = _dot(mixbuf[...], wout_ref[...])
    h1 = _layer_norm(ALPHA * h + mix, g_ref[0], b_ref[0])
    _store_row_tiled(h1t_ref, h1)

    logits = lax.dot_general(wr_ref[...].astype(BF16), h1.astype(BF16), (((1,), (1,)), ((), ())),
                             preferred_element_type=F32)
    mx = jnp.max(logits, axis=0, keepdims=True)
    ex = jnp.exp(logits - mx)
    probs = ex / jnp.sum(ex, axis=0, keepdims=True)
    gidx, i1, i2 = _route(probs + rb_ref[...])
    cls = _pair_class(gidx, i1, i2)

    ciota = lax.broadcasted_iota(I32, (CLASS_ROWS, TS), 0)
    onehot = jnp.where(ciota == cls, 1.0, 0.0).astype(F32)
    before = _dot(onehot.astype(BF16), tri_ref[...])
    total = jnp.sum(onehot, axis=1, keepdims=True)
    c = carry[:, 0:1]
    rank = jnp.sum(onehot * (c + before), axis=0, keepdims=True)
    route_ref[0:1, :] = cls
    route_ref[1:2, :] = rank.astype(I32)
    newc = jnp.broadcast_to(c + total, carry.shape)
    carry[...] = newc
    cnt_ref[...] = newc


def _layer_call(layer, bsz, seq, prev, w_in, pool_w, pool_scale, conv_w, conv_b, w_out, ln_g, ln_b,
                wr_t, rbias, tri):
    t = bsz * seq
    nt = t // TS
    const2 = lambda i: (0, 0)
    lsel3 = lambda i: (layer, 0, 0)
    lsel4 = lambda i: (layer, 0, 0, 0)
    tile = lambda i: (i, 0)
    lane_tile = lambda i: (0, i)
    if layer == 0:
        x, gin, bin_ = prev
        head_args = (x, gin, bin_)
        head_specs = [
            pl.BlockSpec((TS, D_MODEL), tile),
            pl.BlockSpec((1, D_MODEL), const2),
            pl.BlockSpec((1, D_MODEL), const2),
        ]
    else:
        slot_prev, h1t_prev, y_prev, g2, b2 = prev
        head_args = (slot_prev, slot_prev, h1t_prev, y_prev, g2, b2)
        head_specs = [
            pl.BlockSpec((1, 1, TS), lambda i: (i, 0, 0), memory_space=pltpu.SMEM),
            pl.BlockSpec((1, 1, TS), lambda i: (jnp.minimum(i + 1, nt - 1), 0, 0),
                         memory_space=pltpu.SMEM),
            pl.BlockSpec((TS * ROW_TILE, LANES), tile),
            pl.BlockSpec(memory_space=pl.ANY),
            pl.BlockSpec((1, 1, D_MODEL), lambda i: (layer - 1, 0, 0)),
            pl.BlockSpec((1, 1, D_MODEL), lambda i: (layer - 1, 0, 0)),
        ]
    return pl.pallas_call(
        functools.partial(_layer_kernel, layer=layer, tiles_per_seq=seq // TS),
        grid=(nt,),
        in_specs=head_specs + [
            pl.BlockSpec(memory_space=pl.ANY),
            pl.BlockSpec((1, len(POOL_WINDOWS), POOL_GROUP, POOL_GROUP), lsel4),
            pl.BlockSpec((1, 1, D_POOL), lsel3),
            pl.BlockSpec((1, CONV_WIDTH, D_CONV), lsel3),
            pl.BlockSpec((1, 1, D_CONV), lsel3),
            pl.BlockSpec(memory_space=pl.ANY),
            pl.BlockSpec((1, 1, D_MODEL), lsel3),
            pl.BlockSpec((1, 1, D_MODEL), lsel3),
            pl.BlockSpec((N_EXPERTS, D_MODEL), const2),
            pl.BlockSpec((N_EXPERTS, 1), const2),
            pl.BlockSpec((TS, TS), const2),
        ],
        out_specs=[
            pl.BlockSpec((TS * ROW_TILE, LANES), tile),
            pl.BlockSpec((2, TS), lane_tile),
            pl.BlockSpec((CLASS_ROWS, LANES), const2),
        ],
        out_shape=[
            jax.ShapeDtypeStruct((t * ROW_TILE, LANES), F32),
            jax.ShapeDtypeStruct((2, t), I32),
            jax.ShapeDtypeStruct((CLASS_ROWS, LANES), F32),
        ],
        scratch_shapes=[
            pltpu.VMEM((TS + V_HALO, D_POOL), F32),
            pltpu.VMEM((2, TS + V_HALO, D_POOL), F32),
            pltpu.VMEM((TS + Z_HALO, D_CONV), F32),
            pltpu.VMEM((TS, D_MODEL), BF16),
            pltpu.VMEM((CLASS_ROWS, LANES), F32),
            pltpu.VMEM((D_MODEL, D_PROJ), BF16),
            pltpu.VMEM((D_MODEL, D_MODEL), BF16),
            pltpu.VMEM((2, W_STAGE_ROWS, D_PROJ), F32),
            pltpu.SemaphoreType.DMA((2,)),
            pltpu.VMEM((2, TS * ROW_TILE, LANES), F32),
            pltpu.SemaphoreType.DMA((2,)),
        ],
        compiler_params=pltpu.CompilerParams(
            dimension_semantics=("arbitrary",), vmem_limit_bytes=VMEM_LIMIT),
        name="layer_mix_route",
    )(*head_args, w_in, pool_w, pool_scale, conv_w, conv_b, w_out, ln_g, ln_b, wr_t, rbias, tri)


def _slot_kernel(pstart_ref, route_ref, slot_ref):
    cls = route_ref[0:1, :]
    start = jnp.zeros(cls.shape, I32)
    for k in range(N_CLASSES):
        start = jnp.where(cls == k, pstart_ref[k], start)
    slot_ref[...] = (start + route_ref[1:2, :]) * ROW_TILE


def _slots(pstart, route):
    t = route.shape[1]
    grid_spec = pltpu.PrefetchScalarGridSpec(
        num_scalar_prefetch=1,
        grid=(1,),
        in_specs=[pl.BlockSpec((2, t), lambda i, ps: (0, 0))],
        out_specs=pl.BlockSpec((1, t), lambda i, ps: (0, 0)),
    )
    return pl.pallas_call(
        _slot_kernel,
        grid_spec=grid_spec,
        out_shape=jax.ShapeDtypeStruct((1, t), I32),
        name="token_slots",
    )(pstart, route)


def _scatter_kernel(slot_ref, h1t_ref, xin_ref, xout_ref, sem):
    del xin_ref
    for t in range(TS):
        dst = _row(xout_ref, pl.multiple_of(slot_ref[0, 0, t], ROW_TILE))
        pltpu.make_async_copy(_row(h1t_ref, t * ROW_TILE), dst, sem).start(priority=t % DMA_QUEUES)
    pltpu.make_async_copy(h1t_ref, xout_ref.at[pl.ds(0, TS * ROW_TILE)], sem).wait()


def _scatter_rows(slot3, h1t, xbuf):
    nt = slot3.shape[0]
    return pl.pallas_call(
        _scatter_kernel,
        grid=(nt,),
        in_specs=[
            pl.BlockSpec((1, 1, TS), lambda i: (i, 0, 0), memory_space=pltpu.SMEM),
            pl.BlockSpec((TS * ROW_TILE, LANES), lambda i: (i, 0)),
            pl.BlockSpec(memory_space=pl.ANY),
        ],
        out_specs=pl.BlockSpec(memory_space=pl.ANY),
        out_shape=jax.ShapeDtypeStruct(xbuf.shape, xbuf.dtype),
        scratch_shapes=[pltpu.SemaphoreType.DMA(())],
        input_output_aliases={2: 0},
        compiler_params=pltpu.CompilerParams(dimension_semantics=("arbitrary",)),
        name="scatter_rows",
    )(slot3, h1t, xbuf)


def _expert_kernel(ea_ref, eb_ref, valid_ref, nu_ref, x_ref, w1_hbm, w3_hbm, w2_hbm, wr_ref, y_ref,
                   wup, w2a, w2b, s1a, s3a, s2a, s1b, s3b, s2b, wsem, *, layer):
    i = pl.program_id(0)
    nb = pl.num_programs(0)
    nu = nu_ref[0]
    used = i < nu
    ea = ea_ref[i]
    eb = eb_ref[i]

    def swap_weights(e_ref, slot, live, staged):
        e = e_ref[i]

        def fetch(expert):
            return [pltpu.make_async_copy(src.at[layer, expert], dst, wsem.at[slot])
                    for src, dst in zip((w1_hbm, w3_hbm, w2_hbm), staged)]

        @pl.when(used & (i == 0))
        def _():
            for cp in fetch(e):
                cp.start()

        @pl.when(used & ((i == 0) | (e != e_ref[jnp.maximum(i - 1, 0)])))
        def _():
            for cp in fetch(e):
                cp.wait()
            for dst, src in zip(live, staged):
                dst[...] = src[...].astype(BF16)
            nxt = lax.while_loop(lambda j: (j < nu) & (e_ref[jnp.minimum(j, nb - 1)] == e),
                                 lambda j: j + 1, i + 1)

            @pl.when(nxt < nu)
            def _():
                for cp in fetch(e_ref[jnp.minimum(nxt, nb - 1)]):
                    cp.start()

    def up_cols(k):
        return wup.at[:, pl.ds(k * D_EXPERT, D_EXPERT)]

    swap_weights(ea_ref, 0, (up_cols(0), up_cols(2), w2a), (s1a, s3a, s2a))
    swap_weights(eb_ref, 1, (up_cols(1), up_cols(3), w2b), (s1b, s3b, s2b))

    def ffn_rows(rows):
        sublanes = pl.ds(0, rows * ROW_TILE)
        xb = _load_row_tiled(x_ref.at[sublanes], rows).astype(BF16)

        logits = _dot(xb, wr_ref[...].astype(BF16))
        lane = lax.broadcasted_iota(I32, logits.shape, 1)
        logits = jnp.where(lane < N_EXPERTS, logits, -jnp.inf)
        ex = jnp.exp(logits - jnp.max(logits, axis=1, keepdims=True))
        probs = ex / jnp.sum(ex, axis=1, keepdims=True)
        pa = jnp.sum(jnp.where(lane == ea, probs, 0.0), axis=1, keepdims=True)
        pb = jnp.sum(jnp.where(lane == eb, probs, 0.0), axis=1, keepdims=True)
        den = pa + pb

        up = _dot(xb, wup[...])
        a = up[:, 0:2 * D_EXPERT]
        hid = (a * jax.nn.sigmoid(a) * up[:, 2 * D_EXPERT:4 * D_EXPERT]).astype(BF16)
        y = ((pa / den) * _dot(hid[:, 0:D_EXPERT], w2a[...])
             + (pb / den) * _dot(hid[:, D_EXPERT:2 * D_EXPERT], w2b[...]))
        _store_row_tiled(y_ref.at[sublanes], y)

    few = valid_ref[i] <= BM // 2

    @pl.when(used & jnp.logical_not(few))
    def _():
        ffn_rows(BM)

    @pl.when(used & few)
    def _():
        ffn_rows(BM // 2)
        rest = pl.ds(BM // 2 * ROW_TILE, BM // 2 * ROW_TILE)
        y_ref[rest, :] = jnp.zeros((BM // 2 * ROW_TILE, LANES), F32)

    @pl.when(jnp.logical_not(used))
    def _():
        y_ref[...] = jnp.zeros(y_ref.shape, F32)


def _expert_ffn(layer, blk_ea, blk_eb, blk_valid, nused, xbuf, w1, w3, w2, wr_pad):
    n_blocks = xbuf.shape[0] // (BM * ROW_TILE)
    used_blk = lambda i, ea, eb, va, nu: (jnp.minimum(i, nu[0] - 1), 0)
    up = (D_MODEL, D_EXPERT)
    down = (D_EXPERT, D_MODEL)
    expert_slot = lambda dtype: [pltpu.VMEM(up, dtype), pltpu.VMEM(up, dtype), pltpu.VMEM(down, dtype)]
    grid_spec = pltpu.PrefetchScalarGridSpec(
        num_scalar_prefetch=4,
        grid=(n_blocks,),
        in_specs=[
            pl.BlockSpec((BM * ROW_TILE, LANES), used_blk),
            pl.BlockSpec(memory_space=pl.ANY),
            pl.BlockSpec(memory_space=pl.ANY),
            pl.BlockSpec(memory_space=pl.ANY),
            pl.BlockSpec((D_MODEL, LANES), lambda i, ea, eb, va, nu: (0, 0)),
        ],
        out_specs=pl.BlockSpec((BM * ROW_TILE, LANES), lambda i, ea, eb, va, nu: (i, 0)),
        scratch_shapes=([pltpu.VMEM((D_MODEL, 4 * D_EXPERT), BF16), pltpu.VMEM(down, BF16),
                         pltpu.VMEM(down, BF16)] + expert_slot(F32) + expert_slot(F32)
                        + [pltpu.SemaphoreType.DMA((2,))]),
    )
    return pl.pallas_call(
        functools.partial(_expert_kernel, layer=layer),
        grid_spec=grid_spec,
        out_shape=jax.ShapeDtypeStruct(xbuf.shape, F32),
        compiler_params=pltpu.CompilerParams(
            dimension_semantics=("arbitrary",), vmem_limit_bytes=VMEM_LIMIT),
        name="expert_ffn",
    )(blk_ea, blk_eb, blk_valid, nused, xbuf, w1, w3, w2, wr_pad)


def _dispatch_tables(cnt, nb):
    counts = cnt[:N_CLASSES, 0].astype(I32)
    padded = ((counts + BM - 1) // BM) * BM
    pend = jnp.cumsum(padded).astype(I32)
    pstart = pend - padded
    nused = pend[-1:] // BM
    step = jnp.minimum(jnp.arange(nb, dtype=I32), nused - 1)
    c = jnp.sum((step[:, None] * BM >= pend[None, :N_CLASSES - 1]).astype(I32), axis=1)
    g = c // PAIRS_PER_GROUP
    p = c - g * PAIRS_PER_GROUP
    ge3 = (p >= 3).astype(I32)
    ge5 = (p >= 5).astype(I32)
    blk_ea = g * EXPERTS_PER_GROUP + ge3 + ge5
    blk_eb = g * EXPERTS_PER_GROUP + p + 1 - 2 * ge3 - ge5
    blk_valid = jnp.clip(pstart[c] + counts[c] - step * BM, 0, BM)
    return pstart, blk_ea, blk_eb, blk_valid, nused


def _final_kernel(scur_ref, snext_ref, h1p_ref, y_hbm, g2_ref, b2_ref, o_ref, ybuf, gsem):
    i = pl.program_id(0)
    y = _prefetch_expert_rows(i, pl.num_programs(0), scur_ref, snext_ref, y_hbm, ybuf, gsem)
    o_ref[...] = _layer_norm(ALPHA * _load_row_tiled(h1p_ref, TS) + y, g2_ref[0], b2_ref[0])


def _final_call(slot_prev, h1t_prev, y_prev, g2, b2):
    t = h1t_prev.shape[0] // ROW_TILE
    nt = t // TS
    last = DEPTH - 1
    return pl.pallas_call(
        _final_kernel,
        grid=(nt,),
        in_specs=[
            pl.BlockSpec((1, 1, TS), lambda i: (i, 0, 0), memory_space=pltpu.SMEM),
            pl.BlockSpec((1, 1, TS), lambda i: (jnp.minimum(i + 1, nt - 1), 0, 0),
                         memory_space=pltpu.SMEM),
            pl.BlockSpec((TS * ROW_TILE, LANES), lambda i: (i, 0)),
            pl.BlockSpec(memory_space=pl.ANY),
            pl.BlockSpec((1, 1, D_MODEL), lambda i: (last, 0, 0)),
            pl.BlockSpec((1, 1, D_MODEL), lambda i: (last, 0, 0)),
        ],
        out_specs=pl.BlockSpec((TS, D_MODEL), lambda i: (i, 0)),
        out_shape=jax.ShapeDtypeStruct((t, D_MODEL), F32),
        scratch_shapes=[pltpu.VMEM((2, TS * ROW_TILE, LANES), F32), pltpu.SemaphoreType.DMA((2,))],
        compiler_params=pltpu.CompilerParams(
            dimension_semantics=("arbitrary",), vmem_limit_bytes=VMEM_LIMIT),
        name="final_combine_ln",
    )(slot_prev, slot_prev, h1t_prev, y_prev, g2, b2)


def kernel(x, ln_in_g, ln_in_b, w_in, pool_w, pool_scale, conv_w, conv_b, w_out, ln1_g, ln1_b,
           w_router, router_bias, exp_w1, exp_w3, exp_w2, ln2_g, ln2_b):
    bsz, seq, d = x.shape
    t = bsz * seq
    nt = t // TS
    n_rows = t + N_CLASSES * BM
    nb = n_rows // BM

    per_layer_row = lambda v: v.reshape(DEPTH, 1, -1)
    tri = (lax.broadcasted_iota(I32, (TS, TS), 0)
           < lax.broadcasted_iota(I32, (TS, TS), 1)).astype(BF16)
    wr_t = w_router.T
    wr_pad = jnp.pad(w_router, ((0, 0), (0, LANES - N_EXPERTS)))
    rbias = router_bias.reshape(N_EXPERTS, 1).astype(F32)
    pool_scale3, conv_b3 = per_layer_row(pool_scale), per_layer_row(conv_b)
    ln1_g3, ln1_b3 = per_layer_row(ln1_g), per_layer_row(ln1_b)
    ln2_g3, ln2_b3 = per_layer_row(ln2_g), per_layer_row(ln2_b)

    prev = (x.reshape(t, d), ln_in_g.reshape(1, -1), ln_in_b.reshape(1, -1))
    xbuf = jnp.zeros((n_rows * ROW_TILE, LANES), F32)
    for l in range(DEPTH):
        h1t, route, cnt = _layer_call(l, bsz, seq, prev, w_in, pool_w, pool_scale3, conv_w, conv_b3,
                                          w_out, ln1_g3, ln1_b3, wr_t, rbias, tri)
        pstart, blk_ea, blk_eb, blk_valid, nused = _dispatch_tables(cnt, nb)
        slot3 = _slots(pstart, route).reshape(nt, 1, TS)
        xbuf = _scatter_rows(slot3, h1t, xbuf)
        ybuf = _expert_ffn(l, blk_ea, blk_eb, blk_valid, nused, xbuf, exp_w1, exp_w3, exp_w2, wr_pad)
        prev = (slot3, h1t, ybuf, ln2_g3, ln2_b3)
    out = _final_call(*prev)
    return out.reshape(bsz, seq, d)
```

```python
import functools

import jax
import jax.numpy as jnp
from jax import lax
from jax.experimental import pallas as pl
from jax.experimental.pallas import tpu as pltpu

D_MODEL = 1024
DEPTH = 4
D_POOL = 512
POOL_WINDOWS = (2, 4, 8, 16)
POOL_GROUP = 128
D_CONV = 512
CONV_WIDTH = 3
D_PROJ = D_POOL + 3 * D_CONV
N_EXPERTS = 16
N_EXPERT_GROUPS = 4
EXPERTS_PER_GROUP = 4
PAIRS_PER_GROUP = 6
N_CLASSES = N_EXPERT_GROUPS * PAIRS_PER_GROUP
CLASS_ROWS = 32
D_EXPERT = 512
ALPHA = float((2 * DEPTH) ** 0.25)
LN_EPS = 1e-5

F32 = jnp.float32
BF16 = jnp.bfloat16
I32 = jnp.int32

LANES = 128
ROW_TILE = D_MODEL // LANES
TS = 512
SUBLANES = 8
V_HALO = 32
Z_HALO = 8
BM = 256
W_STAGE_ROWS = 256
DMA_QUEUES = 2
VMEM_LIMIT = 56 * 1024 * 1024


def _layer_norm(x, g, b):
    mu = jnp.mean(x, axis=-1, keepdims=True)
    xc = x - mu
    var = jnp.mean(xc * xc, axis=-1, keepdims=True)
    return xc * lax.rsqrt(var + LN_EPS) * g + b


def _dot(a, b):
    return jnp.dot(a, b, preferred_element_type=F32)


def _store_row_tiled(ref, value):
    n = value.shape[0]
    for j in range(ROW_TILE):
        ref[pl.ds(j, n, stride=ROW_TILE), :] = value[:, j * LANES:(j + 1) * LANES]


def _load_row_tiled(ref, n):
    return jnp.concatenate([ref[pl.ds(j, n, stride=ROW_TILE), :] for j in range(ROW_TILE)], axis=1)


def _row(ref, first_sublane):
    return ref.at[pl.ds(first_sublane, ROW_TILE)]


def _route(sel):
    rows = [sel[e:e + 1, :] for e in range(N_EXPERTS)]
    gscore = []
    for g in range(N_EXPERT_GROUPS):
        a, b, c, d = rows[4 * g:4 * g + 4]
        hi1, lo1 = jnp.maximum(a, b), jnp.minimum(a, b)
        hi2, lo2 = jnp.maximum(c, d), jnp.minimum(c, d)
        m1 = jnp.maximum(hi1, hi2)
        m2 = jnp.maximum(jnp.minimum(hi1, hi2), jnp.maximum(lo1, lo2))
        gscore.append(m1 + m2)
    best = gscore[0]
    gidx = jnp.zeros(best.shape, I32)
    for g in range(1, N_EXPERT_GROUPS):
        upd = gscore[g] > best
        best = jnp.where(upd, gscore[g], best)
        gidx = jnp.where(upd, g, gidx)
    neg = jnp.full(best.shape, -jnp.inf, F32)
    masked = [jnp.where(gidx == (e // EXPERTS_PER_GROUP), rows[e], neg) for e in range(N_EXPERTS)]

    def top1(vals):
        bv = neg
        bi = jnp.zeros(best.shape, I32)
        for e in range(N_EXPERTS):
            upd = vals[e] > bv
            bv = jnp.where(upd, vals[e], bv)
            bi = jnp.where(upd, e, bi)
        return bi

    i1 = top1(masked)
    i2 = top1([jnp.where(i1 == e, neg, masked[e]) for e in range(N_EXPERTS)])
    return gidx, i1, i2


def _pair_class(gidx, i1, i2):
    a = jnp.minimum(i1, i2) - gidx * EXPERTS_PER_GROUP
    b = jnp.maximum(i1, i2) - gidx * EXPERTS_PER_GROUP
    base = jnp.where(a == 0, 0, jnp.where(a == 1, 2, 3))
    return gidx * PAIRS_PER_GROUP + base + b - 1


def _prefetch_expert_rows(i, nt, scur_ref, snext_ref, y_hbm, ybuf, gsem):
    slot = i % 2

    def gather(slot_ref, to_slot, t, t_sublane):
        src = _row(y_hbm, pl.multiple_of(slot_ref[0, 0, t], ROW_TILE))
        return pltpu.make_async_copy(src, _row(ybuf.at[to_slot], t_sublane), gsem.at[to_slot])

    @pl.when(i == 0)
    def _():
        def body(t, carry):
            gather(scur_ref, 0, t, pl.multiple_of(t * ROW_TILE, ROW_TILE)).start()
            return carry
        lax.fori_loop(0, TS, body, 0, unroll=8)

    @pl.when(i + 1 < nt)
    def _():
        for t in range(TS):
            gather(snext_ref, 1 - slot, t, t * ROW_TILE).start(priority=t % DMA_QUEUES)

    pltpu.make_async_copy(y_hbm.at[pl.ds(0, TS * ROW_TILE)], ybuf.at[slot], gsem.at[slot]).wait()
    return _load_row_tiled(ybuf.at[slot], TS)


def _layer_kernel(*refs, layer, tiles_per_seq):
    if layer == 0:
        x_ref, gin_ref, bin_ref = refs[:3]
        rest = refs[3:]
    else:
        scur_ref, snext_ref, h1p_ref, y_hbm, g2_ref, b2_ref = refs[:6]
        rest = refs[6:]
    (win_hbm, pw_ref, ps_ref, cw_ref, cb_ref, wout_hbm, g_ref, b_ref, wr_ref, rb_ref, tri_ref,
     h1t_ref, route_ref, cnt_ref,
     vbuf, pwork, zbuf, mixbuf, carry, win_ref, wout_ref, stage, wsem, ybuf, gsem) = rest

    i = pl.program_id(0)
    nt = pl.num_programs(0)

    @pl.when(i == 0)
    def _():
        chunks = []
        for src, dst, width in ((win_hbm, win_ref, D_PROJ), (wout_hbm, wout_ref, D_MODEL)):
            for c in range(D_MODEL // W_STAGE_ROWS):
                chunks.append((src, dst, width, pl.ds(c * W_STAGE_ROWS, W_STAGE_ROWS)))

        def fetch(k):
            src, _, width, rows = chunks[k]
            return pltpu.make_async_copy(src.at[layer, rows], stage.at[k % 2, :, pl.ds(0, width)],
                                         wsem.at[k % 2])

        fetch(0).start()
        for k, (_, dst, width, rows) in enumerate(chunks):
            if k + 1 < len(chunks):
                fetch(k + 1).start()
            fetch(k).wait()
            dst[rows, :] = stage[k % 2, :, 0:width].astype(BF16)
        carry[...] = jnp.zeros(carry.shape, F32)

    if layer == 0:
        h = _layer_norm(x_ref[...], gin_ref[...], bin_ref[...])
    else:
        y = _prefetch_expert_rows(i, nt, scur_ref, snext_ref, y_hbm, ybuf, gsem)
        h = _layer_norm(ALPHA * _load_row_tiled(h1p_ref, TS) + y, g2_ref[0], b2_ref[0])

    si = i % tiles_per_seq

    @pl.when(si == 0)
    def _():
        vbuf[0:V_HALO, :] = jnp.zeros((V_HALO, D_POOL), F32)
        zbuf[0:Z_HALO, :] = jnp.zeros((Z_HALO, D_CONV), F32)

    hb = h.astype(BF16)
    vbuf[V_HALO:V_HALO + TS, :] = _dot(hb, win_ref[:, 0:D_POOL])
    gate_b = _dot(hb, win_ref[:, D_POOL:D_POOL + D_CONV])
    gate_c = _dot(hb, win_ref[:, D_POOL + D_CONV:D_POOL + 2 * D_CONV])
    u = _dot(hb, win_ref[:, D_POOL + 2 * D_CONV:D_PROJ])
    z = gate_c * u
    zbuf[Z_HALO:Z_HALO + TS, :] = z

    src = vbuf
    for level in range(1, len(POOL_WINDOWS)):
        dst = pwork.at[(level - 1) % 2]
        lo = SUBLANES * level
        shift = 2 ** (level - 1)
        n = TS + V_HALO - lo
        cols = pl.ds(level * POOL_GROUP, D_POOL - level * POOL_GROUP)
        dst[pl.ds(lo, n), cols] = src[pl.ds(lo, n), cols] + src[pl.ds(lo - shift, n), cols]
        src = dst
    tpos = si * TS + lax.broadcasted_iota(I32, (TS, 1), 0)
    for g, win in enumerate(POOL_WINDOWS):
        cols = pl.ds(g * POOL_GROUP, POOL_GROUP)
        cur = vbuf[pl.ds(V_HALO, TS), cols]
        half = vbuf if g == 0 else pwork.at[(g - 1) % 2]
        acc = half[pl.ds(V_HALO, TS), cols] + half[pl.ds(V_HALO - win // 2, TS), cols]
        denom = jnp.minimum(tpos + 1, win).astype(F32)
        pooled = acc / denom - cur
        mixed = _dot(pooled.astype(BF16), pw_ref[0, g].astype(BF16)) * ps_ref[0, :, cols]
        mixbuf[:, cols] = mixed.astype(BF16)

    yc = (cw_ref[0, 2:3, :] * z
          + cw_ref[0, 1:2, :] * zbuf[pl.ds(Z_HALO - 1, TS), :]
          + cw_ref[0, 0:1, :] * zbuf[pl.ds(Z_HALO - 2, TS), :])
    mixbuf[:, D_POOL:D_POOL + D_CONV] = (gate_b * (yc + cb_ref[0])).astype(BF16)

    vbuf[0:V_HALO, :] = vbuf[TS:TS + V_HALO, :]
    zbuf[0:Z_HALO, :] = zbuf[TS:TS + Z_HALO, :]

    mix = _dot(mixbuf[...], wout_ref[...])
    h1 = _layer_norm(ALPHA * h + mix, g_ref[0], b_ref[0])
    _store_row_tiled(h1t_ref, h1)

    logits = lax.dot_general(wr_ref[...].astype(BF16), h1.astype(BF16), (((1,), (1,)), ((), ())),
                             preferred_element_type=F32)
    mx = jnp.max(logits, axis=0, keepdims=True)
    ex = jnp.exp(logits - mx)
    probs = ex / jnp.sum(ex, axis=0, keepdims=True)
    gidx, i1, i2 = _route(probs + rb_ref[...])
    cls = _pair_class(gidx, i1, i2)

    ciota = lax.broadcasted_iota(I32, (CLASS_ROWS, TS), 0)
    onehot = jnp.where(ciota == cls, 1.0, 0.0).astype(F32)
    before = _dot(onehot.astype(BF16), tri_ref[...])
    total = jnp.sum(onehot, axis=1, keepdims=True)
    c = carry[:, 0:1]
    rank = jnp.sum(onehot * (c + before), axis=0, keepdims=True)
    route_ref[0:1, :] = cls
    route_ref[1:2, :] = rank.astype(I32)
    newc = jnp.broadcast_to(c + total, carry.shape)
    carry[...] = newc
    cnt_ref[...] = newc


def _layer_call(layer, bsz, seq, prev, w_in, pool_w, pool_scale, conv_w, conv_b, w_out, ln_g, ln_b,
                wr_t, rbias, tri):
    t = bsz * seq
    nt = t // TS
    const2 = lambda i: (0, 0)
    lsel3 = lambda i: (layer, 0, 0)
    lsel4 = lambda i: (layer, 0, 0, 0)
    tile = lambda i: (i, 0)
    lane_tile = lambda i: (0, i)
    if layer == 0:
        x, gin, bin_ = prev
        head_args = (x, gin, bin_)
        head_specs = [
            pl.BlockSpec((TS, D_MODEL), tile),
            pl.BlockSpec((1, D_MODEL), const2),
            pl.BlockSpec((1, D_MODEL), const2),
        ]
    else:
        slot_prev, h1t_prev, y_prev, g2, b2 = prev
        head_args = (slot_prev, slot_prev, h1t_prev, y_prev, g2, b2)
        head_specs = [
            pl.BlockSpec((1, 1, TS), lambda i: (i, 0, 0), memory_space=pltpu.SMEM),
            pl.BlockSpec((1, 1, TS), lambda i: (jnp.minimum(i + 1, nt - 1), 0, 0),
                         memory_space=pltpu.SMEM),
            pl.BlockSpec((TS * ROW_TILE, LANES), tile),
            pl.BlockSpec(memory_space=pl.ANY),
            pl.BlockSpec((1, 1, D_MODEL), lambda i: (layer - 1, 0, 0)),
            pl.BlockSpec((1, 1, D_MODEL), lambda i: (layer - 1, 0, 0)),
        ]
    return pl.pallas_call(
        functools.partial(_layer_kernel, layer=layer, tiles_per_seq=seq // TS),
        grid=(nt,),
        in_specs=head_specs + [
            pl.BlockSpec(memory_space=pl.ANY),
            pl.BlockSpec((1, len(POOL_WINDOWS), POOL_GROUP, POOL_GROUP), lsel4),
            pl.BlockSpec((1, 1, D_POOL), lsel3),
            pl.BlockSpec((1, CONV_WIDTH, D_CONV), lsel3),
            pl.BlockSpec((1, 1, D_CONV), lsel3),
            pl.BlockSpec(memory_space=pl.ANY),
            pl.BlockSpec((1, 1, D_MODEL), lsel3),
            pl.BlockSpec((1, 1, D_MODEL), lsel3),
            pl.BlockSpec((N_EXPERTS, D_MODEL), const2),
            pl.BlockSpec((N_EXPERTS, 1), const2),
            pl.BlockSpec((TS, TS), const2),
        ],
        out_specs=[
            pl.BlockSpec((TS * ROW_TILE, LANES), tile),
            pl.BlockSpec((2, TS), lane_tile),
            pl.BlockSpec((CLASS_ROWS, LANES), const2),
        ],
        out_shape=[
            jax.ShapeDtypeStruct((t * ROW_TILE, LANES), F32),
            jax.ShapeDtypeStruct((2, t), I32),
            jax.ShapeDtypeStruct((CLASS_ROWS, LANES), F32),
        ],
        scratch_shapes=[
            pltpu.VMEM((TS + V_HALO, D_POOL), F32),
            pltpu.VMEM((2, TS + V_HALO, D_POOL), F32),
            pltpu.VMEM((TS + Z_HALO, D_CONV), F32),
            pltpu.VMEM((TS, D_MODEL), BF16),
            pltpu.VMEM((CLASS_ROWS, LANES), F32),
            pltpu.VMEM((D_MODEL, D_PROJ), BF16),
            pltpu.VMEM((D_MODEL, D_MODEL), BF16),
            pltpu.VMEM((2, W_STAGE_ROWS, D_PROJ), F32),
            pltpu.SemaphoreType.DMA((2,)),
            pltpu.VMEM((2, TS * ROW_TILE, LANES), F32),
            pltpu.SemaphoreType.DMA((2,)),
        ],
        compiler_params=pltpu.CompilerParams(
            dimension_semantics=("arbitrary",), vmem_limit_bytes=VMEM_LIMIT),
        name="layer_mix_route",
    )(*head_args, w_in, pool_w, pool_scale, conv_w, conv_b, w_out, ln_g, ln_b, wr_t, rbias, tri)


def _slot_kernel(pstart_ref, route_ref, slot_ref):
    cls = route_ref[0:1, :]
    start = jnp.zeros(cls.shape, I32)
    for k in range(N_CLASSES):
        start = jnp.where(cls == k, pstart_ref[k], start)
    slot_ref[...] = (start + route_ref[1:2, :]) * ROW_TILE


def _slots(pstart, route):
    t = route.shape[1]
    grid_spec = pltpu.PrefetchScalarGridSpec(
        num_scalar_prefetch=1,
        grid=(1,),
        in_specs=[pl.BlockSpec((2, t), lambda i, ps: (0, 0))],
        out_specs=pl.BlockSpec((1, t), lambda i, ps: (0, 0)),
    )
    return pl.pallas_call(
        _slot_kernel,
        grid_spec=grid_spec,
        out_shape=jax.ShapeDtypeStruct((1, t), I32),
        name="token_slots",
    )(pstart, route)


def _scatter_kernel(slot_ref, h1t_ref, xin_ref, xout_ref, sem):
    del xin_ref
    for t in range(TS):
        dst = _row(xout_ref, pl.multiple_of(slot_ref[0, 0, t], ROW_TILE))
        pltpu.make_async_copy(_row(h1t_ref, t * ROW_TILE), dst, sem).start(priority=t % DMA_QUEUES)
    pltpu.make_async_copy(h1t_ref, xout_ref.at[pl.ds(0, TS * ROW_TILE)], sem).wait()


def _scatter_rows(slot3, h1t, xbuf):
    nt = slot3.shape[0]
    return pl.pallas_call(
        _scatter_kernel,
        grid=(nt,),
        in_specs=[
            pl.BlockSpec((1, 1, TS), lambda i: (i, 0, 0), memory_space=pltpu.SMEM),
            pl.BlockSpec((TS * ROW_TILE, LANES), lambda i: (i, 0)),
            pl.BlockSpec(memory_space=pl.ANY),
        ],
        out_specs=pl.BlockSpec(memory_space=pl.ANY),
        out_shape=jax.ShapeDtypeStruct(xbuf.shape, xbuf.dtype),
        scratch_shapes=[pltpu.SemaphoreType.DMA(())],
        input_output_aliases={2: 0},
        compiler_params=pltpu.CompilerParams(dimension_semantics=("arbitrary",)),
        name="scatter_rows",
    )(slot3, h1t, xbuf)


def _expert_kernel(ea_ref, eb_ref, valid_ref, nu_ref, x_ref, w1_hbm, w3_hbm, w2_hbm, wr_ref, y_ref,
                   wup, w2a, w2b, s1a, s3a, s2a, s1b, s3b, s2b, wsem, *, layer):
    i = pl.program_id(0)
    nb = pl.num_programs(0)
    nu = nu_ref[0]
    used = i < nu
    ea = ea_ref[i]
    eb = eb_ref[i]

    def swap_weights(e_ref, slot, live, staged):
        e = e_ref[i]

        def fetch(expert):
            return [pltpu.make_async_copy(src.at[layer, expert], dst, wsem.at[slot])
                    for src, dst in zip((w1_hbm, w3_hbm, w2_hbm), staged)]

        @pl.when(used & (i == 0))
        def _():
            for cp in fetch(e):
                cp.start()

        @pl.when(used & ((i == 0) | (e != e_ref[jnp.maximum(i - 1, 0)])))
        def _():
            for cp in fetch(e):
                cp.wait()
            for dst, src in zip(live, staged):
                dst[...] = src[...].astype(BF16)
            nxt = lax.while_loop(lambda j: (j < nu) & (e_ref[jnp.minimum(j, nb - 1)] == e),
                                 lambda j: j + 1, i + 1)

            @pl.when(nxt < nu)
            def _():
                for cp in fetch(e_ref[jnp.minimum(nxt, nb - 1)]):
                    cp.start()

    def up_cols(k):
        return wup.at[:, pl.ds(k * D_EXPERT, D_EXPERT)]

    swap_weights(ea_ref, 0, (up_cols(0), up_cols(2), w2a), (s1a, s3a, s2a))
    swap_weights(eb_ref, 1, (up_cols(1), up_cols(3), w2b), (s1b, s3b, s2b))

    def ffn_rows(rows):
        sublanes = pl.ds(0, rows * ROW_TILE)
        xb = _load_row_tiled(x_ref.at[sublanes], rows).astype(BF16)

        logits = _dot(xb, wr_ref[...].astype(BF16))
        lane = lax.broadcasted_iota(I32, logits.shape, 1)
        logits = jnp.where(lane < N_EXPERTS, logits, -jnp.inf)
        ex = jnp.exp(logits - jnp.max(logits, axis=1, keepdims=True))
        probs = ex / jnp.sum(ex, axis=1, keepdims=True)
        pa = jnp.sum(jnp.where(lane == ea, probs, 0.0), axis=1, keepdims=True)
        pb = jnp.sum(jnp.where(lane == eb, probs, 0.0), axis=1, keepdims=True)
        den = pa + pb

        up = _dot(xb, wup[...])
        a = up[:, 0:2 * D_EXPERT]
        hid = (a * jax.nn.sigmoid(a) * up[:, 2 * D_EXPERT:4 * D_EXPERT]).astype(BF16)
        y = ((pa / den) * _dot(hid[:, 0:D_EXPERT], w2a[...])
             + (pb / den) * _dot(hid[:, D_EXPERT:2 * D_EXPERT], w2b[...]))
        _store_row_tiled(y_ref.at[sublanes], y)

    few = valid_ref[i] <= BM // 2

    @pl.when(used & jnp.logical_not(few))
    def _():
        ffn_rows(BM)

    @pl.when(used & few)
    def _():
        ffn_rows(BM // 2)
        rest = pl.ds(BM // 2 * ROW_TILE, BM // 2 * ROW_TILE)
        y_ref[rest, :] = jnp.zeros((BM // 2 * ROW_TILE, LANES), F32)


def _expert_ffn(layer, blk_ea, blk_eb, blk_valid, nused, xbuf, w1, w3, w2, wr_pad):
    n_blocks = xbuf.shape[0] // (BM * ROW_TILE)
    used_blk = lambda i, ea, eb, va, nu: (jnp.minimum(i, nu[0] - 1), 0)
    up = (D_MODEL, D_EXPERT)
    down = (D_EXPERT, D_MODEL)
    expert_slot = lambda dtype: [pltpu.VMEM(up, dtype), pltpu.VMEM(up, dtype), pltpu.VMEM(down, dtype)]
    grid_spec = pltpu.PrefetchScalarGridSpec(
        num_scalar_prefetch=4,
        grid=(n_blocks,),
        in_specs=[
            pl.BlockSpec((BM * ROW_TILE, LANES), used_blk),
            pl.BlockSpec(memory_space=pl.ANY),
            pl.BlockSpec(memory_space=pl.ANY),
            pl.BlockSpec(memory_space=pl.ANY),
            pl.BlockSpec((D_MODEL, LANES), lambda i, ea, eb, va, nu: (0, 0)),
        ],
        out_specs=pl.BlockSpec((BM * ROW_TILE, LANES), used_blk),
        scratch_shapes=([pltpu.VMEM((D_MODEL, 4 * D_EXPERT), BF16), pltpu.VMEM(down, BF16),
                         pltpu.VMEM(down, BF16)] + expert_slot(F32) + expert_slot(F32)
                        + [pltpu.SemaphoreType.DMA((2,))]),
    )
    return pl.pallas_call(
        functools.partial(_expert_kernel, layer=layer),
        grid_spec=grid_spec,
        out_shape=jax.ShapeDtypeStruct(xbuf.shape, F32),
        input_output_aliases={4: 0},
        compiler_params=pltpu.CompilerParams(
            dimension_semantics=("arbitrary",), vmem_limit_bytes=VMEM_LIMIT),
        name="expert_ffn",
    )(blk_ea, blk_eb, blk_valid, nused, xbuf, w1, w3, w2, wr_pad)


def _dispatch_tables(cnt, nb):
    counts = cnt[:N_CLASSES, 0].astype(I32)
    padded = ((counts + BM - 1) // BM) * BM
    pend = jnp.cumsum(padded).astype(I32)
    pstart = pend - padded
    nused = pend[-1:] // BM
    step = jnp.minimum(jnp.arange(nb, dtype=I32), nused - 1)
    c = jnp.sum((step[:, None] * BM >= pend[None, :N_CLASSES - 1]).astype(I32), axis=1)
    g = c // PAIRS_PER_GROUP
    p = c - g * PAIRS_PER_GROUP
    ge3 = (p >= 3).astype(I32)
    ge5 = (p >= 5).astype(I32)
    blk_ea = g * EXPERTS_PER_GROUP + ge3 + ge5
    blk_eb = g * EXPERTS_PER_GROUP + p + 1 - 2 * ge3 - ge5
    blk_valid = jnp.clip(pstart[c] + counts[c] - step * BM, 0, BM)
    return pstart, blk_ea, blk_eb, blk_valid, nused


def _final_kernel(scur_ref, snext_ref, h1p_ref, y_hbm, g2_ref, b2_ref, o_ref, ybuf, gsem):
    i = pl.program_id(0)
    y = _prefetch_expert_rows(i, pl.num_programs(0), scur_ref, snext_ref, y_hbm, ybuf, gsem)
    o_ref[...] = _layer_norm(ALPHA * _load_row_tiled(h1p_ref, TS) + y, g2_ref[0], b2_ref[0])


def _final_call(slot_prev, h1t_prev, y_prev, g2, b2):
    t = h1t_prev.shape[0] // ROW_TILE
    nt = t // TS
    last = DEPTH - 1
    return pl.pallas_call(
        _final_kernel,
        grid=(nt,),
        in_specs=[
            pl.BlockSpec((1, 1, TS), lambda i: (i, 0, 0), memory_space=pltpu.SMEM),
            pl.BlockSpec((1, 1, TS), lambda i: (jnp.minimum(i + 1, nt - 1), 0, 0),
                         memory_space=pltpu.SMEM),
            pl.BlockSpec((TS * ROW_TILE, LANES), lambda i: (i, 0)),
            pl.BlockSpec(memory_space=pl.ANY),
            pl.BlockSpec((1, 1, D_MODEL), lambda i: (last, 0, 0)),
            pl.BlockSpec((1, 1, D_MODEL), lambda i: (last, 0, 0)),
        ],
        out_specs=pl.BlockSpec((TS, D_MODEL), lambda i: (i, 0)),
        out_shape=jax.ShapeDtypeStruct((t, D_MODEL), F32),
        scratch_shapes=[pltpu.VMEM((2, TS * ROW_TILE, LANES), F32), pltpu.SemaphoreType.DMA((2,))],
        compiler_params=pltpu.CompilerParams(
            dimension_semantics=("arbitrary",), vmem_limit_bytes=VMEM_LIMIT),
        name="final_combine_ln",
    )(slot_prev, slot_prev, h1t_prev, y_prev, g2, b2)


def kernel(x, ln_in_g, ln_in_b, w_in, pool_w, pool_scale, conv_w, conv_b, w_out, ln1_g, ln1_b,
           w_router, router_bias, exp_w1, exp_w3, exp_w2, ln2_g, ln2_b):
    bsz, seq, d = x.shape
    t = bsz * seq
    nt = t // TS
    n_rows = t + N_CLASSES * BM
    nb = n_rows // BM

    per_layer_row = lambda v: v.reshape(DEPTH, 1, -1)
    tri = (lax.broadcasted_iota(I32, (TS, TS), 0)
           < lax.broadcasted_iota(I32, (TS, TS), 1)).astype(BF16)
    wr_t = w_router.T
    wr_pad = jnp.pad(w_router, ((0, 0), (0, LANES - N_EXPERTS)))
    rbias = router_bias.reshape(N_EXPERTS, 1).astype(F32)
    pool_scale3, conv_b3 = per_layer_row(pool_scale), per_layer_row(conv_b)
    ln1_g3, ln1_b3 = per_layer_row(ln1_g), per_layer_row(ln1_b)
    ln2_g3, ln2_b3 = per_layer_row(ln2_g), per_layer_row(ln2_b)

    prev = (x.reshape(t, d), ln_in_g.reshape(1, -1), ln_in_b.reshape(1, -1))
    xbuf = jnp.zeros((n_rows * ROW_TILE, LANES), F32)
    for l in range(DEPTH):
        h1t, route, cnt = _layer_call(l, bsz, seq, prev, w_in, pool_w, pool_scale3, conv_w, conv_b3,
                                          w_out, ln1_g3, ln1_b3, wr_t, rbias, tri)
        pstart, blk_ea, blk_eb, blk_valid, nused = _dispatch_tables(cnt, nb)
        slot3 = _slots(pstart, route).reshape(nt, 1, TS)
        xbuf = _scatter_rows(slot3, h1t, xbuf)
        xbuf = _expert_ffn(l, blk_ea, blk_eb, blk_valid, nused, xbuf, exp_w1, exp_w3, exp_w2, wr_pad)
        prev = (slot3, h1t, xbuf, ln2_g3, ln2_b3)
    out = _final_call(*prev)
    return out.reshape(bsz, seq, d)
```

```python
import functools

import jax
import jax.numpy as jnp
from jax import lax
from jax.experimental import pallas as pl
from jax.experimental.pallas import tpu as pltpu

D_MODEL = 1024
DEPTH = 4
D_POOL = 512
POOL_WINDOWS = (2, 4, 8, 16)
POOL_GROUP = 128
D_CONV = 512
CONV_WIDTH = 3
D_PROJ = D_POOL + 3 * D_CONV
N_EXPERTS = 16
N_EXPERT_GROUPS = 4
EXPERTS_PER_GROUP = 4
PAIRS_PER_GROUP = 6
N_CLASSES = N_EXPERT_GROUPS * PAIRS_PER_GROUP
CLASS_ROWS = 32
D_EXPERT = 512
ALPHA = float((2 * DEPTH) ** 0.25)
LN_EPS = 1e-5

F32 = jnp.float32
BF16 = jnp.bfloat16
I32 = jnp.int32

LANES = 128
ROW_TILE = D_MODEL // LANES
TS = 512
SUBLANES = 8
V_HALO = 32
Z_HALO = 8
BM = 256
W_STAGE_ROWS = 256
DMA_QUEUES = 2
VMEM_LIMIT = 56 * 1024 * 1024


def _layer_norm(x, g, b):
    mu = jnp.mean(x, axis=-1, keepdims=True)
    xc = x - mu
    var = jnp.mean(xc * xc, axis=-1, keepdims=True)
    return xc * lax.rsqrt(var + LN_EPS) * g + b


def _dot(a, b):
    return jnp.dot(a, b, preferred_element_type=F32)


def _store_row_tiled(ref, value):
    n = value.shape[0]
    for j in range(ROW_TILE):
        ref[pl.ds(j, n, stride=ROW_TILE), :] = value[:, j * LANES:(j + 1) * LANES]


def _load_row_tiled(ref, n):
    return jnp.concatenate([ref[pl.ds(j, n, stride=ROW_TILE), :] for j in range(ROW_TILE)], axis=1)


def _row(ref, first_sublane):
    return ref.at[pl.ds(first_sublane, ROW_TILE)]


def _route(sel):
    rows = [sel[e:e + 1, :] for e in range(N_EXPERTS)]
    gscore = []
    for g in range(N_EXPERT_GROUPS):
        a, b, c, d = rows[4 * g:4 * g + 4]
        hi1, lo1 = jnp.maximum(a, b), jnp.minimum(a, b)
        hi2, lo2 = jnp.maximum(c, d), jnp.minimum(c, d)
        m1 = jnp.maximum(hi1, hi2)
        m2 = jnp.maximum(jnp.minimum(hi1, hi2), jnp.maximum(lo1, lo2))
        gscore.append(m1 + m2)
    best = gscore[0]
    gidx = jnp.zeros(best.shape, I32)
    for g in range(1, N_EXPERT_GROUPS):
        upd = gscore[g] > best
        best = jnp.where(upd, gscore[g], best)
        gidx = jnp.where(upd, g, gidx)
    neg = jnp.full(best.shape, -jnp.inf, F32)
    masked = [jnp.where(gidx == (e // EXPERTS_PER_GROUP), rows[e], neg) for e in range(N_EXPERTS)]

    def top1(vals):
        bv = neg
        bi = jnp.zeros(best.shape, I32)
        for e in range(N_EXPERTS):
            upd = vals[e] > bv
            bv = jnp.where(upd, vals[e], bv)
            bi = jnp.where(upd, e, bi)
        return bi

    i1 = top1(masked)
    i2 = top1([jnp.where(i1 == e, neg, masked[e]) for e in range(N_EXPERTS)])
    return gidx, i1, i2


def _pair_class(gidx, i1, i2):
    a = jnp.minimum(i1, i2) - gidx * EXPERTS_PER_GROUP
    b = jnp.maximum(i1, i2) - gidx * EXPERTS_PER_GROUP
    base = jnp.where(a == 0, 0, jnp.where(a == 1, 2, 3))
    return gidx * PAIRS_PER_GROUP + base + b - 1


def _prefetch_expert_rows(i, nt, scur_ref, snext_ref, y_hbm, ybuf, gsem, before_wait=None):
    slot = i % 2

    def gather(slot_ref, to_slot, t, t_sublane):
        src = _row(y_hbm, pl.multiple_of(slot_ref[0, 0, t], ROW_TILE))
        return pltpu.make_async_copy(src, _row(ybuf.at[to_slot], t_sublane), gsem.at[to_slot])

    @pl.when(i == 0)
    def _():
        def body(t, carry):
            gather(scur_ref, 0, t, pl.multiple_of(t * ROW_TILE, ROW_TILE)).start()
            return carry
        lax.fori_loop(0, TS, body, 0, unroll=8)

    @pl.when(i + 1 < nt)
    def _():
        for t in range(TS):
            gather(snext_ref, 1 - slot, t, t * ROW_TILE).start(priority=t % DMA_QUEUES)

    if before_wait is not None:
        before_wait()
    pltpu.make_async_copy(y_hbm.at[pl.ds(0, TS * ROW_TILE)], ybuf.at[slot], gsem.at[slot]).wait()
    return _load_row_tiled(ybuf.at[slot], TS)


def _layer_kernel(*refs, layer, tiles_per_seq):
    if layer == 0:
        x_ref, gin_ref, bin_ref = refs[:3]
        rest = refs[3:]
    else:
        scur_ref, snext_ref, h1p_ref, y_hbm, g2_ref, b2_ref = refs[:6]
        rest = refs[6:]
    (win_hbm, pw_ref, ps_ref, cw_ref, cb_ref, wout_hbm, g_ref, b_ref, wr_ref, rb_ref, tri_ref,
     h1t_ref, route_ref, cnt_ref,
     vbuf, pwork, zbuf, mixbuf, carry, win_ref, wout_ref, stage, wsem, ybuf, gsem) = rest

    i = pl.program_id(0)
    nt = pl.num_programs(0)

    def stage_weights():
        chunks = []
        for src, dst, width in ((win_hbm, win_ref, D_PROJ), (wout_hbm, wout_ref, D_MODEL)):
            for c in range(D_MODEL // W_STAGE_ROWS):
                chunks.append((src, dst, width, pl.ds(c * W_STAGE_ROWS, W_STAGE_ROWS)))

        def fetch(k):
            src, _, width, rows = chunks[k]
            return pltpu.make_async_copy(src.at[layer, rows], stage.at[k % 2, :, pl.ds(0, width)],
                                         wsem.at[k % 2])

        fetch(0).start()
        for k, (_, dst, width, rows) in enumerate(chunks):
            if k + 1 < len(chunks):
                fetch(k + 1).start()
            fetch(k).wait()
            dst[rows, :] = stage[k % 2, :, 0:width].astype(BF16)
        carry[...] = jnp.zeros(carry.shape, F32)

    first_step_setup = functools.partial(pl.when(i == 0), stage_weights)
    if layer == 0:
        first_step_setup()
        h = _layer_norm(x_ref[...], gin_ref[...], bin_ref[...])
    else:
        y = _prefetch_expert_rows(i, nt, scur_ref, snext_ref, y_hbm, ybuf, gsem,
                                  before_wait=first_step_setup)
        h = _layer_norm(ALPHA * _load_row_tiled(h1p_ref, TS) + y, g2_ref[0], b2_ref[0])

    si = i % tiles_per_seq

    @pl.when(si == 0)
    def _():
        vbuf[0:V_HALO, :] = jnp.zeros((V_HALO, D_POOL), F32)
        zbuf[0:Z_HALO, :] = jnp.zeros((Z_HALO, D_CONV), F32)

    hb = h.astype(BF16)
    vbuf[V_HALO:V_HALO + TS, :] = _dot(hb, win_ref[:, 0:D_POOL])
    gate_b = _dot(hb, win_ref[:, D_POOL:D_POOL + D_CONV])
    gate_c = _dot(hb, win_ref[:, D_POOL + D_CONV:D_POOL + 2 * D_CONV])
    u = _dot(hb, win_ref[:, D_POOL + 2 * D_CONV:D_PROJ])
    z = gate_c * u
    zbuf[Z_HALO:Z_HALO + TS, :] = z

    src = vbuf
    for level in range(1, len(POOL_WINDOWS)):
        dst = pwork.at[(level - 1) % 2]
        lo = SUBLANES * level
        shift = 2 ** (level - 1)
        n = TS + V_HALO - lo
        cols = pl.ds(level * POOL_GROUP, D_POOL - level * POOL_GROUP)
        dst[pl.ds(lo, n), cols] = src[pl.ds(lo, n), cols] + src[pl.ds(lo - shift, n), cols]
        src = dst
    tpos = si * TS + lax.broadcasted_iota(I32, (TS, 1), 0)
    for g, win in enumerate(POOL_WINDOWS):
        cols = pl.ds(g * POOL_GROUP, POOL_GROUP)
        cur = vbuf[pl.ds(V_HALO, TS), cols]
        half = vbuf if g == 0 else pwork.at[(g - 1) % 2]
        acc = half[pl.ds(V_HALO, TS), cols] + half[pl.ds(V_HALO - win // 2, TS), cols]
        denom = jnp.minimum(tpos + 1, win).astype(F32)
        pooled = acc / denom - cur
        mixed = _dot(pooled.astype(BF16), pw_ref[0, g].astype(BF16)) * ps_ref[0, :, cols]
        mixbuf[:, cols] = mixed.astype(BF16)

    yc = (cw_ref[0, 2:3, :] * z
          + cw_ref[0, 1:2, :] * zbuf[pl.ds(Z_HALO - 1, TS), :]
          + cw_ref[0, 0:1, :] * zbuf[pl.ds(Z_HALO - 2, TS), :])
    mixbuf[:, D_POOL:D_POOL + D_CONV] = (gate_b * (yc + cb_ref[0])).astype(BF16)

    vbuf[0:V_HALO, :] = vbuf[TS:TS + V_HALO, :]
    zbuf[0:Z_HALO, :] = zbuf[TS:TS + Z_HALO, :]

    mix = _dot(mixbuf[...], wout_ref[...])
    h1 = _layer_norm(ALPHA * h + mix, g_ref[0], b_ref[0])
    _store_row_tiled(h1t_ref, h1)

    logits = lax.dot_general(wr_ref[...].astype(BF16), h1.astype(BF16), (((1,), (1,)), ((), ())),
                             preferred_element_type=F32)
    mx = jnp.max(logits, axis=0, keepdims=True)
    ex = jnp.exp(logits - mx)
    probs = ex / jnp.sum(ex, axis=0, keepdims=True)
    gidx, i1, i2 = _route(probs + rb_ref[...])
    cls = _pair_class(gidx, i1, i2)

    ciota = lax.broadcasted_iota(I32, (CLASS_ROWS, TS), 0)
    onehot = jnp.where(ciota == cls, 1.0, 0.0).astype(F32)
    before = _dot(onehot.astype(BF16), tri_ref[...])
    total = jnp.sum(onehot, axis=1, keepdims=True)
    c = carry[:, 0:1]
    rank = jnp.sum(onehot * (c + before), axis=0, keepdims=True)
    route_ref[0:1, :] = cls
    route_ref[1:2, :] = rank.astype(I32)
    newc = jnp.broadcast_to(c + total, carry.shape)
    carry[...] = newc
    cnt_ref[...] = newc


def _layer_call(layer, bsz, seq, prev, w_in, pool_w, pool_scale, conv_w, conv_b, w_out, ln_g, ln_b,
                wr_t, rbias, tri):
    t = bsz * seq
    nt = t // TS
    const2 = lambda i: (0, 0)
    lsel3 = lambda i: (layer, 0, 0)
    lsel4 = lambda i: (layer, 0, 0, 0)
    tile = lambda i: (i, 0)
    lane_tile = lambda i: (0, i)
    if layer == 0:
        x, gin, bin_ = prev
        head_args = (x, gin, bin_)
        head_specs = [
            pl.BlockSpec((TS, D_MODEL), tile),
            pl.BlockSpec((1, D_MODEL), const2),
            pl.BlockSpec((1, D_MODEL), const2),
        ]
    else:
        slot_prev, h1t_prev, y_prev, g2, b2 = prev
        head_args = (slot_prev, slot_prev, h1t_prev, y_prev, g2, b2)
        head_specs = [
            pl.BlockSpec((1, 1, TS), lambda i: (i, 0, 0), memory_space=pltpu.SMEM),
            pl.BlockSpec((1, 1, TS), lambda i: (jnp.minimum(i + 1, nt - 1), 0, 0),
                         memory_space=pltpu.SMEM),
            pl.BlockSpec((TS * ROW_TILE, LANES), tile),
            pl.BlockSpec(memory_space=pl.ANY),
            pl.BlockSpec((1, 1, D_MODEL), lambda i: (layer - 1, 0, 0)),
            pl.BlockSpec((1, 1, D_MODEL), lambda i: (layer - 1, 0, 0)),
        ]
    return pl.pallas_call(
        functools.partial(_layer_kernel, layer=layer, tiles_per_seq=seq // TS),
        grid=(nt,),
        in_specs=head_specs + [
            pl.BlockSpec(memory_space=pl.ANY),
            pl.BlockSpec((1, len(POOL_WINDOWS), POOL_GROUP, POOL_GROUP), lsel4),
            pl.BlockSpec((1, 1, D_POOL), lsel3),
            pl.BlockSpec((1, CONV_WIDTH, D_CONV), lsel3),
            pl.BlockSpec((1, 1, D_CONV), lsel3),
            pl.BlockSpec(memory_space=pl.ANY),
            pl.BlockSpec((1, 1, D_MODEL), lsel3),
            pl.BlockSpec((1, 1, D_MODEL), lsel3),
            pl.BlockSpec((N_EXPERTS, D_MODEL), const2),
            pl.BlockSpec((N_EXPERTS, 1), const2),
            pl.BlockSpec((TS, TS), const2),
        ],
        out_specs=[
            pl.BlockSpec((TS * ROW_TILE, LANES), tile),
            pl.BlockSpec((2, TS), lane_tile),
            pl.BlockSpec((CLASS_ROWS, LANES), const2),
        ],
        out_shape=[
            jax.ShapeDtypeStruct((t * ROW_TILE, LANES), F32),
            jax.ShapeDtypeStruct((2, t), I32),
            jax.ShapeDtypeStruct((CLASS_ROWS, LANES), F32),
        ],
        scratch_shapes=[
            pltpu.VMEM((TS + V_HALO, D_POOL), F32),
            pltpu.VMEM((2, TS + V_HALO, D_POOL), F32),
            pltpu.VMEM((TS + Z_HALO, D_CONV), F32),
            pltpu.VMEM((TS, D_MODEL), BF16),
            pltpu.VMEM((CLASS_ROWS, LANES), F32),
            pltpu.VMEM((D_MODEL, D_PROJ), BF16),
            pltpu.VMEM((D_MODEL, D_MODEL), BF16),
            pltpu.VMEM((2, W_STAGE_ROWS, D_PROJ), F32),
            pltpu.SemaphoreType.DMA((2,)),
            pltpu.VMEM((2, TS * ROW_TILE, LANES), F32),
            pltpu.SemaphoreType.DMA((2,)),
        ],
        compiler_params=pltpu.CompilerParams(
            dimension_semantics=("arbitrary",), vmem_limit_bytes=VMEM_LIMIT),
        name="layer_mix_route",
    )(*head_args, w_in, pool_w, pool_scale, conv_w, conv_b, w_out, ln_g, ln_b, wr_t, rbias, tri)


TABLE_EA, TABLE_EB, TABLE_VALID, TABLE_NUSED = 0, 1, 2, 3


def _dispatch_kernel(cnt_ref, route_ref, slot_ref, table_ref):
    log_bm = BM.bit_length() - 1
    starts, ends, fills = [], [], []
    end = jnp.zeros((1, LANES), I32)
    for k in range(N_CLASSES):
        count = cnt_ref[k:k + 1, :].astype(I32)
        starts.append(end)
        fills.append(end + count)
        end = end + lax.shift_left(lax.shift_right_logical(count + (BM - 1), log_bm), log_bm)
        ends.append(end)
    nused = lax.shift_right_logical(end, log_bm)

    cls = route_ref[0:1, :]
    start = jnp.zeros(cls.shape, I32)
    for k in range(N_CLASSES):
        start = jnp.where(cls == k, starts[k][:, 0:1], start)
    slot_ref[...] = (start + route_ref[1:2, :]) * ROW_TILE

    step = jnp.minimum(lax.broadcasted_iota(I32, (1, LANES), 1), nused - 1)
    first_row = step * BM
    c = jnp.zeros((1, LANES), I32)
    fill = fills[0]
    for k in range(N_CLASSES - 1):
        beyond = first_row >= ends[k]
        c = c + beyond.astype(I32)
        fill = jnp.where(beyond, fills[k + 1], fill)
    g = sum((c >= k * PAIRS_PER_GROUP).astype(I32) for k in range(1, N_EXPERT_GROUPS))
    p = c - g * PAIRS_PER_GROUP
    ge3 = (p >= 3).astype(I32)
    ge5 = (p >= 5).astype(I32)
    table_ref[TABLE_EA:TABLE_EA + 1, :] = g * EXPERTS_PER_GROUP + ge3 + ge5
    table_ref[TABLE_EB:TABLE_EB + 1, :] = g * EXPERTS_PER_GROUP + p + 1 - 2 * ge3 - ge5
    table_ref[TABLE_VALID:TABLE_VALID + 1, :] = jnp.clip(fill - first_row, 0, BM)
    table_ref[TABLE_NUSED:TABLE_NUSED + 1, :] = nused
    table_ref[TABLE_NUSED + 1:, :] = jnp.zeros((SUBLANES - TABLE_NUSED - 1, LANES), I32)


def _dispatch_plan(cnt, route):
    t = route.shape[1]
    return pl.pallas_call(
        _dispatch_kernel,
        grid=(1,),
        in_specs=[pl.BlockSpec((CLASS_ROWS, LANES), lambda i: (0, 0)),
                  pl.BlockSpec((2, t), lambda i: (0, 0))],
        out_specs=[pl.BlockSpec((1, t), lambda i: (0, 0)),
                   pl.BlockSpec((SUBLANES, LANES), lambda i: (0, 0))],
        out_shape=[jax.ShapeDtypeStruct((1, t), I32), jax.ShapeDtypeStruct((SUBLANES, LANES), I32)],
        name="dispatch_plan",
    )(cnt, route)


def _scatter_kernel(slot_ref, h1t_ref, xin_ref, xout_ref, sem):
    del xin_ref
    for t in range(TS):
        dst = _row(xout_ref, pl.multiple_of(slot_ref[0, 0, t], ROW_TILE))
        pltpu.make_async_copy(_row(h1t_ref, t * ROW_TILE), dst, sem).start(priority=t % DMA_QUEUES)
    pltpu.make_async_copy(h1t_ref, xout_ref.at[pl.ds(0, TS * ROW_TILE)], sem).wait()


def _scatter_rows(slot3, h1t, xbuf):
    nt = slot3.shape[0]
    return pl.pallas_call(
        _scatter_kernel,
        grid=(nt,),
        in_specs=[
            pl.BlockSpec((1, 1, TS), lambda i: (i, 0, 0), memory_space=pltpu.SMEM),
            pl.BlockSpec((TS * ROW_TILE, LANES), lambda i: (i, 0)),
            pl.BlockSpec(memory_space=pl.ANY),
        ],
        out_specs=pl.BlockSpec(memory_space=pl.ANY),
        out_shape=jax.ShapeDtypeStruct(xbuf.shape, xbuf.dtype),
        scratch_shapes=[pltpu.SemaphoreType.DMA(())],
        input_output_aliases={2: 0},
        compiler_params=pltpu.CompilerParams(dimension_semantics=("arbitrary",)),
        name="scatter_rows",
    )(slot3, h1t, xbuf)


def _expert_kernel(ea_ref, eb_ref, valid_ref, nu_ref, x_ref, w1_hbm, w3_hbm, w2_hbm, wr_ref, y_ref,
                   wup, w2a, w2b, s1a, s3a, s2a, s1b, s3b, s2b, wsem, *, layer):
    i = pl.program_id(0)
    nb = pl.num_programs(0)
    nu = nu_ref[0]
    used = i < nu
    ea = ea_ref[i]
    eb = eb_ref[i]

    def swap_weights(e_ref, slot, live, staged):
        e = e_ref[i]

        def fetch(expert):
            return [pltpu.make_async_copy(src.at[layer, expert], dst, wsem.at[slot])
                    for src, dst in zip((w1_hbm, w3_hbm, w2_hbm), staged)]

        @pl.when(used & (i == 0))
        def _():
            for cp in fetch(e):
                cp.start()

        @pl.when(used & ((i == 0) | (e != e_ref[jnp.maximum(i - 1, 0)])))
        def _():
            for cp in fetch(e):
                cp.wait()
            for dst, src in zip(live, staged):
                dst[...] = src[...].astype(BF16)
            nxt = lax.while_loop(lambda j: (j < nu) & (e_ref[jnp.minimum(j, nb - 1)] == e),
                                 lambda j: j + 1, i + 1)

            @pl.when(nxt < nu)
            def _():
                for cp in fetch(e_ref[jnp.minimum(nxt, nb - 1)]):
                    cp.start()

    def up_cols(k):
        return wup.at[:, pl.ds(k * D_EXPERT, D_EXPERT)]

    swap_weights(ea_ref, 0, (up_cols(0), up_cols(2), w2a), (s1a, s3a, s2a))
    swap_weights(eb_ref, 1, (up_cols(1), up_cols(3), w2b), (s1b, s3b, s2b))

    def ffn_rows(rows):
        sublanes = pl.ds(0, rows * ROW_TILE)
        xb = _load_row_tiled(x_ref.at[sublanes], rows).astype(BF16)

        logits = _dot(xb, wr_ref[...].astype(BF16))
        lane = lax.broadcasted_iota(I32, logits.shape, 1)
        logits = jnp.where(lane < N_EXPERTS, logits, -jnp.inf)
        ex = jnp.exp(logits - jnp.max(logits, axis=1, keepdims=True))
        probs = ex / jnp.sum(ex, axis=1, keepdims=True)
        pa = jnp.sum(jnp.where(lane == ea, probs, 0.0), axis=1, keepdims=True)
        pb = jnp.sum(jnp.where(lane == eb, probs, 0.0), axis=1, keepdims=True)
        den = pa + pb

        up = _dot(xb, wup[...])
        a = up[:, 0:2 * D_EXPERT]
        hid = (a * jax.nn.sigmoid(a) * up[:, 2 * D_EXPERT:4 * D_EXPERT]).astype(BF16)
        y = ((pa / den) * _dot(hid[:, 0:D_EXPERT], w2a[...])
             + (pb / den) * _dot(hid[:, D_EXPERT:2 * D_EXPERT], w2b[...]))
        _store_row_tiled(y_ref.at[sublanes], y)

    few = valid_ref[i] <= BM // 2

    @pl.when(used & jnp.logical_not(few))
    def _():
        ffn_rows(BM)

    @pl.when(used & few)
    def _():
        ffn_rows(BM // 2)
        rest = pl.ds(BM // 2 * ROW_TILE, BM // 2 * ROW_TILE)
        y_ref[rest, :] = jnp.zeros((BM // 2 * ROW_TILE, LANES), F32)


def _expert_ffn(layer, blk_ea, blk_eb, blk_valid, nused, xbuf, w1, w3, w2, wr_pad):
    n_blocks = xbuf.shape[0] // (BM * ROW_TILE)
    used_blk = lambda i, ea, eb, va, nu: (jnp.minimum(i, nu[0] - 1), 0)
    up = (D_MODEL, D_EXPERT)
    down = (D_EXPERT, D_MODEL)
    expert_slot = lambda dtype: [pltpu.VMEM(up, dtype), pltpu.VMEM(up, dtype), pltpu.VMEM(down, dtype)]
    grid_spec = pltpu.PrefetchScalarGridSpec(
        num_scalar_prefetch=4,
        grid=(n_blocks,),
        in_specs=[
            pl.BlockSpec((BM * ROW_TILE, LANES), used_blk),
            pl.BlockSpec(memory_space=pl.ANY),
            pl.BlockSpec(memory_space=pl.ANY),
            pl.BlockSpec(memory_space=pl.ANY),
            pl.BlockSpec((D_MODEL, LANES), lambda i, ea, eb, va, nu: (0, 0)),
        ],
        out_specs=pl.BlockSpec((BM * ROW_TILE, LANES), used_blk),
        scratch_shapes=([pltpu.VMEM((D_MODEL, 4 * D_EXPERT), BF16), pltpu.VMEM(down, BF16),
                         pltpu.VMEM(down, BF16)] + expert_slot(F32) + expert_slot(F32)
                        + [pltpu.SemaphoreType.DMA((2,))]),
    )
    return pl.pallas_call(
        functools.partial(_expert_kernel, layer=layer),
        grid_spec=grid_spec,
        out_shape=jax.ShapeDtypeStruct(xbuf.shape, F32),
        input_output_aliases={4: 0},
        compiler_params=pltpu.CompilerParams(
            dimension_semantics=("arbitrary",), vmem_limit_bytes=VMEM_LIMIT),
        name="expert_ffn",
    )(blk_ea, blk_eb, blk_valid, nused, xbuf, w1, w3, w2, wr_pad)


def _final_kernel(scur_ref, snext_ref, h1p_ref, y_hbm, g2_ref, b2_ref, o_ref, ybuf, gsem):
    i = pl.program_id(0)
    y = _prefetch_expert_rows(i, pl.num_programs(0), scur_ref, snext_ref, y_hbm, ybuf, gsem)
    o_ref[...] = _layer_norm(ALPHA * _load_row_tiled(h1p_ref, TS) + y, g2_ref[0], b2_ref[0])


def _final_call(slot_prev, h1t_prev, y_prev, g2, b2):
    t = h1t_prev.shape[0] // ROW_TILE
    nt = t // TS
    last = DEPTH - 1
    return pl.pallas_call(
        _final_kernel,
        grid=(nt,),
        in_specs=[
            pl.BlockSpec((1, 1, TS), lambda i: (i, 0, 0), memory_space=pltpu.SMEM),
            pl.BlockSpec((1, 1, TS), lambda i: (jnp.minimum(i + 1, nt - 1), 0, 0),
                         memory_space=pltpu.SMEM),
            pl.BlockSpec((TS * ROW_TILE, LANES), lambda i: (i, 0)),
            pl.BlockSpec(memory_space=pl.ANY),
            pl.BlockSpec((1, 1, D_MODEL), lambda i: (last, 0, 0)),
            pl.BlockSpec((1, 1, D_MODEL), lambda i: (last, 0, 0)),
        ],
        out_specs=pl.BlockSpec((TS, D_MODEL), lambda i: (i, 0)),
        out_shape=jax.ShapeDtypeStruct((t, D_MODEL), F32),
        scratch_shapes=[pltpu.VMEM((2, TS * ROW_TILE, LANES), F32), pltpu.SemaphoreType.DMA((2,))],
        compiler_params=pltpu.CompilerParams(
            dimension_semantics=("arbitrary",), vmem_limit_bytes=VMEM_LIMIT),
        name="final_combine_ln",
    )(slot_prev, slot_prev, h1t_prev, y_prev, g2, b2)


def kernel(x, ln_in_g, ln_in_b, w_in, pool_w, pool_scale, conv_w, conv_b, w_out, ln1_g, ln1_b,
           w_router, router_bias, exp_w1, exp_w3, exp_w2, ln2_g, ln2_b):
    bsz, seq, d = x.shape
    t = bsz * seq
    nt = t // TS
    n_rows = t + N_CLASSES * BM
    nb = n_rows // BM
    assert nb <= LANES

    per_layer_row = lambda v: v.reshape(DEPTH, 1, -1)
    tri = (lax.broadcasted_iota(I32, (TS, TS), 0)
           < lax.broadcasted_iota(I32, (TS, TS), 1)).astype(BF16)
    wr_t = w_router.T
    wr_pad = jnp.pad(w_router, ((0, 0), (0, LANES - N_EXPERTS)))
    rbias = router_bias.reshape(N_EXPERTS, 1).astype(F32)
    pool_scale3, conv_b3 = per_layer_row(pool_scale), per_layer_row(conv_b)
    ln1_g3, ln1_b3 = per_layer_row(ln1_g), per_layer_row(ln1_b)
    ln2_g3, ln2_b3 = per_layer_row(ln2_g), per_layer_row(ln2_b)

    prev = (x.reshape(t, d), ln_in_g.reshape(1, -1), ln_in_b.reshape(1, -1))
    xbuf = jnp.zeros((n_rows * ROW_TILE, LANES), F32)
    for l in range(DEPTH):
        h1t, route, cnt = _layer_call(l, bsz, seq, prev, w_in, pool_w, pool_scale3, conv_w, conv_b3,
                                          w_out, ln1_g3, ln1_b3, wr_t, rbias, tri)
        slots, table = _dispatch_plan(cnt, route)
        slot3 = slots.reshape(nt, 1, TS)
        blk_ea, blk_eb, blk_valid = table[TABLE_EA, :nb], table[TABLE_EB, :nb], table[TABLE_VALID, :nb]
        nused = table[TABLE_NUSED, :1]
        xbuf = _scatter_rows(slot3, h1t, xbuf)
        xbuf = _expert_ffn(l, blk_ea, blk_eb, blk_valid, nused, xbuf, exp_w1, exp_w3, exp_w2, wr_pad)
        prev = (slot3, h1t, xbuf, ln2_g3, ln2_b3)
    out = _final_call(*prev)
    return out.reshape(bsz, seq, d)
```

```python
import functools

import jax
import jax.numpy as jnp
from jax import lax
from jax.experimental import pallas as pl
from jax.experimental.pallas import tpu as pltpu

D_MODEL = 1024
DEPTH = 4
D_POOL = 512
POOL_WINDOWS = (2, 4, 8, 16)
POOL_GROUP = 128
D_CONV = 512
CONV_WIDTH = 3
D_PROJ = D_POOL + 3 * D_CONV
N_EXPERTS = 16
N_EXPERT_GROUPS = 4
EXPERTS_PER_GROUP = 4
PAIRS_PER_GROUP = 6
N_CLASSES = N_EXPERT_GROUPS * PAIRS_PER_GROUP
CLASS_ROWS = 32
D_EXPERT = 512
ALPHA = float((2 * DEPTH) ** 0.25)
LN_EPS = 1e-5

F32 = jnp.float32
BF16 = jnp.bfloat16
I32 = jnp.int32

LANES = 128
ROW_TILE = D_MODEL // LANES
TS = 512
SUBLANES = 8
V_HALO = 32
Z_HALO = 8
BM = 256
W_STAGE_ROWS = 256
DMA_QUEUES = 2
VMEM_LIMIT = 56 * 1024 * 1024


def _layer_norm(x, g, b):
    mu = jnp.mean(x, axis=-1, keepdims=True)
    xc = x - mu
    var = jnp.mean(xc * xc, axis=-1, keepdims=True)
    return xc * lax.rsqrt(var + LN_EPS) * g + b


def _dot(a, b):
    return jnp.dot(a, b, preferred_element_type=F32)


def _store_row_tiled(ref, value):
    n = value.shape[0]
    for j in range(ROW_TILE):
        ref[pl.ds(j, n, stride=ROW_TILE), :] = value[:, j * LANES:(j + 1) * LANES]


def _load_row_tiled(ref, n):
    return jnp.concatenate([ref[pl.ds(j, n, stride=ROW_TILE), :] for j in range(ROW_TILE)], axis=1)


def _row(ref, first_sublane):
    return ref.at[pl.ds(first_sublane, ROW_TILE)]


def _route(sel):
    rows = [sel[e:e + 1, :] for e in range(N_EXPERTS)]
    gscore = []
    for g in range(N_EXPERT_GROUPS):
        a, b, c, d = rows[4 * g:4 * g + 4]
        hi1, lo1 = jnp.maximum(a, b), jnp.minimum(a, b)
        hi2, lo2 = jnp.maximum(c, d), jnp.minimum(c, d)
        m1 = jnp.maximum(hi1, hi2)
        m2 = jnp.maximum(jnp.minimum(hi1, hi2), jnp.maximum(lo1, lo2))
        gscore.append(m1 + m2)
    best = gscore[0]
    gidx = jnp.zeros(best.shape, I32)
    for g in range(1, N_EXPERT_GROUPS):
        upd = gscore[g] > best
        best = jnp.where(upd, gscore[g], best)
        gidx = jnp.where(upd, g, gidx)
    neg = jnp.full(best.shape, -jnp.inf, F32)
    masked = [jnp.where(gidx == (e // EXPERTS_PER_GROUP), rows[e], neg) for e in range(N_EXPERTS)]

    def top1(vals):
        bv = neg
        bi = jnp.zeros(best.shape, I32)
        for e in range(N_EXPERTS):
            upd = vals[e] > bv
            bv = jnp.where(upd, vals[e], bv)
            bi = jnp.where(upd, e, bi)
        return bi

    i1 = top1(masked)
    i2 = top1([jnp.where(i1 == e, neg, masked[e]) for e in range(N_EXPERTS)])
    return gidx, i1, i2


def _pair_class(gidx, i1, i2):
    a = jnp.minimum(i1, i2) - gidx * EXPERTS_PER_GROUP
    b = jnp.maximum(i1, i2) - gidx * EXPERTS_PER_GROUP
    base = jnp.where(a == 0, 0, jnp.where(a == 1, 2, 3))
    return gidx * PAIRS_PER_GROUP + base + b - 1


def _prefetch_expert_rows(i, nt, scur_ref, snext_ref, y_hbm, ybuf, gsem, before_wait=None):
    slot = i % 2

    def gather(slot_ref, to_slot, t, t_sublane):
        src = _row(y_hbm, pl.multiple_of(slot_ref[0, 0, t], ROW_TILE))
        return pltpu.make_async_copy(src, _row(ybuf.at[to_slot], t_sublane), gsem.at[to_slot])

    @pl.when(i == 0)
    def _():
        def body(t, carry):
            gather(scur_ref, 0, t, pl.multiple_of(t * ROW_TILE, ROW_TILE)).start()
            return carry
        lax.fori_loop(0, TS, body, 0, unroll=8)

    @pl.when(i + 1 < nt)
    def _():
        for t in range(TS):
            gather(snext_ref, 1 - slot, t, t * ROW_TILE).start(priority=t % DMA_QUEUES)

    if before_wait is not None:
        before_wait()
    pltpu.make_async_copy(y_hbm.at[pl.ds(0, TS * ROW_TILE)], ybuf.at[slot], gsem.at[slot]).wait()
    return _load_row_tiled(ybuf.at[slot], TS)


def _layer_kernel(*refs, layer, tiles_per_seq):
    if layer == 0:
        x_ref, gin_ref, bin_ref = refs[:3]
        rest = refs[3:]
    else:
        scur_ref, snext_ref, h1p_ref, y_hbm, g2_ref, b2_ref = refs[:6]
        rest = refs[6:]
    (win_hbm, pw_ref, ps_ref, cw_ref, cb_ref, wout_hbm, g_ref, b_ref, wr_ref, rb_ref, tri_ref,
     h1t_ref, route_ref, cnt_ref,
     vbuf, pwork, zbuf, mixbuf, carry, win_ref, wout_ref, stage, wsem, ybuf, gsem) = rest

    i = pl.program_id(0)
    nt = pl.num_programs(0)

    def stage_weights():
        chunks = []
        for src, dst, width in ((win_hbm, win_ref, D_PROJ), (wout_hbm, wout_ref, D_MODEL)):
            for c in range(D_MODEL // W_STAGE_ROWS):
                chunks.append((src, dst, width, pl.ds(c * W_STAGE_ROWS, W_STAGE_ROWS)))

        def fetch(k):
            src, _, width, rows = chunks[k]
            return pltpu.make_async_copy(src.at[layer, rows], stage.at[k % 2, :, pl.ds(0, width)],
                                         wsem.at[k % 2])

        fetch(0).start()
        for k, (_, dst, width, rows) in enumerate(chunks):
            if k + 1 < len(chunks):
                fetch(k + 1).start()
            fetch(k).wait()
            dst[rows, :] = stage[k % 2, :, 0:width].astype(BF16)
        carry[...] = jnp.zeros(carry.shape, F32)

    first_step_setup = functools.partial(pl.when(i == 0), stage_weights)
    if layer == 0:
        first_step_setup()
        h = _layer_norm(x_ref[...], gin_ref[...], bin_ref[...])
    else:
        y = _prefetch_expert_rows(i, nt, scur_ref, snext_ref, y_hbm, ybuf, gsem,
                                  before_wait=first_step_setup)
        h = _layer_norm(ALPHA * _load_row_tiled(h1p_ref, TS) + y, g2_ref[0], b2_ref[0])

    si = i % tiles_per_seq

    @pl.when(si == 0)
    def _():
        vbuf[0:V_HALO, :] = jnp.zeros((V_HALO, D_POOL), F32)
        zbuf[0:Z_HALO, :] = jnp.zeros((Z_HALO, D_CONV), F32)

    hb = h.astype(BF16)
    vbuf[V_HALO:V_HALO + TS, :] = _dot(hb, win_ref[:, 0:D_POOL])
    gate_b = _dot(hb, win_ref[:, D_POOL:D_POOL + D_CONV])
    gate_c = _dot(hb, win_ref[:, D_POOL + D_CONV:D_POOL + 2 * D_CONV])
    u = _dot(hb, win_ref[:, D_POOL + 2 * D_CONV:D_PROJ])
    z = gate_c * u
    zbuf[Z_HALO:Z_HALO + TS, :] = z

    src = vbuf
    for level in range(1, len(POOL_WINDOWS)):
        dst = pwork.at[(level - 1) % 2]
        lo = SUBLANES * level
        shift = 2 ** (level - 1)
        n = TS + V_HALO - lo
        cols = pl.ds(level * POOL_GROUP, D_POOL - level * POOL_GROUP)
        dst[pl.ds(lo, n), cols] = src[pl.ds(lo, n), cols] + src[pl.ds(lo - shift, n), cols]
        src = dst
    tpos = si * TS + lax.broadcasted_iota(I32, (TS, 1), 0)
    for g, win in enumerate(POOL_WINDOWS):
        cols = pl.ds(g * POOL_GROUP, POOL_GROUP)
        cur = vbuf[pl.ds(V_HALO, TS), cols]
        half = vbuf if g == 0 else pwork.at[(g - 1) % 2]
        acc = half[pl.ds(V_HALO, TS), cols] + half[pl.ds(V_HALO - win // 2, TS), cols]
        denom = jnp.minimum(tpos + 1, win).astype(F32)
        pooled = acc / denom - cur
        mixed = _dot(pooled.astype(BF16), pw_ref[0, g].astype(BF16)) * ps_ref[0, :, cols]
        mixbuf[:, cols] = mixed.astype(BF16)

    yc = (cw_ref[0, 2:3, :] * z
          + cw_ref[0, 1:2, :] * zbuf[pl.ds(Z_HALO - 1, TS), :]
          + cw_ref[0, 0:1, :] * zbuf[pl.ds(Z_HALO - 2, TS), :])
    mixbuf[:, D_POOL:D_POOL + D_CONV] = (gate_b * (yc + cb_ref[0])).astype(BF16)

    vbuf[0:V_HALO, :] = vbuf[TS:TS + V_HALO, :]
    zbuf[0:Z_HALO, :] = zbuf[TS:TS + Z_HALO, :]

    mix = _dot(mixbuf[...], wout_ref[...])
    h1 = _layer_norm(ALPHA * h + mix, g_ref[0], b_ref[0])
    _store_row_tiled(h1t_ref, h1)

    logits = lax.dot_general(wr_ref[...].astype(BF16), h1.astype(BF16), (((1,), (1,)), ((), ())),
                             preferred_element_type=F32)
    mx = jnp.max(logits, axis=0, keepdims=True)
    ex = jnp.exp(logits - mx)
    probs = ex / jnp.sum(ex, axis=0, keepdims=True)
    gidx, i1, i2 = _route(probs + rb_ref[...])
    cls = _pair_class(gidx, i1, i2)

    ciota = lax.broadcasted_iota(I32, (CLASS_ROWS, TS), 0)
    onehot = jnp.where(ciota == cls, 1.0, 0.0).astype(F32)
    before = _dot(onehot.astype(BF16), tri_ref[...])
    total = jnp.sum(onehot, axis=1, keepdims=True)
    c = carry[:, 0:1]
    rank = jnp.sum(onehot * (c + before), axis=0, keepdims=True)
    route_ref[0:1, :] = cls
    route_ref[1:2, :] = rank.astype(I32)
    newc = jnp.broadcast_to(c + total, carry.shape)
    carry[...] = newc
    cnt_ref[...] = newc


def _layer_call(layer, bsz, seq, prev, w_in, pool_w, pool_scale, conv_w, conv_b, w_out, ln_g, ln_b,
                wr_t, rbias, tri):
    t = bsz * seq
    nt = t // TS
    const2 = lambda i: (0, 0)
    lsel3 = lambda i: (layer, 0, 0)
    lsel4 = lambda i: (layer, 0, 0, 0)
    tile = lambda i: (i, 0)
    lane_tile = lambda i: (0, i)
    if layer == 0:
        x, gin, bin_ = prev
        head_args = (x, gin, bin_)
        head_specs = [
            pl.BlockSpec((TS, D_MODEL), tile),
            pl.BlockSpec((1, D_MODEL), const2),
            pl.BlockSpec((1, D_MODEL), const2),
        ]
    else:
        slot_prev, h1t_prev, y_prev, g2, b2 = prev
        head_args = (slot_prev, slot_prev, h1t_prev, y_prev, g2, b2)
        head_specs = [
            pl.BlockSpec((1, 1, TS), lambda i: (i, 0, 0), memory_space=pltpu.SMEM),
            pl.BlockSpec((1, 1, TS), lambda i: (jnp.minimum(i + 1, nt - 1), 0, 0),
                         memory_space=pltpu.SMEM),
            pl.BlockSpec((TS * ROW_TILE, LANES), tile),
            pl.BlockSpec(memory_space=pl.ANY),
            pl.BlockSpec((1, 1, D_MODEL), lambda i: (layer - 1, 0, 0)),
            pl.BlockSpec((1, 1, D_MODEL), lambda i: (layer - 1, 0, 0)),
        ]
    return pl.pallas_call(
        functools.partial(_layer_kernel, layer=layer, tiles_per_seq=seq // TS),
        grid=(nt,),
        in_specs=head_specs + [
            pl.BlockSpec(memory_space=pl.ANY),
            pl.BlockSpec((1, len(POOL_WINDOWS), POOL_GROUP, POOL_GROUP), lsel4),
            pl.BlockSpec((1, 1, D_POOL), lsel3),
            pl.BlockSpec((1, CONV_WIDTH, D_CONV), lsel3),
            pl.BlockSpec((1, 1, D_CONV), lsel3),
            pl.BlockSpec(memory_space=pl.ANY),
            pl.BlockSpec((1, 1, D_MODEL), lsel3),
            pl.BlockSpec((1, 1, D_MODEL), lsel3),
            pl.BlockSpec((N_EXPERTS, D_MODEL), const2),
            pl.BlockSpec((N_EXPERTS, 1), const2),
            pl.BlockSpec((TS, TS), const2),
        ],
        out_specs=[
            pl.BlockSpec((TS * ROW_TILE, LANES), tile),
            pl.BlockSpec((2, TS), lane_tile),
            pl.BlockSpec((CLASS_ROWS, LANES), const2),
        ],
        out_shape=[
            jax.ShapeDtypeStruct((t * ROW_TILE, LANES), F32),
            jax.ShapeDtypeStruct((2, t), I32),
            jax.ShapeDtypeStruct((CLASS_ROWS, LANES), F32),
        ],
        scratch_shapes=[
            pltpu.VMEM((TS + V_HALO, D_POOL), F32),
            pltpu.VMEM((2, TS + V_HALO, D_POOL), F32),
            pltpu.VMEM((TS + Z_HALO, D_CONV), F32),
            pltpu.VMEM((TS, D_MODEL), BF16),
            pltpu.VMEM((CLASS_ROWS, LANES), F32),
            pltpu.VMEM((D_MODEL, D_PROJ), BF16),
            pltpu.VMEM((D_MODEL, D_MODEL), BF16),
            pltpu.VMEM((2, W_STAGE_ROWS, D_PROJ), F32),
            pltpu.SemaphoreType.DMA((2,)),
            pltpu.VMEM((2, TS * ROW_TILE, LANES), F32),
            pltpu.SemaphoreType.DMA((2,)),
        ],
        compiler_params=pltpu.CompilerParams(
            dimension_semantics=("arbitrary",), vmem_limit_bytes=VMEM_LIMIT),
        name="layer_mix_route",
    )(*head_args, w_in, pool_w, pool_scale, conv_w, conv_b, w_out, ln_g, ln_b, wr_t, rbias, tri)


TABLE_EA, TABLE_EB, TABLE_VALID, TABLE_NUSED = 0, 1, 2, 3
TABLE_TAIL_START, TABLE_TAIL_LEN = 4, 5


def _dispatch_kernel(cnt_ref, route_ref, slot_ref, table_ref):
    log_bm = BM.bit_length() - 1
    starts, ends, fills = [], [], []
    end = jnp.zeros((1, LANES), I32)
    for k in range(N_CLASSES):
        count = cnt_ref[k:k + 1, :].astype(I32)
        starts.append(end)
        fills.append(end + count)
        end = end + lax.shift_left(lax.shift_right_logical(count + (BM - 1), log_bm), log_bm)
        ends.append(end)
    nused = lax.shift_right_logical(end, log_bm)

    cls = route_ref[0:1, :]
    start = jnp.zeros(cls.shape, I32)
    for k in range(N_CLASSES):
        start = jnp.where(cls == k, starts[k][:, 0:1], start)
    slot_ref[...] = (start + route_ref[1:2, :]) * ROW_TILE

    step = jnp.minimum(lax.broadcasted_iota(I32, (1, LANES), 1), nused - 1)
    first_row = step * BM
    c = jnp.zeros((1, LANES), I32)
    fill = fills[0]
    for k in range(N_CLASSES - 1):
        beyond = first_row >= ends[k]
        c = c + beyond.astype(I32)
        fill = jnp.where(beyond, fills[k + 1], fill)
    g = sum((c >= k * PAIRS_PER_GROUP).astype(I32) for k in range(1, N_EXPERT_GROUPS))
    p = c - g * PAIRS_PER_GROUP
    ge3 = (p >= 3).astype(I32)
    ge5 = (p >= 5).astype(I32)
    table_ref[TABLE_EA:TABLE_EA + 1, :] = g * EXPERTS_PER_GROUP + ge3 + ge5
    table_ref[TABLE_EB:TABLE_EB + 1, :] = g * EXPERTS_PER_GROUP + p + 1 - 2 * ge3 - ge5
    table_ref[TABLE_VALID:TABLE_VALID + 1, :] = jnp.clip(fill - first_row, 0, BM)
    table_ref[TABLE_NUSED:TABLE_NUSED + 1, :] = nused

    lane = lax.broadcasted_iota(I32, (1, LANES), 1)
    tail_start = jnp.zeros((1, LANES), I32)
    tail_len = jnp.zeros((1, LANES), I32)
    for k in range(N_CLASSES):
        tail_start = jnp.where(lane == k, fills[k], tail_start)
        tail_len = jnp.where(lane == k, ends[k] - fills[k], tail_len)
    table_ref[TABLE_TAIL_START:TABLE_TAIL_START + 1, :] = tail_start
    table_ref[TABLE_TAIL_LEN:TABLE_TAIL_LEN + 1, :] = tail_len
    table_ref[TABLE_TAIL_LEN + 1:, :] = jnp.zeros((SUBLANES - TABLE_TAIL_LEN - 1, LANES), I32)


def _dispatch_plan(cnt, route):
    t = route.shape[1]
    return pl.pallas_call(
        _dispatch_kernel,
        grid=(1,),
        in_specs=[pl.BlockSpec((CLASS_ROWS, LANES), lambda i: (0, 0)),
                  pl.BlockSpec((2, t), lambda i: (0, 0))],
        out_specs=[pl.BlockSpec((1, t), lambda i: (0, 0)),
                   pl.BlockSpec((SUBLANES, LANES), lambda i: (0, 0))],
        out_shape=[jax.ShapeDtypeStruct((1, t), I32), jax.ShapeDtypeStruct((SUBLANES, LANES), I32)],
        name="dispatch_plan",
    )(cnt, route)


def _scatter_kernel(slot_ref, h1t_ref, xin_ref, xout_ref, sem):
    del xin_ref
    for t in range(TS):
        dst = _row(xout_ref, pl.multiple_of(slot_ref[0, 0, t], ROW_TILE))
        pltpu.make_async_copy(_row(h1t_ref, t * ROW_TILE), dst, sem).start(priority=t % DMA_QUEUES)
    pltpu.make_async_copy(h1t_ref, xout_ref.at[pl.ds(0, TS * ROW_TILE)], sem).wait()


def _first_scatter_kernel(tail_start_ref, tail_len_ref, nu_ref, slot_ref, h1t_ref, xout_ref,
                          zeros, sem, zsem):
    i = pl.program_id(0)
    block = BM * ROW_TILE
    n_blocks = xout_ref.shape[0] // block

    def fill_rows(visit):
        for k in range(N_CLASSES):
            length = tail_len_ref[k]
            row = tail_start_ref[k]
            piece = BM // 2
            while piece >= 1:
                take = (length & piece) != 0
                dst = xout_ref.at[pl.ds(pl.multiple_of(row * ROW_TILE, ROW_TILE), piece * ROW_TILE)]
                cp = pltpu.make_async_copy(zeros.at[pl.ds(0, piece * ROW_TILE)], dst, zsem)
                pl.when(take)(functools.partial(visit, cp))
                row = row + jnp.where(take, piece, 0)
                piece //= 2

        def idle_block(j, carry):
            dst = xout_ref.at[pl.ds(pl.multiple_of(j * block, block), block)]
            visit(pltpu.make_async_copy(zeros, dst, zsem))
            return carry
        lax.fori_loop(nu_ref[0], n_blocks, idle_block, 0)

    @pl.when(i == 0)
    def _():
        zeros[...] = jnp.zeros(zeros.shape, F32)
        fill_rows(lambda cp: cp.start())

    for t in range(TS):
        dst = _row(xout_ref, pl.multiple_of(slot_ref[0, 0, t], ROW_TILE))
        pltpu.make_async_copy(_row(h1t_ref, t * ROW_TILE), dst, sem).start(priority=t % DMA_QUEUES)
    pltpu.make_async_copy(h1t_ref, xout_ref.at[pl.ds(0, TS * ROW_TILE)], sem).wait()

    @pl.when(i == 0)
    def _():
        fill_rows(lambda cp: cp.wait())


def _first_scatter_rows(tail_start, tail_len, nused, slot3, h1t, n_rows):
    nt = slot3.shape[0]
    grid_spec = pltpu.PrefetchScalarGridSpec(
        num_scalar_prefetch=3,
        grid=(nt,),
        in_specs=[
            pl.BlockSpec((1, 1, TS), lambda i, *_: (i, 0, 0), memory_space=pltpu.SMEM),
            pl.BlockSpec((TS * ROW_TILE, LANES), lambda i, *_: (i, 0)),
        ],
        out_specs=pl.BlockSpec(memory_space=pl.ANY),
        scratch_shapes=[pltpu.VMEM((BM * ROW_TILE, LANES), F32), pltpu.SemaphoreType.DMA(()),
                        pltpu.SemaphoreType.DMA(())],
    )
    return pl.pallas_call(
        _first_scatter_kernel,
        grid_spec=grid_spec,
        out_shape=jax.ShapeDtypeStruct((n_rows * ROW_TILE, LANES), F32),
        compiler_params=pltpu.CompilerParams(dimension_semantics=("arbitrary",)),
        name="first_scatter_rows",
    )(tail_start, tail_len, nused, slot3, h1t)


def _scatter_rows(slot3, h1t, xbuf):
    nt = slot3.shape[0]
    return pl.pallas_call(
        _scatter_kernel,
        grid=(nt,),
        in_specs=[
            pl.BlockSpec((1, 1, TS), lambda i: (i, 0, 0), memory_space=pltpu.SMEM),
            pl.BlockSpec((TS * ROW_TILE, LANES), lambda i: (i, 0)),
            pl.BlockSpec(memory_space=pl.ANY),
        ],
        out_specs=pl.BlockSpec(memory_space=pl.ANY),
        out_shape=jax.ShapeDtypeStruct(xbuf.shape, xbuf.dtype),
        scratch_shapes=[pltpu.SemaphoreType.DMA(())],
        input_output_aliases={2: 0},
        compiler_params=pltpu.CompilerParams(dimension_semantics=("arbitrary",)),
        name="scatter_rows",
    )(slot3, h1t, xbuf)


def _expert_kernel(ea_ref, eb_ref, valid_ref, nu_ref, x_ref, w1_hbm, w3_hbm, w2_hbm, wr_ref, y_ref,
                   wup, w2a, w2b, s1a, s3a, s2a, s1b, s3b, s2b, wsem, *, layer):
    i = pl.program_id(0)
    nb = pl.num_programs(0)
    nu = nu_ref[0]
    used = i < nu
    ea = ea_ref[i]
    eb = eb_ref[i]

    def swap_weights(e_ref, slot, live, staged):
        e = e_ref[i]

        def fetch(expert):
            return [pltpu.make_async_copy(src.at[layer, expert], dst, wsem.at[slot])
                    for src, dst in zip((w1_hbm, w3_hbm, w2_hbm), staged)]

        @pl.when(used & (i == 0))
        def _():
            for cp in fetch(e):
                cp.start()

        @pl.when(used & ((i == 0) | (e != e_ref[jnp.maximum(i - 1, 0)])))
        def _():
            for cp in fetch(e):
                cp.wait()
            for dst, src in zip(live, staged):
                dst[...] = src[...].astype(BF16)
            nxt = lax.while_loop(lambda j: (j < nu) & (e_ref[jnp.minimum(j, nb - 1)] == e),
                                 lambda j: j + 1, i + 1)

            @pl.when(nxt < nu)
            def _():
                for cp in fetch(e_ref[jnp.minimum(nxt, nb - 1)]):
                    cp.start()

    def up_cols(k):
        return wup.at[:, pl.ds(k * D_EXPERT, D_EXPERT)]

    swap_weights(ea_ref, 0, (up_cols(0), up_cols(2), w2a), (s1a, s3a, s2a))
    swap_weights(eb_ref, 1, (up_cols(1), up_cols(3), w2b), (s1b, s3b, s2b))

    def ffn_rows(rows):
        sublanes = pl.ds(0, rows * ROW_TILE)
        xb = _load_row_tiled(x_ref.at[sublanes], rows).astype(BF16)

        logits = _dot(xb, wr_ref[...].astype(BF16))
        lane = lax.broadcasted_iota(I32, logits.shape, 1)
        logits = jnp.where(lane < N_EXPERTS, logits, -jnp.inf)
        ex = jnp.exp(logits - jnp.max(logits, axis=1, keepdims=True))
        probs = ex / jnp.sum(ex, axis=1, keepdims=True)
        pa = jnp.sum(jnp.where(lane == ea, probs, 0.0), axis=1, keepdims=True)
        pb = jnp.sum(jnp.where(lane == eb, probs, 0.0), axis=1, keepdims=True)
        den = pa + pb

        up = _dot(xb, wup[...])
        a = up[:, 0:2 * D_EXPERT]
        hid = (a * jax.nn.sigmoid(a) * up[:, 2 * D_EXPERT:4 * D_EXPERT]).astype(BF16)
        y = ((pa / den) * _dot(hid[:, 0:D_EXPERT], w2a[...])
             + (pb / den) * _dot(hid[:, D_EXPERT:2 * D_EXPERT], w2b[...]))
        _store_row_tiled(y_ref.at[sublanes], y)

    valid = valid_ref[i]
    for done, rows in zip(range(0, BM, BM // 4), range(BM // 4, BM + 1, BM // 4)):
        @pl.when(used & (valid > done) & (valid <= rows))
        def _(rows=rows):
            ffn_rows(rows)
            if rows < BM:
                rest = pl.ds(rows * ROW_TILE, (BM - rows) * ROW_TILE)
                y_ref[rest, :] = jnp.zeros(((BM - rows) * ROW_TILE, LANES), F32)


def _expert_ffn(layer, blk_ea, blk_eb, blk_valid, nused, xbuf, w1, w3, w2, wr_pad):
    n_blocks = xbuf.shape[0] // (BM * ROW_TILE)
    used_blk = lambda i, ea, eb, va, nu: (jnp.minimum(i, nu[0] - 1), 0)
    up = (D_MODEL, D_EXPERT)
    down = (D_EXPERT, D_MODEL)
    expert_slot = lambda dtype: [pltpu.VMEM(up, dtype), pltpu.VMEM(up, dtype), pltpu.VMEM(down, dtype)]
    grid_spec = pltpu.PrefetchScalarGridSpec(
        num_scalar_prefetch=4,
        grid=(n_blocks,),
        in_specs=[
            pl.BlockSpec((BM * ROW_TILE, LANES), used_blk),
            pl.BlockSpec(memory_space=pl.ANY),
            pl.BlockSpec(memory_space=pl.ANY),
            pl.BlockSpec(memory_space=pl.ANY),
            pl.BlockSpec((D_MODEL, LANES), lambda i, ea, eb, va, nu: (0, 0)),
        ],
        out_specs=pl.BlockSpec((BM * ROW_TILE, LANES), used_blk),
        scratch_shapes=([pltpu.VMEM((D_MODEL, 4 * D_EXPERT), BF16), pltpu.VMEM(down, BF16),
                         pltpu.VMEM(down, BF16)] + expert_slot(F32) + expert_slot(F32)
                        + [pltpu.SemaphoreType.DMA((2,))]),
    )
    return pl.pallas_call(
        functools.partial(_expert_kernel, layer=layer),
        grid_spec=grid_spec,
        out_shape=jax.ShapeDtypeStruct(xbuf.shape, F32),
        input_output_aliases={4: 0},
        compiler_params=pltpu.CompilerParams(
            dimension_semantics=("arbitrary",), vmem_limit_bytes=VMEM_LIMIT),
        name="expert_ffn",
    )(blk_ea, blk_eb, blk_valid, nused, xbuf, w1, w3, w2, wr_pad)


def _final_kernel(scur_ref, snext_ref, h1p_ref, y_hbm, g2_ref, b2_ref, o_ref, ybuf, gsem):
    i = pl.program_id(0)
    y = _prefetch_expert_rows(i, pl.num_programs(0), scur_ref, snext_ref, y_hbm, ybuf, gsem)
    o_ref[...] = _layer_norm(ALPHA * _load_row_tiled(h1p_ref, TS) + y, g2_ref[0], b2_ref[0])


def _final_call(slot_prev, h1t_prev, y_prev, g2, b2):
    t = h1t_prev.shape[0] // ROW_TILE
    nt = t // TS
    last = DEPTH - 1
    return pl.pallas_call(
        _final_kernel,
        grid=(nt,),
        in_specs=[
            pl.BlockSpec((1, 1, TS), lambda i: (i, 0, 0), memory_space=pltpu.SMEM),
            pl.BlockSpec((1, 1, TS), lambda i: (jnp.minimum(i + 1, nt - 1), 0, 0),
                         memory_space=pltpu.SMEM),
            pl.BlockSpec((TS * ROW_TILE, LANES), lambda i: (i, 0)),
            pl.BlockSpec(memory_space=pl.ANY),
            pl.BlockSpec((1, 1, D_MODEL), lambda i: (last, 0, 0)),
            pl.BlockSpec((1, 1, D_MODEL), lambda i: (last, 0, 0)),
        ],
        out_specs=pl.BlockSpec((TS, D_MODEL), lambda i: (i, 0)),
        out_shape=jax.ShapeDtypeStruct((t, D_MODEL), F32),
        scratch_shapes=[pltpu.VMEM((2, TS * ROW_TILE, LANES), F32), pltpu.SemaphoreType.DMA((2,))],
        compiler_params=pltpu.CompilerParams(
            dimension_semantics=("arbitrary",), vmem_limit_bytes=VMEM_LIMIT),
        name="final_combine_ln",
    )(slot_prev, slot_prev, h1t_prev, y_prev, g2, b2)


def kernel(x, ln_in_g, ln_in_b, w_in, pool_w, pool_scale, conv_w, conv_b, w_out, ln1_g, ln1_b,
           w_router, router_bias, exp_w1, exp_w3, exp_w2, ln2_g, ln2_b):
    bsz, seq, d = x.shape
    t = bsz * seq
    nt = t // TS
    n_rows = t + N_CLASSES * BM
    nb = n_rows // BM
    assert nb <= LANES

    per_layer_row = lambda v: v.reshape(DEPTH, 1, -1)
    tri = (lax.broadcasted_iota(I32, (TS, TS), 0)
           < lax.broadcasted_iota(I32, (TS, TS), 1)).astype(BF16)
    wr_t = w_router.T
    wr_pad = jnp.pad(w_router, ((0, 0), (0, LANES - N_EXPERTS)))
    rbias = router_bias.reshape(N_EXPERTS, 1).astype(F32)
    pool_scale3, conv_b3 = per_layer_row(pool_scale), per_layer_row(conv_b)
    ln1_g3, ln1_b3 = per_layer_row(ln1_g), per_layer_row(ln1_b)
    ln2_g3, ln2_b3 = per_layer_row(ln2_g), per_layer_row(ln2_b)

    prev = (x.reshape(t, d), ln_in_g.reshape(1, -1), ln_in_b.reshape(1, -1))
    xbuf = None
    for l in range(DEPTH):
        h1t, route, cnt = _layer_call(l, bsz, seq, prev, w_in, pool_w, pool_scale3, conv_w, conv_b3,
                                          w_out, ln1_g3, ln1_b3, wr_t, rbias, tri)
        slots, table = _dispatch_plan(cnt, route)
        slot3 = slots.reshape(nt, 1, TS)
        blk_ea, blk_eb, blk_valid = table[TABLE_EA, :nb], table[TABLE_EB, :nb], table[TABLE_VALID, :nb]
        nused = table[TABLE_NUSED, :1]
        if xbuf is None:
            xbuf = _first_scatter_rows(table[TABLE_TAIL_START, :N_CLASSES],
                                       table[TABLE_TAIL_LEN, :N_CLASSES], nused, slot3, h1t, n_rows)
        else:
            xbuf = _scatter_rows(slot3, h1t, xbuf)
        xbuf = _expert_ffn(l, blk_ea, blk_eb, blk_valid, nused, xbuf, exp_w1, exp_w3, exp_w2, wr_pad)
        prev = (slot3, h1t, xbuf, ln2_g3, ln2_b3)
    out = _final_call(*prev)
    return out.reshape(bsz, seq, d)
```

```python
import functools

import jax
import jax.numpy as jnp
from jax import lax
from jax.experimental import pallas as pl
from jax.experimental.pallas import tpu as pltpu

D_MODEL = 1024
DEPTH = 4
D_POOL = 512
POOL_WINDOWS = (2, 4, 8, 16)
POOL_GROUP = 128
D_CONV = 512
CONV_WIDTH = 3
D_PROJ = D_POOL + 3 * D_CONV
N_EXPERTS = 16
N_EXPERT_GROUPS = 4
EXPERTS_PER_GROUP = 4
PAIRS_PER_GROUP = 6
N_CLASSES = N_EXPERT_GROUPS * PAIRS_PER_GROUP
CLASS_ROWS = 32
D_EXPERT = 512
ALPHA = float((2 * DEPTH) ** 0.25)
LN_EPS = 1e-5

F32 = jnp.float32
BF16 = jnp.bfloat16
I32 = jnp.int32

LANES = 128
ROW_TILE = D_MODEL // LANES
TS = 512
SUBLANES = 8
V_HALO = 32
Z_HALO = 8
BM = 256
W_STAGE_ROWS = 256
DMA_QUEUES = 2
VMEM_LIMIT = 56 * 1024 * 1024


def _layer_norm(x, g, b):
    mu = jnp.mean(x, axis=-1, keepdims=True)
    xc = x - mu
    var = jnp.mean(xc * xc, axis=-1, keepdims=True)
    return xc * lax.rsqrt(var + LN_EPS) * g + b


def _dot(a, b):
    return jnp.dot(a, b, preferred_element_type=F32)


def _store_row_tiled(ref, value):
    n = value.shape[0]
    for j in range(ROW_TILE):
        ref[pl.ds(j, n, stride=ROW_TILE), :] = value[:, j * LANES:(j + 1) * LANES]


def _load_row_tiled(ref, n):
    return jnp.concatenate([ref[pl.ds(j, n, stride=ROW_TILE), :] for j in range(ROW_TILE)], axis=1)


def _row(ref, first_sublane):
    return ref.at[pl.ds(first_sublane, ROW_TILE)]


def _route(sel):
    rows = [sel[e:e + 1, :] for e in range(N_EXPERTS)]
    gscore = []
    for g in range(N_EXPERT_GROUPS):
        a, b, c, d = rows[4 * g:4 * g + 4]
        hi1, lo1 = jnp.maximum(a, b), jnp.minimum(a, b)
        hi2, lo2 = jnp.maximum(c, d), jnp.minimum(c, d)
        m1 = jnp.maximum(hi1, hi2)
        m2 = jnp.maximum(jnp.minimum(hi1, hi2), jnp.maximum(lo1, lo2))
        gscore.append(m1 + m2)
    best = gscore[0]
    gidx = jnp.zeros(best.shape, I32)
    for g in range(1, N_EXPERT_GROUPS):
        upd = gscore[g] > best
        best = jnp.where(upd, gscore[g], best)
        gidx = jnp.where(upd, g, gidx)
    neg = jnp.full(best.shape, -jnp.inf, F32)
    masked = [jnp.where(gidx == (e // EXPERTS_PER_GROUP), rows[e], neg) for e in range(N_EXPERTS)]

    def top1(vals):
        bv = neg
        bi = jnp.zeros(best.shape, I32)
        for e in range(N_EXPERTS):
            upd = vals[e] > bv
            bv = jnp.where(upd, vals[e], bv)
            bi = jnp.where(upd, e, bi)
        return bi

    i1 = top1(masked)
    i2 = top1([jnp.where(i1 == e, neg, masked[e]) for e in range(N_EXPERTS)])
    return gidx, i1, i2


def _pair_class(gidx, i1, i2):
    a = jnp.minimum(i1, i2) - gidx * EXPERTS_PER_GROUP
    b = jnp.maximum(i1, i2) - gidx * EXPERTS_PER_GROUP
    base = jnp.where(a == 0, 0, jnp.where(a == 1, 2, 3))
    return gidx * PAIRS_PER_GROUP + base + b - 1


def _prefetch_expert_rows(i, nt, scur_ref, snext_ref, y_hbm, ybuf, gsem, before_wait=None):
    slot = i % 2

    def gather(slot_ref, to_slot, t, t_sublane):
        src = _row(y_hbm, pl.multiple_of(slot_ref[0, 0, t], ROW_TILE))
        return pltpu.make_async_copy(src, _row(ybuf.at[to_slot], t_sublane), gsem.at[to_slot])

    @pl.when(i == 0)
    def _():
        def body(t, carry):
            gather(scur_ref, 0, t, pl.multiple_of(t * ROW_TILE, ROW_TILE)).start()
            return carry
        lax.fori_loop(0, TS, body, 0, unroll=8)

    @pl.when(i + 1 < nt)
    def _():
        for t in range(TS):
            gather(snext_ref, 1 - slot, t, t * ROW_TILE).start(priority=t % DMA_QUEUES)

    if before_wait is not None:
        before_wait()
    pltpu.make_async_copy(y_hbm.at[pl.ds(0, TS * ROW_TILE)], ybuf.at[slot], gsem.at[slot]).wait()
    return _load_row_tiled(ybuf.at[slot], TS)


def _layer_kernel(*refs, layer, tiles_per_seq):
    if layer == 0:
        x_ref, gin_ref, bin_ref = refs[:3]
        rest = refs[3:]
    else:
        scur_ref, snext_ref, h1p_ref, y_hbm, g2_ref, b2_ref = refs[:6]
        rest = refs[6:]
    (win_hbm, pw_ref, ps_ref, cw_ref, cb_ref, wout_hbm, g_ref, b_ref, wr_ref, rb_ref, tri_ref,
     h1t_ref, route_ref, cnt_ref,
     vbuf, pwork, zbuf, mixbuf, carry, win_ref, wout_ref, stage, wsem, ybuf, gsem) = rest

    i = pl.program_id(0)
    nt = pl.num_programs(0)

    def stage_weights():
        chunks = []
        for src, dst, width in ((win_hbm, win_ref, D_PROJ), (wout_hbm, wout_ref, D_MODEL)):
            for c in range(D_MODEL // W_STAGE_ROWS):
                chunks.append((src, dst, width, pl.ds(c * W_STAGE_ROWS, W_STAGE_ROWS)))

        def fetch(k):
            src, _, width, rows = chunks[k]
            return pltpu.make_async_copy(src.at[layer, rows], stage.at[k % 2, :, pl.ds(0, width)],
                                         wsem.at[k % 2])

        fetch(0).start()
        for k, (_, dst, width, rows) in enumerate(chunks):
            if k + 1 < len(chunks):
                fetch(k + 1).start()
            fetch(k).wait()
            dst[rows, :] = stage[k % 2, :, 0:width].astype(BF16)
        carry[...] = jnp.zeros(carry.shape, F32)

    first_step_setup = functools.partial(pl.when(i == 0), stage_weights)
    if layer == 0:
        first_step_setup()
        h = _layer_norm(x_ref[...], gin_ref[...], bin_ref[...])
    else:
        y = _prefetch_expert_rows(i, nt, scur_ref, snext_ref, y_hbm, ybuf, gsem,
                                  before_wait=first_step_setup)
        h = _layer_norm(ALPHA * _load_row_tiled(h1p_ref, TS) + y, g2_ref[0], b2_ref[0])

    si = i % tiles_per_seq

    @pl.when(si == 0)
    def _():
        vbuf[0:V_HALO, :] = jnp.zeros((V_HALO, D_POOL), F32)
        zbuf[0:Z_HALO, :] = jnp.zeros((Z_HALO, D_CONV), F32)

    hb = h.astype(BF16)
    vbuf[V_HALO:V_HALO + TS, :] = _dot(hb, win_ref[:, 0:D_POOL])
    gate_b = _dot(hb, win_ref[:, D_POOL:D_POOL + D_CONV])
    gate_c = _dot(hb, win_ref[:, D_POOL + D_CONV:D_POOL + 2 * D_CONV])
    u = _dot(hb, win_ref[:, D_POOL + 2 * D_CONV:D_PROJ])
    z = gate_c * u
    zbuf[Z_HALO:Z_HALO + TS, :] = z

    src = vbuf
    for level in range(1, len(POOL_WINDOWS)):
        dst = pwork.at[(level - 1) % 2]
        lo = SUBLANES * level
        shift = 2 ** (level - 1)
        n = TS + V_HALO - lo
        cols = pl.ds(level * POOL_GROUP, D_POOL - level * POOL_GROUP)
        dst[pl.ds(lo, n), cols] = src[pl.ds(lo, n), cols] + src[pl.ds(lo - shift, n), cols]
        src = dst
    tpos = si * TS + lax.broadcasted_iota(I32, (TS, 1), 0)
    for g, win in enumerate(POOL_WINDOWS):
        cols = pl.ds(g * POOL_GROUP, POOL_GROUP)
        cur = vbuf[pl.ds(V_HALO, TS), cols]
        half = vbuf if g == 0 else pwork.at[(g - 1) % 2]
        acc = half[pl.ds(V_HALO, TS), cols] + half[pl.ds(V_HALO - win // 2, TS), cols]
        denom = jnp.minimum(tpos + 1, win).astype(F32)
        pooled = acc / denom - cur
        mixed = _dot(pooled.astype(BF16), pw_ref[0, g].astype(BF16)) * ps_ref[0, :, cols]
        mixbuf[:, cols] = mixed.astype(BF16)

    yc = (cw_ref[0, 2:3, :] * z
          + cw_ref[0, 1:2, :] * zbuf[pl.ds(Z_HALO - 1, TS), :]
          + cw_ref[0, 0:1, :] * zbuf[pl.ds(Z_HALO - 2, TS), :])
    mixbuf[:, D_POOL:D_POOL + D_CONV] = (gate_b * (yc + cb_ref[0])).astype(BF16)

    vbuf[0:V_HALO, :] = vbuf[TS:TS + V_HALO, :]
    zbuf[0:Z_HALO, :] = zbuf[TS:TS + Z_HALO, :]

    mix = _dot(mixbuf[...], wout_ref[...])
    h1 = _layer_norm(ALPHA * h + mix, g_ref[0], b_ref[0])
    _store_row_tiled(h1t_ref, h1)

    logits = lax.dot_general(wr_ref[...].astype(BF16), h1.astype(BF16), (((1,), (1,)), ((), ())),
                             preferred_element_type=F32)
    mx = jnp.max(logits, axis=0, keepdims=True)
    ex = jnp.exp(logits - mx)
    probs = ex / jnp.sum(ex, axis=0, keepdims=True)
    gidx, i1, i2 = _route(probs + rb_ref[...])
    cls = _pair_class(gidx, i1, i2)

    ciota = lax.broadcasted_iota(I32, (CLASS_ROWS, TS), 0)
    onehot = jnp.where(ciota == cls, 1.0, 0.0).astype(F32)
    before = _dot(onehot.astype(BF16), tri_ref[...])
    total = jnp.sum(onehot, axis=1, keepdims=True)
    c = carry[:, 0:1]
    rank = jnp.sum(onehot * (c + before), axis=0, keepdims=True)
    route_ref[0:1, :] = cls
    route_ref[1:2, :] = rank.astype(I32)
    newc = jnp.broadcast_to(c + total, carry.shape)
    carry[...] = newc
    cnt_ref[...] = newc


def _layer_call(layer, bsz, seq, prev, w_in, pool_w, pool_scale, conv_w, conv_b, w_out, ln_g, ln_b,
                wr_t, rbias, tri):
    t = bsz * seq
    nt = t // TS
    const2 = lambda i: (0, 0)
    lsel3 = lambda i: (layer, 0, 0)
    lsel4 = lambda i: (layer, 0, 0, 0)
    tile = lambda i: (i, 0)
    lane_tile = lambda i: (0, i)
    if layer == 0:
        x, gin, bin_ = prev
        head_args = (x, gin, bin_)
        head_specs = [
            pl.BlockSpec((TS, D_MODEL), tile),
            pl.BlockSpec((1, D_MODEL), const2),
            pl.BlockSpec((1, D_MODEL), const2),
        ]
    else:
        slot_prev, h1t_prev, y_prev, g2, b2 = prev
        head_args = (slot_prev, slot_prev, h1t_prev, y_prev, g2, b2)
        head_specs = [
            pl.BlockSpec((1, 1, TS), lambda i: (i, 0, 0), memory_space=pltpu.SMEM),
            pl.BlockSpec((1, 1, TS), lambda i: (jnp.minimum(i + 1, nt - 1), 0, 0),
                         memory_space=pltpu.SMEM),
            pl.BlockSpec((TS * ROW_TILE, LANES), tile),
            pl.BlockSpec(memory_space=pl.ANY),
            pl.BlockSpec((1, 1, D_MODEL), lambda i: (layer - 1, 0, 0)),
            pl.BlockSpec((1, 1, D_MODEL), lambda i: (layer - 1, 0, 0)),
        ]
    return pl.pallas_call(
        functools.partial(_layer_kernel, layer=layer, tiles_per_seq=seq // TS),
        grid=(nt,),
        in_specs=head_specs + [
            pl.BlockSpec(memory_space=pl.ANY),
            pl.BlockSpec((1, len(POOL_WINDOWS), POOL_GROUP, POOL_GROUP), lsel4),
            pl.BlockSpec((1, 1, D_POOL), lsel3),
            pl.BlockSpec((1, CONV_WIDTH, D_CONV), lsel3),
            pl.BlockSpec((1, 1, D_CONV), lsel3),
            pl.BlockSpec(memory_space=pl.ANY),
            pl.BlockSpec((1, 1, D_MODEL), lsel3),
            pl.BlockSpec((1, 1, D_MODEL), lsel3),
            pl.BlockSpec((N_EXPERTS, D_MODEL), const2),
            pl.BlockSpec((N_EXPERTS, 1), const2),
            pl.BlockSpec((TS, TS), const2),
        ],
        out_specs=[
            pl.BlockSpec((TS * ROW_TILE, LANES), tile),
            pl.BlockSpec((2, TS), lane_tile),
            pl.BlockSpec((CLASS_ROWS, LANES), const2),
        ],
        out_shape=[
            jax.ShapeDtypeStruct((t * ROW_TILE, LANES), F32),
            jax.ShapeDtypeStruct((2, t), I32),
            jax.ShapeDtypeStruct((CLASS_ROWS, LANES), F32),
        ],
        scratch_shapes=[
            pltpu.VMEM((TS + V_HALO, D_POOL), F32),
            pltpu.VMEM((2, TS + V_HALO, D_POOL), F32),
            pltpu.VMEM((TS + Z_HALO, D_CONV), F32),
            pltpu.VMEM((TS, D_MODEL), BF16),
            pltpu.VMEM((CLASS_ROWS, LANES), F32),
            pltpu.VMEM((D_MODEL, D_PROJ), BF16),
            pltpu.VMEM((D_MODEL, D_MODEL), BF16),
            pltpu.VMEM((2, W_STAGE_ROWS, D_PROJ), F32),
            pltpu.SemaphoreType.DMA((2,)),
            pltpu.VMEM((2, TS * ROW_TILE, LANES), F32),
            pltpu.SemaphoreType.DMA((2,)),
        ],
        compiler_params=pltpu.CompilerParams(
            dimension_semantics=("arbitrary",), vmem_limit_bytes=VMEM_LIMIT),
        name="layer_mix_route",
    )(*head_args, w_in, pool_w, pool_scale, conv_w, conv_b, w_out, ln_g, ln_b, wr_t, rbias, tri)


TABLE_EA, TABLE_EB, TABLE_VALID, TABLE_NUSED = 0, 1, 2, 3
TABLE_TAIL_START, TABLE_TAIL_LEN = 4, 5


def _dispatch_kernel(cnt_ref, route_ref, slot_ref, table_ref):
    log_bm = BM.bit_length() - 1
    starts, ends, fills = [], [], []
    end = jnp.zeros((1, LANES), I32)
    for k in range(N_CLASSES):
        count = cnt_ref[k:k + 1, :].astype(I32)
        starts.append(end)
        fills.append(end + count)
        end = end + lax.shift_left(lax.shift_right_logical(count + (BM - 1), log_bm), log_bm)
        ends.append(end)
    nused = lax.shift_right_logical(end, log_bm)

    cls = route_ref[0:1, :]
    start = jnp.zeros(cls.shape, I32)
    for k in range(N_CLASSES):
        start = jnp.where(cls == k, starts[k][:, 0:1], start)
    slot_ref[...] = (start + route_ref[1:2, :]) * ROW_TILE

    step = jnp.minimum(lax.broadcasted_iota(I32, (1, LANES), 1), nused - 1)
    first_row = step * BM
    c = jnp.zeros((1, LANES), I32)
    fill = fills[0]
    for k in range(N_CLASSES - 1):
        beyond = first_row >= ends[k]
        c = c + beyond.astype(I32)
        fill = jnp.where(beyond, fills[k + 1], fill)
    g = sum((c >= k * PAIRS_PER_GROUP).astype(I32) for k in range(1, N_EXPERT_GROUPS))
    p = c - g * PAIRS_PER_GROUP
    ge3 = (p >= 3).astype(I32)
    ge5 = (p >= 5).astype(I32)
    table_ref[TABLE_EA:TABLE_EA + 1, :] = g * EXPERTS_PER_GROUP + ge3 + ge5
    table_ref[TABLE_EB:TABLE_EB + 1, :] = g * EXPERTS_PER_GROUP + p + 1 - 2 * ge3 - ge5
    table_ref[TABLE_VALID:TABLE_VALID + 1, :] = jnp.clip(fill - first_row, 0, BM)
    table_ref[TABLE_NUSED:TABLE_NUSED + 1, :] = nused

    lane = lax.broadcasted_iota(I32, (1, LANES), 1)
    tail_start = jnp.zeros((1, LANES), I32)
    tail_len = jnp.zeros((1, LANES), I32)
    for k in range(N_CLASSES):
        tail_start = jnp.where(lane == k, fills[k], tail_start)
        tail_len = jnp.where(lane == k, ends[k] - fills[k], tail_len)
    table_ref[TABLE_TAIL_START:TABLE_TAIL_START + 1, :] = tail_start
    table_ref[TABLE_TAIL_LEN:TABLE_TAIL_LEN + 1, :] = tail_len
    table_ref[TABLE_TAIL_LEN + 1:, :] = jnp.zeros((SUBLANES - TABLE_TAIL_LEN - 1, LANES), I32)


def _dispatch_plan(cnt, route):
    t = route.shape[1]
    return pl.pallas_call(
        _dispatch_kernel,
        grid=(1,),
        in_specs=[pl.BlockSpec((CLASS_ROWS, LANES), lambda i: (0, 0)),
                  pl.BlockSpec((2, t), lambda i: (0, 0))],
        out_specs=[pl.BlockSpec((1, t), lambda i: (0, 0)),
                   pl.BlockSpec((SUBLANES, LANES), lambda i: (0, 0))],
        out_shape=[jax.ShapeDtypeStruct((1, t), I32), jax.ShapeDtypeStruct((SUBLANES, LANES), I32)],
        name="dispatch_plan",
    )(cnt, route)


def _scatter_kernel(slot_ref, h1t_ref, xin_ref, xout_ref, sem):
    del xin_ref
    for t in range(TS):
        dst = _row(xout_ref, pl.multiple_of(slot_ref[0, 0, t], ROW_TILE))
        pltpu.make_async_copy(_row(h1t_ref, t * ROW_TILE), dst, sem).start(priority=t % DMA_QUEUES)
    pltpu.make_async_copy(h1t_ref, xout_ref.at[pl.ds(0, TS * ROW_TILE)], sem).wait()


def _first_scatter_kernel(tail_start_ref, tail_len_ref, nu_ref, slot_ref, h1t_ref, xout_ref,
                          zeros, sem, zsem):
    i = pl.program_id(0)
    block = BM * ROW_TILE
    n_blocks = xout_ref.shape[0] // block

    def fill_rows(visit):
        for k in range(N_CLASSES):
            length = tail_len_ref[k]
            row = tail_start_ref[k]
            piece = BM // 2
            while piece >= 1:
                take = (length & piece) != 0
                dst = xout_ref.at[pl.ds(pl.multiple_of(row * ROW_TILE, ROW_TILE), piece * ROW_TILE)]
                cp = pltpu.make_async_copy(zeros.at[pl.ds(0, piece * ROW_TILE)], dst, zsem)
                pl.when(take)(functools.partial(visit, cp))
                row = row + jnp.where(take, piece, 0)
                piece //= 2

        def idle_block(j, carry):
            dst = xout_ref.at[pl.ds(pl.multiple_of(j * block, block), block)]
            visit(pltpu.make_async_copy(zeros, dst, zsem))
            return carry
        lax.fori_loop(nu_ref[0], n_blocks, idle_block, 0)

    @pl.when(i == 0)
    def _():
        zeros[...] = jnp.zeros(zeros.shape, F32)
        fill_rows(lambda cp: cp.start())

    for t in range(TS):
        dst = _row(xout_ref, pl.multiple_of(slot_ref[0, 0, t], ROW_TILE))
        pltpu.make_async_copy(_row(h1t_ref, t * ROW_TILE), dst, sem).start(priority=t % DMA_QUEUES)
    pltpu.make_async_copy(h1t_ref, xout_ref.at[pl.ds(0, TS * ROW_TILE)], sem).wait()

    @pl.when(i == 0)
    def _():
        fill_rows(lambda cp: cp.wait())


def _first_scatter_rows(tail_start, tail_len, nused, slot3, h1t, n_rows):
    nt = slot3.shape[0]
    grid_spec = pltpu.PrefetchScalarGridSpec(
        num_scalar_prefetch=3,
        grid=(nt,),
        in_specs=[
            pl.BlockSpec((1, 1, TS), lambda i, *_: (i, 0, 0), memory_space=pltpu.SMEM),
            pl.BlockSpec((TS * ROW_TILE, LANES), lambda i, *_: (i, 0)),
        ],
        out_specs=pl.BlockSpec(memory_space=pl.ANY),
        scratch_shapes=[pltpu.VMEM((BM * ROW_TILE, LANES), F32), pltpu.SemaphoreType.DMA(()),
                        pltpu.SemaphoreType.DMA(())],
    )
    return pl.pallas_call(
        _first_scatter_kernel,
        grid_spec=grid_spec,
        out_shape=jax.ShapeDtypeStruct((n_rows * ROW_TILE, LANES), F32),
        compiler_params=pltpu.CompilerParams(dimension_semantics=("arbitrary",)),
        name="first_scatter_rows",
    )(tail_start, tail_len, nused, slot3, h1t)


def _scatter_rows(slot3, h1t, xbuf):
    nt = slot3.shape[0]
    return pl.pallas_call(
        _scatter_kernel,
        grid=(nt,),
        in_specs=[
            pl.BlockSpec((1, 1, TS), lambda i: (i, 0, 0), memory_space=pltpu.SMEM),
            pl.BlockSpec((TS * ROW_TILE, LANES), lambda i: (i, 0)),
            pl.BlockSpec(memory_space=pl.ANY),
        ],
        out_specs=pl.BlockSpec(memory_space=pl.ANY),
        out_shape=jax.ShapeDtypeStruct(xbuf.shape, xbuf.dtype),
        scratch_shapes=[pltpu.SemaphoreType.DMA(())],
        input_output_aliases={2: 0},
        compiler_params=pltpu.CompilerParams(dimension_semantics=("arbitrary",)),
        name="scatter_rows",
    )(slot3, h1t, xbuf)


def _expert_kernel(ea_ref, eb_ref, valid_ref, nu_ref, x_ref, w1_hbm, w3_hbm, w2_hbm, wr_ref, y_ref,
                   wup, w2a, w2b, s1a, s3a, s2a, s1b, s3b, s2b, wsem, *, layer):
    i = pl.program_id(0)
    nb = pl.num_programs(0)
    nu = nu_ref[0]
    used = i < nu
    ea = ea_ref[i]
    eb = eb_ref[i]

    def swap_weights(e_ref, slot, live, staged):
        e = e_ref[i]

        def fetch(expert):
            return [pltpu.make_async_copy(src.at[layer, expert], dst, wsem.at[slot])
                    for src, dst in zip((w1_hbm, w3_hbm, w2_hbm), staged)]

        @pl.when(used & (i == 0))
        def _():
            for cp in fetch(e):
                cp.start()

        @pl.when(used & ((i == 0) | (e != e_ref[jnp.maximum(i - 1, 0)])))
        def _():
            for cp in fetch(e):
                cp.wait()
            for dst, src in zip(live, staged):
                dst[...] = src[...].astype(BF16)
            nxt = lax.while_loop(lambda j: (j < nu) & (e_ref[jnp.minimum(j, nb - 1)] == e),
                                 lambda j: j + 1, i + 1)

            @pl.when(nxt < nu)
            def _():
                for cp in fetch(e_ref[jnp.minimum(nxt, nb - 1)]):
                    cp.start()

    def up_cols(k):
        return wup.at[:, pl.ds(k * D_EXPERT, D_EXPERT)]

    swap_weights(ea_ref, 0, (up_cols(0), up_cols(2), w2a), (s1a, s3a, s2a))
    swap_weights(eb_ref, 1, (up_cols(1), up_cols(3), w2b), (s1b, s3b, s2b))

    def ffn_rows(rows):
        sublanes = pl.ds(0, rows * ROW_TILE)
        xb = _load_row_tiled(x_ref.at[sublanes], rows).astype(BF16)

        logits = _dot(xb, wr_ref[...].astype(BF16))
        lane = lax.broadcasted_iota(I32, logits.shape, 1)
        logits = jnp.where(lane < N_EXPERTS, logits, -jnp.inf)
        ex = jnp.exp(logits - jnp.max(logits, axis=1, keepdims=True))
        probs = ex / jnp.sum(ex, axis=1, keepdims=True)
        pa = jnp.sum(jnp.where(lane == ea, probs, 0.0), axis=1, keepdims=True)
        pb = jnp.sum(jnp.where(lane == eb, probs, 0.0), axis=1, keepdims=True)
        den = pa + pb

        up = _dot(xb, wup[...])
        a = up[:, 0:2 * D_EXPERT]
        hid = (a * jax.nn.sigmoid(a) * up[:, 2 * D_EXPERT:4 * D_EXPERT]).astype(BF16)
        y = ((pa / den) * _dot(hid[:, 0:D_EXPERT], w2a[...])
             + (pb / den) * _dot(hid[:, D_EXPERT:2 * D_EXPERT], w2b[...]))
        _store_row_tiled(y_ref.at[sublanes], y)

    few = valid_ref[i] <= BM // 2

    @pl.when(used & jnp.logical_not(few))
    def _():
        ffn_rows(BM)

    @pl.when(used & few)
    def _():
        ffn_rows(BM // 2)
        rest = pl.ds(BM // 2 * ROW_TILE, BM // 2 * ROW_TILE)
        y_ref[rest, :] = jnp.zeros((BM // 2 * ROW_TILE, LANES), F32)


def _expert_ffn(layer, blk_ea, blk_eb, blk_valid, nused, xbuf, w1, w3, w2, wr_pad):
    n_blocks = xbuf.shape[0] // (BM * ROW_TILE)
    used_blk = lambda i, ea, eb, va, nu: (jnp.minimum(i, nu[0] - 1), 0)
    up = (D_MODEL, D_EXPERT)
    down = (D_EXPERT, D_MODEL)
    expert_slot = lambda dtype: [pltpu.VMEM(up, dtype), pltpu.VMEM(up, dtype), pltpu.VMEM(down, dtype)]
    grid_spec = pltpu.PrefetchScalarGridSpec(
        num_scalar_prefetch=4,
        grid=(n_blocks,),
        in_specs=[
            pl.BlockSpec((BM * ROW_TILE, LANES), used_blk),
            pl.BlockSpec(memory_space=pl.ANY),
            pl.BlockSpec(memory_space=pl.ANY),
            pl.BlockSpec(memory_space=pl.ANY),
            pl.BlockSpec((D_MODEL, LANES), lambda i, ea, eb, va, nu: (0, 0)),
        ],
        out_specs=pl.BlockSpec((BM * ROW_TILE, LANES), used_blk),
        scratch_shapes=([pltpu.VMEM((D_MODEL, 4 * D_EXPERT), BF16), pltpu.VMEM(down, BF16),
                         pltpu.VMEM(down, BF16)] + expert_slot(F32) + expert_slot(F32)
                        + [pltpu.SemaphoreType.DMA((2,))]),
    )
    return pl.pallas_call(
        functools.partial(_expert_kernel, layer=layer),
        grid_spec=grid_spec,
        out_shape=jax.ShapeDtypeStruct(xbuf.shape, F32),
        input_output_aliases={4: 0},
        compiler_params=pltpu.CompilerParams(
            dimension_semantics=("arbitrary",), vmem_limit_bytes=VMEM_LIMIT),
        name="expert_ffn",
    )(blk_ea, blk_eb, blk_valid, nused, xbuf, w1, w3, w2, wr_pad)


def _final_kernel(scur_ref, snext_ref, h1p_ref, y_hbm, g2_ref, b2_ref, o_ref, ybuf, gsem):
    i = pl.program_id(0)
    y = _prefetch_expert_rows(i, pl.num_programs(0), scur_ref, snext_ref, y_hbm, ybuf, gsem)
    o_ref[...] = _layer_norm(ALPHA * _load_row_tiled(h1p_ref, TS) + y, g2_ref[0], b2_ref[0])


def _final_call(slot_prev, h1t_prev, y_prev, g2, b2):
    t = h1t_prev.shape[0] // ROW_TILE
    nt = t // TS
    last = DEPTH - 1
    return pl.pallas_call(
        _final_kernel,
        grid=(nt,),
        in_specs=[
            pl.BlockSpec((1, 1, TS), lambda i: (i, 0, 0), memory_space=pltpu.SMEM),
            pl.BlockSpec((1, 1, TS), lambda i: (jnp.minimum(i + 1, nt - 1), 0, 0),
                         memory_space=pltpu.SMEM),
            pl.BlockSpec((TS * ROW_TILE, LANES), lambda i: (i, 0)),
            pl.BlockSpec(memory_space=pl.ANY),
            pl.BlockSpec((1, 1, D_MODEL), lambda i: (last, 0, 0)),
            pl.BlockSpec((1, 1, D_MODEL), lambda i: (last, 0, 0)),
        ],
        out_specs=pl.BlockSpec((TS, D_MODEL), lambda i: (i, 0)),
        out_shape=jax.ShapeDtypeStruct((t, D_MODEL), F32),
        scratch_shapes=[pltpu.VMEM((2, TS * ROW_TILE, LANES), F32), pltpu.SemaphoreType.DMA((2,))],
        compiler_params=pltpu.CompilerParams(
            dimension_semantics=("arbitrary",), vmem_limit_bytes=VMEM_LIMIT),
        name="final_combine_ln",
    )(slot_prev, slot_prev, h1t_prev, y_prev, g2, b2)


def kernel(x, ln_in_g, ln_in_b, w_in, pool_w, pool_scale, conv_w, conv_b, w_out, ln1_g, ln1_b,
           w_router, router_bias, exp_w1, exp_w3, exp_w2, ln2_g, ln2_b):
    bsz, seq, d = x.shape
    t = bsz * seq
    nt = t // TS
    n_rows = t + N_CLASSES * BM
    nb = n_rows // BM
    assert nb <= LANES

    per_layer_row = lambda v: v.reshape(DEPTH, 1, -1)
    tri = (lax.broadcasted_iota(I32, (TS, TS), 0)
           < lax.broadcasted_iota(I32, (TS, TS), 1)).astype(BF16)
    wr_t = w_router.T
    wr_pad = jnp.pad(w_router, ((0, 0), (0, LANES - N_EXPERTS)))
    rbias = router_bias.reshape(N_EXPERTS, 1).astype(F32)
    pool_scale3, conv_b3 = per_layer_row(pool_scale), per_layer_row(conv_b)
    ln1_g3, ln1_b3 = per_layer_row(ln1_g), per_layer_row(ln1_b)
    ln2_g3, ln2_b3 = per_layer_row(ln2_g), per_layer_row(ln2_b)

    prev = (x.reshape(t, d), ln_in_g.reshape(1, -1), ln_in_b.reshape(1, -1))
    xbuf = None
    for l in range(DEPTH):
        h1t, route, cnt = _layer_call(l, bsz, seq, prev, w_in, pool_w, pool_scale3, conv_w, conv_b3,
                                          w_out, ln1_g3, ln1_b3, wr_t, rbias, tri)
        slots, table = _dispatch_plan(cnt, route)
        slot3 = slots.reshape(nt, 1, TS)
        blk_ea, blk_eb, blk_valid = table[TABLE_EA, :nb], table[TABLE_EB, :nb], table[TABLE_VALID, :nb]
        nused = table[TABLE_NUSED, :1]
        if xbuf is None:
            xbuf = _first_scatter_rows(table[TABLE_TAIL_START, :N_CLASSES],
                                       table[TABLE_TAIL_LEN, :N_CLASSES], nused, slot3, h1t, n_rows)
        else:
            xbuf = _scatter_rows(slot3, h1t, xbuf)
        xbuf = _expert_ffn(l, blk_ea, blk_eb, blk_valid, nused, xbuf, exp_w1, exp_w3, exp_w2, wr_pad)
        prev = (slot3, h1t, xbuf, ln2_g3, ln2_b3)
    out = _final_call(*prev)
    return out.reshape(bsz, seq, d)
```

```python
import functools

import jax
import jax.numpy as jnp
from jax import lax
from jax.experimental import pallas as pl
from jax.experimental.pallas import tpu as pltpu

D_MODEL = 1024
DEPTH = 4
D_POOL = 512
POOL_WINDOWS = (2, 4, 8, 16)
POOL_GROUP = 128
D_CONV = 512
CONV_WIDTH = 3
D_PROJ = D_POOL + 3 * D_CONV
N_EXPERTS = 16
N_EXPERT_GROUPS = 4
EXPERTS_PER_GROUP = 4
PAIRS_PER_GROUP = 6
N_CLASSES = N_EXPERT_GROUPS * PAIRS_PER_GROUP
CLASS_ROWS = 32
D_EXPERT = 512
ALPHA = float((2 * DEPTH) ** 0.25)
LN_EPS = 1e-5

F32 = jnp.float32
BF16 = jnp.bfloat16
I32 = jnp.int32

LANES = 128
ROW_TILE = D_MODEL // LANES
TS = 512
SUBLANES = 8
V_HALO = 32
Z_HALO = 8
BM = 256
W_STAGE_ROWS = 256
DMA_QUEUES = 2
WEIGHT_DMA_QUEUE = 1
VMEM_LIMIT = 56 * 1024 * 1024


def _layer_norm(x, g, b):
    mu = jnp.mean(x, axis=-1, keepdims=True)
    xc = x - mu
    var = jnp.mean(xc * xc, axis=-1, keepdims=True)
    return xc * lax.rsqrt(var + LN_EPS) * g + b


def _dot(a, b):
    return jnp.dot(a, b, preferred_element_type=F32)


def _store_row_tiled(ref, value):
    n = value.shape[0]
    for j in range(ROW_TILE):
        ref[pl.ds(j, n, stride=ROW_TILE), :] = value[:, j * LANES:(j + 1) * LANES]


def _load_row_tiled(ref, n):
    return jnp.concatenate([ref[pl.ds(j, n, stride=ROW_TILE), :] for j in range(ROW_TILE)], axis=1)


def _row(ref, first_sublane):
    return ref.at[pl.ds(first_sublane, ROW_TILE)]


def _route(sel):
    rows = [sel[e:e + 1, :] for e in range(N_EXPERTS)]
    gscore = []
    for g in range(N_EXPERT_GROUPS):
        a, b, c, d = rows[4 * g:4 * g + 4]
        hi1, lo1 = jnp.maximum(a, b), jnp.minimum(a, b)
        hi2, lo2 = jnp.maximum(c, d), jnp.minimum(c, d)
        m1 = jnp.maximum(hi1, hi2)
        m2 = jnp.maximum(jnp.minimum(hi1, hi2), jnp.maximum(lo1, lo2))
        gscore.append(m1 + m2)
    best = gscore[0]
    gidx = jnp.zeros(best.shape, I32)
    for g in range(1, N_EXPERT_GROUPS):
        upd = gscore[g] > best
        best = jnp.where(upd, gscore[g], best)
        gidx = jnp.where(upd, g, gidx)
    neg = jnp.full(best.shape, -jnp.inf, F32)
    masked = [jnp.where(gidx == (e // EXPERTS_PER_GROUP), rows[e], neg) for e in range(N_EXPERTS)]

    def top1(vals):
        bv = neg
        bi = jnp.zeros(best.shape, I32)
        for e in range(N_EXPERTS):
            upd = vals[e] > bv
            bv = jnp.where(upd, vals[e], bv)
            bi = jnp.where(upd, e, bi)
        return bi

    i1 = top1(masked)
    i2 = top1([jnp.where(i1 == e, neg, masked[e]) for e in range(N_EXPERTS)])
    return gidx, i1, i2


def _pair_class(gidx, i1, i2):
    a = jnp.minimum(i1, i2) - gidx * EXPERTS_PER_GROUP
    b = jnp.maximum(i1, i2) - gidx * EXPERTS_PER_GROUP
    base = jnp.where(a == 0, 0, jnp.where(a == 1, 2, 3))
    return gidx * PAIRS_PER_GROUP + base + b - 1


def _prefetch_expert_rows(i, nt, scur_ref, snext_ref, y_hbm, ybuf, gsem, before_wait=None):
    slot = i % 2

    def gather(slot_ref, to_slot, t, t_sublane):
        src = _row(y_hbm, pl.multiple_of(slot_ref[0, 0, t], ROW_TILE))
        return pltpu.make_async_copy(src, _row(ybuf.at[to_slot], t_sublane), gsem.at[to_slot])

    @pl.when(i == 0)
    def _():
        for t in range(TS):
            gather(scur_ref, 0, t, t * ROW_TILE).start(priority=t % DMA_QUEUES)

    @pl.when(i + 1 < nt)
    def _():
        for t in range(TS):
            gather(snext_ref, 1 - slot, t, t * ROW_TILE).start(priority=t % DMA_QUEUES)

    if before_wait is not None:
        before_wait()
    pltpu.make_async_copy(y_hbm.at[pl.ds(0, TS * ROW_TILE)], ybuf.at[slot], gsem.at[slot]).wait()
    return _load_row_tiled(ybuf.at[slot], TS)


def _layer_kernel(*refs, layer, tiles_per_seq):
    if layer == 0:
        x_ref, gin_ref, bin_ref = refs[:3]
        rest = refs[3:]
    else:
        scur_ref, snext_ref, h1p_ref, y_hbm, g2_ref, b2_ref = refs[:6]
        rest = refs[6:]
    (win_hbm, pw_ref, ps_ref, cw_ref, cb_ref, wout_hbm, g_ref, b_ref, wr_ref, rb_ref, tri_ref,
     h1t_ref, route_ref, cnt_ref,
     vbuf, pwork, zbuf, mixbuf, carry, win_ref, wout_ref, stage, wsem, ybuf, gsem) = rest

    i = pl.program_id(0)
    nt = pl.num_programs(0)

    def stage_weights():
        chunks = []
        for src, dst, width in ((win_hbm, win_ref, D_PROJ), (wout_hbm, wout_ref, D_MODEL)):
            for c in range(D_MODEL // W_STAGE_ROWS):
                chunks.append((src, dst, width, pl.ds(c * W_STAGE_ROWS, W_STAGE_ROWS)))

        def fetch(k):
            src, _, width, rows = chunks[k]
            return pltpu.make_async_copy(src.at[layer, rows], stage.at[k % 2, :, pl.ds(0, width)],
                                         wsem.at[k % 2])

        fetch(0).start()
        for k, (_, dst, width, rows) in enumerate(chunks):
            if k + 1 < len(chunks):
                fetch(k + 1).start()
            fetch(k).wait()
            dst[rows, :] = stage[k % 2, :, 0:width].astype(BF16)
        carry[...] = jnp.zeros(carry.shape, F32)

    first_step_setup = functools.partial(pl.when(i == 0), stage_weights)
    if layer == 0:
        first_step_setup()
        h = _layer_norm(x_ref[...], gin_ref[...], bin_ref[...])
    else:
        y = _prefetch_expert_rows(i, nt, scur_ref, snext_ref, y_hbm, ybuf, gsem,
                                  before_wait=first_step_setup)
        h = _layer_norm(ALPHA * _load_row_tiled(h1p_ref, TS) + y, g2_ref[0], b2_ref[0])

    si = i % tiles_per_seq

    @pl.when(si == 0)
    def _():
        vbuf[0:V_HALO, :] = jnp.zeros((V_HALO, D_POOL), F32)
        zbuf[0:Z_HALO, :] = jnp.zeros((Z_HALO, D_CONV), F32)

    hb = h.astype(BF16)
    vbuf[V_HALO:V_HALO + TS, :] = _dot(hb, win_ref[:, 0:D_POOL])
    gate_b = _dot(hb, win_ref[:, D_POOL:D_POOL + D_CONV])
    gate_c = _dot(hb, win_ref[:, D_POOL + D_CONV:D_POOL + 2 * D_CONV])
    u = _dot(hb, win_ref[:, D_POOL + 2 * D_CONV:D_PROJ])
    z = gate_c * u
    zbuf[Z_HALO:Z_HALO + TS, :] = z

    src = vbuf
    for level in range(1, len(POOL_WINDOWS)):
        dst = pwork.at[(level - 1) % 2]
        lo = SUBLANES * level
        shift = 2 ** (level - 1)
        n = TS + V_HALO - lo
        cols = pl.ds(level * POOL_GROUP, D_POOL - level * POOL_GROUP)
        dst[pl.ds(lo, n), cols] = src[pl.ds(lo, n), cols] + src[pl.ds(lo - shift, n), cols]
        src = dst
    tpos = si * TS + lax.broadcasted_iota(I32, (TS, 1), 0)
    for g, win in enumerate(POOL_WINDOWS):
        cols = pl.ds(g * POOL_GROUP, POOL_GROUP)
        cur = vbuf[pl.ds(V_HALO, TS), cols]
        half = vbuf if g == 0 else pwork.at[(g - 1) % 2]
        acc = half[pl.ds(V_HALO, TS), cols] + half[pl.ds(V_HALO - win // 2, TS), cols]
        denom = jnp.minimum(tpos + 1, win).astype(F32)
        pooled = acc / denom - cur
        mixed = _dot(pooled.astype(BF16), pw_ref[0, g].astype(BF16)) * ps_ref[0, :, cols]
        mixbuf[:, cols] = mixed.astype(BF16)

    yc = (cw_ref[0, 2:3, :] * z
          + cw_ref[0, 1:2, :] * zbuf[pl.ds(Z_HALO - 1, TS), :]
          + cw_ref[0, 0:1, :] * zbuf[pl.ds(Z_HALO - 2, TS), :])
    mixbuf[:, D_POOL:D_POOL + D_CONV] = (gate_b * (yc + cb_ref[0])).astype(BF16)

    vbuf[0:V_HALO, :] = vbuf[TS:TS + V_HALO, :]
    zbuf[0:Z_HALO, :] = zbuf[TS:TS + Z_HALO, :]

    mix = _dot(mixbuf[...], wout_ref[...])
    h1 = _layer_norm(ALPHA * h + mix, g_ref[0], b_ref[0])
    _store_row_tiled(h1t_ref, h1)

    logits = lax.dot_general(wr_ref[...].astype(BF16), h1.astype(BF16), (((1,), (1,)), ((), ())),
                             preferred_element_type=F32)
    mx = jnp.max(logits, axis=0, keepdims=True)
    ex = jnp.exp(logits - mx)
    probs = ex / jnp.sum(ex, axis=0, keepdims=True)
    gidx, i1, i2 = _route(probs + rb_ref[...])
    cls = _pair_class(gidx, i1, i2)

    ciota = lax.broadcasted_iota(I32, (CLASS_ROWS, TS), 0)
    onehot = jnp.where(ciota == cls, 1.0, 0.0).astype(F32)
    before = _dot(onehot.astype(BF16), tri_ref[...])
    total = jnp.sum(onehot, axis=1, keepdims=True)
    c = carry[:, 0:1]
    rank = jnp.sum(onehot * (c + before), axis=0, keepdims=True)
    route_ref[0:1, :] = cls
    route_ref[1:2, :] = rank.astype(I32)
    newc = jnp.broadcast_to(c + total, carry.shape)
    carry[...] = newc
    cnt_ref[...] = newc


def _layer_call(layer, bsz, seq, prev, w_in, pool_w, pool_scale, conv_w, conv_b, w_out, ln_g, ln_b,
                wr_t, rbias, tri):
    t = bsz * seq
    nt = t // TS
    const2 = lambda i: (0, 0)
    lsel3 = lambda i: (layer, 0, 0)
    lsel4 = lambda i: (layer, 0, 0, 0)
    tile = lambda i: (i, 0)
    lane_tile = lambda i: (0, i)
    if layer == 0:
        x, gin, bin_ = prev
        head_args = (x, gin, bin_)
        head_specs = [
            pl.BlockSpec((TS, D_MODEL), tile),
            pl.BlockSpec((1, D_MODEL), const2),
            pl.BlockSpec((1, D_MODEL), const2),
        ]
    else:
        slot_prev, h1t_prev, y_prev, g2, b2 = prev
        head_args = (slot_prev, slot_prev, h1t_prev, y_prev, g2, b2)
        head_specs = [
            pl.BlockSpec((1, 1, TS), lambda i: (i, 0, 0), memory_space=pltpu.SMEM),
            pl.BlockSpec((1, 1, TS), lambda i: (jnp.minimum(i + 1, nt - 1), 0, 0),
                         memory_space=pltpu.SMEM),
            pl.BlockSpec((TS * ROW_TILE, LANES), tile),
            pl.BlockSpec(memory_space=pl.ANY),
            pl.BlockSpec((1, 1, D_MODEL), lambda i: (layer - 1, 0, 0)),
            pl.BlockSpec((1, 1, D_MODEL), lambda i: (layer - 1, 0, 0)),
        ]
    return pl.pallas_call(
        functools.partial(_layer_kernel, layer=layer, tiles_per_seq=seq // TS),
        grid=(nt,),
        in_specs=head_specs + [
            pl.BlockSpec(memory_space=pl.ANY),
            pl.BlockSpec((1, len(POOL_WINDOWS), POOL_GROUP, POOL_GROUP), lsel4),
            pl.BlockSpec((1, 1, D_POOL), lsel3),
            pl.BlockSpec((1, CONV_WIDTH, D_CONV), lsel3),
            pl.BlockSpec((1, 1, D_CONV), lsel3),
            pl.BlockSpec(memory_space=pl.ANY),
            pl.BlockSpec((1, 1, D_MODEL), lsel3),
            pl.BlockSpec((1, 1, D_MODEL), lsel3),
            pl.BlockSpec((N_EXPERTS, D_MODEL), const2),
            pl.BlockSpec((N_EXPERTS, 1), const2),
            pl.BlockSpec((TS, TS), const2),
        ],
        out_specs=[
            pl.BlockSpec((TS * ROW_TILE, LANES), tile),
            pl.BlockSpec((2, TS), lane_tile),
            pl.BlockSpec((CLASS_ROWS, LANES), const2),
        ],
        out_shape=[
            jax.ShapeDtypeStruct((t * ROW_TILE, LANES), F32),
            jax.ShapeDtypeStruct((2, t), I32),
            jax.ShapeDtypeStruct((CLASS_ROWS, LANES), F32),
        ],
        scratch_shapes=[
            pltpu.VMEM((TS + V_HALO, D_POOL), F32),
            pltpu.VMEM((2, TS + V_HALO, D_POOL), F32),
            pltpu.VMEM((TS + Z_HALO, D_CONV), F32),
            pltpu.VMEM((TS, D_MODEL), BF16),
            pltpu.VMEM((CLASS_ROWS, LANES), F32),
            pltpu.VMEM((D_MODEL, D_PROJ), BF16),
            pltpu.VMEM((D_MODEL, D_MODEL), BF16),
            pltpu.VMEM((2, W_STAGE_ROWS, D_PROJ), F32),
            pltpu.SemaphoreType.DMA((2,)),
            pltpu.VMEM((2, TS * ROW_TILE, LANES), F32),
            pltpu.SemaphoreType.DMA((2,)),
        ],
        compiler_params=pltpu.CompilerParams(
            dimension_semantics=("arbitrary",), vmem_limit_bytes=VMEM_LIMIT),
        name="layer_mix_route",
    )(*head_args, w_in, pool_w, pool_scale, conv_w, conv_b, w_out, ln_g, ln_b, wr_t, rbias, tri)


TABLE_EA, TABLE_EB, TABLE_VALID, TABLE_NUSED = 0, 1, 2, 3
TABLE_TAIL_START, TABLE_TAIL_LEN = 4, 5


def _dispatch_kernel(cnt_ref, route_ref, slot_ref, table_ref):
    log_bm = BM.bit_length() - 1
    starts, ends, fills = [], [], []
    end = jnp.zeros((1, LANES), I32)
    for k in range(N_CLASSES):
        count = cnt_ref[k:k + 1, :].astype(I32)
        starts.append(end)
        fills.append(end + count)
        end = end + lax.shift_left(lax.shift_right_logical(count + (BM - 1), log_bm), log_bm)
        ends.append(end)
    nused = lax.shift_right_logical(end, log_bm)

    cls = route_ref[0:1, :]
    start = jnp.zeros(cls.shape, I32)
    for k in range(N_CLASSES):
        start = jnp.where(cls == k, starts[k][:, 0:1], start)
    slot_ref[...] = (start + route_ref[1:2, :]) * ROW_TILE

    step = jnp.minimum(lax.broadcasted_iota(I32, (1, LANES), 1), nused - 1)
    first_row = step * BM
    c = jnp.zeros((1, LANES), I32)
    fill = fills[0]
    for k in range(N_CLASSES - 1):
        beyond = first_row >= ends[k]
        c = c + beyond.astype(I32)
        fill = jnp.where(beyond, fills[k + 1], fill)
    g = sum((c >= k * PAIRS_PER_GROUP).astype(I32) for k in range(1, N_EXPERT_GROUPS))
    p = c - g * PAIRS_PER_GROUP
    ge3 = (p >= 3).astype(I32)
    ge5 = (p >= 5).astype(I32)
    table_ref[TABLE_EA:TABLE_EA + 1, :] = g * EXPERTS_PER_GROUP + ge3 + ge5
    table_ref[TABLE_EB:TABLE_EB + 1, :] = g * EXPERTS_PER_GROUP + p + 1 - 2 * ge3 - ge5
    table_ref[TABLE_VALID:TABLE_VALID + 1, :] = jnp.clip(fill - first_row, 0, BM)
    table_ref[TABLE_NUSED:TABLE_NUSED + 1, :] = nused

    lane = lax.broadcasted_iota(I32, (1, LANES), 1)
    tail_start = jnp.zeros((1, LANES), I32)
    tail_len = jnp.zeros((1, LANES), I32)
    for k in range(N_CLASSES):
        tail_start = jnp.where(lane == k, fills[k], tail_start)
        tail_len = jnp.where(lane == k, ends[k] - fills[k], tail_len)
    table_ref[TABLE_TAIL_START:TABLE_TAIL_START + 1, :] = tail_start
    table_ref[TABLE_TAIL_LEN:TABLE_TAIL_LEN + 1, :] = tail_len
    table_ref[TABLE_TAIL_LEN + 1:, :] = jnp.zeros((SUBLANES - TABLE_TAIL_LEN - 1, LANES), I32)


def _dispatch_plan(cnt, route):
    t = route.shape[1]
    return pl.pallas_call(
        _dispatch_kernel,
        grid=(1,),
        in_specs=[pl.BlockSpec((CLASS_ROWS, LANES), lambda i: (0, 0)),
                  pl.BlockSpec((2, t), lambda i: (0, 0))],
        out_specs=[pl.BlockSpec((1, t), lambda i: (0, 0)),
                   pl.BlockSpec((SUBLANES, LANES), lambda i: (0, 0))],
        out_shape=[jax.ShapeDtypeStruct((1, t), I32), jax.ShapeDtypeStruct((SUBLANES, LANES), I32)],
        name="dispatch_plan",
    )(cnt, route)


def _scatter_kernel(slot_ref, h1t_ref, xin_ref, xout_ref, sem):
    del xin_ref
    for t in range(TS):
        dst = _row(xout_ref, pl.multiple_of(slot_ref[0, 0, t], ROW_TILE))
        pltpu.make_async_copy(_row(h1t_ref, t * ROW_TILE), dst, sem).start(priority=t % DMA_QUEUES)
    pltpu.make_async_copy(h1t_ref, xout_ref.at[pl.ds(0, TS * ROW_TILE)], sem).wait()


def _first_scatter_kernel(tail_start_ref, tail_len_ref, nu_ref, slot_ref, h1t_ref, xout_ref,
                          zeros, sem, zsem):
    i = pl.program_id(0)
    block = BM * ROW_TILE
    n_blocks = xout_ref.shape[0] // block

    def fill_rows(visit):
        for k in range(N_CLASSES):
            length = tail_len_ref[k]
            row = tail_start_ref[k]
            piece = BM // 2
            while piece >= 1:
                take = (length & piece) != 0
                dst = xout_ref.at[pl.ds(pl.multiple_of(row * ROW_TILE, ROW_TILE), piece * ROW_TILE)]
                cp = pltpu.make_async_copy(zeros.at[pl.ds(0, piece * ROW_TILE)], dst, zsem)
                pl.when(take)(functools.partial(visit, cp))
                row = row + jnp.where(take, piece, 0)
                piece //= 2

        def idle_block(j, carry):
            dst = xout_ref.at[pl.ds(pl.multiple_of(j * block, block), block)]
            visit(pltpu.make_async_copy(zeros, dst, zsem))
            return carry
        lax.fori_loop(nu_ref[0], n_blocks, idle_block, 0)

    @pl.when(i == 0)
    def _():
        zeros[...] = jnp.zeros(zeros.shape, F32)
        fill_rows(lambda cp: cp.start())

    for t in range(TS):
        dst = _row(xout_ref, pl.multiple_of(slot_ref[0, 0, t], ROW_TILE))
        pltpu.make_async_copy(_row(h1t_ref, t * ROW_TILE), dst, sem).start(priority=t % DMA_QUEUES)
    pltpu.make_async_copy(h1t_ref, xout_ref.at[pl.ds(0, TS * ROW_TILE)], sem).wait()

    @pl.when(i == 0)
    def _():
        fill_rows(lambda cp: cp.wait())


def _first_scatter_rows(tail_start, tail_len, nused, slot3, h1t, n_rows):
    nt = slot3.shape[0]
    grid_spec = pltpu.PrefetchScalarGridSpec(
        num_scalar_prefetch=3,
        grid=(nt,),
        in_specs=[
            pl.BlockSpec((1, 1, TS), lambda i, *_: (i, 0, 0), memory_space=pltpu.SMEM),
            pl.BlockSpec((TS * ROW_TILE, LANES), lambda i, *_: (i, 0)),
        ],
        out_specs=pl.BlockSpec(memory_space=pl.ANY),
        scratch_shapes=[pltpu.VMEM((BM * ROW_TILE, LANES), F32), pltpu.SemaphoreType.DMA(()),
                        pltpu.SemaphoreType.DMA(())],
    )
    return pl.pallas_call(
        _first_scatter_kernel,
        grid_spec=grid_spec,
        out_shape=jax.ShapeDtypeStruct((n_rows * ROW_TILE, LANES), F32),
        compiler_params=pltpu.CompilerParams(dimension_semantics=("arbitrary",)),
        name="first_scatter_rows",
    )(tail_start, tail_len, nused, slot3, h1t)


def _scatter_rows(slot3, h1t, xbuf):
    nt = slot3.shape[0]
    return pl.pallas_call(
        _scatter_kernel,
        grid=(nt,),
        in_specs=[
            pl.BlockSpec((1, 1, TS), lambda i: (i, 0, 0), memory_space=pltpu.SMEM),
            pl.BlockSpec((TS * ROW_TILE, LANES), lambda i: (i, 0)),
            pl.BlockSpec(memory_space=pl.ANY),
        ],
        out_specs=pl.BlockSpec(memory_space=pl.ANY),
        out_shape=jax.ShapeDtypeStruct(xbuf.shape, xbuf.dtype),
        scratch_shapes=[pltpu.SemaphoreType.DMA(())],
        input_output_aliases={2: 0},
        compiler_params=pltpu.CompilerParams(dimension_semantics=("arbitrary",)),
        name="scatter_rows",
    )(slot3, h1t, xbuf)


def _expert_kernel(ea_ref, eb_ref, valid_ref, nu_ref, x_ref, w1_hbm, w3_hbm, w2_hbm, wr_ref, y_ref,
                   wup, w2a, w2b, s1a, s3a, s2a, s1b, s3b, s2b, wsem, *, layer):
    i = pl.program_id(0)
    nb = pl.num_programs(0)
    nu = nu_ref[0]
    used = i < nu
    ea = ea_ref[i]
    eb = eb_ref[i]

    def swap_weights(e_ref, slot, live, staged):
        e = e_ref[i]

        def fetch(expert):
            return [pltpu.make_async_copy(src.at[layer, expert], dst, wsem.at[slot])
                    for src, dst in zip((w1_hbm, w3_hbm, w2_hbm), staged)]

        @pl.when(used & (i == 0))
        def _():
            for cp in fetch(e):
                cp.start(priority=WEIGHT_DMA_QUEUE)

        @pl.when(used & ((i == 0) | (e != e_ref[jnp.maximum(i - 1, 0)])))
        def _():
            for cp in fetch(e):
                cp.wait()
            for dst, src in zip(live, staged):
                dst[...] = src[...].astype(BF16)
            nxt = lax.while_loop(lambda j: (j < nu) & (e_ref[jnp.minimum(j, nb - 1)] == e),
                                 lambda j: j + 1, i + 1)

            @pl.when(nxt < nu)
            def _():
                for cp in fetch(e_ref[jnp.minimum(nxt, nb - 1)]):
                    cp.start(priority=WEIGHT_DMA_QUEUE)

    def up_cols(k):
        return wup.at[:, pl.ds(k * D_EXPERT, D_EXPERT)]

    swap_weights(ea_ref, 0, (up_cols(0), up_cols(2), w2a), (s1a, s3a, s2a))
    swap_weights(eb_ref, 1, (up_cols(1), up_cols(3), w2b), (s1b, s3b, s2b))

    def ffn_rows(rows):
        sublanes = pl.ds(0, rows * ROW_TILE)
        xb = _load_row_tiled(x_ref.at[sublanes], rows).astype(BF16)

        logits = _dot(xb, wr_ref[...].astype(BF16))
        lane = lax.broadcasted_iota(I32, logits.shape, 1)
        logits = jnp.where(lane < N_EXPERTS, logits, -jnp.inf)
        ex = jnp.exp(logits - jnp.max(logits, axis=1, keepdims=True))
        probs = ex / jnp.sum(ex, axis=1, keepdims=True)
        pa = jnp.sum(jnp.where(lane == ea, probs, 0.0), axis=1, keepdims=True)
        pb = jnp.sum(jnp.where(lane == eb, probs, 0.0), axis=1, keepdims=True)
        den = pa + pb

        up = _dot(xb, wup[...])
        a = up[:, 0:2 * D_EXPERT]
        hid = (a * jax.nn.sigmoid(a) * up[:, 2 * D_EXPERT:4 * D_EXPERT]).astype(BF16)
        y = ((pa / den) * _dot(hid[:, 0:D_EXPERT], w2a[...])
             + (pb / den) * _dot(hid[:, D_EXPERT:2 * D_EXPERT], w2b[...]))
        _store_row_tiled(y_ref.at[sublanes], y)

    few = valid_ref[i] <= BM // 2

    @pl.when(used & jnp.logical_not(few))
    def _():
        ffn_rows(BM)

    @pl.when(used & few)
    def _():
        ffn_rows(BM // 2)
        rest = pl.ds(BM // 2 * ROW_TILE, BM // 2 * ROW_TILE)
        y_ref[rest, :] = jnp.zeros((BM // 2 * ROW_TILE, LANES), F32)


def _expert_ffn(layer, blk_ea, blk_eb, blk_valid, nused, xbuf, w1, w3, w2, wr_pad):
    n_blocks = xbuf.shape[0] // (BM * ROW_TILE)
    used_blk = lambda i, ea, eb, va, nu: (jnp.minimum(i, nu[0] - 1), 0)
    up = (D_MODEL, D_EXPERT)
    down = (D_EXPERT, D_MODEL)
    expert_slot = lambda dtype: [pltpu.VMEM(up, dtype), pltpu.VMEM(up, dtype), pltpu.VMEM(down, dtype)]
    grid_spec = pltpu.PrefetchScalarGridSpec(
        num_scalar_prefetch=4,
        grid=(n_blocks,),
        in_specs=[
            pl.BlockSpec((BM * ROW_TILE, LANES), used_blk),
            pl.BlockSpec(memory_space=pl.ANY),
            pl.BlockSpec(memory_space=pl.ANY),
            pl.BlockSpec(memory_space=pl.ANY),
            pl.BlockSpec((D_MODEL, LANES), lambda i, ea, eb, va, nu: (0, 0)),
        ],
        out_specs=pl.BlockSpec((BM * ROW_TILE, LANES), used_blk),
        scratch_shapes=([pltpu.VMEM((D_MODEL, 4 * D_EXPERT), BF16), pltpu.VMEM(down, BF16),
                         pltpu.VMEM(down, BF16)] + expert_slot(F32) + expert_slot(F32)
                        + [pltpu.SemaphoreType.DMA((2,))]),
    )
    return pl.pallas_call(
        functools.partial(_expert_kernel, layer=layer),
        grid_spec=grid_spec,
        out_shape=jax.ShapeDtypeStruct(xbuf.shape, F32),
        input_output_aliases={4: 0},
        compiler_params=pltpu.CompilerParams(
            dimension_semantics=("arbitrary",), vmem_limit_bytes=VMEM_LIMIT),
        name="expert_ffn",
    )(blk_ea, blk_eb, blk_valid, nused, xbuf, w1, w3, w2, wr_pad)


def _final_kernel(scur_ref, snext_ref, h1p_ref, y_hbm, g2_ref, b2_ref, o_ref, ybuf, gsem):
    i = pl.program_id(0)
    y = _prefetch_expert_rows(i, pl.num_programs(0), scur_ref, snext_ref, y_hbm, ybuf, gsem)
    o_ref[...] = _layer_norm(ALPHA * _load_row_tiled(h1p_ref, TS) + y, g2_ref[0], b2_ref[0])


def _final_call(slot_prev, h1t_prev, y_prev, g2, b2):
    t = h1t_prev.shape[0] // ROW_TILE
    nt = t // TS
    last = DEPTH - 1
    return pl.pallas_call(
        _final_kernel,
        grid=(nt,),
        in_specs=[
            pl.BlockSpec((1, 1, TS), lambda i: (i, 0, 0), memory_space=pltpu.SMEM),
            pl.BlockSpec((1, 1, TS), lambda i: (jnp.minimum(i + 1, nt - 1), 0, 0),
                         memory_space=pltpu.SMEM),
            pl.BlockSpec((TS * ROW_TILE, LANES), lambda i: (i, 0)),
            pl.BlockSpec(memory_space=pl.ANY),
            pl.BlockSpec((1, 1, D_MODEL), lambda i: (last, 0, 0)),
            pl.BlockSpec((1, 1, D_MODEL), lambda i: (last, 0, 0)),
        ],
        out_specs=pl.BlockSpec((TS, D_MODEL), lambda i: (i, 0)),
        out_shape=jax.ShapeDtypeStruct((t, D_MODEL), F32),
        scratch_shapes=[pltpu.VMEM((2, TS * ROW_TILE, LANES), F32), pltpu.SemaphoreType.DMA((2,))],
        compiler_params=pltpu.CompilerParams(
            dimension_semantics=("arbitrary",), vmem_limit_bytes=VMEM_LIMIT),
        name="final_combine_ln",
    )(slot_prev, slot_prev, h1t_prev, y_prev, g2, b2)


def kernel(x, ln_in_g, ln_in_b, w_in, pool_w, pool_scale, conv_w, conv_b, w_out, ln1_g, ln1_b,
           w_router, router_bias, exp_w1, exp_w3, exp_w2, ln2_g, ln2_b):
    bsz, seq, d = x.shape
    t = bsz * seq
    nt = t // TS
    n_rows = t + N_CLASSES * BM
    nb = n_rows // BM
    assert nb <= LANES

    per_layer_row = lambda v: v.reshape(DEPTH, 1, -1)
    tri = (lax.broadcasted_iota(I32, (TS, TS), 0)
           < lax.broadcasted_iota(I32, (TS, TS), 1)).astype(BF16)
    wr_t = w_router.T
    wr_pad = jnp.pad(w_router, ((0, 0), (0, LANES - N_EXPERTS)))
    rbias = router_bias.reshape(N_EXPERTS, 1).astype(F32)
    pool_scale3, conv_b3 = per_layer_row(pool_scale), per_layer_row(conv_b)
    ln1_g3, ln1_b3 = per_layer_row(ln1_g), per_layer_row(ln1_b)
    ln2_g3, ln2_b3 = per_layer_row(ln2_g), per_layer_row(ln2_b)

    prev = (x.reshape(t, d), ln_in_g.reshape(1, -1), ln_in_b.reshape(1, -1))
    xbuf = None
    for l in range(DEPTH):
        h1t, route, cnt = _layer_call(l, bsz, seq, prev, w_in, pool_w, pool_scale3, conv_w, conv_b3,
                                          w_out, ln1_g3, ln1_b3, wr_t, rbias, tri)
        slots, table = _dispatch_plan(cnt, route)
        slot3 = slots.reshape(nt, 1, TS)
        blk_ea, blk_eb, blk_valid = table[TABLE_EA, :nb], table[TABLE_EB, :nb], table[TABLE_VALID, :nb]
        nused = table[TABLE_NUSED, :1]
        if xbuf is None:
            xbuf = _first_scatter_rows(table[TABLE_TAIL_START, :N_CLASSES],
                                       table[TABLE_TAIL_LEN, :N_CLASSES], nused, slot3, h1t, n_rows)
        else:
            xbuf = _scatter_rows(slot3, h1t, xbuf)
        xbuf = _expert_ffn(l, blk_ea, blk_eb, blk_valid, nused, xbuf, exp_w1, exp_w3, exp_w2, wr_pad)
        prev = (slot3, h1t, xbuf, ln2_g3, ln2_b3)
    out = _final_call(*prev)
    return out.reshape(bsz, seq, d)
```

```python
import functools

import jax
import jax.numpy as jnp
from jax import lax
from jax.experimental import pallas as pl
from jax.experimental.pallas import tpu as pltpu

D_MODEL = 1024
DEPTH = 4
D_POOL = 512
POOL_WINDOWS = (2, 4, 8, 16)
POOL_GROUP = 128
D_CONV = 512
CONV_WIDTH = 3
D_PROJ = D_POOL + 3 * D_CONV
N_EXPERTS = 16
N_EXPERT_GROUPS = 4
EXPERTS_PER_GROUP = 4
PAIRS_PER_GROUP = 6
N_CLASSES = N_EXPERT_GROUPS * PAIRS_PER_GROUP
CLASS_ROWS = 32
D_EXPERT = 512
ALPHA = float((2 * DEPTH) ** 0.25)
LN_EPS = 1e-5

F32 = jnp.float32
BF16 = jnp.bfloat16
I32 = jnp.int32

LANES = 128
ROW_TILE = D_MODEL // LANES
TS = 512
SUBLANES = 8
V_HALO = 32
Z_HALO = 8
BM = 256
W_STAGE_ROWS = 256
DMA_QUEUES = 2
WEIGHT_DMA_QUEUE = 1
VMEM_LIMIT = 56 * 1024 * 1024


def _layer_norm(x, g, b):
    mu = jnp.mean(x, axis=-1, keepdims=True)
    xc = x - mu
    var = jnp.mean(xc * xc, axis=-1, keepdims=True)
    return xc * lax.rsqrt(var + LN_EPS) * g + b


def _dot(a, b):
    return jnp.dot(a, b, preferred_element_type=F32)


def _store_row_tiled(ref, value):
    n = value.shape[0]
    for j in range(ROW_TILE):
        ref[pl.ds(j, n, stride=ROW_TILE), :] = value[:, j * LANES:(j + 1) * LANES]


def _load_row_tiled(ref, n):
    return jnp.concatenate([ref[pl.ds(j, n, stride=ROW_TILE), :] for j in range(ROW_TILE)], axis=1)


def _row(ref, first_sublane):
    return ref.at[pl.ds(first_sublane, ROW_TILE)]


def _route(sel):
    rows = [sel[e:e + 1, :] for e in range(N_EXPERTS)]
    gscore = []
    for g in range(N_EXPERT_GROUPS):
        a, b, c, d = rows[4 * g:4 * g + 4]
        hi1, lo1 = jnp.maximum(a, b), jnp.minimum(a, b)
        hi2, lo2 = jnp.maximum(c, d), jnp.minimum(c, d)
        m1 = jnp.maximum(hi1, hi2)
        m2 = jnp.maximum(jnp.minimum(hi1, hi2), jnp.maximum(lo1, lo2))
        gscore.append(m1 + m2)
    best = gscore[0]
    gidx = jnp.zeros(best.shape, I32)
    for g in range(1, N_EXPERT_GROUPS):
        upd = gscore[g] > best
        best = jnp.where(upd, gscore[g], best)
        gidx = jnp.where(upd, g, gidx)
    neg = jnp.full(best.shape, -jnp.inf, F32)
    masked = [jnp.where(gidx == (e // EXPERTS_PER_GROUP), rows[e], neg) for e in range(N_EXPERTS)]

    def top1(vals):
        bv = neg
        bi = jnp.zeros(best.shape, I32)
        for e in range(N_EXPERTS):
            upd = vals[e] > bv
            bv = jnp.where(upd, vals[e], bv)
            bi = jnp.where(upd, e, bi)
        return bi

    i1 = top1(masked)
    i2 = top1([jnp.where(i1 == e, neg, masked[e]) for e in range(N_EXPERTS)])
    return gidx, i1, i2


def _pair_class(gidx, i1, i2):
    a = jnp.minimum(i1, i2) - gidx * EXPERTS_PER_GROUP
    b = jnp.maximum(i1, i2) - gidx * EXPERTS_PER_GROUP
    base = jnp.where(a == 0, 0, jnp.where(a == 1, 2, 3))
    return gidx * PAIRS_PER_GROUP + base + b - 1


def _prefetch_expert_rows(i, nt, scur_ref, snext_ref, y_hbm, ybuf, gsem, before_wait=None):
    slot = i % 2

    def gather(slot_ref, to_slot, t, t_sublane):
        src = _row(y_hbm, pl.multiple_of(slot_ref[0, 0, t], ROW_TILE))
        return pltpu.make_async_copy(src, _row(ybuf.at[to_slot], t_sublane), gsem.at[to_slot])

    @pl.when(i == 0)
    def _():
        for t in range(TS):
            gather(scur_ref, 0, t, t * ROW_TILE).start(priority=t % DMA_QUEUES)

    @pl.when(i + 1 < nt)
    def _():
        for t in range(TS):
            gather(snext_ref, 1 - slot, t, t * ROW_TILE).start(priority=t % DMA_QUEUES)

    if before_wait is not None:
        before_wait()
    pltpu.make_async_copy(y_hbm.at[pl.ds(0, TS * ROW_TILE)], ybuf.at[slot], gsem.at[slot]).wait()
    return _load_row_tiled(ybuf.at[slot], TS)


def _layer_kernel(*refs, layer, tiles_per_seq):
    if layer == 0:
        x_ref, gin_ref, bin_ref = refs[:3]
        rest = refs[3:]
    else:
        scur_ref, snext_ref, h1p_ref, y_hbm, g2_ref, b2_ref = refs[:6]
        rest = refs[6:]
    (win_hbm, pw_ref, ps_ref, cw_ref, cb_ref, wout_hbm, g_ref, b_ref, wr_ref, rb_ref, tri_ref,
     h1t_ref, route_ref, cnt_ref,
     vbuf, pwork, zbuf, mixbuf, carry, win_ref, wout_ref, stage, wsem, ybuf, gsem) = rest

    i = pl.program_id(0)
    nt = pl.num_programs(0)

    def stage_weights():
        chunks = []
        for src, dst, width in ((win_hbm, win_ref, D_PROJ), (wout_hbm, wout_ref, D_MODEL)):
            for c in range(D_MODEL // W_STAGE_ROWS):
                chunks.append((src, dst, width, pl.ds(c * W_STAGE_ROWS, W_STAGE_ROWS)))

        def fetch(k):
            src, _, width, rows = chunks[k]
            return pltpu.make_async_copy(src.at[layer, rows], stage.at[k % 2, :, pl.ds(0, width)],
                                         wsem.at[k % 2])

        fetch(0).start()
        for k, (_, dst, width, rows) in enumerate(chunks):
            if k + 1 < len(chunks):
                fetch(k + 1).start()
            fetch(k).wait()
            dst[rows, :] = stage[k % 2, :, 0:width].astype(BF16)
        carry[...] = jnp.zeros(carry.shape, F32)

    first_step_setup = functools.partial(pl.when(i == 0), stage_weights)
    if layer == 0:
        first_step_setup()
        h = _layer_norm(x_ref[...], gin_ref[...], bin_ref[...])
    else:
        y = _prefetch_expert_rows(i, nt, scur_ref, snext_ref, y_hbm, ybuf, gsem,
                                  before_wait=first_step_setup)
        h = _layer_norm(ALPHA * _load_row_tiled(h1p_ref, TS) + y, g2_ref[0], b2_ref[0])

    si = i % tiles_per_seq

    @pl.when(si == 0)
    def _():
        vbuf[0:V_HALO, :] = jnp.zeros((V_HALO, D_POOL), F32)
        zbuf[0:Z_HALO, :] = jnp.zeros((Z_HALO, D_CONV), F32)

    hb = h.astype(BF16)
    vbuf[V_HALO:V_HALO + TS, :] = _dot(hb, win_ref[:, 0:D_POOL])
    gate_b = _dot(hb, win_ref[:, D_POOL:D_POOL + D_CONV])
    gate_c = _dot(hb, win_ref[:, D_POOL + D_CONV:D_POOL + 2 * D_CONV])
    u = _dot(hb, win_ref[:, D_POOL + 2 * D_CONV:D_PROJ])
    z = gate_c * u
    zbuf[Z_HALO:Z_HALO + TS, :] = z

    src = vbuf
    for level in range(1, len(POOL_WINDOWS)):
        dst = pwork.at[(level - 1) % 2]
        lo = SUBLANES * level
        shift = 2 ** (level - 1)
        n = TS + V_HALO - lo
        cols = pl.ds(level * POOL_GROUP, D_POOL - level * POOL_GROUP)
        dst[pl.ds(lo, n), cols] = src[pl.ds(lo, n), cols] + src[pl.ds(lo - shift, n), cols]
        src = dst
    tpos = si * TS + lax.broadcasted_iota(I32, (TS, 1), 0)
    for g, win in enumerate(POOL_WINDOWS):
        cols = pl.ds(g * POOL_GROUP, POOL_GROUP)
        cur = vbuf[pl.ds(V_HALO, TS), cols]
        half = vbuf if g == 0 else pwork.at[(g - 1) % 2]
        acc = half[pl.ds(V_HALO, TS), cols] + half[pl.ds(V_HALO - win // 2, TS), cols]
        denom = jnp.minimum(tpos + 1, win).astype(F32)
        pooled = acc / denom - cur
        mixed = _dot(pooled.astype(BF16), pw_ref[0, g].astype(BF16)) * ps_ref[0, :, cols]
        mixbuf[:, cols] = mixed.astype(BF16)

    yc = (cw_ref[0, 2:3, :] * z
          + cw_ref[0, 1:2, :] * zbuf[pl.ds(Z_HALO - 1, TS), :]
          + cw_ref[0, 0:1, :] * zbuf[pl.ds(Z_HALO - 2, TS), :])
    mixbuf[:, D_POOL:D_POOL + D_CONV] = (gate_b * (yc + cb_ref[0])).astype(BF16)

    vbuf[0:V_HALO, :] = vbuf[TS:TS + V_HALO, :]
    zbuf[0:Z_HALO, :] = zbuf[TS:TS + Z_HALO, :]

    mix = _dot(mixbuf[...], wout_ref[...])
    h1 = _layer_norm(ALPHA * h + mix, g_ref[0], b_ref[0])
    _store_row_tiled(h1t_ref, h1)

    logits = lax.dot_general(wr_ref[...].astype(BF16), h1.astype(BF16), (((1,), (1,)), ((), ())),
                             preferred_element_type=F32)
    mx = jnp.max(logits, axis=0, keepdims=True)
    ex = jnp.exp(logits - mx)
    probs = ex / jnp.sum(ex, axis=0, keepdims=True)
    gidx, i1, i2 = _route(probs + rb_ref[...])
    cls = _pair_class(gidx, i1, i2)

    ciota = lax.broadcasted_iota(I32, (CLASS_ROWS, TS), 0)
    onehot = jnp.where(ciota == cls, 1.0, 0.0).astype(F32)
    before = _dot(onehot.astype(BF16), tri_ref[...])
    total = jnp.sum(onehot, axis=1, keepdims=True)
    c = carry[:, 0:1]
    rank = jnp.sum(onehot * (c + before), axis=0, keepdims=True)
    route_ref[0:1, :] = cls
    route_ref[1:2, :] = rank.astype(I32)
    newc = jnp.broadcast_to(c + total, carry.shape)
    carry[...] = newc
    cnt_ref[...] = newc


def _layer_call(layer, bsz, seq, prev, w_in, pool_w, pool_scale, conv_w, conv_b, w_out, ln_g, ln_b,
                wr_t, rbias, tri):
    t = bsz * seq
    nt = t // TS
    const2 = lambda i: (0, 0)
    lsel3 = lambda i: (layer, 0, 0)
    lsel4 = lambda i: (layer, 0, 0, 0)
    tile = lambda i: (i, 0)
    lane_tile = lambda i: (0, i)
    if layer == 0:
        x, gin, bin_ = prev
        head_args = (x, gin, bin_)
        head_specs = [
            pl.BlockSpec((TS, D_MODEL), tile),
            pl.BlockSpec((1, D_MODEL), const2),
            pl.BlockSpec((1, D_MODEL), const2),
        ]
    else:
        slot_prev, h1t_prev, y_prev, g2, b2 = prev
        head_args = (slot_prev, slot_prev, h1t_prev, y_prev, g2, b2)
        head_specs = [
            pl.BlockSpec((1, 1, TS), lambda i: (i, 0, 0), memory_space=pltpu.SMEM),
            pl.BlockSpec((1, 1, TS), lambda i: (jnp.minimum(i + 1, nt - 1), 0, 0),
                         memory_space=pltpu.SMEM),
            pl.BlockSpec((TS * ROW_TILE, LANES), tile),
            pl.BlockSpec(memory_space=pl.ANY),
            pl.BlockSpec((1, 1, D_MODEL), lambda i: (layer - 1, 0, 0)),
            pl.BlockSpec((1, 1, D_MODEL), lambda i: (layer - 1, 0, 0)),
        ]
    return pl.pallas_call(
        functools.partial(_layer_kernel, layer=layer, tiles_per_seq=seq // TS),
        grid=(nt,),
        in_specs=head_specs + [
            pl.BlockSpec(memory_space=pl.ANY),
            pl.BlockSpec((1, len(POOL_WINDOWS), POOL_GROUP, POOL_GROUP), lsel4),
            pl.BlockSpec((1, 1, D_POOL), lsel3),
            pl.BlockSpec((1, CONV_WIDTH, D_CONV), lsel3),
            pl.BlockSpec((1, 1, D_CONV), lsel3),
            pl.BlockSpec(memory_space=pl.ANY),
            pl.BlockSpec((1, 1, D_MODEL), lsel3),
            pl.BlockSpec((1, 1, D_MODEL), lsel3),
            pl.BlockSpec((N_EXPERTS, D_MODEL), const2),
            pl.BlockSpec((N_EXPERTS, 1), const2),
            pl.BlockSpec((TS, TS), const2),
        ],
        out_specs=[
            pl.BlockSpec((TS * ROW_TILE, LANES), tile),
            pl.BlockSpec((2, TS), lane_tile),
            pl.BlockSpec((CLASS_ROWS, LANES), const2),
        ],
        out_shape=[
            jax.ShapeDtypeStruct((t * ROW_TILE, LANES), F32),
            jax.ShapeDtypeStruct((2, t), I32),
            jax.ShapeDtypeStruct((CLASS_ROWS, LANES), F32),
        ],
        scratch_shapes=[
            pltpu.VMEM((TS + V_HALO, D_POOL), F32),
            pltpu.VMEM((2, TS + V_HALO, D_POOL), F32),
            pltpu.VMEM((TS + Z_HALO, D_CONV), F32),
            pltpu.VMEM((TS, D_MODEL), BF16),
            pltpu.VMEM((CLASS_ROWS, LANES), F32),
            pltpu.VMEM((D_MODEL, D_PROJ), BF16),
            pltpu.VMEM((D_MODEL, D_MODEL), BF16),
            pltpu.VMEM((2, W_STAGE_ROWS, D_PROJ), F32),
            pltpu.SemaphoreType.DMA((2,)),
            pltpu.VMEM((2, TS * ROW_TILE, LANES), F32),
            pltpu.SemaphoreType.DMA((2,)),
        ],
        compiler_params=pltpu.CompilerParams(
            dimension_semantics=("arbitrary",), vmem_limit_bytes=VMEM_LIMIT),
        name="layer_mix_route",
    )(*head_args, w_in, pool_w, pool_scale, conv_w, conv_b, w_out, ln_g, ln_b, wr_t, rbias, tri)


TABLE_EA, TABLE_EB, TABLE_VALID, TABLE_NUSED = 0, 1, 2, 3
TABLE_TAIL_START, TABLE_TAIL_LEN = 4, 5


def _dispatch_kernel(cnt_ref, route_ref, slot_ref, table_ref):
    log_bm = BM.bit_length() - 1
    starts, ends, fills = [], [], []
    end = jnp.zeros((1, LANES), I32)
    for k in range(N_CLASSES):
        count = cnt_ref[k:k + 1, :].astype(I32)
        starts.append(end)
        fills.append(end + count)
        end = end + lax.shift_left(lax.shift_right_logical(count + (BM - 1), log_bm), log_bm)
        ends.append(end)
    nused = lax.shift_right_logical(end, log_bm)

    cls = route_ref[0:1, :]
    start = jnp.zeros(cls.shape, I32)
    for k in range(N_CLASSES):
        start = jnp.where(cls == k, starts[k][:, 0:1], start)
    slot_ref[...] = (start + route_ref[1:2, :]) * ROW_TILE

    step = jnp.minimum(lax.broadcasted_iota(I32, (1, LANES), 1), nused - 1)
    first_row = step * BM
    c = jnp.zeros((1, LANES), I32)
    fill = fills[0]
    for k in range(N_CLASSES - 1):
        beyond = first_row >= ends[k]
        c = c + beyond.astype(I32)
        fill = jnp.where(beyond, fills[k + 1], fill)
    g = sum((c >= k * PAIRS_PER_GROUP).astype(I32) for k in range(1, N_EXPERT_GROUPS))
    p = c - g * PAIRS_PER_GROUP
    ge3 = (p >= 3).astype(I32)
    ge5 = (p >= 5).astype(I32)
    table_ref[TABLE_EA:TABLE_EA + 1, :] = g * EXPERTS_PER_GROUP + ge3 + ge5
    table_ref[TABLE_EB:TABLE_EB + 1, :] = g * EXPERTS_PER_GROUP + p + 1 - 2 * ge3 - ge5
    table_ref[TABLE_VALID:TABLE_VALID + 1, :] = jnp.clip(fill - first_row, 0, BM)
    table_ref[TABLE_NUSED:TABLE_NUSED + 1, :] = nused

    lane = lax.broadcasted_iota(I32, (1, LANES), 1)
    tail_start = jnp.zeros((1, LANES), I32)
    tail_len = jnp.zeros((1, LANES), I32)
    for k in range(N_CLASSES):
        tail_start = jnp.where(lane == k, fills[k], tail_start)
        tail_len = jnp.where(lane == k, ends[k] - fills[k], tail_len)
    table_ref[TABLE_TAIL_START:TABLE_TAIL_START + 1, :] = tail_start
    table_ref[TABLE_TAIL_LEN:TABLE_TAIL_LEN + 1, :] = tail_len
    table_ref[TABLE_TAIL_LEN + 1:, :] = jnp.zeros((SUBLANES - TABLE_TAIL_LEN - 1, LANES), I32)


def _dispatch_plan(cnt, route):
    t = route.shape[1]
    return pl.pallas_call(
        _dispatch_kernel,
        grid=(1,),
        in_specs=[pl.BlockSpec((CLASS_ROWS, LANES), lambda i: (0, 0)),
                  pl.BlockSpec((2, t), lambda i: (0, 0))],
        out_specs=[pl.BlockSpec((1, t), lambda i: (0, 0)),
                   pl.BlockSpec((SUBLANES, LANES), lambda i: (0, 0))],
        out_shape=[jax.ShapeDtypeStruct((1, t), I32), jax.ShapeDtypeStruct((SUBLANES, LANES), I32)],
        name="dispatch_plan",
    )(cnt, route)


def _scatter_kernel(slot_ref, h1t_ref, xin_ref, xout_ref, sem):
    del xin_ref
    for t in range(TS):
        dst = _row(xout_ref, pl.multiple_of(slot_ref[0, 0, t], ROW_TILE))
        pltpu.make_async_copy(_row(h1t_ref, t * ROW_TILE), dst, sem).start(priority=t % DMA_QUEUES)
    pltpu.make_async_copy(h1t_ref, xout_ref.at[pl.ds(0, TS * ROW_TILE)], sem).wait()


def _first_scatter_kernel(tail_start_ref, tail_len_ref, nu_ref, slot_ref, h1t_ref, xout_ref,
                          zeros, sem, zsem):
    i = pl.program_id(0)
    block = BM * ROW_TILE
    n_blocks = xout_ref.shape[0] // block

    def fill_rows(visit):
        for k in range(N_CLASSES):
            length = tail_len_ref[k]
            row = tail_start_ref[k]
            piece = BM // 2
            while piece >= 1:
                take = (length & piece) != 0
                dst = xout_ref.at[pl.ds(pl.multiple_of(row * ROW_TILE, ROW_TILE), piece * ROW_TILE)]
                cp = pltpu.make_async_copy(zeros.at[pl.ds(0, piece * ROW_TILE)], dst, zsem)
                pl.when(take)(functools.partial(visit, cp))
                row = row + jnp.where(take, piece, 0)
                piece //= 2

        def idle_block(j, carry):
            dst = xout_ref.at[pl.ds(pl.multiple_of(j * block, block), block)]
            visit(pltpu.make_async_copy(zeros, dst, zsem))
            return carry
        lax.fori_loop(nu_ref[0], n_blocks, idle_block, 0)

    for t in range(TS):
        dst = _row(xout_ref, pl.multiple_of(slot_ref[0, 0, t], ROW_TILE))
        pltpu.make_async_copy(_row(h1t_ref, t * ROW_TILE), dst, sem).start(priority=t % DMA_QUEUES)

    @pl.when(i == 0)
    def _():
        zeros[...] = jnp.zeros(zeros.shape, F32)
        fill_rows(lambda cp: cp.start())

    pltpu.make_async_copy(h1t_ref, xout_ref.at[pl.ds(0, TS * ROW_TILE)], sem).wait()

    @pl.when(i == pl.num_programs(0) - 1)
    def _():
        fill_rows(lambda cp: cp.wait())


def _first_scatter_rows(tail_start, tail_len, nused, slot3, h1t, n_rows):
    nt = slot3.shape[0]
    grid_spec = pltpu.PrefetchScalarGridSpec(
        num_scalar_prefetch=3,
        grid=(nt,),
        in_specs=[
            pl.BlockSpec((1, 1, TS), lambda i, *_: (i, 0, 0), memory_space=pltpu.SMEM),
            pl.BlockSpec((TS * ROW_TILE, LANES), lambda i, *_: (i, 0)),
        ],
        out_specs=pl.BlockSpec(memory_space=pl.ANY),
        scratch_shapes=[pltpu.VMEM((BM * ROW_TILE, LANES), F32), pltpu.SemaphoreType.DMA(()),
                        pltpu.SemaphoreType.DMA(())],
    )
    return pl.pallas_call(
        _first_scatter_kernel,
        grid_spec=grid_spec,
        out_shape=jax.ShapeDtypeStruct((n_rows * ROW_TILE, LANES), F32),
        compiler_params=pltpu.CompilerParams(dimension_semantics=("arbitrary",)),
        name="first_scatter_rows",
    )(tail_start, tail_len, nused, slot3, h1t)


def _scatter_rows(slot3, h1t, xbuf):
    nt = slot3.shape[0]
    return pl.pallas_call(
        _scatter_kernel,
        grid=(nt,),
        in_specs=[
            pl.BlockSpec((1, 1, TS), lambda i: (i, 0, 0), memory_space=pltpu.SMEM),
            pl.BlockSpec((TS * ROW_TILE, LANES), lambda i: (i, 0)),
            pl.BlockSpec(memory_space=pl.ANY),
        ],
        out_specs=pl.BlockSpec(memory_space=pl.ANY),
        out_shape=jax.ShapeDtypeStruct(xbuf.shape, xbuf.dtype),
        scratch_shapes=[pltpu.SemaphoreType.DMA(())],
        input_output_aliases={2: 0},
        compiler_params=pltpu.CompilerParams(dimension_semantics=("arbitrary",)),
        name="scatter_rows",
    )(slot3, h1t, xbuf)


def _expert_kernel(ea_ref, eb_ref, valid_ref, nu_ref, x_ref, w1_hbm, w3_hbm, w2_hbm, wr_ref, y_ref,
                   wup, w2a, w2b, s1a, s3a, s2a, s1b, s3b, s2b, wsem, *, layer):
    i = pl.program_id(0)
    nb = pl.num_programs(0)
    nu = nu_ref[0]
    used = i < nu
    ea = ea_ref[i]
    eb = eb_ref[i]

    def swap_weights(e_ref, slot, live, staged):
        e = e_ref[i]

        def fetch(expert):
            return [pltpu.make_async_copy(src.at[layer, expert], dst, wsem.at[slot])
                    for src, dst in zip((w1_hbm, w3_hbm, w2_hbm), staged)]

        @pl.when(used & (i == 0))
        def _():
            for cp in fetch(e):
                cp.start(priority=WEIGHT_DMA_QUEUE)

        @pl.when(used & ((i == 0) | (e != e_ref[jnp.maximum(i - 1, 0)])))
        def _():
            for cp in fetch(e):
                cp.wait()
            for dst, src in zip(live, staged):
                dst[...] = src[...].astype(BF16)
            nxt = lax.while_loop(lambda j: (j < nu) & (e_ref[jnp.minimum(j, nb - 1)] == e),
                                 lambda j: j + 1, i + 1)

            @pl.when(nxt < nu)
            def _():
                for cp in fetch(e_ref[jnp.minimum(nxt, nb - 1)]):
                    cp.start(priority=WEIGHT_DMA_QUEUE)

    def up_cols(k):
        return wup.at[:, pl.ds(k * D_EXPERT, D_EXPERT)]

    swap_weights(ea_ref, 0, (up_cols(0), up_cols(2), w2a), (s1a, s3a, s2a))
    swap_weights(eb_ref, 1, (up_cols(1), up_cols(3), w2b), (s1b, s3b, s2b))

    def ffn_rows(rows):
        sublanes = pl.ds(0, rows * ROW_TILE)
        xb = _load_row_tiled(x_ref.at[sublanes], rows).astype(BF16)

        logits = _dot(xb, wr_ref[...].astype(BF16))
        lane = lax.broadcasted_iota(I32, logits.shape, 1)
        logits = jnp.where(lane < N_EXPERTS, logits, -jnp.inf)
        ex = jnp.exp(logits - jnp.max(logits, axis=1, keepdims=True))
        probs = ex / jnp.sum(ex, axis=1, keepdims=True)
        pa = jnp.sum(jnp.where(lane == ea, probs, 0.0), axis=1, keepdims=True)
        pb = jnp.sum(jnp.where(lane == eb, probs, 0.0), axis=1, keepdims=True)
        den = pa + pb

        up = _dot(xb, wup[...])
        a = up[:, 0:2 * D_EXPERT]
        hid = (a * jax.nn.sigmoid(a) * up[:, 2 * D_EXPERT:4 * D_EXPERT]).astype(BF16)
        y = ((pa / den) * _dot(hid[:, 0:D_EXPERT], w2a[...])
             + (pb / den) * _dot(hid[:, D_EXPERT:2 * D_EXPERT], w2b[...]))
        _store_row_tiled(y_ref.at[sublanes], y)

    few = valid_ref[i] <= BM // 2

    @pl.when(used & jnp.logical_not(few))
    def _():
        ffn_rows(BM)

    @pl.when(used & few)
    def _():
        ffn_rows(BM // 2)
        rest = pl.ds(BM // 2 * ROW_TILE, BM // 2 * ROW_TILE)
        y_ref[rest, :] = jnp.zeros((BM // 2 * ROW_TILE, LANES), F32)


def _expert_ffn(layer, blk_ea, blk_eb, blk_valid, nused, xbuf, w1, w3, w2, wr_pad):
    n_blocks = xbuf.shape[0] // (BM * ROW_TILE)
    used_blk = lambda i, ea, eb, va, nu: (jnp.minimum(i, nu[0] - 1), 0)
    up = (D_MODEL, D_EXPERT)
    down = (D_EXPERT, D_MODEL)
    expert_slot = lambda dtype: [pltpu.VMEM(up, dtype), pltpu.VMEM(up, dtype), pltpu.VMEM(down, dtype)]
    grid_spec = pltpu.PrefetchScalarGridSpec(
        num_scalar_prefetch=4,
        grid=(n_blocks,),
        in_specs=[
            pl.BlockSpec((BM * ROW_TILE, LANES), used_blk),
            pl.BlockSpec(memory_space=pl.ANY),
            pl.BlockSpec(memory_space=pl.ANY),
            pl.BlockSpec(memory_space=pl.ANY),
            pl.BlockSpec((D_MODEL, LANES), lambda i, ea, eb, va, nu: (0, 0)),
        ],
        out_specs=pl.BlockSpec((BM * ROW_TILE, LANES), used_blk),
        scratch_shapes=([pltpu.VMEM((D_MODEL, 4 * D_EXPERT), BF16), pltpu.VMEM(down, BF16),
                         pltpu.VMEM(down, BF16)] + expert_slot(F32) + expert_slot(F32)
                        + [pltpu.SemaphoreType.DMA((2,))]),
    )
    return pl.pallas_call(
        functools.partial(_expert_kernel, layer=layer),
        grid_spec=grid_spec,
        out_shape=jax.ShapeDtypeStruct(xbuf.shape, F32),
        input_output_aliases={4: 0},
        compiler_params=pltpu.CompilerParams(
            dimension_semantics=("arbitrary",), vmem_limit_bytes=VMEM_LIMIT),
        name="expert_ffn",
    )(blk_ea, blk_eb, blk_valid, nused, xbuf, w1, w3, w2, wr_pad)


def _final_kernel(scur_ref, snext_ref, h1p_ref, y_hbm, g2_ref, b2_ref, o_ref, ybuf, gsem):
    i = pl.program_id(0)
    y = _prefetch_expert_rows(i, pl.num_programs(0), scur_ref, snext_ref, y_hbm, ybuf, gsem)
    o_ref[...] = _layer_norm(ALPHA * _load_row_tiled(h1p_ref, TS) + y, g2_ref[0], b2_ref[0])


def _final_call(slot_prev, h1t_prev, y_prev, g2, b2):
    t = h1t_prev.shape[0] // ROW_TILE
    nt = t // TS
    last = DEPTH - 1
    return pl.pallas_call(
        _final_kernel,
        grid=(nt,),
        in_specs=[
            pl.BlockSpec((1, 1, TS), lambda i: (i, 0, 0), memory_space=pltpu.SMEM),
            pl.BlockSpec((1, 1, TS), lambda i: (jnp.minimum(i + 1, nt - 1), 0, 0),
                         memory_space=pltpu.SMEM),
            pl.BlockSpec((TS * ROW_TILE, LANES), lambda i: (i, 0)),
            pl.BlockSpec(memory_space=pl.ANY),
            pl.BlockSpec((1, 1, D_MODEL), lambda i: (last, 0, 0)),
            pl.BlockSpec((1, 1, D_MODEL), lambda i: (last, 0, 0)),
        ],
        out_specs=pl.BlockSpec((TS, D_MODEL), lambda i: (i, 0)),
        out_shape=jax.ShapeDtypeStruct((t, D_MODEL), F32),
        scratch_shapes=[pltpu.VMEM((2, TS * ROW_TILE, LANES), F32), pltpu.SemaphoreType.DMA((2,))],
        compiler_params=pltpu.CompilerParams(
            dimension_semantics=("arbitrary",), vmem_limit_bytes=VMEM_LIMIT),
        name="final_combine_ln",
    )(slot_prev, slot_prev, h1t_prev, y_prev, g2, b2)


def kernel(x, ln_in_g, ln_in_b, w_in, pool_w, pool_scale, conv_w, conv_b, w_out, ln1_g, ln1_b,
           w_router, router_bias, exp_w1, exp_w3, exp_w2, ln2_g, ln2_b):
    bsz, seq, d = x.shape
    t = bsz * seq
    nt = t // TS
    n_rows = t + N_CLASSES * BM
    nb = n_rows // BM
    assert nb <= LANES

    per_layer_row = lambda v: v.reshape(DEPTH, 1, -1)
    tri = (lax.broadcasted_iota(I32, (TS, TS), 0)
           < lax.broadcasted_iota(I32, (TS, TS), 1)).astype(BF16)
    wr_t = w_router.T
    wr_pad = jnp.pad(w_router, ((0, 0), (0, LANES - N_EXPERTS)))
    rbias = router_bias.reshape(N_EXPERTS, 1).astype(F32)
    pool_scale3, conv_b3 = per_layer_row(pool_scale), per_layer_row(conv_b)
    ln1_g3, ln1_b3 = per_layer_row(ln1_g), per_layer_row(ln1_b)
    ln2_g3, ln2_b3 = per_layer_row(ln2_g), per_layer_row(ln2_b)

    prev = (x.reshape(t, d), ln_in_g.reshape(1, -1), ln_in_b.reshape(1, -1))
    xbuf = None
    for l in range(DEPTH):
        h1t, route, cnt = _layer_call(l, bsz, seq, prev, w_in, pool_w, pool_scale3, conv_w, conv_b3,
                                          w_out, ln1_g3, ln1_b3, wr_t, rbias, tri)
        slots, table = _dispatch_plan(cnt, route)
        slot3 = slots.reshape(nt, 1, TS)
        blk_ea, blk_eb, blk_valid = table[TABLE_EA, :nb], table[TABLE_EB, :nb], table[TABLE_VALID, :nb]
        nused = table[TABLE_NUSED, :1]
        if xbuf is None:
            xbuf = _first_scatter_rows(table[TABLE_TAIL_START, :N_CLASSES],
                                       table[TABLE_TAIL_LEN, :N_CLASSES], nused, slot3, h1t, n_rows)
        else:
            xbuf = _scatter_rows(slot3, h1t, xbuf)
        xbuf = _expert_ffn(l, blk_ea, blk_eb, blk_valid, nused, xbuf, exp_w1, exp_w3, exp_w2, wr_pad)
        prev = (slot3, h1t, xbuf, ln2_g3, ln2_b3)
    out = _final_call(*prev)
    return out.reshape(bsz, seq, d)
```

```python
import functools

import jax
import jax.numpy as jnp
from jax import lax
from jax.experimental import pallas as pl
from jax.experimental.pallas import tpu as pltpu

D_MODEL = 1024
DEPTH = 4
D_POOL = 512
POOL_WINDOWS = (2, 4, 8, 16)
POOL_GROUP = 128
D_CONV = 512
CONV_WIDTH = 3
D_PROJ = D_POOL + 3 * D_CONV
N_EXPERTS = 16
N_EXPERT_GROUPS = 4
EXPERTS_PER_GROUP = 4
PAIRS_PER_GROUP = 6
N_CLASSES = N_EXPERT_GROUPS * PAIRS_PER_GROUP
CLASS_ROWS = 32
D_EXPERT = 512
ALPHA = float((2 * DEPTH) ** 0.25)
LN_EPS = 1e-5

F32 = jnp.float32
BF16 = jnp.bfloat16
I32 = jnp.int32

LANES = 128
ROW_TILE = D_MODEL // LANES
TS = 512
SUBLANES = 8
V_HALO = 32
Z_HALO = 8
BM = 256
W_STAGE_ROWS = 256
DMA_QUEUES = 2
WEIGHT_DMA_QUEUE = 1
VMEM_LIMIT = 56 * 1024 * 1024


def _layer_norm(x, g, b):
    mu = jnp.mean(x, axis=-1, keepdims=True)
    xc = x - mu
    var = jnp.mean(xc * xc, axis=-1, keepdims=True)
    return xc * lax.rsqrt(var + LN_EPS) * g + b


def _dot(a, b):
    return jnp.dot(a, b, preferred_element_type=F32)


def _store_row_tiled(ref, value):
    n = value.shape[0]
    for j in range(ROW_TILE):
        ref[pl.ds(j, n, stride=ROW_TILE), :] = value[:, j * LANES:(j + 1) * LANES]


def _load_row_tiled(ref, n):
    return jnp.concatenate([ref[pl.ds(j, n, stride=ROW_TILE), :] for j in range(ROW_TILE)], axis=1)


def _row(ref, first_sublane):
    return ref.at[pl.ds(first_sublane, ROW_TILE)]


def _route(sel):
    rows = [sel[e:e + 1, :] for e in range(N_EXPERTS)]
    gscore = []
    for g in range(N_EXPERT_GROUPS):
        a, b, c, d = rows[4 * g:4 * g + 4]
        hi1, lo1 = jnp.maximum(a, b), jnp.minimum(a, b)
        hi2, lo2 = jnp.maximum(c, d), jnp.minimum(c, d)
        m1 = jnp.maximum(hi1, hi2)
        m2 = jnp.maximum(jnp.minimum(hi1, hi2), jnp.maximum(lo1, lo2))
        gscore.append(m1 + m2)
    best = gscore[0]
    gidx = jnp.zeros(best.shape, I32)
    for g in range(1, N_EXPERT_GROUPS):
        upd = gscore[g] > best
        best = jnp.where(upd, gscore[g], best)
        gidx = jnp.where(upd, g, gidx)
    neg = jnp.full(best.shape, -jnp.inf, F32)
    masked = [jnp.where(gidx == (e // EXPERTS_PER_GROUP), rows[e], neg) for e in range(N_EXPERTS)]

    def top1(vals):
        bv = neg
        bi = jnp.zeros(best.shape, I32)
        for e in range(N_EXPERTS):
            upd = vals[e] > bv
            bv = jnp.where(upd, vals[e], bv)
            bi = jnp.where(upd, e, bi)
        return bi

    i1 = top1(masked)
    i2 = top1([jnp.where(i1 == e, neg, masked[e]) for e in range(N_EXPERTS)])
    return gidx, i1, i2


def _pair_class(gidx, i1, i2):
    a = jnp.minimum(i1, i2) - gidx * EXPERTS_PER_GROUP
    b = jnp.maximum(i1, i2) - gidx * EXPERTS_PER_GROUP
    base = jnp.where(a == 0, 0, jnp.where(a == 1, 2, 3))
    return gidx * PAIRS_PER_GROUP + base + b - 1


def _prefetch_expert_rows(i, nt, scur_ref, snext_ref, y_hbm, ybuf, gsem, before_wait=None):
    slot = i % 2

    def gather(slot_ref, to_slot, t, t_sublane):
        src = _row(y_hbm, pl.multiple_of(slot_ref[0, 0, t], ROW_TILE))
        return pltpu.make_async_copy(src, _row(ybuf.at[to_slot], t_sublane), gsem.at[to_slot])

    @pl.when(i == 0)
    def _():
        for t in range(TS):
            gather(scur_ref, 0, t, t * ROW_TILE).start(priority=t % DMA_QUEUES)

    @pl.when(i + 1 < nt)
    def _():
        for t in range(TS):
            gather(snext_ref, 1 - slot, t, t * ROW_TILE).start(priority=t % DMA_QUEUES)

    if before_wait is not None:
        before_wait()
    pltpu.make_async_copy(y_hbm.at[pl.ds(0, TS * ROW_TILE)], ybuf.at[slot], gsem.at[slot]).wait()
    return _load_row_tiled(ybuf.at[slot], TS)


def _layer_kernel(*refs, layer, tiles_per_seq):
    if layer == 0:
        x_ref, gin_ref, bin_ref = refs[:3]
        rest = refs[3:]
    else:
        scur_ref, snext_ref, h1p_ref, y_hbm, g2_ref, b2_ref = refs[:6]
        rest = refs[6:]
    (win_hbm, pw_ref, ps_ref, cw_ref, cb_ref, wout_hbm, g_ref, b_ref, wr_ref, rb_ref, tri_ref,
     h1t_ref, route_ref, cnt_ref,
     vbuf, pwork, zbuf, mixbuf, carry, win_ref, wout_ref, stage, wsem, ybuf, gsem) = rest

    i = pl.program_id(0)
    nt = pl.num_programs(0)

    def stage_weights():
        chunks = []
        for src, dst, width in ((win_hbm, win_ref, D_PROJ), (wout_hbm, wout_ref, D_MODEL)):
            for c in range(D_MODEL // W_STAGE_ROWS):
                chunks.append((src, dst, width, pl.ds(c * W_STAGE_ROWS, W_STAGE_ROWS)))

        def fetch(k):
            src, _, width, rows = chunks[k]
            return pltpu.make_async_copy(src.at[layer, rows], stage.at[k % 2, :, pl.ds(0, width)],
                                         wsem.at[k % 2])

        fetch(0).start()
        for k, (_, dst, width, rows) in enumerate(chunks):
            if k + 1 < len(chunks):
                fetch(k + 1).start()
            fetch(k).wait()
            dst[rows, :] = stage[k % 2, :, 0:width].astype(BF16)
        carry[...] = jnp.zeros(carry.shape, F32)

    first_step_setup = functools.partial(pl.when(i == 0), stage_weights)
    if layer == 0:
        first_step_setup()
        h = _layer_norm(x_ref[...], gin_ref[...], bin_ref[...])
    else:
        y = _prefetch_expert_rows(i, nt, scur_ref, snext_ref, y_hbm, ybuf, gsem,
                                  before_wait=first_step_setup)
        h = _layer_norm(ALPHA * _load_row_tiled(h1p_ref, TS) + y, g2_ref[0], b2_ref[0])

    si = i % tiles_per_seq

    @pl.when(si == 0)
    def _():
        vbuf[0:V_HALO, :] = jnp.zeros((V_HALO, D_POOL), F32)
        zbuf[0:Z_HALO, :] = jnp.zeros((Z_HALO, D_CONV), F32)

    hb = h.astype(BF16)
    vbuf[V_HALO:V_HALO + TS, :] = _dot(hb, win_ref[:, 0:D_POOL])
    gate_b = _dot(hb, win_ref[:, D_POOL:D_POOL + D_CONV])
    gate_c = _dot(hb, win_ref[:, D_POOL + D_CONV:D_POOL + 2 * D_CONV])
    u = _dot(hb, win_ref[:, D_POOL + 2 * D_CONV:D_PROJ])
    z = gate_c * u
    zbuf[Z_HALO:Z_HALO + TS, :] = z

    src = vbuf
    for level in range(1, len(POOL_WINDOWS)):
        dst = pwork.at[(level - 1) % 2]
        lo = SUBLANES * level
        shift = 2 ** (level - 1)
        n = TS + V_HALO - lo
        cols = pl.ds(level * POOL_GROUP, D_POOL - level * POOL_GROUP)
        dst[pl.ds(lo, n), cols] = src[pl.ds(lo, n), cols] + src[pl.ds(lo - shift, n), cols]
        src = dst
    tpos = si * TS + lax.broadcasted_iota(I32, (TS, 1), 0)
    for g, win in enumerate(POOL_WINDOWS):
        cols = pl.ds(g * POOL_GROUP, POOL_GROUP)
        cur = vbuf[pl.ds(V_HALO, TS), cols]
        half = vbuf if g == 0 else pwork.at[(g - 1) % 2]
        acc = half[pl.ds(V_HALO, TS), cols] + half[pl.ds(V_HALO - win // 2, TS), cols]
        denom = jnp.minimum(tpos + 1, win).astype(F32)
        pooled = acc / denom - cur
        mixed = _dot(pooled.astype(BF16), pw_ref[0, g].astype(BF16)) * ps_ref[0, :, cols]
        mixbuf[:, cols] = mixed.astype(BF16)

    yc = (cw_ref[0, 2:3, :] * z
          + cw_ref[0, 1:2, :] * zbuf[pl.ds(Z_HALO - 1, TS), :]
          + cw_ref[0, 0:1, :] * zbuf[pl.ds(Z_HALO - 2, TS), :])
    mixbuf[:, D_POOL:D_POOL + D_CONV] = (gate_b * (yc + cb_ref[0])).astype(BF16)

    vbuf[0:V_HALO, :] = vbuf[TS:TS + V_HALO, :]
    zbuf[0:Z_HALO, :] = zbuf[TS:TS + Z_HALO, :]

    mix = _dot(mixbuf[...], wout_ref[...])
    h1 = _layer_norm(ALPHA * h + mix, g_ref[0], b_ref[0])
    _store_row_tiled(h1t_ref, h1)

    logits = lax.dot_general(wr_ref[...].astype(BF16), h1.astype(BF16), (((1,), (1,)), ((), ())),
                             preferred_element_type=F32)
    mx = jnp.max(logits, axis=0, keepdims=True)
    ex = jnp.exp(logits - mx)
    probs = ex / jnp.sum(ex, axis=0, keepdims=True)
    gidx, i1, i2 = _route(probs + rb_ref[...])
    cls = _pair_class(gidx, i1, i2)

    ciota = lax.broadcasted_iota(I32, (CLASS_ROWS, TS), 0)
    onehot = jnp.where(ciota == cls, 1.0, 0.0).astype(F32)
    before = _dot(onehot.astype(BF16), tri_ref[...])
    total = jnp.sum(onehot, axis=1, keepdims=True)
    c = carry[:, 0:1]
    rank = jnp.sum(onehot * (c + before), axis=0, keepdims=True)
    route_ref[0:1, :] = cls
    route_ref[1:2, :] = rank.astype(I32)
    newc = jnp.broadcast_to(c + total, carry.shape)
    carry[...] = newc
    cnt_ref[...] = newc


def _layer_call(layer, bsz, seq, prev, w_in, pool_w, pool_scale, conv_w, conv_b, w_out, ln_g, ln_b,
                wr_t, rbias, tri):
    t = bsz * seq
    nt = t // TS
    const2 = lambda i: (0, 0)
    lsel3 = lambda i: (layer, 0, 0)
    lsel4 = lambda i: (layer, 0, 0, 0)
    tile = lambda i: (i, 0)
    lane_tile = lambda i: (0, i)
    if layer == 0:
        x, gin, bin_ = prev
        head_args = (x, gin, bin_)
        head_specs = [
            pl.BlockSpec((TS, D_MODEL), tile),
            pl.BlockSpec((1, D_MODEL), const2),
            pl.BlockSpec((1, D_MODEL), const2),
        ]
    else:
        slot_prev, h1t_prev, y_prev, g2, b2 = prev
        head_args = (slot_prev, slot_prev, h1t_prev, y_prev, g2, b2)
        head_specs = [
            pl.BlockSpec((1, 1, TS), lambda i: (i, 0, 0), memory_space=pltpu.SMEM),
            pl.BlockSpec((1, 1, TS), lambda i: (jnp.minimum(i + 1, nt - 1), 0, 0),
                         memory_space=pltpu.SMEM),
            pl.BlockSpec((TS * ROW_TILE, LANES), tile),
            pl.BlockSpec(memory_space=pl.ANY),
            pl.BlockSpec((1, 1, D_MODEL), lambda i: (layer - 1, 0, 0)),
            pl.BlockSpec((1, 1, D_MODEL), lambda i: (layer - 1, 0, 0)),
        ]
    return pl.pallas_call(
        functools.partial(_layer_kernel, layer=layer, tiles_per_seq=seq // TS),
        grid=(nt,),
        in_specs=head_specs + [
            pl.BlockSpec(memory_space=pl.ANY),
            pl.BlockSpec((1, len(POOL_WINDOWS), POOL_GROUP, POOL_GROUP), lsel4),
            pl.BlockSpec((1, 1, D_POOL), lsel3),
            pl.BlockSpec((1, CONV_WIDTH, D_CONV), lsel3),
            pl.BlockSpec((1, 1, D_CONV), lsel3),
            pl.BlockSpec(memory_space=pl.ANY),
            pl.BlockSpec((1, 1, D_MODEL), lsel3),
            pl.BlockSpec((1, 1, D_MODEL), lsel3),
            pl.BlockSpec((N_EXPERTS, D_MODEL), const2),
            pl.BlockSpec((N_EXPERTS, 1), const2),
            pl.BlockSpec((TS, TS), const2),
        ],
        out_specs=[
            pl.BlockSpec((TS * ROW_TILE, LANES), tile),
            pl.BlockSpec((2, TS), lane_tile),
            pl.BlockSpec((CLASS_ROWS, LANES), const2),
        ],
        out_shape=[
            jax.ShapeDtypeStruct((t * ROW_TILE, LANES), F32),
            jax.ShapeDtypeStruct((2, t), I32),
            jax.ShapeDtypeStruct((CLASS_ROWS, LANES), F32),
        ],
        scratch_shapes=[
            pltpu.VMEM((TS + V_HALO, D_POOL), F32),
            pltpu.VMEM((2, TS + V_HALO, D_POOL), F32),
            pltpu.VMEM((TS + Z_HALO, D_CONV), F32),
            pltpu.VMEM((TS, D_MODEL), BF16),
            pltpu.VMEM((CLASS_ROWS, LANES), F32),
            pltpu.VMEM((D_MODEL, D_PROJ), BF16),
            pltpu.VMEM((D_MODEL, D_MODEL), BF16),
            pltpu.VMEM((2, W_STAGE_ROWS, D_PROJ), F32),
            pltpu.SemaphoreType.DMA((2,)),
            pltpu.VMEM((2, TS * ROW_TILE, LANES), F32),
            pltpu.SemaphoreType.DMA((2,)),
        ],
        compiler_params=pltpu.CompilerParams(
            dimension_semantics=("arbitrary",), vmem_limit_bytes=VMEM_LIMIT),
        name="layer_mix_route",
    )(*head_args, w_in, pool_w, pool_scale, conv_w, conv_b, w_out, ln_g, ln_b, wr_t, rbias, tri)


TABLE_EA, TABLE_EB, TABLE_VALID, TABLE_NUSED = 0, 1, 2, 3
TABLE_TAIL_START, TABLE_TAIL_LEN = 4, 5


def _dispatch_kernel(cnt_ref, route_ref, slot_ref, table_ref):
    log_bm = BM.bit_length() - 1
    starts, ends, fills = [], [], []
    end = jnp.zeros((1, LANES), I32)
    for k in range(N_CLASSES):
        count = cnt_ref[k:k + 1, :].astype(I32)
        starts.append(end)
        fills.append(end + count)
        end = end + lax.shift_left(lax.shift_right_logical(count + (BM - 1), log_bm), log_bm)
        ends.append(end)
    nused = lax.shift_right_logical(end, log_bm)

    cls = route_ref[0:1, :]
    start = jnp.zeros(cls.shape, I32)
    for k in range(N_CLASSES):
        start = jnp.where(cls == k, starts[k][:, 0:1], start)
    slots = (start + route_ref[1:2, :]) * ROW_TILE
    for j in range(slot_ref.shape[0]):
        slot_ref[j] = slots[:, j * TS:(j + 1) * TS]

    step = jnp.minimum(lax.broadcasted_iota(I32, (1, LANES), 1), nused - 1)
    first_row = step * BM
    c = jnp.zeros((1, LANES), I32)
    fill = fills[0]
    for k in range(N_CLASSES - 1):
        beyond = first_row >= ends[k]
        c = c + beyond.astype(I32)
        fill = jnp.where(beyond, fills[k + 1], fill)
    g = sum((c >= k * PAIRS_PER_GROUP).astype(I32) for k in range(1, N_EXPERT_GROUPS))
    p = c - g * PAIRS_PER_GROUP
    ge3 = (p >= 3).astype(I32)
    ge5 = (p >= 5).astype(I32)
    table_ref[TABLE_EA:TABLE_EA + 1, :] = g * EXPERTS_PER_GROUP + ge3 + ge5
    table_ref[TABLE_EB:TABLE_EB + 1, :] = g * EXPERTS_PER_GROUP + p + 1 - 2 * ge3 - ge5
    table_ref[TABLE_VALID:TABLE_VALID + 1, :] = jnp.clip(fill - first_row, 0, BM)
    table_ref[TABLE_NUSED:TABLE_NUSED + 1, :] = nused

    lane = lax.broadcasted_iota(I32, (1, LANES), 1)
    tail_start = jnp.zeros((1, LANES), I32)
    tail_len = jnp.zeros((1, LANES), I32)
    for k in range(N_CLASSES):
        tail_start = jnp.where(lane == k, fills[k], tail_start)
        tail_len = jnp.where(lane == k, ends[k] - fills[k], tail_len)
    table_ref[TABLE_TAIL_START:TABLE_TAIL_START + 1, :] = tail_start
    table_ref[TABLE_TAIL_LEN:TABLE_TAIL_LEN + 1, :] = tail_len
    table_ref[TABLE_TAIL_LEN + 1:, :] = jnp.zeros((SUBLANES - TABLE_TAIL_LEN - 1, LANES), I32)


def _dispatch_plan(cnt, route):
    t = route.shape[1]
    return pl.pallas_call(
        _dispatch_kernel,
        grid=(1,),
        in_specs=[pl.BlockSpec((CLASS_ROWS, LANES), lambda i: (0, 0)),
                  pl.BlockSpec((2, t), lambda i: (0, 0))],
        out_specs=[pl.BlockSpec((t // TS, 1, TS), lambda i: (0, 0, 0)),
                   pl.BlockSpec((SUBLANES, LANES), lambda i: (0, 0))],
        out_shape=[jax.ShapeDtypeStruct((t // TS, 1, TS), I32),
                   jax.ShapeDtypeStruct((SUBLANES, LANES), I32)],
        name="dispatch_plan",
    )(cnt, route)


def _scatter_kernel(slot_ref, h1t_ref, xin_ref, xout_ref, sem):
    del xin_ref
    for t in range(TS):
        dst = _row(xout_ref, pl.multiple_of(slot_ref[0, 0, t], ROW_TILE))
        pltpu.make_async_copy(_row(h1t_ref, t * ROW_TILE), dst, sem).start(priority=t % DMA_QUEUES)
    pltpu.make_async_copy(h1t_ref, xout_ref.at[pl.ds(0, TS * ROW_TILE)], sem).wait()


def _first_scatter_kernel(plan_ref, slot_ref, h1t_ref, xout_ref, zeros, sem, zsem):
    i = pl.program_id(0)
    block = BM * ROW_TILE
    n_blocks = xout_ref.shape[0] // block

    def fill_rows(visit):
        for k in range(N_CLASSES):
            length = plan_ref[TABLE_TAIL_LEN, k]
            row = plan_ref[TABLE_TAIL_START, k]
            piece = BM // 2
            while piece >= 1:
                take = (length & piece) != 0
                dst = xout_ref.at[pl.ds(pl.multiple_of(row * ROW_TILE, ROW_TILE), piece * ROW_TILE)]
                cp = pltpu.make_async_copy(zeros.at[pl.ds(0, piece * ROW_TILE)], dst, zsem)
                pl.when(take)(functools.partial(visit, cp))
                row = row + jnp.where(take, piece, 0)
                piece //= 2

        def idle_block(j, carry):
            dst = xout_ref.at[pl.ds(pl.multiple_of(j * block, block), block)]
            visit(pltpu.make_async_copy(zeros, dst, zsem))
            return carry
        lax.fori_loop(plan_ref[TABLE_NUSED, 0], n_blocks, idle_block, 0)

    for t in range(TS):
        dst = _row(xout_ref, pl.multiple_of(slot_ref[0, 0, t], ROW_TILE))
        pltpu.make_async_copy(_row(h1t_ref, t * ROW_TILE), dst, sem).start(priority=t % DMA_QUEUES)

    @pl.when(i == 0)
    def _():
        zeros[...] = jnp.zeros(zeros.shape, F32)
        fill_rows(lambda cp: cp.start())

    pltpu.make_async_copy(h1t_ref, xout_ref.at[pl.ds(0, TS * ROW_TILE)], sem).wait()

    @pl.when(i == pl.num_programs(0) - 1)
    def _():
        fill_rows(lambda cp: cp.wait())


def _first_scatter_rows(plan, slot3, h1t, n_rows):
    nt = slot3.shape[0]
    grid_spec = pltpu.PrefetchScalarGridSpec(
        num_scalar_prefetch=1,
        grid=(nt,),
        in_specs=[
            pl.BlockSpec((1, 1, TS), lambda i, *_: (i, 0, 0), memory_space=pltpu.SMEM),
            pl.BlockSpec((TS * ROW_TILE, LANES), lambda i, *_: (i, 0)),
        ],
        out_specs=pl.BlockSpec(memory_space=pl.ANY),
        scratch_shapes=[pltpu.VMEM((BM * ROW_TILE, LANES), F32), pltpu.SemaphoreType.DMA(()),
                        pltpu.SemaphoreType.DMA(())],
    )
    return pl.pallas_call(
        _first_scatter_kernel,
        grid_spec=grid_spec,
        out_shape=jax.ShapeDtypeStruct((n_rows * ROW_TILE, LANES), F32),
        compiler_params=pltpu.CompilerParams(dimension_semantics=("arbitrary",)),
        name="first_scatter_rows",
    )(plan, slot3, h1t)


def _scatter_rows(slot3, h1t, xbuf):
    nt = slot3.shape[0]
    return pl.pallas_call(
        _scatter_kernel,
        grid=(nt,),
        in_specs=[
            pl.BlockSpec((1, 1, TS), lambda i: (i, 0, 0), memory_space=pltpu.SMEM),
            pl.BlockSpec((TS * ROW_TILE, LANES), lambda i: (i, 0)),
            pl.BlockSpec(memory_space=pl.ANY),
        ],
        out_specs=pl.BlockSpec(memory_space=pl.ANY),
        out_shape=jax.ShapeDtypeStruct(xbuf.shape, xbuf.dtype),
        scratch_shapes=[pltpu.SemaphoreType.DMA(())],
        input_output_aliases={2: 0},
        compiler_params=pltpu.CompilerParams(dimension_semantics=("arbitrary",)),
        name="scatter_rows",
    )(slot3, h1t, xbuf)


def _expert_kernel(plan_ref, x_ref, w1_hbm, w3_hbm, w2_hbm, wr_ref, y_ref,
                   wup, w2a, w2b, s1a, s3a, s2a, s1b, s3b, s2b, wsem, *, layer):
    i = pl.program_id(0)
    nb = pl.num_programs(0)
    nu = plan_ref[TABLE_NUSED, 0]
    used = i < nu
    ea = plan_ref[TABLE_EA, i]
    eb = plan_ref[TABLE_EB, i]

    def swap_weights(table_row, slot, live, staged):
        e_ref = plan_ref.at[table_row]
        e = e_ref[i]

        def fetch(expert):
            return [pltpu.make_async_copy(src.at[layer, expert], dst, wsem.at[slot])
                    for src, dst in zip((w1_hbm, w3_hbm, w2_hbm), staged)]

        @pl.when(used & (i == 0))
        def _():
            for cp in fetch(e):
                cp.start(priority=WEIGHT_DMA_QUEUE)

        @pl.when(used & ((i == 0) | (e != e_ref[jnp.maximum(i - 1, 0)])))
        def _():
            for cp in fetch(e):
                cp.wait()
            for dst, src in zip(live, staged):
                dst[...] = src[...].astype(BF16)
            nxt = lax.while_loop(lambda j: (j < nu) & (e_ref[jnp.minimum(j, nb - 1)] == e),
                                 lambda j: j + 1, i + 1)

            @pl.when(nxt < nu)
            def _():
                for cp in fetch(e_ref[jnp.minimum(nxt, nb - 1)]):
                    cp.start(priority=WEIGHT_DMA_QUEUE)

    def up_cols(k):
        return wup.at[:, pl.ds(k * D_EXPERT, D_EXPERT)]

    swap_weights(TABLE_EA, 0, (up_cols(0), up_cols(2), w2a), (s1a, s3a, s2a))
    swap_weights(TABLE_EB, 1, (up_cols(1), up_cols(3), w2b), (s1b, s3b, s2b))

    def ffn_rows(rows):
        sublanes = pl.ds(0, rows * ROW_TILE)
        xb = _load_row_tiled(x_ref.at[sublanes], rows).astype(BF16)

        logits = _dot(xb, wr_ref[...].astype(BF16))
        lane = lax.broadcasted_iota(I32, logits.shape, 1)
        logits = jnp.where(lane < N_EXPERTS, logits, -jnp.inf)
        ex = jnp.exp(logits - jnp.max(logits, axis=1, keepdims=True))
        probs = ex / jnp.sum(ex, axis=1, keepdims=True)
        pa = jnp.sum(jnp.where(lane == ea, probs, 0.0), axis=1, keepdims=True)
        pb = jnp.sum(jnp.where(lane == eb, probs, 0.0), axis=1, keepdims=True)
        den = pa + pb

        up = _dot(xb, wup[...])
        a = up[:, 0:2 * D_EXPERT]
        hid = (a * jax.nn.sigmoid(a) * up[:, 2 * D_EXPERT:4 * D_EXPERT]).astype(BF16)
        y = ((pa / den) * _dot(hid[:, 0:D_EXPERT], w2a[...])
             + (pb / den) * _dot(hid[:, D_EXPERT:2 * D_EXPERT], w2b[...]))
        _store_row_tiled(y_ref.at[sublanes], y)

    few = plan_ref[TABLE_VALID, i] <= BM // 2

    @pl.when(used & jnp.logical_not(few))
    def _():
        ffn_rows(BM)

    @pl.when(used & few)
    def _():
        ffn_rows(BM // 2)
        rest = pl.ds(BM // 2 * ROW_TILE, BM // 2 * ROW_TILE)
        y_ref[rest, :] = jnp.zeros((BM // 2 * ROW_TILE, LANES), F32)


def _expert_ffn(layer, plan, xbuf, w1, w3, w2, wr_pad):
    n_blocks = xbuf.shape[0] // (BM * ROW_TILE)
    used_blk = lambda i, plan_ref: (jnp.minimum(i, plan_ref[TABLE_NUSED, 0] - 1), 0)
    up = (D_MODEL, D_EXPERT)
    down = (D_EXPERT, D_MODEL)
    expert_slot = lambda dtype: [pltpu.VMEM(up, dtype), pltpu.VMEM(up, dtype), pltpu.VMEM(down, dtype)]
    grid_spec = pltpu.PrefetchScalarGridSpec(
        num_scalar_prefetch=1,
        grid=(n_blocks,),
        in_specs=[
            pl.BlockSpec((BM * ROW_TILE, LANES), used_blk),
            pl.BlockSpec(memory_space=pl.ANY),
            pl.BlockSpec(memory_space=pl.ANY),
            pl.BlockSpec(memory_space=pl.ANY),
            pl.BlockSpec((D_MODEL, LANES), lambda i, plan_ref: (0, 0)),
        ],
        out_specs=pl.BlockSpec((BM * ROW_TILE, LANES), used_blk),
        scratch_shapes=([pltpu.VMEM((D_MODEL, 4 * D_EXPERT), BF16), pltpu.VMEM(down, BF16),
                         pltpu.VMEM(down, BF16)] + expert_slot(F32) + expert_slot(F32)
                        + [pltpu.SemaphoreType.DMA((2,))]),
    )
    return pl.pallas_call(
        functools.partial(_expert_kernel, layer=layer),
        grid_spec=grid_spec,
        out_shape=jax.ShapeDtypeStruct(xbuf.shape, F32),
        input_output_aliases={1: 0},
        compiler_params=pltpu.CompilerParams(
            dimension_semantics=("arbitrary",), vmem_limit_bytes=VMEM_LIMIT),
        name="expert_ffn",
    )(plan, xbuf, w1, w3, w2, wr_pad)


def _final_kernel(scur_ref, snext_ref, h1p_ref, y_hbm, g2_ref, b2_ref, o_ref, ybuf, gsem):
    i = pl.program_id(0)
    y = _prefetch_expert_rows(i, pl.num_programs(0), scur_ref, snext_ref, y_hbm, ybuf, gsem)
    o_ref[...] = _layer_norm(ALPHA * _load_row_tiled(h1p_ref, TS) + y, g2_ref[0], b2_ref[0])


def _final_call(slot_prev, h1t_prev, y_prev, g2, b2):
    t = h1t_prev.shape[0] // ROW_TILE
    nt = t // TS
    last = DEPTH - 1
    return pl.pallas_call(
        _final_kernel,
        grid=(nt,),
        in_specs=[
            pl.BlockSpec((1, 1, TS), lambda i: (i, 0, 0), memory_space=pltpu.SMEM),
            pl.BlockSpec((1, 1, TS), lambda i: (jnp.minimum(i + 1, nt - 1), 0, 0),
                         memory_space=pltpu.SMEM),
            pl.BlockSpec((TS * ROW_TILE, LANES), lambda i: (i, 0)),
            pl.BlockSpec(memory_space=pl.ANY),
            pl.BlockSpec((1, 1, D_MODEL), lambda i: (last, 0, 0)),
            pl.BlockSpec((1, 1, D_MODEL), lambda i: (last, 0, 0)),
        ],
        out_specs=pl.BlockSpec((TS, D_MODEL), lambda i: (i, 0)),
        out_shape=jax.ShapeDtypeStruct((t, D_MODEL), F32),
        scratch_shapes=[pltpu.VMEM((2, TS * ROW_TILE, LANES), F32), pltpu.SemaphoreType.DMA((2,))],
        compiler_params=pltpu.CompilerParams(
            dimension_semantics=("arbitrary",), vmem_limit_bytes=VMEM_LIMIT),
        name="final_combine_ln",
    )(slot_prev, slot_prev, h1t_prev, y_prev, g2, b2)


def kernel(x, ln_in_g, ln_in_b, w_in, pool_w, pool_scale, conv_w, conv_b, w_out, ln1_g, ln1_b,
           w_router, router_bias, exp_w1, exp_w3, exp_w2, ln2_g, ln2_b):
    bsz, seq, d = x.shape
    t = bsz * seq
    nt = t // TS
    n_rows = t + N_CLASSES * BM
    nb = n_rows // BM
    assert nb <= LANES

    per_layer_row = lambda v: v.reshape(DEPTH, 1, -1)
    tri = (lax.broadcasted_iota(I32, (TS, TS), 0)
           < lax.broadcasted_iota(I32, (TS, TS), 1)).astype(BF16)
    wr_t = w_router.T
    wr_pad = jnp.pad(w_router, ((0, 0), (0, LANES - N_EXPERTS)))
    rbias = router_bias.reshape(N_EXPERTS, 1).astype(F32)
    pool_scale3, conv_b3 = per_layer_row(pool_scale), per_layer_row(conv_b)
    ln1_g3, ln1_b3 = per_layer_row(ln1_g), per_layer_row(ln1_b)
    ln2_g3, ln2_b3 = per_layer_row(ln2_g), per_layer_row(ln2_b)

    prev = (x.reshape(t, d), ln_in_g.reshape(1, -1), ln_in_b.reshape(1, -1))
    xbuf = None
    for l in range(DEPTH):
        h1t, route, cnt = _layer_call(l, bsz, seq, prev, w_in, pool_w, pool_scale3, conv_w, conv_b3,
                                          w_out, ln1_g3, ln1_b3, wr_t, rbias, tri)
        slot3, plan = _dispatch_plan(cnt, route)
        if xbuf is None:
            xbuf = _first_scatter_rows(plan, slot3, h1t, n_rows)
        else:
            xbuf = _scatter_rows(slot3, h1t, xbuf)
        xbuf = _expert_ffn(l, plan, xbuf, exp_w1, exp_w3, exp_w2, wr_pad)
        prev = (slot3, h1t, xbuf, ln2_g3, ln2_b3)
    out = _final_call(*prev)
    return out.reshape(bsz, seq, d)
```

```python
import functools

import jax
import jax.numpy as jnp
from jax import lax
from jax.experimental import pallas as pl
from jax.experimental.pallas import tpu as pltpu

D_MODEL = 1024
DEPTH = 4
D_POOL = 512
POOL_WINDOWS = (2, 4, 8, 16)
POOL_GROUP = 128
D_CONV = 512
CONV_WIDTH = 3
D_PROJ = D_POOL + 3 * D_CONV
N_EXPERTS = 16
N_EXPERT_GROUPS = 4
EXPERTS_PER_GROUP = 4
PAIRS_PER_GROUP = 6
N_CLASSES = N_EXPERT_GROUPS * PAIRS_PER_GROUP
CLASS_ROWS = 32
D_EXPERT = 512
ALPHA = float((2 * DEPTH) ** 0.25)
LN_EPS = 1e-5

F32 = jnp.float32
BF16 = jnp.bfloat16
I32 = jnp.int32

LANES = 128
ROW_TILE = D_MODEL // LANES
TS = 512
SUBLANES = 8
V_HALO = 32
Z_HALO = 8
BM = 256
W_STAGE_ROWS = 256
DMA_QUEUES = 2
WEIGHT_DMA_QUEUE = 1
VMEM_LIMIT = 56 * 1024 * 1024


def _layer_norm(x, g, b):
    mu = jnp.mean(x, axis=-1, keepdims=True)
    xc = x - mu
    var = jnp.mean(xc * xc, axis=-1, keepdims=True)
    return xc * lax.rsqrt(var + LN_EPS) * g + b


def _dot(a, b):
    return jnp.dot(a, b, preferred_element_type=F32)


def _store_row_tiled(ref, value):
    n = value.shape[0]
    for j in range(ROW_TILE):
        ref[pl.ds(j, n, stride=ROW_TILE), :] = value[:, j * LANES:(j + 1) * LANES]


def _load_row_tiled(ref, n):
    return jnp.concatenate([ref[pl.ds(j, n, stride=ROW_TILE), :] for j in range(ROW_TILE)], axis=1)


def _row(ref, first_sublane):
    return ref.at[pl.ds(first_sublane, ROW_TILE)]


def _route(sel):
    rows = [sel[e:e + 1, :] for e in range(N_EXPERTS)]
    gscore = []
    for g in range(N_EXPERT_GROUPS):
        a, b, c, d = rows[4 * g:4 * g + 4]
        hi1, lo1 = jnp.maximum(a, b), jnp.minimum(a, b)
        hi2, lo2 = jnp.maximum(c, d), jnp.minimum(c, d)
        m1 = jnp.maximum(hi1, hi2)
        m2 = jnp.maximum(jnp.minimum(hi1, hi2), jnp.maximum(lo1, lo2))
        gscore.append(m1 + m2)
    best = gscore[0]
    gidx = jnp.zeros(best.shape, I32)
    for g in range(1, N_EXPERT_GROUPS):
        upd = gscore[g] > best
        best = jnp.where(upd, gscore[g], best)
        gidx = jnp.where(upd, g, gidx)
    neg = jnp.full(best.shape, -jnp.inf, F32)
    masked = [jnp.where(gidx == (e // EXPERTS_PER_GROUP), rows[e], neg) for e in range(N_EXPERTS)]

    def top1(vals):
        bv = neg
        bi = jnp.zeros(best.shape, I32)
        for e in range(N_EXPERTS):
            upd = vals[e] > bv
            bv = jnp.where(upd, vals[e], bv)
            bi = jnp.where(upd, e, bi)
        return bi

    i1 = top1(masked)
    i2 = top1([jnp.where(i1 == e, neg, masked[e]) for e in range(N_EXPERTS)])
    return gidx, i1, i2


def _pair_class(gidx, i1, i2):
    a = jnp.minimum(i1, i2) - gidx * EXPERTS_PER_GROUP
    b = jnp.maximum(i1, i2) - gidx * EXPERTS_PER_GROUP
    base = jnp.where(a == 0, 0, jnp.where(a == 1, 2, 3))
    return gidx * PAIRS_PER_GROUP + base + b - 1


def _prefetch_expert_rows(i, nt, scur_ref, snext_ref, y_hbm, ybuf, gsem, before_wait=None):
    slot = i % 2

    def gather(slot_ref, to_slot, t, t_sublane):
        src = _row(y_hbm, pl.multiple_of(slot_ref[0, 0, t], ROW_TILE))
        return pltpu.make_async_copy(src, _row(ybuf.at[to_slot], t_sublane), gsem.at[to_slot])

    @pl.when(i == 0)
    def _():
        for t in range(TS):
            gather(scur_ref, 0, t, t * ROW_TILE).start(priority=t % DMA_QUEUES)

    @pl.when(i + 1 < nt)
    def _():
        for t in range(TS):
            gather(snext_ref, 1 - slot, t, t * ROW_TILE).start(priority=t % DMA_QUEUES)

    if before_wait is not None:
        before_wait()
    pltpu.make_async_copy(y_hbm.at[pl.ds(0, TS * ROW_TILE)], ybuf.at[slot], gsem.at[slot]).wait()
    return _load_row_tiled(ybuf.at[slot], TS)


def _layer_kernel(*refs, layer, tiles_per_seq):
    if layer == 0:
        x_ref, gin_ref, bin_ref = refs[:3]
        rest = refs[3:]
    else:
        scur_ref, snext_ref, h1p_ref, y_hbm, g2_ref, b2_ref = refs[:6]
        rest = refs[6:]
    (win_hbm, pw_ref, ps_ref, cw_ref, cb_ref, wout_hbm, g_ref, b_ref, wr_ref, rb_ref, tri_ref,
     h1t_ref, route_ref, cnt_ref,
     vbuf, pwork, zbuf, mixbuf, carry, win_ref, wout_ref, stage, wsem, ybuf, gsem) = rest

    i = pl.program_id(0)
    nt = pl.num_programs(0)

    def stage_weights():
        chunks = []
        for src, dst, width in ((win_hbm, win_ref, D_PROJ), (wout_hbm, wout_ref, D_MODEL)):
            for c in range(D_MODEL // W_STAGE_ROWS):
                chunks.append((src, dst, width, pl.ds(c * W_STAGE_ROWS, W_STAGE_ROWS)))

        def fetch(k):
            src, _, width, rows = chunks[k]
            return pltpu.make_async_copy(src.at[layer, rows], stage.at[k % 2, :, pl.ds(0, width)],
                                         wsem.at[k % 2])

        fetch(0).start()
        for k, (_, dst, width, rows) in enumerate(chunks):
            if k + 1 < len(chunks):
                fetch(k + 1).start()
            fetch(k).wait()
            dst[rows, :] = stage[k % 2, :, 0:width].astype(BF16)
        carry[...] = jnp.zeros(carry.shape, F32)

    first_step_setup = functools.partial(pl.when(i == 0), stage_weights)
    if layer == 0:
        first_step_setup()
        h = _layer_norm(x_ref[...], gin_ref[...], bin_ref[...])
    else:
        y = _prefetch_expert_rows(i, nt, scur_ref, snext_ref, y_hbm, ybuf, gsem,
                                  before_wait=first_step_setup)
        h = _layer_norm(ALPHA * _load_row_tiled(h1p_ref, TS) + y,
                        g2_ref[layer - 1:layer, :], b2_ref[layer - 1:layer, :])

    si = i % tiles_per_seq

    @pl.when(si == 0)
    def _():
        vbuf[0:V_HALO, :] = jnp.zeros((V_HALO, D_POOL), F32)
        zbuf[0:Z_HALO, :] = jnp.zeros((Z_HALO, D_CONV), F32)

    hb = h.astype(BF16)
    vbuf[V_HALO:V_HALO + TS, :] = _dot(hb, win_ref[:, 0:D_POOL])
    gate_b = _dot(hb, win_ref[:, D_POOL:D_POOL + D_CONV])
    gate_c = _dot(hb, win_ref[:, D_POOL + D_CONV:D_POOL + 2 * D_CONV])
    u = _dot(hb, win_ref[:, D_POOL + 2 * D_CONV:D_PROJ])
    z = gate_c * u
    zbuf[Z_HALO:Z_HALO + TS, :] = z

    src = vbuf
    for level in range(1, len(POOL_WINDOWS)):
        dst = pwork.at[(level - 1) % 2]
        lo = SUBLANES * level
        shift = 2 ** (level - 1)
        n = TS + V_HALO - lo
        cols = pl.ds(level * POOL_GROUP, D_POOL - level * POOL_GROUP)
        dst[pl.ds(lo, n), cols] = src[pl.ds(lo, n), cols] + src[pl.ds(lo - shift, n), cols]
        src = dst
    tpos = si * TS + lax.broadcasted_iota(I32, (TS, 1), 0)
    for g, win in enumerate(POOL_WINDOWS):
        cols = pl.ds(g * POOL_GROUP, POOL_GROUP)
        cur = vbuf[pl.ds(V_HALO, TS), cols]
        half = vbuf if g == 0 else pwork.at[(g - 1) % 2]
        acc = half[pl.ds(V_HALO, TS), cols] + half[pl.ds(V_HALO - win // 2, TS), cols]
        denom = jnp.minimum(tpos + 1, win).astype(F32)
        pooled = acc / denom - cur
        mixed = _dot(pooled.astype(BF16), pw_ref[0, g].astype(BF16)) * ps_ref[layer:layer + 1, cols]
        mixbuf[:, cols] = mixed.astype(BF16)

    yc = (cw_ref[0, 2:3, :] * z
          + cw_ref[0, 1:2, :] * zbuf[pl.ds(Z_HALO - 1, TS), :]
          + cw_ref[0, 0:1, :] * zbuf[pl.ds(Z_HALO - 2, TS), :])
    mixbuf[:, D_POOL:D_POOL + D_CONV] = (gate_b * (yc + cb_ref[layer:layer + 1, :])).astype(BF16)

    vbuf[0:V_HALO, :] = vbuf[TS:TS + V_HALO, :]
    zbuf[0:Z_HALO, :] = zbuf[TS:TS + Z_HALO, :]

    mix = _dot(mixbuf[...], wout_ref[...])
    h1 = _layer_norm(ALPHA * h + mix, g_ref[layer:layer + 1, :], b_ref[layer:layer + 1, :])
    _store_row_tiled(h1t_ref, h1)

    logits = lax.dot_general(wr_ref[...].astype(BF16), h1.astype(BF16), (((1,), (1,)), ((), ())),
                             preferred_element_type=F32)
    mx = jnp.max(logits, axis=0, keepdims=True)
    ex = jnp.exp(logits - mx)
    probs = ex / jnp.sum(ex, axis=0, keepdims=True)
    gidx, i1, i2 = _route(probs + rb_ref[...])
    cls = _pair_class(gidx, i1, i2)

    ciota = lax.broadcasted_iota(I32, (CLASS_ROWS, TS), 0)
    onehot = jnp.where(ciota == cls, 1.0, 0.0).astype(F32)
    before = _dot(onehot.astype(BF16), tri_ref[...])
    total = jnp.sum(onehot, axis=1, keepdims=True)
    c = carry[:, 0:1]
    rank = jnp.sum(onehot * (c + before), axis=0, keepdims=True)
    route_ref[0:1, :] = cls
    route_ref[1:2, :] = rank.astype(I32)
    newc = jnp.broadcast_to(c + total, carry.shape)
    carry[...] = newc
    cnt_ref[...] = newc


def _layer_call(layer, bsz, seq, prev, w_in, pool_w, pool_scale, conv_w, conv_b, w_out, ln_g, ln_b,
                wr_t, rbias, tri):
    t = bsz * seq
    nt = t // TS
    const2 = lambda i: (0, 0)
    lsel3 = lambda i: (layer, 0, 0)
    lsel4 = lambda i: (layer, 0, 0, 0)
    tile = lambda i: (i, 0)
    lane_tile = lambda i: (0, i)
    if layer == 0:
        x, gin, bin_ = prev
        head_args = (x, gin, bin_)
        head_specs = [
            pl.BlockSpec((TS, D_MODEL), tile),
            pl.BlockSpec((1, D_MODEL), const2),
            pl.BlockSpec((1, D_MODEL), const2),
        ]
    else:
        slot_prev, h1t_prev, y_prev, g2, b2 = prev
        head_args = (slot_prev, slot_prev, h1t_prev, y_prev, g2, b2)
        head_specs = [
            pl.BlockSpec((1, 1, TS), lambda i: (i, 0, 0), memory_space=pltpu.SMEM),
            pl.BlockSpec((1, 1, TS), lambda i: (jnp.minimum(i + 1, nt - 1), 0, 0),
                         memory_space=pltpu.SMEM),
            pl.BlockSpec((TS * ROW_TILE, LANES), tile),
            pl.BlockSpec(memory_space=pl.ANY),
            pl.BlockSpec((DEPTH, D_MODEL), const2),
            pl.BlockSpec((DEPTH, D_MODEL), const2),
        ]
    return pl.pallas_call(
        functools.partial(_layer_kernel, layer=layer, tiles_per_seq=seq // TS),
        grid=(nt,),
        in_specs=head_specs + [
            pl.BlockSpec(memory_space=pl.ANY),
            pl.BlockSpec((1, len(POOL_WINDOWS), POOL_GROUP, POOL_GROUP), lsel4),
            pl.BlockSpec((DEPTH, D_POOL), const2),
            pl.BlockSpec((1, CONV_WIDTH, D_CONV), lsel3),
            pl.BlockSpec((DEPTH, D_CONV), const2),
            pl.BlockSpec(memory_space=pl.ANY),
            pl.BlockSpec((DEPTH, D_MODEL), const2),
            pl.BlockSpec((DEPTH, D_MODEL), const2),
            pl.BlockSpec((N_EXPERTS, D_MODEL), const2),
            pl.BlockSpec((N_EXPERTS, 1), const2),
            pl.BlockSpec((TS, TS), const2),
        ],
        out_specs=[
            pl.BlockSpec((TS * ROW_TILE, LANES), tile),
            pl.BlockSpec((2, TS), lane_tile),
            pl.BlockSpec((CLASS_ROWS, LANES), const2),
        ],
        out_shape=[
            jax.ShapeDtypeStruct((t * ROW_TILE, LANES), F32),
            jax.ShapeDtypeStruct((2, t), I32),
            jax.ShapeDtypeStruct((CLASS_ROWS, LANES), F32),
        ],
        scratch_shapes=[
            pltpu.VMEM((TS + V_HALO, D_POOL), F32),
            pltpu.VMEM((2, TS + V_HALO, D_POOL), F32),
            pltpu.VMEM((TS + Z_HALO, D_CONV), F32),
            pltpu.VMEM((TS, D_MODEL), BF16),
            pltpu.VMEM((CLASS_ROWS, LANES), F32),
            pltpu.VMEM((D_MODEL, D_PROJ), BF16),
            pltpu.VMEM((D_MODEL, D_MODEL), BF16),
            pltpu.VMEM((2, W_STAGE_ROWS, D_PROJ), F32),
            pltpu.SemaphoreType.DMA((2,)),
            pltpu.VMEM((2, TS * ROW_TILE, LANES), F32),
            pltpu.SemaphoreType.DMA((2,)),
        ],
        compiler_params=pltpu.CompilerParams(
            dimension_semantics=("arbitrary",), vmem_limit_bytes=VMEM_LIMIT),
        name="layer_mix_route",
    )(*head_args, w_in, pool_w, pool_scale, conv_w, conv_b, w_out, ln_g, ln_b, wr_t, rbias, tri)


TABLE_EA, TABLE_EB, TABLE_VALID, TABLE_NUSED = 0, 1, 2, 3
TABLE_TAIL_START, TABLE_TAIL_LEN = 4, 5


def _dispatch_kernel(cnt_ref, route_ref, slot_ref, table_ref):
    log_bm = BM.bit_length() - 1
    starts, ends, fills = [], [], []
    end = jnp.zeros((1, LANES), I32)
    for k in range(N_CLASSES):
        count = cnt_ref[k:k + 1, :].astype(I32)
        starts.append(end)
        fills.append(end + count)
        end = end + lax.shift_left(lax.shift_right_logical(count + (BM - 1), log_bm), log_bm)
        ends.append(end)
    nused = lax.shift_right_logical(end, log_bm)

    cls = route_ref[0:1, :]
    start = jnp.zeros(cls.shape, I32)
    for k in range(N_CLASSES):
        start = jnp.where(cls == k, starts[k][:, 0:1], start)
    slots = (start + route_ref[1:2, :]) * ROW_TILE
    for j in range(slot_ref.shape[0]):
        slot_ref[j] = slots[:, j * TS:(j + 1) * TS]

    step = jnp.minimum(lax.broadcasted_iota(I32, (1, LANES), 1), nused - 1)
    first_row = step * BM
    c = jnp.zeros((1, LANES), I32)
    fill = fills[0]
    for k in range(N_CLASSES - 1):
        beyond = first_row >= ends[k]
        c = c + beyond.astype(I32)
        fill = jnp.where(beyond, fills[k + 1], fill)
    g = sum((c >= k * PAIRS_PER_GROUP).astype(I32) for k in range(1, N_EXPERT_GROUPS))
    p = c - g * PAIRS_PER_GROUP
    ge3 = (p >= 3).astype(I32)
    ge5 = (p >= 5).astype(I32)
    table_ref[TABLE_EA:TABLE_EA + 1, :] = g * EXPERTS_PER_GROUP + ge3 + ge5
    table_ref[TABLE_EB:TABLE_EB + 1, :] = g * EXPERTS_PER_GROUP + p + 1 - 2 * ge3 - ge5
    table_ref[TABLE_VALID:TABLE_VALID + 1, :] = jnp.clip(fill - first_row, 0, BM)
    table_ref[TABLE_NUSED:TABLE_NUSED + 1, :] = nused

    lane = lax.broadcasted_iota(I32, (1, LANES), 1)
    tail_start = jnp.zeros((1, LANES), I32)
    tail_len = jnp.zeros((1, LANES), I32)
    for k in range(N_CLASSES):
        tail_start = jnp.where(lane == k, fills[k], tail_start)
        tail_len = jnp.where(lane == k, ends[k] - fills[k], tail_len)
    table_ref[TABLE_TAIL_START:TABLE_TAIL_START + 1, :] = tail_start
    table_ref[TABLE_TAIL_LEN:TABLE_TAIL_LEN + 1, :] = tail_len
    table_ref[TABLE_TAIL_LEN + 1:, :] = jnp.zeros((SUBLANES - TABLE_TAIL_LEN - 1, LANES), I32)


def _dispatch_plan(cnt, route):
    t = route.shape[1]
    return pl.pallas_call(
        _dispatch_kernel,
        grid=(1,),
        in_specs=[pl.BlockSpec((CLASS_ROWS, LANES), lambda i: (0, 0)),
                  pl.BlockSpec((2, t), lambda i: (0, 0))],
        out_specs=[pl.BlockSpec((t // TS, 1, TS), lambda i: (0, 0, 0)),
                   pl.BlockSpec((SUBLANES, LANES), lambda i: (0, 0))],
        out_shape=[jax.ShapeDtypeStruct((t // TS, 1, TS), I32),
                   jax.ShapeDtypeStruct((SUBLANES, LANES), I32)],
        name="dispatch_plan",
    )(cnt, route)


def _scatter_kernel(slot_ref, h1t_ref, xin_ref, xout_ref, sem):
    del xin_ref
    for t in range(TS):
        dst = _row(xout_ref, pl.multiple_of(slot_ref[0, 0, t], ROW_TILE))
        pltpu.make_async_copy(_row(h1t_ref, t * ROW_TILE), dst, sem).start(priority=t % DMA_QUEUES)
    pltpu.make_async_copy(h1t_ref, xout_ref.at[pl.ds(0, TS * ROW_TILE)], sem).wait()


def _first_scatter_kernel(plan_ref, slot_ref, h1t_ref, xout_ref, zeros, sem, zsem):
    i = pl.program_id(0)
    block = BM * ROW_TILE
    n_blocks = xout_ref.shape[0] // block

    def fill_rows(visit):
        for k in range(N_CLASSES):
            length = plan_ref[TABLE_TAIL_LEN, k]
            row = plan_ref[TABLE_TAIL_START, k]
            piece = BM // 2
            while piece >= 1:
                take = (length & piece) != 0
                dst = xout_ref.at[pl.ds(pl.multiple_of(row * ROW_TILE, ROW_TILE), piece * ROW_TILE)]
                cp = pltpu.make_async_copy(zeros.at[pl.ds(0, piece * ROW_TILE)], dst, zsem)
                pl.when(take)(functools.partial(visit, cp))
                row = row + jnp.where(take, piece, 0)
                piece //= 2

        def idle_block(j, carry):
            dst = xout_ref.at[pl.ds(pl.multiple_of(j * block, block), block)]
            visit(pltpu.make_async_copy(zeros, dst, zsem))
            return carry
        lax.fori_loop(plan_ref[TABLE_NUSED, 0], n_blocks, idle_block, 0)

    for t in range(TS):
        dst = _row(xout_ref, pl.multiple_of(slot_ref[0, 0, t], ROW_TILE))
        pltpu.make_async_copy(_row(h1t_ref, t * ROW_TILE), dst, sem).start(priority=t % DMA_QUEUES)

    @pl.when(i == 0)
    def _():
        zeros[...] = jnp.zeros(zeros.shape, F32)
        fill_rows(lambda cp: cp.start())

    pltpu.make_async_copy(h1t_ref, xout_ref.at[pl.ds(0, TS * ROW_TILE)], sem).wait()

    @pl.when(i == pl.num_programs(0) - 1)
    def _():
        fill_rows(lambda cp: cp.wait())


def _first_scatter_rows(plan, slot3, h1t, n_rows):
    nt = slot3.shape[0]
    grid_spec = pltpu.PrefetchScalarGridSpec(
        num_scalar_prefetch=1,
        grid=(nt,),
        in_specs=[
            pl.BlockSpec((1, 1, TS), lambda i, *_: (i, 0, 0), memory_space=pltpu.SMEM),
            pl.BlockSpec((TS * ROW_TILE, LANES), lambda i, *_: (i, 0)),
        ],
        out_specs=pl.BlockSpec(memory_space=pl.ANY),
        scratch_shapes=[pltpu.VMEM((BM * ROW_TILE, LANES), F32), pltpu.SemaphoreType.DMA(()),
                        pltpu.SemaphoreType.DMA(())],
    )
    return pl.pallas_call(
        _first_scatter_kernel,
        grid_spec=grid_spec,
        out_shape=jax.ShapeDtypeStruct((n_rows * ROW_TILE, LANES), F32),
        compiler_params=pltpu.CompilerParams(dimension_semantics=("arbitrary",)),
        name="first_scatter_rows",
    )(plan, slot3, h1t)


def _scatter_rows(slot3, h1t, xbuf):
    nt = slot3.shape[0]
    return pl.pallas_call(
        _scatter_kernel,
        grid=(nt,),
        in_specs=[
            pl.BlockSpec((1, 1, TS), lambda i: (i, 0, 0), memory_space=pltpu.SMEM),
            pl.BlockSpec((TS * ROW_TILE, LANES), lambda i: (i, 0)),
            pl.BlockSpec(memory_space=pl.ANY),
        ],
        out_specs=pl.BlockSpec(memory_space=pl.ANY),
        out_shape=jax.ShapeDtypeStruct(xbuf.shape, xbuf.dtype),
        scratch_shapes=[pltpu.SemaphoreType.DMA(())],
        input_output_aliases={2: 0},
        compiler_params=pltpu.CompilerParams(dimension_semantics=("arbitrary",)),
        name="scatter_rows",
    )(slot3, h1t, xbuf)


def _expert_kernel(plan_ref, x_ref, w1_hbm, w3_hbm, w2_hbm, wr_ref, y_ref,
                   wup, w2a, w2b, s1a, s3a, s2a, s1b, s3b, s2b, wsem, *, layer):
    i = pl.program_id(0)
    nb = pl.num_programs(0)
    nu = plan_ref[TABLE_NUSED, 0]
    used = i < nu
    ea = plan_ref[TABLE_EA, i]
    eb = plan_ref[TABLE_EB, i]

    def swap_weights(table_row, slot, live, staged):
        e_ref = plan_ref.at[table_row]
        e = e_ref[i]

        def fetch(expert):
            return [pltpu.make_async_copy(src.at[layer, expert], dst, wsem.at[slot])
                    for src, dst in zip((w1_hbm, w3_hbm, w2_hbm), staged)]

        @pl.when(used & (i == 0))
        def _():
            for cp in fetch(e):
                cp.start(priority=WEIGHT_DMA_QUEUE)

        @pl.when(used & ((i == 0) | (e != e_ref[jnp.maximum(i - 1, 0)])))
        def _():
            for cp in fetch(e):
                cp.wait()
            for dst, src in zip(live, staged):
                dst[...] = src[...].astype(BF16)
            nxt = lax.while_loop(lambda j: (j < nu) & (e_ref[jnp.minimum(j, nb - 1)] == e),
                                 lambda j: j + 1, i + 1)

            @pl.when(nxt < nu)
            def _():
                for cp in fetch(e_ref[jnp.minimum(nxt, nb - 1)]):
                    cp.start(priority=WEIGHT_DMA_QUEUE)

    def up_cols(k):
        return wup.at[:, pl.ds(k * D_EXPERT, D_EXPERT)]

    swap_weights(TABLE_EA, 0, (up_cols(0), up_cols(2), w2a), (s1a, s3a, s2a))
    swap_weights(TABLE_EB, 1, (up_cols(1), up_cols(3), w2b), (s1b, s3b, s2b))

    def ffn_rows(rows):
        sublanes = pl.ds(0, rows * ROW_TILE)
        xb = _load_row_tiled(x_ref.at[sublanes], rows).astype(BF16)

        logits = _dot(xb, wr_ref[...].astype(BF16))
        lane = lax.broadcasted_iota(I32, logits.shape, 1)
        logits = jnp.where(lane < N_EXPERTS, logits, -jnp.inf)
        ex = jnp.exp(logits - jnp.max(logits, axis=1, keepdims=True))
        probs = ex / jnp.sum(ex, axis=1, keepdims=True)
        pa = jnp.sum(jnp.where(lane == ea, probs, 0.0), axis=1, keepdims=True)
        pb = jnp.sum(jnp.where(lane == eb, probs, 0.0), axis=1, keepdims=True)
        den = pa + pb

        up = _dot(xb, wup[...])
        a = up[:, 0:2 * D_EXPERT]
        hid = (a * jax.nn.sigmoid(a) * up[:, 2 * D_EXPERT:4 * D_EXPERT]).astype(BF16)
        y = ((pa / den) * _dot(hid[:, 0:D_EXPERT], w2a[...])
             + (pb / den) * _dot(hid[:, D_EXPERT:2 * D_EXPERT], w2b[...]))
        _store_row_tiled(y_ref.at[sublanes], y)

    few = plan_ref[TABLE_VALID, i] <= BM // 2

    @pl.when(used & jnp.logical_not(few))
    def _():
        ffn_rows(BM)

    @pl.when(used & few)
    def _():
        ffn_rows(BM // 2)
        rest = pl.ds(BM // 2 * ROW_TILE, BM // 2 * ROW_TILE)
        y_ref[rest, :] = jnp.zeros((BM // 2 * ROW_TILE, LANES), F32)


def _expert_ffn(layer, plan, xbuf, w1, w3, w2, wr_pad):
    n_blocks = xbuf.shape[0] // (BM * ROW_TILE)
    used_blk = lambda i, plan_ref: (jnp.minimum(i, plan_ref[TABLE_NUSED, 0] - 1), 0)
    up = (D_MODEL, D_EXPERT)
    down = (D_EXPERT, D_MODEL)
    expert_slot = lambda dtype: [pltpu.VMEM(up, dtype), pltpu.VMEM(up, dtype), pltpu.VMEM(down, dtype)]
    grid_spec = pltpu.PrefetchScalarGridSpec(
        num_scalar_prefetch=1,
        grid=(n_blocks,),
        in_specs=[
            pl.BlockSpec((BM * ROW_TILE, LANES), used_blk),
            pl.BlockSpec(memory_space=pl.ANY),
            pl.BlockSpec(memory_space=pl.ANY),
            pl.BlockSpec(memory_space=pl.ANY),
            pl.BlockSpec((D_MODEL, LANES), lambda i, plan_ref: (0, 0)),
        ],
        out_specs=pl.BlockSpec((BM * ROW_TILE, LANES), used_blk),
        scratch_shapes=([pltpu.VMEM((D_MODEL, 4 * D_EXPERT), BF16), pltpu.VMEM(down, BF16),
                         pltpu.VMEM(down, BF16)] + expert_slot(F32) + expert_slot(F32)
                        + [pltpu.SemaphoreType.DMA((2,))]),
    )
    return pl.pallas_call(
        functools.partial(_expert_kernel, layer=layer),
        grid_spec=grid_spec,
        out_shape=jax.ShapeDtypeStruct(xbuf.shape, F32),
        input_output_aliases={1: 0},
        compiler_params=pltpu.CompilerParams(
            dimension_semantics=("arbitrary",), vmem_limit_bytes=VMEM_LIMIT),
        name="expert_ffn",
    )(plan, xbuf, w1, w3, w2, wr_pad)


def _final_kernel(scur_ref, snext_ref, h1p_ref, y_hbm, g2_ref, b2_ref, o_ref, ybuf, gsem):
    i = pl.program_id(0)
    y = _prefetch_expert_rows(i, pl.num_programs(0), scur_ref, snext_ref, y_hbm, ybuf, gsem)
    o_ref[...] = _layer_norm(ALPHA * _load_row_tiled(h1p_ref, TS) + y,
                             g2_ref[DEPTH - 1:DEPTH, :], b2_ref[DEPTH - 1:DEPTH, :])


def _final_call(slot_prev, h1t_prev, y_prev, g2, b2):
    t = h1t_prev.shape[0] // ROW_TILE
    nt = t // TS
    return pl.pallas_call(
        _final_kernel,
        grid=(nt,),
        in_specs=[
            pl.BlockSpec((1, 1, TS), lambda i: (i, 0, 0), memory_space=pltpu.SMEM),
            pl.BlockSpec((1, 1, TS), lambda i: (jnp.minimum(i + 1, nt - 1), 0, 0),
                         memory_space=pltpu.SMEM),
            pl.BlockSpec((TS * ROW_TILE, LANES), lambda i: (i, 0)),
            pl.BlockSpec(memory_space=pl.ANY),
            pl.BlockSpec((DEPTH, D_MODEL), lambda i: (0, 0)),
            pl.BlockSpec((DEPTH, D_MODEL), lambda i: (0, 0)),
        ],
        out_specs=pl.BlockSpec((TS, D_MODEL), lambda i: (i, 0)),
        out_shape=jax.ShapeDtypeStruct((t, D_MODEL), F32),
        scratch_shapes=[pltpu.VMEM((2, TS * ROW_TILE, LANES), F32), pltpu.SemaphoreType.DMA((2,))],
        compiler_params=pltpu.CompilerParams(
            dimension_semantics=("arbitrary",), vmem_limit_bytes=VMEM_LIMIT),
        name="final_combine_ln",
    )(slot_prev, slot_prev, h1t_prev, y_prev, g2, b2)


def kernel(x, ln_in_g, ln_in_b, w_in, pool_w, pool_scale, conv_w, conv_b, w_out, ln1_g, ln1_b,
           w_router, router_bias, exp_w1, exp_w3, exp_w2, ln2_g, ln2_b):
    bsz, seq, d = x.shape
    t = bsz * seq
    n_rows = t + N_CLASSES * BM
    assert n_rows // BM <= LANES

    tri = (lax.broadcasted_iota(I32, (TS, TS), 0)
           < lax.broadcasted_iota(I32, (TS, TS), 1)).astype(BF16)
    wr_t = w_router.T
    wr_pad = jnp.pad(w_router, ((0, 0), (0, LANES - N_EXPERTS)))
    rbias = router_bias.reshape(N_EXPERTS, 1).astype(F32)

    prev = (x.reshape(t, d), ln_in_g.reshape(1, -1), ln_in_b.reshape(1, -1))
    xbuf = None
    for l in range(DEPTH):
        h1t, route, cnt = _layer_call(l, bsz, seq, prev, w_in, pool_w, pool_scale, conv_w, conv_b,
                                      w_out, ln1_g, ln1_b, wr_t, rbias, tri)
        slot3, plan = _dispatch_plan(cnt, route)
        if xbuf is None:
            xbuf = _first_scatter_rows(plan, slot3, h1t, n_rows)
        else:
            xbuf = _scatter_rows(slot3, h1t, xbuf)
        xbuf = _expert_ffn(l, plan, xbuf, exp_w1, exp_w3, exp_w2, wr_pad)
        prev = (slot3, h1t, xbuf, ln2_g, ln2_b)
    out = _final_call(*prev)
    return out.reshape(bsz, seq, d)
```

```python
import functools

import jax
import jax.numpy as jnp
from jax import lax
from jax.experimental import pallas as pl
from jax.experimental.pallas import tpu as pltpu

D_MODEL = 1024
DEPTH = 4
D_POOL = 512
POOL_WINDOWS = (2, 4, 8, 16)
POOL_GROUP = 128
D_CONV = 512
CONV_WIDTH = 3
D_PROJ = D_POOL + 3 * D_CONV
N_EXPERTS = 16
N_EXPERT_GROUPS = 4
EXPERTS_PER_GROUP = 4
PAIRS_PER_GROUP = 6
N_CLASSES = N_EXPERT_GROUPS * PAIRS_PER_GROUP
CLASS_ROWS = 32
D_EXPERT = 512
ALPHA = float((2 * DEPTH) ** 0.25)
LN_EPS = 1e-5

F32 = jnp.float32
BF16 = jnp.bfloat16
I32 = jnp.int32

LANES = 128
ROW_TILE = D_MODEL // LANES
TS = 512
SUBLANES = 8
V_HALO = 32
Z_HALO = 8
BM = 256
W_STAGE_ROWS = 256
DMA_QUEUES = 2
WEIGHT_DMA_QUEUE = 1
VMEM_LIMIT = 56 * 1024 * 1024


def _layer_norm(x, g, b):
    mu = jnp.mean(x, axis=-1, keepdims=True)
    xc = x - mu
    var = jnp.mean(xc * xc, axis=-1, keepdims=True)
    return xc * lax.rsqrt(var + LN_EPS) * g + b


def _dot(a, b):
    return jnp.dot(a, b, preferred_element_type=F32)


def _store_row_tiled(ref, value):
    n = value.shape[0]
    for j in range(ROW_TILE):
        ref[pl.ds(j, n, stride=ROW_TILE), :] = value[:, j * LANES:(j + 1) * LANES]


def _load_row_tiled(ref, n):
    return jnp.concatenate([ref[pl.ds(j, n, stride=ROW_TILE), :] for j in range(ROW_TILE)], axis=1)


def _row(ref, first_sublane):
    return ref.at[pl.ds(first_sublane, ROW_TILE)]


def _route(sel):
    rows = [sel[e:e + 1, :] for e in range(N_EXPERTS)]
    gscore = []
    for g in range(N_EXPERT_GROUPS):
        a, b, c, d = rows[4 * g:4 * g + 4]
        hi1, lo1 = jnp.maximum(a, b), jnp.minimum(a, b)
        hi2, lo2 = jnp.maximum(c, d), jnp.minimum(c, d)
        m1 = jnp.maximum(hi1, hi2)
        m2 = jnp.maximum(jnp.minimum(hi1, hi2), jnp.maximum(lo1, lo2))
        gscore.append(m1 + m2)
    best = gscore[0]
    gidx = jnp.zeros(best.shape, I32)
    for g in range(1, N_EXPERT_GROUPS):
        upd = gscore[g] > best
        best = jnp.where(upd, gscore[g], best)
        gidx = jnp.where(upd, g, gidx)
    neg = jnp.full(best.shape, -jnp.inf, F32)
    masked = [jnp.where(gidx == (e // EXPERTS_PER_GROUP), rows[e], neg) for e in range(N_EXPERTS)]

    def top1(vals):
        bv = neg
        bi = jnp.zeros(best.shape, I32)
        for e in range(N_EXPERTS):
            upd = vals[e] > bv
            bv = jnp.where(upd, vals[e], bv)
            bi = jnp.where(upd, e, bi)
        return bi

    i1 = top1(masked)
    i2 = top1([jnp.where(i1 == e, neg, masked[e]) for e in range(N_EXPERTS)])
    return gidx, i1, i2


def _pair_class(gidx, i1, i2):
    a = jnp.minimum(i1, i2) - gidx * EXPERTS_PER_GROUP
    b = jnp.maximum(i1, i2) - gidx * EXPERTS_PER_GROUP
    pair = jnp.where(b == 3, 3 + a, jnp.where(a == 1, 1, jnp.where(b == 1, 0, 2)))
    return gidx * PAIRS_PER_GROUP + pair


def _prefetch_expert_rows(i, nt, scur_ref, snext_ref, y_hbm, ybuf, gsem, before_wait=None):
    slot = i % 2

    def gather(slot_ref, to_slot, t, t_sublane):
        src = _row(y_hbm, pl.multiple_of(slot_ref[0, 0, t], ROW_TILE))
        return pltpu.make_async_copy(src, _row(ybuf.at[to_slot], t_sublane), gsem.at[to_slot])

    @pl.when(i == 0)
    def _():
        for t in range(TS):
            gather(scur_ref, 0, t, t * ROW_TILE).start(priority=t % DMA_QUEUES)

    @pl.when(i + 1 < nt)
    def _():
        for t in range(TS):
            gather(snext_ref, 1 - slot, t, t * ROW_TILE).start(priority=t % DMA_QUEUES)

    if before_wait is not None:
        before_wait()
    pltpu.make_async_copy(y_hbm.at[pl.ds(0, TS * ROW_TILE)], ybuf.at[slot], gsem.at[slot]).wait()
    return _load_row_tiled(ybuf.at[slot], TS)


def _layer_kernel(*refs, layer, tiles_per_seq):
    if layer == 0:
        x_ref, gin_ref, bin_ref = refs[:3]
        rest = refs[3:]
    else:
        scur_ref, snext_ref, h1p_ref, y_hbm, g2_ref, b2_ref = refs[:6]
        rest = refs[6:]
    (win_hbm, pw_ref, ps_ref, cw_ref, cb_ref, wout_hbm, g_ref, b_ref, wr_ref, rb_ref, tri_ref,
     h1t_ref, route_ref, cnt_ref,
     vbuf, pwork, zbuf, mixbuf, carry, win_ref, wout_ref, stage, wsem, ybuf, gsem) = rest

    i = pl.program_id(0)
    nt = pl.num_programs(0)

    def stage_weights():
        chunks = []
        for src, dst, width in ((win_hbm, win_ref, D_PROJ), (wout_hbm, wout_ref, D_MODEL)):
            for c in range(D_MODEL // W_STAGE_ROWS):
                chunks.append((src, dst, width, pl.ds(c * W_STAGE_ROWS, W_STAGE_ROWS)))

        def fetch(k):
            src, _, width, rows = chunks[k]
            return pltpu.make_async_copy(src.at[layer, rows], stage.at[k % 2, :, pl.ds(0, width)],
                                         wsem.at[k % 2])

        fetch(0).start()
        for k, (_, dst, width, rows) in enumerate(chunks):
            if k + 1 < len(chunks):
                fetch(k + 1).start()
            fetch(k).wait()
            dst[rows, :] = stage[k % 2, :, 0:width].astype(BF16)
        carry[...] = jnp.zeros(carry.shape, F32)

    first_step_setup = functools.partial(pl.when(i == 0), stage_weights)
    if layer == 0:
        first_step_setup()
        h = _layer_norm(x_ref[...], gin_ref[...], bin_ref[...])
    else:
        y = _prefetch_expert_rows(i, nt, scur_ref, snext_ref, y_hbm, ybuf, gsem,
                                  before_wait=first_step_setup)
        h = _layer_norm(ALPHA * _load_row_tiled(h1p_ref, TS) + y,
                        g2_ref[layer - 1:layer, :], b2_ref[layer - 1:layer, :])

    si = i % tiles_per_seq

    @pl.when(si == 0)
    def _():
        vbuf[0:V_HALO, :] = jnp.zeros((V_HALO, D_POOL), F32)
        zbuf[0:Z_HALO, :] = jnp.zeros((Z_HALO, D_CONV), F32)

    hb = h.astype(BF16)
    vbuf[V_HALO:V_HALO + TS, :] = _dot(hb, win_ref[:, 0:D_POOL])
    gate_b = _dot(hb, win_ref[:, D_POOL:D_POOL + D_CONV])
    gate_c = _dot(hb, win_ref[:, D_POOL + D_CONV:D_POOL + 2 * D_CONV])
    u = _dot(hb, win_ref[:, D_POOL + 2 * D_CONV:D_PROJ])
    z = gate_c * u
    zbuf[Z_HALO:Z_HALO + TS, :] = z

    src = vbuf
    for level in range(1, len(POOL_WINDOWS)):
        dst = pwork.at[(level - 1) % 2]
        lo = SUBLANES * level
        shift = 2 ** (level - 1)
        n = TS + V_HALO - lo
        cols = pl.ds(level * POOL_GROUP, D_POOL - level * POOL_GROUP)
        dst[pl.ds(lo, n), cols] = src[pl.ds(lo, n), cols] + src[pl.ds(lo - shift, n), cols]
        src = dst
    tpos = si * TS + lax.broadcasted_iota(I32, (TS, 1), 0)
    for g, win in enumerate(POOL_WINDOWS):
        cols = pl.ds(g * POOL_GROUP, POOL_GROUP)
        cur = vbuf[pl.ds(V_HALO, TS), cols]
        half = vbuf if g == 0 else pwork.at[(g - 1) % 2]
        acc = half[pl.ds(V_HALO, TS), cols] + half[pl.ds(V_HALO - win // 2, TS), cols]
        denom = jnp.minimum(tpos + 1, win).astype(F32)
        pooled = acc / denom - cur
        mixed = _dot(pooled.astype(BF16), pw_ref[0, g].astype(BF16)) * ps_ref[layer:layer + 1, cols]
        mixbuf[:, cols] = mixed.astype(BF16)

    yc = (cw_ref[0, 2:3, :] * z
          + cw_ref[0, 1:2, :] * zbuf[pl.ds(Z_HALO - 1, TS), :]
          + cw_ref[0, 0:1, :] * zbuf[pl.ds(Z_HALO - 2, TS), :])
    mixbuf[:, D_POOL:D_POOL + D_CONV] = (gate_b * (yc + cb_ref[layer:layer + 1, :])).astype(BF16)

    vbuf[0:V_HALO, :] = vbuf[TS:TS + V_HALO, :]
    zbuf[0:Z_HALO, :] = zbuf[TS:TS + Z_HALO, :]

    mix = _dot(mixbuf[...], wout_ref[...])
    h1 = _layer_norm(ALPHA * h + mix, g_ref[layer:layer + 1, :], b_ref[layer:layer + 1, :])
    _store_row_tiled(h1t_ref, h1)

    logits = lax.dot_general(wr_ref[...].astype(BF16), h1.astype(BF16), (((1,), (1,)), ((), ())),
                             preferred_element_type=F32)
    mx = jnp.max(logits, axis=0, keepdims=True)
    ex = jnp.exp(logits - mx)
    probs = ex / jnp.sum(ex, axis=0, keepdims=True)
    gidx, i1, i2 = _route(probs + rb_ref[...])
    cls = _pair_class(gidx, i1, i2)

    ciota = lax.broadcasted_iota(I32, (CLASS_ROWS, TS), 0)
    onehot = jnp.where(ciota == cls, 1.0, 0.0).astype(F32)
    before = _dot(onehot.astype(BF16), tri_ref[...])
    total = jnp.sum(onehot, axis=1, keepdims=True)
    c = carry[:, 0:1]
    rank = jnp.sum(onehot * (c + before), axis=0, keepdims=True)
    route_ref[0:1, :] = cls
    route_ref[1:2, :] = rank.astype(I32)
    newc = jnp.broadcast_to(c + total, carry.shape)
    carry[...] = newc
    cnt_ref[...] = newc


def _layer_call(layer, bsz, seq, prev, w_in, pool_w, pool_scale, conv_w, conv_b, w_out, ln_g, ln_b,
                wr_t, rbias, tri):
    t = bsz * seq
    nt = t // TS
    const2 = lambda i: (0, 0)
    lsel3 = lambda i: (layer, 0, 0)
    lsel4 = lambda i: (layer, 0, 0, 0)
    tile = lambda i: (i, 0)
    lane_tile = lambda i: (0, i)
    if layer == 0:
        x, gin, bin_ = prev
        head_args = (x, gin, bin_)
        head_specs = [
            pl.BlockSpec((TS, D_MODEL), tile),
            pl.BlockSpec((1, D_MODEL), const2),
            pl.BlockSpec((1, D_MODEL), const2),
        ]
    else:
        slot_prev, h1t_prev, y_prev, g2, b2 = prev
        head_args = (slot_prev, slot_prev, h1t_prev, y_prev, g2, b2)
        head_specs = [
            pl.BlockSpec((1, 1, TS), lambda i: (i, 0, 0), memory_space=pltpu.SMEM),
            pl.BlockSpec((1, 1, TS), lambda i: (jnp.minimum(i + 1, nt - 1), 0, 0),
                         memory_space=pltpu.SMEM),
            pl.BlockSpec((TS * ROW_TILE, LANES), tile),
            pl.BlockSpec(memory_space=pl.ANY),
            pl.BlockSpec((DEPTH, D_MODEL), const2),
            pl.BlockSpec((DEPTH, D_MODEL), const2),
        ]
    return pl.pallas_call(
        functools.partial(_layer_kernel, layer=layer, tiles_per_seq=seq // TS),
        grid=(nt,),
        in_specs=head_specs + [
            pl.BlockSpec(memory_space=pl.ANY),
            pl.BlockSpec((1, len(POOL_WINDOWS), POOL_GROUP, POOL_GROUP), lsel4),
            pl.BlockSpec((DEPTH, D_POOL), const2),
            pl.BlockSpec((1, CONV_WIDTH, D_CONV), lsel3),
            pl.BlockSpec((DEPTH, D_CONV), const2),
            pl.BlockSpec(memory_space=pl.ANY),
            pl.BlockSpec((DEPTH, D_MODEL), const2),
            pl.BlockSpec((DEPTH, D_MODEL), const2),
            pl.BlockSpec((N_EXPERTS, D_MODEL), const2),
            pl.BlockSpec((N_EXPERTS, 1), const2),
            pl.BlockSpec((TS, TS), const2),
        ],
        out_specs=[
            pl.BlockSpec((TS * ROW_TILE, LANES), tile),
            pl.BlockSpec((2, TS), lane_tile),
            pl.BlockSpec((CLASS_ROWS, LANES), const2),
        ],
        out_shape=[
            jax.ShapeDtypeStruct((t * ROW_TILE, LANES), F32),
            jax.ShapeDtypeStruct((2, t), I32),
            jax.ShapeDtypeStruct((CLASS_ROWS, LANES), F32),
        ],
        scratch_shapes=[
            pltpu.VMEM((TS + V_HALO, D_POOL), F32),
            pltpu.VMEM((2, TS + V_HALO, D_POOL), F32),
            pltpu.VMEM((TS + Z_HALO, D_CONV), F32),
            pltpu.VMEM((TS, D_MODEL), BF16),
            pltpu.VMEM((CLASS_ROWS, LANES), F32),
            pltpu.VMEM((D_MODEL, D_PROJ), BF16),
            pltpu.VMEM((D_MODEL, D_MODEL), BF16),
            pltpu.VMEM((2, W_STAGE_ROWS, D_PROJ), F32),
            pltpu.SemaphoreType.DMA((2,)),
            pltpu.VMEM((2, TS * ROW_TILE, LANES), F32),
            pltpu.SemaphoreType.DMA((2,)),
        ],
        compiler_params=pltpu.CompilerParams(
            dimension_semantics=("arbitrary",), vmem_limit_bytes=VMEM_LIMIT),
        name="layer_mix_route",
    )(*head_args, w_in, pool_w, pool_scale, conv_w, conv_b, w_out, ln_g, ln_b, wr_t, rbias, tri)


TABLE_EA, TABLE_EB, TABLE_VALID, TABLE_NUSED = 0, 1, 2, 3
TABLE_TAIL_START, TABLE_TAIL_LEN = 4, 5


def _dispatch_kernel(cnt_ref, route_ref, slot_ref, table_ref):
    log_bm = BM.bit_length() - 1
    starts, ends, fills = [], [], []
    end = jnp.zeros((1, LANES), I32)
    for k in range(N_CLASSES):
        count = cnt_ref[k:k + 1, :].astype(I32)
        starts.append(end)
        fills.append(end + count)
        end = end + lax.shift_left(lax.shift_right_logical(count + (BM - 1), log_bm), log_bm)
        ends.append(end)
    nused = lax.shift_right_logical(end, log_bm)

    cls = route_ref[0:1, :]
    start = jnp.zeros(cls.shape, I32)
    for k in range(N_CLASSES):
        start = jnp.where(cls == k, starts[k][:, 0:1], start)
    slots = (start + route_ref[1:2, :]) * ROW_TILE
    for j in range(slot_ref.shape[0]):
        slot_ref[j] = slots[:, j * TS:(j + 1) * TS]

    step = jnp.minimum(lax.broadcasted_iota(I32, (1, LANES), 1), nused - 1)
    first_row = step * BM
    c = jnp.zeros((1, LANES), I32)
    fill = fills[0]
    for k in range(N_CLASSES - 1):
        beyond = first_row >= ends[k]
        c = c + beyond.astype(I32)
        fill = jnp.where(beyond, fills[k + 1], fill)
    g = sum((c >= k * PAIRS_PER_GROUP).astype(I32) for k in range(1, N_EXPERT_GROUPS))
    p = c - g * PAIRS_PER_GROUP
    ge = lambda k: (p >= k).astype(I32)
    table_ref[TABLE_EA:TABLE_EA + 1, :] = g * EXPERTS_PER_GROUP + 2 * ge(1) + ge(3)
    table_ref[TABLE_EB:TABLE_EB + 1, :] = g * EXPERTS_PER_GROUP + 1 - ge(2) + ge(4) + ge(5)
    table_ref[TABLE_VALID:TABLE_VALID + 1, :] = jnp.clip(fill - first_row, 0, BM)
    table_ref[TABLE_NUSED:TABLE_NUSED + 1, :] = nused

    lane = lax.broadcasted_iota(I32, (1, LANES), 1)
    tail_start = jnp.zeros((1, LANES), I32)
    tail_len = jnp.zeros((1, LANES), I32)
    for k in range(N_CLASSES):
        tail_start = jnp.where(lane == k, fills[k], tail_start)
        tail_len = jnp.where(lane == k, ends[k] - fills[k], tail_len)
    table_ref[TABLE_TAIL_START:TABLE_TAIL_START + 1, :] = tail_start
    table_ref[TABLE_TAIL_LEN:TABLE_TAIL_LEN + 1, :] = tail_len
    table_ref[TABLE_TAIL_LEN + 1:, :] = jnp.zeros((SUBLANES - TABLE_TAIL_LEN - 1, LANES), I32)


def _dispatch_plan(cnt, route):
    t = route.shape[1]
    return pl.pallas_call(
        _dispatch_kernel,
        grid=(1,),
        in_specs=[pl.BlockSpec((CLASS_ROWS, LANES), lambda i: (0, 0)),
                  pl.BlockSpec((2, t), lambda i: (0, 0))],
        out_specs=[pl.BlockSpec((t // TS, 1, TS), lambda i: (0, 0, 0)),
                   pl.BlockSpec((SUBLANES, LANES), lambda i: (0, 0))],
        out_shape=[jax.ShapeDtypeStruct((t // TS, 1, TS), I32),
                   jax.ShapeDtypeStruct((SUBLANES, LANES), I32)],
        name="dispatch_plan",
    )(cnt, route)


def _scatter_kernel(slot_ref, h1t_ref, xin_ref, xout_ref, sem):
    del xin_ref
    for t in range(TS):
        dst = _row(xout_ref, pl.multiple_of(slot_ref[0, 0, t], ROW_TILE))
        pltpu.make_async_copy(_row(h1t_ref, t * ROW_TILE), dst, sem).start(priority=t % DMA_QUEUES)
    pltpu.make_async_copy(h1t_ref, xout_ref.at[pl.ds(0, TS * ROW_TILE)], sem).wait()


def _first_scatter_kernel(plan_ref, slot_ref, h1t_ref, xout_ref, zeros, sem, zsem):
    i = pl.program_id(0)
    block = BM * ROW_TILE
    n_blocks = xout_ref.shape[0] // block

    def fill_rows(visit):
        for k in range(N_CLASSES):
            length = plan_ref[TABLE_TAIL_LEN, k]
            row = plan_ref[TABLE_TAIL_START, k]
            piece = BM // 2
            while piece >= 1:
                take = (length & piece) != 0
                dst = xout_ref.at[pl.ds(pl.multiple_of(row * ROW_TILE, ROW_TILE), piece * ROW_TILE)]
                cp = pltpu.make_async_copy(zeros.at[pl.ds(0, piece * ROW_TILE)], dst, zsem)
                pl.when(take)(functools.partial(visit, cp))
                row = row + jnp.where(take, piece, 0)
                piece //= 2

        def idle_block(j, carry):
            dst = xout_ref.at[pl.ds(pl.multiple_of(j * block, block), block)]
            visit(pltpu.make_async_copy(zeros, dst, zsem))
            return carry
        lax.fori_loop(plan_ref[TABLE_NUSED, 0], n_blocks, idle_block, 0)

    for t in range(TS):
        dst = _row(xout_ref, pl.multiple_of(slot_ref[0, 0, t], ROW_TILE))
        pltpu.make_async_copy(_row(h1t_ref, t * ROW_TILE), dst, sem).start(priority=t % DMA_QUEUES)

    @pl.when(i == 0)
    def _():
        zeros[...] = jnp.zeros(zeros.shape, F32)
        fill_rows(lambda cp: cp.start())

    pltpu.make_async_copy(h1t_ref, xout_ref.at[pl.ds(0, TS * ROW_TILE)], sem).wait()

    @pl.when(i == pl.num_programs(0) - 1)
    def _():
        fill_rows(lambda cp: cp.wait())


def _first_scatter_rows(plan, slot3, h1t, n_rows):
    nt = slot3.shape[0]
    grid_spec = pltpu.PrefetchScalarGridSpec(
        num_scalar_prefetch=1,
        grid=(nt,),
        in_specs=[
            pl.BlockSpec((1, 1, TS), lambda i, *_: (i, 0, 0), memory_space=pltpu.SMEM),
            pl.BlockSpec((TS * ROW_TILE, LANES), lambda i, *_: (i, 0)),
        ],
        out_specs=pl.BlockSpec(memory_space=pl.ANY),
        scratch_shapes=[pltpu.VMEM((BM * ROW_TILE, LANES), F32), pltpu.SemaphoreType.DMA(()),
                        pltpu.SemaphoreType.DMA(())],
    )
    return pl.pallas_call(
        _first_scatter_kernel,
        grid_spec=grid_spec,
        out_shape=jax.ShapeDtypeStruct((n_rows * ROW_TILE, LANES), F32),
        compiler_params=pltpu.CompilerParams(dimension_semantics=("arbitrary",)),
        name="first_scatter_rows",
    )(plan, slot3, h1t)


def _scatter_rows(slot3, h1t, xbuf):
    nt = slot3.shape[0]
    return pl.pallas_call(
        _scatter_kernel,
        grid=(nt,),
        in_specs=[
            pl.BlockSpec((1, 1, TS), lambda i: (i, 0, 0), memory_space=pltpu.SMEM),
            pl.BlockSpec((TS * ROW_TILE, LANES), lambda i: (i, 0)),
            pl.BlockSpec(memory_space=pl.ANY),
        ],
        out_specs=pl.BlockSpec(memory_space=pl.ANY),
        out_shape=jax.ShapeDtypeStruct(xbuf.shape, xbuf.dtype),
        scratch_shapes=[pltpu.SemaphoreType.DMA(())],
        input_output_aliases={2: 0},
        compiler_params=pltpu.CompilerParams(dimension_semantics=("arbitrary",)),
        name="scatter_rows",
    )(slot3, h1t, xbuf)


def _expert_kernel(plan_ref, x_ref, w1_hbm, w3_hbm, w2_hbm, wr_ref, y_ref,
                   wup, w2a, w2b, s1a, s3a, s2a, s1b, s3b, s2b, wsem, *, layer):
    i = pl.program_id(0)
    nb = pl.num_programs(0)
    nu = plan_ref[TABLE_NUSED, 0]
    used = i < nu
    ea = plan_ref[TABLE_EA, i]
    eb = plan_ref[TABLE_EB, i]

    def swap_weights(table_row, slot, live, staged):
        e_ref = plan_ref.at[table_row]
        e = e_ref[i]

        def fetch(expert):
            return [pltpu.make_async_copy(src.at[layer, expert], dst, wsem.at[slot])
                    for src, dst in zip((w1_hbm, w3_hbm, w2_hbm), staged)]

        @pl.when(used & (i == 0))
        def _():
            for cp in fetch(e):
                cp.start(priority=WEIGHT_DMA_QUEUE)

        @pl.when(used & ((i == 0) | (e != e_ref[jnp.maximum(i - 1, 0)])))
        def _():
            for cp in fetch(e):
                cp.wait()
            for dst, src in zip(live, staged):
                dst[...] = src[...].astype(BF16)
            nxt = lax.while_loop(lambda j: (j < nu) & (e_ref[jnp.minimum(j, nb - 1)] == e),
                                 lambda j: j + 1, i + 1)

            @pl.when(nxt < nu)
            def _():
                for cp in fetch(e_ref[jnp.minimum(nxt, nb - 1)]):
                    cp.start(priority=WEIGHT_DMA_QUEUE)

    def up_cols(k):
        return wup.at[:, pl.ds(k * D_EXPERT, D_EXPERT)]

    swap_weights(TABLE_EA, 0, (up_cols(0), up_cols(2), w2a), (s1a, s3a, s2a))
    swap_weights(TABLE_EB, 1, (up_cols(1), up_cols(3), w2b), (s1b, s3b, s2b))

    def ffn_rows(rows):
        sublanes = pl.ds(0, rows * ROW_TILE)
        xb = _load_row_tiled(x_ref.at[sublanes], rows).astype(BF16)

        logits = _dot(xb, wr_ref[...].astype(BF16))
        lane = lax.broadcasted_iota(I32, logits.shape, 1)
        logits = jnp.where(lane < N_EXPERTS, logits, -jnp.inf)
        ex = jnp.exp(logits - jnp.max(logits, axis=1, keepdims=True))
        probs = ex / jnp.sum(ex, axis=1, keepdims=True)
        pa = jnp.sum(jnp.where(lane == ea, probs, 0.0), axis=1, keepdims=True)
        pb = jnp.sum(jnp.where(lane == eb, probs, 0.0), axis=1, keepdims=True)
        den = pa + pb

        up = _dot(xb, wup[...])
        a = up[:, 0:2 * D_EXPERT]
        hid = (a * jax.nn.sigmoid(a) * up[:, 2 * D_EXPERT:4 * D_EXPERT]).astype(BF16)
        y = ((pa / den) * _dot(hid[:, 0:D_EXPERT], w2a[...])
             + (pb / den) * _dot(hid[:, D_EXPERT:2 * D_EXPERT], w2b[...]))
        _store_row_tiled(y_ref.at[sublanes], y)

    few = plan_ref[TABLE_VALID, i] <= BM // 2

    @pl.when(used & jnp.logical_not(few))
    def _():
        ffn_rows(BM)

    @pl.when(used & few)
    def _():
        ffn_rows(BM // 2)
        rest = pl.ds(BM // 2 * ROW_TILE, BM // 2 * ROW_TILE)
        y_ref[rest, :] = jnp.zeros((BM // 2 * ROW_TILE, LANES), F32)


def _expert_ffn(layer, plan, xbuf, w1, w3, w2, wr_pad):
    n_blocks = xbuf.shape[0] // (BM * ROW_TILE)
    used_blk = lambda i, plan_ref: (jnp.minimum(i, plan_ref[TABLE_NUSED, 0] - 1), 0)
    up = (D_MODEL, D_EXPERT)
    down = (D_EXPERT, D_MODEL)
    expert_slot = lambda dtype: [pltpu.VMEM(up, dtype), pltpu.VMEM(up, dtype), pltpu.VMEM(down, dtype)]
    grid_spec = pltpu.PrefetchScalarGridSpec(
        num_scalar_prefetch=1,
        grid=(n_blocks,),
        in_specs=[
            pl.BlockSpec((BM * ROW_TILE, LANES), used_blk),
            pl.BlockSpec(memory_space=pl.ANY),
            pl.BlockSpec(memory_space=pl.ANY),
            pl.BlockSpec(memory_space=pl.ANY),
            pl.BlockSpec((D_MODEL, LANES), lambda i, plan_ref: (0, 0)),
        ],
        out_specs=pl.BlockSpec((BM * ROW_TILE, LANES), used_blk),
        scratch_shapes=([pltpu.VMEM((D_MODEL, 4 * D_EXPERT), BF16), pltpu.VMEM(down, BF16),
                         pltpu.VMEM(down, BF16)] + expert_slot(F32) + expert_slot(F32)
                        + [pltpu.SemaphoreType.DMA((2,))]),
    )
    return pl.pallas_call(
        functools.partial(_expert_kernel, layer=layer),
        grid_spec=grid_spec,
        out_shape=jax.ShapeDtypeStruct(xbuf.shape, F32),
        input_output_aliases={1: 0},
        compiler_params=pltpu.CompilerParams(
            dimension_semantics=("arbitrary",), vmem_limit_bytes=VMEM_LIMIT),
        name="expert_ffn",
    )(plan, xbuf, w1, w3, w2, wr_pad)


def _final_kernel(scur_ref, snext_ref, h1p_ref, y_hbm, g2_ref, b2_ref, o_ref, ybuf, gsem):
    i = pl.program_id(0)
    y = _prefetch_expert_rows(i, pl.num_programs(0), scur_ref, snext_ref, y_hbm, ybuf, gsem)
    o_ref[...] = _layer_norm(ALPHA * _load_row_tiled(h1p_ref, TS) + y,
                             g2_ref[DEPTH - 1:DEPTH, :], b2_ref[DEPTH - 1:DEPTH, :])


def _final_call(slot_prev, h1t_prev, y_prev, g2, b2):
    t = h1t_prev.shape[0] // ROW_TILE
    nt = t // TS
    return pl.pallas_call(
        _final_kernel,
        grid=(nt,),
        in_specs=[
            pl.BlockSpec((1, 1, TS), lambda i: (i, 0, 0), memory_space=pltpu.SMEM),
            pl.BlockSpec((1, 1, TS), lambda i: (jnp.minimum(i + 1, nt - 1), 0, 0),
                         memory_space=pltpu.SMEM),
            pl.BlockSpec((TS * ROW_TILE, LANES), lambda i: (i, 0)),
            pl.BlockSpec(memory_space=pl.ANY),
            pl.BlockSpec((DEPTH, D_MODEL), lambda i: (0, 0)),
            pl.BlockSpec((DEPTH, D_MODEL), lambda i: (0, 0)),
        ],
        out_specs=pl.BlockSpec((TS, D_MODEL), lambda i: (i, 0)),
        out_shape=jax.ShapeDtypeStruct((t, D_MODEL), F32),
        scratch_shapes=[pltpu.VMEM((2, TS * ROW_TILE, LANES), F32), pltpu.SemaphoreType.DMA((2,))],
        compiler_params=pltpu.CompilerParams(
            dimension_semantics=("arbitrary",), vmem_limit_bytes=VMEM_LIMIT),
        name="final_combine_ln",
    )(slot_prev, slot_prev, h1t_prev, y_prev, g2, b2)


def kernel(x, ln_in_g, ln_in_b, w_in, pool_w, pool_scale, conv_w, conv_b, w_out, ln1_g, ln1_b,
           w_router, router_bias, exp_w1, exp_w3, exp_w2, ln2_g, ln2_b):
    bsz, seq, d = x.shape
    t = bsz * seq
    n_rows = t + N_CLASSES * BM
    assert n_rows // BM <= LANES

    tri = (lax.broadcasted_iota(I32, (TS, TS), 0)
           < lax.broadcasted_iota(I32, (TS, TS), 1)).astype(BF16)
    wr_t = w_router.T
    wr_pad = jnp.pad(w_router, ((0, 0), (0, LANES - N_EXPERTS)))
    rbias = router_bias.reshape(N_EXPERTS, 1).astype(F32)

    prev = (x.reshape(t, d), ln_in_g.reshape(1, -1), ln_in_b.reshape(1, -1))
    xbuf = None
    for l in range(DEPTH):
        h1t, route, cnt = _layer_call(l, bsz, seq, prev, w_in, pool_w, pool_scale, conv_w, conv_b,
                                      w_out, ln1_g, ln1_b, wr_t, rbias, tri)
        slot3, plan = _dispatch_plan(cnt, route)
        if xbuf is None:
            xbuf = _first_scatter_rows(plan, slot3, h1t, n_rows)
        else:
            xbuf = _scatter_rows(slot3, h1t, xbuf)
        xbuf = _expert_ffn(l, plan, xbuf, exp_w1, exp_w3, exp_w2, wr_pad)
        prev = (slot3, h1t, xbuf, ln2_g, ln2_b)
    out = _final_call(*prev)
    return out.reshape(bsz, seq, d)
```

```python
import functools

import jax
import jax.numpy as jnp
from jax import lax
from jax.experimental import pallas as pl
from jax.experimental.pallas import tpu as pltpu

D_MODEL = 1024
DEPTH = 4
D_POOL = 512
POOL_WINDOWS = (2, 4, 8, 16)
POOL_GROUP = 128
D_CONV = 512
CONV_WIDTH = 3
D_PROJ = D_POOL + 3 * D_CONV
N_EXPERTS = 16
N_EXPERT_GROUPS = 4
EXPERTS_PER_GROUP = 4
PAIRS_PER_GROUP = 6
N_CLASSES = N_EXPERT_GROUPS * PAIRS_PER_GROUP
CLASS_ROWS = 32
D_EXPERT = 512
ALPHA = float((2 * DEPTH) ** 0.25)
LN_EPS = 1e-5

F32 = jnp.float32
BF16 = jnp.bfloat16
I32 = jnp.int32

LANES = 128
ROW_TILE = D_MODEL // LANES
TS = 512
SUBLANES = 8
V_HALO = 32
Z_HALO = 8
BM = 256
W_STAGE_ROWS = 256
DMA_QUEUES = 2
WEIGHT_DMA_QUEUE = 1
VMEM_LIMIT = 56 * 1024 * 1024


def _layer_norm(x, g, b):
    mu = jnp.mean(x, axis=-1, keepdims=True)
    xc = x - mu
    var = jnp.mean(xc * xc, axis=-1, keepdims=True)
    return xc * lax.rsqrt(var + LN_EPS) * g + b


def _dot(a, b):
    return jnp.dot(a, b, preferred_element_type=F32)


def _store_row_tiled(ref, value):
    n = value.shape[0]
    for j in range(ROW_TILE):
        ref[pl.ds(j, n, stride=ROW_TILE), :] = value[:, j * LANES:(j + 1) * LANES]


def _load_row_tiled(ref, n):
    return jnp.concatenate([ref[pl.ds(j, n, stride=ROW_TILE), :] for j in range(ROW_TILE)], axis=1)


def _row(ref, first_sublane):
    return ref.at[pl.ds(first_sublane, ROW_TILE)]


def _route(sel):
    rows = [sel[e:e + 1, :] for e in range(N_EXPERTS)]
    gscore = []
    for g in range(N_EXPERT_GROUPS):
        a, b, c, d = rows[4 * g:4 * g + 4]
        hi1, lo1 = jnp.maximum(a, b), jnp.minimum(a, b)
        hi2, lo2 = jnp.maximum(c, d), jnp.minimum(c, d)
        m1 = jnp.maximum(hi1, hi2)
        m2 = jnp.maximum(jnp.minimum(hi1, hi2), jnp.maximum(lo1, lo2))
        gscore.append(m1 + m2)
    best = gscore[0]
    gidx = jnp.zeros(best.shape, I32)
    for g in range(1, N_EXPERT_GROUPS):
        upd = gscore[g] > best
        best = jnp.where(upd, gscore[g], best)
        gidx = jnp.where(upd, g, gidx)
    neg = jnp.full(best.shape, -jnp.inf, F32)
    masked = [jnp.where(gidx == (e // EXPERTS_PER_GROUP), rows[e], neg) for e in range(N_EXPERTS)]

    def top1(vals):
        bv = neg
        bi = jnp.zeros(best.shape, I32)
        for e in range(N_EXPERTS):
            upd = vals[e] > bv
            bv = jnp.where(upd, vals[e], bv)
            bi = jnp.where(upd, e, bi)
        return bi

    i1 = top1(masked)
    i2 = top1([jnp.where(i1 == e, neg, masked[e]) for e in range(N_EXPERTS)])
    return gidx, i1, i2


def _pair_class(gidx, i1, i2):
    a = jnp.minimum(i1, i2) - gidx * EXPERTS_PER_GROUP
    b = jnp.maximum(i1, i2) - gidx * EXPERTS_PER_GROUP
    pair = jnp.where(b == 3, 3 + a, jnp.where(a == 1, 1, jnp.where(b == 1, 0, 2)))
    return gidx * PAIRS_PER_GROUP + pair


def _prefetch_expert_rows(i, nt, scur_ref, snext_ref, y_hbm, ybuf, gsem, before_wait=None):
    slot = i % 2

    def gather(slot_ref, to_slot, t, t_sublane):
        src = _row(y_hbm, pl.multiple_of(slot_ref[0, 0, t], ROW_TILE))
        return pltpu.make_async_copy(src, _row(ybuf.at[to_slot], t_sublane), gsem.at[to_slot])

    @pl.when(i == 0)
    def _():
        for t in range(TS):
            gather(scur_ref, 0, t, t * ROW_TILE).start(priority=t % DMA_QUEUES)

    @pl.when(i + 1 < nt)
    def _():
        for t in range(TS):
            gather(snext_ref, 1 - slot, t, t * ROW_TILE).start(priority=t % DMA_QUEUES)

    if before_wait is not None:
        before_wait()
    pltpu.make_async_copy(y_hbm.at[pl.ds(0, TS * ROW_TILE)], ybuf.at[slot], gsem.at[slot]).wait()
    return _load_row_tiled(ybuf.at[slot], TS)


def _layer_kernel(*refs, layer, tiles_per_seq):
    if layer == 0:
        x_ref, gin_ref, bin_ref = refs[:3]
        rest = refs[3:]
    else:
        scur_ref, snext_ref, h1p_ref, y_hbm, g2_ref, b2_ref = refs[:6]
        rest = refs[6:]
    (win_hbm, pw_ref, ps_ref, cw_ref, cb_ref, wout_hbm, g_ref, b_ref, wr_ref, rb_ref, tri_ref,
     h1t_ref, route_ref, cnt_ref,
     vbuf, pwork, zbuf, mixbuf, carry, win_ref, wout_ref, stage, wsem, ybuf, gsem) = rest

    i = pl.program_id(0)
    nt = pl.num_programs(0)

    def stage_weights():
        chunks = []
        for src, dst, width in ((win_hbm, win_ref, D_PROJ), (wout_hbm, wout_ref, D_MODEL)):
            for c in range(D_MODEL // W_STAGE_ROWS):
                chunks.append((src, dst, width, pl.ds(c * W_STAGE_ROWS, W_STAGE_ROWS)))

        def fetch(k):
            src, _, width, rows = chunks[k]
            return pltpu.make_async_copy(src.at[layer, rows], stage.at[k % 2, :, pl.ds(0, width)],
                                         wsem.at[k % 2])

        fetch(0).start()
        for k, (_, dst, width, rows) in enumerate(chunks):
            if k + 1 < len(chunks):
                fetch(k + 1).start()
            fetch(k).wait()
            dst[rows, :] = stage[k % 2, :, 0:width].astype(BF16)
        carry[...] = jnp.zeros(carry.shape, F32)

    first_step_setup = functools.partial(pl.when(i == 0), stage_weights)
    if layer == 0:
        first_step_setup()
        h = _layer_norm(x_ref[...], gin_ref[...], bin_ref[...])
    else:
        y = _prefetch_expert_rows(i, nt, scur_ref, snext_ref, y_hbm, ybuf, gsem,
                                  before_wait=first_step_setup)
        h = _layer_norm(ALPHA * _load_row_tiled(h1p_ref, TS) + y,
                        g2_ref[layer - 1:layer, :], b2_ref[layer - 1:layer, :])

    si = i % tiles_per_seq

    @pl.when(si == 0)
    def _():
        vbuf[0:V_HALO, :] = jnp.zeros((V_HALO, D_POOL), F32)
        zbuf[0:Z_HALO, :] = jnp.zeros((Z_HALO, D_CONV), F32)

    hb = h.astype(BF16)
    vbuf[V_HALO:V_HALO + TS, :] = _dot(hb, win_ref[:, 0:D_POOL])
    gate_b = _dot(hb, win_ref[:, D_POOL:D_POOL + D_CONV])
    gate_c = _dot(hb, win_ref[:, D_POOL + D_CONV:D_POOL + 2 * D_CONV])
    u = _dot(hb, win_ref[:, D_POOL + 2 * D_CONV:D_PROJ])
    z = gate_c * u
    zbuf[Z_HALO:Z_HALO + TS, :] = z

    src = vbuf
    for level in range(1, len(POOL_WINDOWS)):
        dst = pwork.at[(level - 1) % 2]
        lo = SUBLANES * level
        shift = 2 ** (level - 1)
        n = TS + V_HALO - lo
        cols = pl.ds(level * POOL_GROUP, D_POOL - level * POOL_GROUP)
        dst[pl.ds(lo, n), cols] = src[pl.ds(lo, n), cols] + src[pl.ds(lo - shift, n), cols]
        src = dst
    tpos = si * TS + lax.broadcasted_iota(I32, (TS, 1), 0)
    for g, win in enumerate(POOL_WINDOWS):
        cols = pl.ds(g * POOL_GROUP, POOL_GROUP)
        cur = vbuf[pl.ds(V_HALO, TS), cols]
        half = vbuf if g == 0 else pwork.at[(g - 1) % 2]
        acc = half[pl.ds(V_HALO, TS), cols] + half[pl.ds(V_HALO - win // 2, TS), cols]
        denom = jnp.minimum(tpos + 1, win).astype(F32)
        pooled = acc / denom - cur
        mixed = _dot(pooled.astype(BF16), pw_ref[0, g].astype(BF16)) * ps_ref[layer:layer + 1, cols]
        mixbuf[:, cols] = mixed.astype(BF16)

    yc = (cw_ref[0, 2:3, :] * z
          + cw_ref[0, 1:2, :] * zbuf[pl.ds(Z_HALO - 1, TS), :]
          + cw_ref[0, 0:1, :] * zbuf[pl.ds(Z_HALO - 2, TS), :])
    mixbuf[:, D_POOL:D_POOL + D_CONV] = (gate_b * (yc + cb_ref[layer:layer + 1, :])).astype(BF16)

    vbuf[0:V_HALO, :] = vbuf[TS:TS + V_HALO, :]
    zbuf[0:Z_HALO, :] = zbuf[TS:TS + Z_HALO, :]

    mix = _dot(mixbuf[...], wout_ref[...])
    h1 = _layer_norm(ALPHA * h + mix, g_ref[layer:layer + 1, :], b_ref[layer:layer + 1, :])
    _store_row_tiled(h1t_ref, h1)

    logits = lax.dot_general(wr_ref[...].astype(BF16), h1.astype(BF16), (((1,), (1,)), ((), ())),
                             preferred_element_type=F32)
    mx = jnp.max(logits, axis=0, keepdims=True)
    ex = jnp.exp(logits - mx)
    probs = ex / jnp.sum(ex, axis=0, keepdims=True)
    gidx, i1, i2 = _route(probs + rb_ref[...])
    cls = _pair_class(gidx, i1, i2)

    ciota = lax.broadcasted_iota(I32, (CLASS_ROWS, TS), 0)
    onehot = jnp.where(ciota == cls, 1.0, 0.0).astype(F32)
    before = _dot(onehot.astype(BF16), tri_ref[...])
    total = jnp.sum(onehot, axis=1, keepdims=True)
    c = carry[:, 0:1]
    rank = jnp.sum(onehot * (c + before), axis=0, keepdims=True)
    route_ref[0:1, :] = cls
    route_ref[1:2, :] = rank.astype(I32)
    newc = jnp.broadcast_to(c + total, carry.shape)
    carry[...] = newc
    cnt_ref[...] = newc


def _layer_call(layer, bsz, seq, prev, w_in, pool_w, pool_scale, conv_w, conv_b, w_out, ln_g, ln_b,
                wr_t, rbias, tri):
    t = bsz * seq
    nt = t // TS
    const2 = lambda i: (0, 0)
    lsel3 = lambda i: (layer, 0, 0)
    lsel4 = lambda i: (layer, 0, 0, 0)
    tile = lambda i: (i, 0)
    lane_tile = lambda i: (0, i)
    if layer == 0:
        x, gin, bin_ = prev
        head_args = (x, gin, bin_)
        head_specs = [
            pl.BlockSpec((TS, D_MODEL), tile),
            pl.BlockSpec((1, D_MODEL), const2),
            pl.BlockSpec((1, D_MODEL), const2),
        ]
    else:
        slot_prev, h1t_prev, y_prev, g2, b2 = prev
        head_args = (slot_prev, slot_prev, h1t_prev, y_prev, g2, b2)
        head_specs = [
            pl.BlockSpec((1, 1, TS), lambda i: (i, 0, 0), memory_space=pltpu.SMEM),
            pl.BlockSpec((1, 1, TS), lambda i: (jnp.minimum(i + 1, nt - 1), 0, 0),
                         memory_space=pltpu.SMEM),
            pl.BlockSpec((TS * ROW_TILE, LANES), tile),
            pl.BlockSpec(memory_space=pl.ANY),
            pl.BlockSpec((DEPTH, D_MODEL), const2),
            pl.BlockSpec((DEPTH, D_MODEL), const2),
        ]
    return pl.pallas_call(
        functools.partial(_layer_kernel, layer=layer, tiles_per_seq=seq // TS),
        grid=(nt,),
        in_specs=head_specs + [
            pl.BlockSpec(memory_space=pl.ANY),
            pl.BlockSpec((1, len(POOL_WINDOWS), POOL_GROUP, POOL_GROUP), lsel4),
            pl.BlockSpec((DEPTH, D_POOL), const2),
            pl.BlockSpec((1, CONV_WIDTH, D_CONV), lsel3),
            pl.BlockSpec((DEPTH, D_CONV), const2),
            pl.BlockSpec(memory_space=pl.ANY),
            pl.BlockSpec((DEPTH, D_MODEL), const2),
            pl.BlockSpec((DEPTH, D_MODEL), const2),
            pl.BlockSpec((N_EXPERTS, D_MODEL), const2),
            pl.BlockSpec((N_EXPERTS, 1), const2),
            pl.BlockSpec((TS, TS), const2),
        ],
        out_specs=[
            pl.BlockSpec((TS * ROW_TILE, LANES), tile),
            pl.BlockSpec((2, TS), lane_tile),
            pl.BlockSpec((CLASS_ROWS, LANES), const2),
        ],
        out_shape=[
            jax.ShapeDtypeStruct((t * ROW_TILE, LANES), F32),
            jax.ShapeDtypeStruct((2, t), I32),
            jax.ShapeDtypeStruct((CLASS_ROWS, LANES), F32),
        ],
        scratch_shapes=[
            pltpu.VMEM((TS + V_HALO, D_POOL), F32),
            pltpu.VMEM((2, TS + V_HALO, D_POOL), F32),
            pltpu.VMEM((TS + Z_HALO, D_CONV), F32),
            pltpu.VMEM((TS, D_MODEL), BF16),
            pltpu.VMEM((CLASS_ROWS, LANES), F32),
            pltpu.VMEM((D_MODEL, D_PROJ), BF16),
            pltpu.VMEM((D_MODEL, D_MODEL), BF16),
            pltpu.VMEM((2, W_STAGE_ROWS, D_PROJ), F32),
            pltpu.SemaphoreType.DMA((2,)),
            pltpu.VMEM((2, TS * ROW_TILE, LANES), F32),
            pltpu.SemaphoreType.DMA((2,)),
        ],
        compiler_params=pltpu.CompilerParams(
            dimension_semantics=("arbitrary",), vmem_limit_bytes=VMEM_LIMIT),
        name="layer_mix_route",
    )(*head_args, w_in, pool_w, pool_scale, conv_w, conv_b, w_out, ln_g, ln_b, wr_t, rbias, tri)


TABLE_EA, TABLE_EB, TABLE_VALID, TABLE_NUSED = 0, 1, 2, 3
TABLE_TAIL_START, TABLE_TAIL_LEN = 4, 5


def _dispatch_kernel(cnt_ref, route_ref, slot_ref, table_ref):
    log_bm = BM.bit_length() - 1
    starts, ends, fills = [], [], []
    end = jnp.zeros((1, LANES), I32)
    for k in range(N_CLASSES):
        count = cnt_ref[k:k + 1, :].astype(I32)
        starts.append(end)
        fills.append(end + count)
        end = end + lax.shift_left(lax.shift_right_logical(count + (BM - 1), log_bm), log_bm)
        ends.append(end)
    nused = lax.shift_right_logical(end, log_bm)

    cls = route_ref[0:1, :]
    start = jnp.zeros(cls.shape, I32)
    for k in range(N_CLASSES):
        start = jnp.where(cls == k, starts[k][:, 0:1], start)
    slots = (start + route_ref[1:2, :]) * ROW_TILE
    for j in range(slot_ref.shape[0]):
        slot_ref[j] = slots[:, j * TS:(j + 1) * TS]

    step = jnp.minimum(lax.broadcasted_iota(I32, (1, LANES), 1), nused - 1)
    first_row = step * BM
    c = jnp.zeros((1, LANES), I32)
    fill = fills[0]
    for k in range(N_CLASSES - 1):
        beyond = first_row >= ends[k]
        c = c + beyond.astype(I32)
        fill = jnp.where(beyond, fills[k + 1], fill)
    g = sum((c >= k * PAIRS_PER_GROUP).astype(I32) for k in range(1, N_EXPERT_GROUPS))
    p = c - g * PAIRS_PER_GROUP
    ge = lambda k: (p >= k).astype(I32)
    table_ref[TABLE_EA:TABLE_EA + 1, :] = g * EXPERTS_PER_GROUP + 2 * ge(1) + ge(3)
    table_ref[TABLE_EB:TABLE_EB + 1, :] = g * EXPERTS_PER_GROUP + 1 - ge(2) + ge(4) + ge(5)
    table_ref[TABLE_VALID:TABLE_VALID + 1, :] = jnp.clip(fill - first_row, 0, BM)
    table_ref[TABLE_NUSED:TABLE_NUSED + 1, :] = nused

    lane = lax.broadcasted_iota(I32, (1, LANES), 1)
    tail_start = jnp.zeros((1, LANES), I32)
    tail_len = jnp.zeros((1, LANES), I32)
    for k in range(N_CLASSES):
        tail_start = jnp.where(lane == k, fills[k], tail_start)
        tail_len = jnp.where(lane == k, ends[k] - fills[k], tail_len)
    table_ref[TABLE_TAIL_START:TABLE_TAIL_START + 1, :] = tail_start
    table_ref[TABLE_TAIL_LEN:TABLE_TAIL_LEN + 1, :] = tail_len
    table_ref[TABLE_TAIL_LEN + 1:, :] = jnp.zeros((SUBLANES - TABLE_TAIL_LEN - 1, LANES), I32)


def _dispatch_plan(cnt, route):
    t = route.shape[1]
    return pl.pallas_call(
        _dispatch_kernel,
        grid=(1,),
        in_specs=[pl.BlockSpec((CLASS_ROWS, LANES), lambda i: (0, 0)),
                  pl.BlockSpec((2, t), lambda i: (0, 0))],
        out_specs=[pl.BlockSpec((t // TS, 1, TS), lambda i: (0, 0, 0)),
                   pl.BlockSpec((SUBLANES, LANES), lambda i: (0, 0))],
        out_shape=[jax.ShapeDtypeStruct((t // TS, 1, TS), I32),
                   jax.ShapeDtypeStruct((SUBLANES, LANES), I32)],
        name="dispatch_plan",
    )(cnt, route)


def _scatter_kernel(slot_ref, h1t_ref, xin_ref, xout_ref, sem):
    del xin_ref
    for t in range(TS):
        dst = _row(xout_ref, pl.multiple_of(slot_ref[0, 0, t], ROW_TILE))
        pltpu.make_async_copy(_row(h1t_ref, t * ROW_TILE), dst, sem).start(priority=t % DMA_QUEUES)
    pltpu.make_async_copy(h1t_ref, xout_ref.at[pl.ds(0, TS * ROW_TILE)], sem).wait()


def _first_scatter_kernel(plan_ref, slot_ref, h1t_ref, xout_ref, zeros, sem, zsem):
    i = pl.program_id(0)
    block = BM * ROW_TILE
    n_blocks = xout_ref.shape[0] // block

    def fill_rows(visit):
        for k in range(N_CLASSES):
            length = plan_ref[TABLE_TAIL_LEN, k]
            row = plan_ref[TABLE_TAIL_START, k]
            piece = BM // 2
            while piece >= 1:
                take = (length & piece) != 0
                dst = xout_ref.at[pl.ds(pl.multiple_of(row * ROW_TILE, ROW_TILE), piece * ROW_TILE)]
                cp = pltpu.make_async_copy(zeros.at[pl.ds(0, piece * ROW_TILE)], dst, zsem)
                pl.when(take)(functools.partial(visit, cp))
                row = row + jnp.where(take, piece, 0)
                piece //= 2

        def idle_block(j, carry):
            dst = xout_ref.at[pl.ds(pl.multiple_of(j * block, block), block)]
            visit(pltpu.make_async_copy(zeros, dst, zsem))
            return carry
        lax.fori_loop(plan_ref[TABLE_NUSED, 0], n_blocks, idle_block, 0)

    for t in range(TS):
        dst = _row(xout_ref, pl.multiple_of(slot_ref[0, 0, t], ROW_TILE))
        pltpu.make_async_copy(_row(h1t_ref, t * ROW_TILE), dst, sem).start(priority=t % DMA_QUEUES)

    @pl.when(i == 0)
    def _():
        zeros[...] = jnp.zeros(zeros.shape, F32)
        fill_rows(lambda cp: cp.start())

    pltpu.make_async_copy(h1t_ref, xout_ref.at[pl.ds(0, TS * ROW_TILE)], sem).wait()

    @pl.when(i == pl.num_programs(0) - 1)
    def _():
        fill_rows(lambda cp: cp.wait())


def _first_scatter_rows(plan, slot3, h1t, n_rows):
    nt = slot3.shape[0]
    grid_spec = pltpu.PrefetchScalarGridSpec(
        num_scalar_prefetch=1,
        grid=(nt,),
        in_specs=[
            pl.BlockSpec((1, 1, TS), lambda i, *_: (i, 0, 0), memory_space=pltpu.SMEM),
            pl.BlockSpec((TS * ROW_TILE, LANES), lambda i, *_: (i, 0)),
        ],
        out_specs=pl.BlockSpec(memory_space=pl.ANY),
        scratch_shapes=[pltpu.VMEM((BM * ROW_TILE, LANES), F32), pltpu.SemaphoreType.DMA(()),
                        pltpu.SemaphoreType.DMA(())],
    )
    return pl.pallas_call(
        _first_scatter_kernel,
        grid_spec=grid_spec,
        out_shape=jax.ShapeDtypeStruct((n_rows * ROW_TILE, LANES), F32),
        compiler_params=pltpu.CompilerParams(dimension_semantics=("arbitrary",)),
        name="first_scatter_rows",
    )(plan, slot3, h1t)


def _scatter_rows(slot3, h1t, xbuf):
    nt = slot3.shape[0]
    return pl.pallas_call(
        _scatter_kernel,
        grid=(nt,),
        in_specs=[
            pl.BlockSpec((1, 1, TS), lambda i: (i, 0, 0), memory_space=pltpu.SMEM),
            pl.BlockSpec((TS * ROW_TILE, LANES), lambda i: (i, 0)),
            pl.BlockSpec(memory_space=pl.ANY),
        ],
        out_specs=pl.BlockSpec(memory_space=pl.ANY),
        out_shape=jax.ShapeDtypeStruct(xbuf.shape, xbuf.dtype),
        scratch_shapes=[pltpu.SemaphoreType.DMA(())],
        input_output_aliases={2: 0},
        compiler_params=pltpu.CompilerParams(dimension_semantics=("arbitrary",)),
        name="scatter_rows",
    )(slot3, h1t, xbuf)


def _expert_kernel(plan_ref, x_ref, w1_hbm, w3_hbm, w2_hbm, wr_ref, y_ref,
                   wup, w2a, w2b, s1a, s3a, s2a, s1b, s3b, s2b, wsem, *, layer):
    i = pl.program_id(0)
    nb = pl.num_programs(0)
    nu = plan_ref[TABLE_NUSED, 0]
    used = i < nu
    ea = plan_ref[TABLE_EA, i]
    eb = plan_ref[TABLE_EB, i]

    def swap_weights(table_row, slot, live, staged, first_fetch_only=False):
        e_ref = plan_ref.at[table_row]
        e = e_ref[i]

        def fetch(expert):
            return [pltpu.make_async_copy(src.at[layer, expert], dst, wsem.at[slot])
                    for src, dst in zip((w1_hbm, w3_hbm, w2_hbm), staged)]

        if first_fetch_only:
            @pl.when(used & (i == 0))
            def _():
                for cp in fetch(e):
                    cp.start(priority=WEIGHT_DMA_QUEUE)
            return

        @pl.when(used & ((i == 0) | (e != e_ref[jnp.maximum(i - 1, 0)])))
        def _():
            for cp in fetch(e):
                cp.wait()
            for dst, src in zip(live, staged):
                dst[...] = src[...].astype(BF16)
            nxt = lax.while_loop(lambda j: (j < nu) & (e_ref[jnp.minimum(j, nb - 1)] == e),
                                 lambda j: j + 1, i + 1)

            @pl.when(nxt < nu)
            def _():
                for cp in fetch(e_ref[jnp.minimum(nxt, nb - 1)]):
                    cp.start(priority=WEIGHT_DMA_QUEUE)

    def up_cols(k):
        return wup.at[:, pl.ds(k * D_EXPERT, D_EXPERT)]

    slots = ((TABLE_EA, 0, (up_cols(0), up_cols(2), w2a), (s1a, s3a, s2a)),
             (TABLE_EB, 1, (up_cols(1), up_cols(3), w2b), (s1b, s3b, s2b)))
    for args in slots:
        swap_weights(*args, first_fetch_only=True)
    for args in slots:
        swap_weights(*args)

    def ffn_rows(rows):
        sublanes = pl.ds(0, rows * ROW_TILE)
        xb = _load_row_tiled(x_ref.at[sublanes], rows).astype(BF16)

        logits = _dot(xb, wr_ref[...].astype(BF16))
        lane = lax.broadcasted_iota(I32, logits.shape, 1)
        logits = jnp.where(lane < N_EXPERTS, logits, -jnp.inf)
        ex = jnp.exp(logits - jnp.max(logits, axis=1, keepdims=True))
        probs = ex / jnp.sum(ex, axis=1, keepdims=True)
        pa = jnp.sum(jnp.where(lane == ea, probs, 0.0), axis=1, keepdims=True)
        pb = jnp.sum(jnp.where(lane == eb, probs, 0.0), axis=1, keepdims=True)
        den = pa + pb

        up = _dot(xb, wup[...])
        a = up[:, 0:2 * D_EXPERT]
        hid = (a * jax.nn.sigmoid(a) * up[:, 2 * D_EXPERT:4 * D_EXPERT]).astype(BF16)
        y = ((pa / den) * _dot(hid[:, 0:D_EXPERT], w2a[...])
             + (pb / den) * _dot(hid[:, D_EXPERT:2 * D_EXPERT], w2b[...]))
        _store_row_tiled(y_ref.at[sublanes], y)

    few = plan_ref[TABLE_VALID, i] <= BM // 2

    @pl.when(used & jnp.logical_not(few))
    def _():
        ffn_rows(BM)

    @pl.when(used & few)
    def _():
        ffn_rows(BM // 2)
        rest = pl.ds(BM // 2 * ROW_TILE, BM // 2 * ROW_TILE)
        y_ref[rest, :] = jnp.zeros((BM // 2 * ROW_TILE, LANES), F32)


def _expert_ffn(layer, plan, xbuf, w1, w3, w2, wr_pad):
    n_blocks = xbuf.shape[0] // (BM * ROW_TILE)
    used_blk = lambda i, plan_ref: (jnp.minimum(i, plan_ref[TABLE_NUSED, 0] - 1), 0)
    up = (D_MODEL, D_EXPERT)
    down = (D_EXPERT, D_MODEL)
    expert_slot = lambda dtype: [pltpu.VMEM(up, dtype), pltpu.VMEM(up, dtype), pltpu.VMEM(down, dtype)]
    grid_spec = pltpu.PrefetchScalarGridSpec(
        num_scalar_prefetch=1,
        grid=(n_blocks,),
        in_specs=[
            pl.BlockSpec((BM * ROW_TILE, LANES), used_blk),
            pl.BlockSpec(memory_space=pl.ANY),
            pl.BlockSpec(memory_space=pl.ANY),
            pl.BlockSpec(memory_space=pl.ANY),
            pl.BlockSpec((D_MODEL, LANES), lambda i, plan_ref: (0, 0)),
        ],
        out_specs=pl.BlockSpec((BM * ROW_TILE, LANES), used_blk),
        scratch_shapes=([pltpu.VMEM((D_MODEL, 4 * D_EXPERT), BF16), pltpu.VMEM(down, BF16),
                         pltpu.VMEM(down, BF16)] + expert_slot(F32) + expert_slot(F32)
                        + [pltpu.SemaphoreType.DMA((2,))]),
    )
    return pl.pallas_call(
        functools.partial(_expert_kernel, layer=layer),
        grid_spec=grid_spec,
        out_shape=jax.ShapeDtypeStruct(xbuf.shape, F32),
        input_output_aliases={1: 0},
        compiler_params=pltpu.CompilerParams(
            dimension_semantics=("arbitrary",), vmem_limit_bytes=VMEM_LIMIT),
        name="expert_ffn",
    )(plan, xbuf, w1, w3, w2, wr_pad)


def _final_kernel(scur_ref, snext_ref, h1p_ref, y_hbm, g2_ref, b2_ref, o_ref, ybuf, gsem):
    i = pl.program_id(0)
    y = _prefetch_expert_rows(i, pl.num_programs(0), scur_ref, snext_ref, y_hbm, ybuf, gsem)
    o_ref[...] = _layer_norm(ALPHA * _load_row_tiled(h1p_ref, TS) + y,
                             g2_ref[DEPTH - 1:DEPTH, :], b2_ref[DEPTH - 1:DEPTH, :])


def _final_call(slot_prev, h1t_prev, y_prev, g2, b2):
    t = h1t_prev.shape[0] // ROW_TILE
    nt = t // TS
    return pl.pallas_call(
        _final_kernel,
        grid=(nt,),
        in_specs=[
            pl.BlockSpec((1, 1, TS), lambda i: (i, 0, 0), memory_space=pltpu.SMEM),
            pl.BlockSpec((1, 1, TS), lambda i: (jnp.minimum(i + 1, nt - 1), 0, 0),
                         memory_space=pltpu.SMEM),
            pl.BlockSpec((TS * ROW_TILE, LANES), lambda i: (i, 0)),
            pl.BlockSpec(memory_space=pl.ANY),
            pl.BlockSpec((DEPTH, D_MODEL), lambda i: (0, 0)),
            pl.BlockSpec((DEPTH, D_MODEL), lambda i: (0, 0)),
        ],
        out_specs=pl.BlockSpec((TS, D_MODEL), lambda i: (i, 0)),
        out_shape=jax.ShapeDtypeStruct((t, D_MODEL), F32),
        scratch_shapes=[pltpu.VMEM((2, TS * ROW_TILE, LANES), F32), pltpu.SemaphoreType.DMA((2,))],
        compiler_params=pltpu.CompilerParams(
            dimension_semantics=("arbitrary",), vmem_limit_bytes=VMEM_LIMIT),
        name="final_combine_ln",
    )(slot_prev, slot_prev, h1t_prev, y_prev, g2, b2)


def kernel(x, ln_in_g, ln_in_b, w_in, pool_w, pool_scale, conv_w, conv_b, w_out, ln1_g, ln1_b,
           w_router, router_bias, exp_w1, exp_w3, exp_w2, ln2_g, ln2_b):
    bsz, seq, d = x.shape
    t = bsz * seq
    n_rows = t + N_CLASSES * BM
    assert n_rows // BM <= LANES

    tri = (lax.broadcasted_iota(I32, (TS, TS), 0)
           < lax.broadcasted_iota(I32, (TS, TS), 1)).astype(BF16)
    wr_t = w_router.T
    wr_pad = jnp.pad(w_router, ((0, 0), (0, LANES - N_EXPERTS)))
    rbias = router_bias.reshape(N_EXPERTS, 1).astype(F32)

    prev = (x.reshape(t, d), ln_in_g.reshape(1, -1), ln_in_b.reshape(1, -1))
    xbuf = None
    for l in range(DEPTH):
        h1t, route, cnt = _layer_call(l, bsz, seq, prev, w_in, pool_w, pool_scale, conv_w, conv_b,
                                      w_out, ln1_g, ln1_b, wr_t, rbias, tri)
        slot3, plan = _dispatch_plan(cnt, route)
        if xbuf is None:
            xbuf = _first_scatter_rows(plan, slot3, h1t, n_rows)
        else:
            xbuf = _scatter_rows(slot3, h1t, xbuf)
        xbuf = _expert_ffn(l, plan, xbuf, exp_w1, exp_w3, exp_w2, wr_pad)
        prev = (slot3, h1t, xbuf, ln2_g, ln2_b)
    out = _final_call(*prev)
    return out.reshape(bsz, seq, d)
```

```python
import functools

import jax
import jax.numpy as jnp
from jax import lax
from jax.experimental import pallas as pl
from jax.experimental.pallas import tpu as pltpu

D_MODEL = 1024
DEPTH = 4
D_POOL = 512
POOL_WINDOWS = (2, 4, 8, 16)
POOL_GROUP = 128
D_CONV = 512
CONV_WIDTH = 3
D_PROJ = D_POOL + 3 * D_CONV
N_EXPERTS = 16
N_EXPERT_GROUPS = 4
EXPERTS_PER_GROUP = 4
PAIRS_PER_GROUP = 6
N_CLASSES = N_EXPERT_GROUPS * PAIRS_PER_GROUP
CLASS_ROWS = 32
D_EXPERT = 512
ALPHA = float((2 * DEPTH) ** 0.25)
LN_EPS = 1e-5

F32 = jnp.float32
BF16 = jnp.bfloat16
I32 = jnp.int32

LANES = 128
ROW_TILE = D_MODEL // LANES
TS = 512
SUBLANES = 8
V_HALO = 32
Z_HALO = 8
BM = 256
W_STAGE_ROWS = 256
DMA_QUEUES = 2
WEIGHT_DMA_QUEUE = 1
VMEM_LIMIT = 56 * 1024 * 1024


def _layer_norm(x, g, b):
    mu = jnp.mean(x, axis=-1, keepdims=True)
    xc = x - mu
    var = jnp.mean(xc * xc, axis=-1, keepdims=True)
    return xc * lax.rsqrt(var + LN_EPS) * g + b


def _dot(a, b):
    return jnp.dot(a, b, preferred_element_type=F32)


def _store_row_tiled(ref, value):
    n = value.shape[0]
    for j in range(ROW_TILE):
        ref[pl.ds(j, n, stride=ROW_TILE), :] = value[:, j * LANES:(j + 1) * LANES]


def _load_row_tiled(ref, n):
    return jnp.concatenate([ref[pl.ds(j, n, stride=ROW_TILE), :] for j in range(ROW_TILE)], axis=1)


def _row(ref, first_sublane):
    return ref.at[pl.ds(first_sublane, ROW_TILE)]


def _route(sel):
    rows = [sel[e:e + 1, :] for e in range(N_EXPERTS)]
    gscore = []
    for g in range(N_EXPERT_GROUPS):
        a, b, c, d = rows[4 * g:4 * g + 4]
        hi1, lo1 = jnp.maximum(a, b), jnp.minimum(a, b)
        hi2, lo2 = jnp.maximum(c, d), jnp.minimum(c, d)
        m1 = jnp.maximum(hi1, hi2)
        m2 = jnp.maximum(jnp.minimum(hi1, hi2), jnp.maximum(lo1, lo2))
        gscore.append(m1 + m2)
    best = gscore[0]
    gidx = jnp.zeros(best.shape, I32)
    for g in range(1, N_EXPERT_GROUPS):
        upd = gscore[g] > best
        best = jnp.where(upd, gscore[g], best)
        gidx = jnp.where(upd, g, gidx)
    neg = jnp.full(best.shape, -jnp.inf, F32)
    masked = [jnp.where(gidx == (e // EXPERTS_PER_GROUP), rows[e], neg) for e in range(N_EXPERTS)]

    def top1(vals):
        bv = neg
        bi = jnp.zeros(best.shape, I32)
        for e in range(N_EXPERTS):
            upd = vals[e] > bv
            bv = jnp.where(upd, vals[e], bv)
            bi = jnp.where(upd, e, bi)
        return bi

    i1 = top1(masked)
    i2 = top1([jnp.where(i1 == e, neg, masked[e]) for e in range(N_EXPERTS)])
    return gidx, i1, i2


def _pair_class(gidx, i1, i2):
    a = jnp.minimum(i1, i2) - gidx * EXPERTS_PER_GROUP
    b = jnp.maximum(i1, i2) - gidx * EXPERTS_PER_GROUP
    pair = jnp.where(b == 3, 3 + a, jnp.where(a == 1, 1, jnp.where(b == 1, 0, 2)))
    return gidx * PAIRS_PER_GROUP + pair


def _prefetch_expert_rows(i, nt, scur_ref, snext_ref, y_hbm, ybuf, gsem, before_wait=None):
    slot = i % 2

    def gather(slot_ref, to_slot, t, t_sublane):
        src = _row(y_hbm, pl.multiple_of(slot_ref[0, 0, t], ROW_TILE))
        return pltpu.make_async_copy(src, _row(ybuf.at[to_slot], t_sublane), gsem.at[to_slot])

    @pl.when(i == 0)
    def _():
        for t in range(TS):
            gather(scur_ref, 0, t, t * ROW_TILE).start(priority=t % DMA_QUEUES)

    @pl.when(i + 1 < nt)
    def _():
        for t in range(TS):
            gather(snext_ref, 1 - slot, t, t * ROW_TILE).start(priority=t % DMA_QUEUES)

    if before_wait is not None:
        before_wait()
    pltpu.make_async_copy(y_hbm.at[pl.ds(0, TS * ROW_TILE)], ybuf.at[slot], gsem.at[slot]).wait()
    return _load_row_tiled(ybuf.at[slot], TS)


def _layer_kernel(*refs, layer, tiles_per_seq):
    if layer == 0:
        x_ref, gin_ref, bin_ref = refs[:3]
        rest = refs[3:]
    else:
        scur_ref, snext_ref, h1p_ref, y_hbm, g2_ref, b2_ref = refs[:6]
        rest = refs[6:]
    (win_hbm, pw_ref, ps_ref, cw_ref, cb_ref, wout_hbm, g_ref, b_ref, wr_ref, rb_ref, tri_ref,
     h1t_ref, route_ref, cnt_ref,
     vbuf, pwork, zbuf, mixbuf, carry, win_ref, wout_ref, stage, wsem, ybuf, gsem) = rest

    i = pl.program_id(0)
    nt = pl.num_programs(0)

    chunks = []
    for src, dst, width in ((win_hbm, win_ref, D_PROJ), (wout_hbm, wout_ref, D_MODEL)):
        for c in range(D_MODEL // W_STAGE_ROWS):
            chunks.append((src, dst, width, pl.ds(c * W_STAGE_ROWS, W_STAGE_ROWS)))

    def fetch(k):
        src, _, width, rows = chunks[k]
        return pltpu.make_async_copy(src.at[layer, rows], stage.at[k % 2, :, pl.ds(0, width)],
                                     wsem.at[k % 2])

    pl.when(i == 0)(lambda: fetch(0).start())

    def stage_weights():
        for k, (_, dst, width, rows) in enumerate(chunks):
            if k + 1 < len(chunks):
                fetch(k + 1).start()
            fetch(k).wait()
            dst[rows, :] = stage[k % 2, :, 0:width].astype(BF16)
        carry[...] = jnp.zeros(carry.shape, F32)

    first_step_setup = functools.partial(pl.when(i == 0), stage_weights)
    if layer == 0:
        first_step_setup()
        h = _layer_norm(x_ref[...], gin_ref[...], bin_ref[...])
    else:
        y = _prefetch_expert_rows(i, nt, scur_ref, snext_ref, y_hbm, ybuf, gsem,
                                  before_wait=first_step_setup)
        h = _layer_norm(ALPHA * _load_row_tiled(h1p_ref, TS) + y,
                        g2_ref[layer - 1:layer, :], b2_ref[layer - 1:layer, :])

    si = i % tiles_per_seq

    @pl.when(si == 0)
    def _():
        vbuf[0:V_HALO, :] = jnp.zeros((V_HALO, D_POOL), F32)
        zbuf[0:Z_HALO, :] = jnp.zeros((Z_HALO, D_CONV), F32)

    hb = h.astype(BF16)
    vbuf[V_HALO:V_HALO + TS, :] = _dot(hb, win_ref[:, 0:D_POOL])
    gate_b = _dot(hb, win_ref[:, D_POOL:D_POOL + D_CONV])
    gate_c = _dot(hb, win_ref[:, D_POOL + D_CONV:D_POOL + 2 * D_CONV])
    u = _dot(hb, win_ref[:, D_POOL + 2 * D_CONV:D_PROJ])
    z = gate_c * u
    zbuf[Z_HALO:Z_HALO + TS, :] = z

    src = vbuf
    for level in range(1, len(POOL_WINDOWS)):
        dst = pwork.at[(level - 1) % 2]
        lo = SUBLANES * level
        shift = 2 ** (level - 1)
        n = TS + V_HALO - lo
        cols = pl.ds(level * POOL_GROUP, D_POOL - level * POOL_GROUP)
        dst[pl.ds(lo, n), cols] = src[pl.ds(lo, n), cols] + src[pl.ds(lo - shift, n), cols]
        src = dst
    tpos = si * TS + lax.broadcasted_iota(I32, (TS, 1), 0)
    for g, win in enumerate(POOL_WINDOWS):
        cols = pl.ds(g * POOL_GROUP, POOL_GROUP)
        cur = vbuf[pl.ds(V_HALO, TS), cols]
        half = vbuf if g == 0 else pwork.at[(g - 1) % 2]
        acc = half[pl.ds(V_HALO, TS), cols] + half[pl.ds(V_HALO - win // 2, TS), cols]
        denom = jnp.minimum(tpos + 1, win).astype(F32)
        pooled = acc / denom - cur
        mixed = _dot(pooled.astype(BF16), pw_ref[0, g].astype(BF16)) * ps_ref[layer:layer + 1, cols]
        mixbuf[:, cols] = mixed.astype(BF16)

    yc = (cw_ref[0, 2:3, :] * z
          + cw_ref[0, 1:2, :] * zbuf[pl.ds(Z_HALO - 1, TS), :]
          + cw_ref[0, 0:1, :] * zbuf[pl.ds(Z_HALO - 2, TS), :])
    mixbuf[:, D_POOL:D_POOL + D_CONV] = (gate_b * (yc + cb_ref[layer:layer + 1, :])).astype(BF16)

    vbuf[0:V_HALO, :] = vbuf[TS:TS + V_HALO, :]
    zbuf[0:Z_HALO, :] = zbuf[TS:TS + Z_HALO, :]

    mix = _dot(mixbuf[...], wout_ref[...])
    h1 = _layer_norm(ALPHA * h + mix, g_ref[layer:layer + 1, :], b_ref[layer:layer + 1, :])
    _store_row_tiled(h1t_ref, h1)

    logits = lax.dot_general(wr_ref[...].astype(BF16), h1.astype(BF16), (((1,), (1,)), ((), ())),
                             preferred_element_type=F32)
    mx = jnp.max(logits, axis=0, keepdims=True)
    ex = jnp.exp(logits - mx)
    probs = ex / jnp.sum(ex, axis=0, keepdims=True)
    gidx, i1, i2 = _route(probs + rb_ref[...])
    cls = _pair_class(gidx, i1, i2)

    ciota = lax.broadcasted_iota(I32, (CLASS_ROWS, TS), 0)
    onehot = jnp.where(ciota == cls, 1.0, 0.0).astype(F32)
    before = _dot(onehot.astype(BF16), tri_ref[...])
    total = jnp.sum(onehot, axis=1, keepdims=True)
    c = carry[:, 0:1]
    rank = jnp.sum(onehot * (c + before), axis=0, keepdims=True)
    route_ref[0:1, :] = cls
    route_ref[1:2, :] = rank.astype(I32)
    newc = jnp.broadcast_to(c + total, carry.shape)
    carry[...] = newc
    cnt_ref[...] = newc


def _layer_call(layer, bsz, seq, prev, w_in, pool_w, pool_scale, conv_w, conv_b, w_out, ln_g, ln_b,
                wr_t, rbias, tri):
    t = bsz * seq
    nt = t // TS
    const2 = lambda i: (0, 0)
    lsel3 = lambda i: (layer, 0, 0)
    lsel4 = lambda i: (layer, 0, 0, 0)
    tile = lambda i: (i, 0)
    lane_tile = lambda i: (0, i)
    if layer == 0:
        x, gin, bin_ = prev
        head_args = (x, gin, bin_)
        head_specs = [
            pl.BlockSpec((TS, D_MODEL), tile),
            pl.BlockSpec((1, D_MODEL), const2),
            pl.BlockSpec((1, D_MODEL), const2),
        ]
    else:
        slot_prev, h1t_prev, y_prev, g2, b2 = prev
        head_args = (slot_prev, slot_prev, h1t_prev, y_prev, g2, b2)
        head_specs = [
            pl.BlockSpec((1, 1, TS), lambda i: (i, 0, 0), memory_space=pltpu.SMEM),
            pl.BlockSpec((1, 1, TS), lambda i: (jnp.minimum(i + 1, nt - 1), 0, 0),
                         memory_space=pltpu.SMEM),
            pl.BlockSpec((TS * ROW_TILE, LANES), tile),
            pl.BlockSpec(memory_space=pl.ANY),
            pl.BlockSpec((DEPTH, D_MODEL), const2),
            pl.BlockSpec((DEPTH, D_MODEL), const2),
        ]
    return pl.pallas_call(
        functools.partial(_layer_kernel, layer=layer, tiles_per_seq=seq // TS),
        grid=(nt,),
        in_specs=head_specs + [
            pl.BlockSpec(memory_space=pl.ANY),
            pl.BlockSpec((1, len(POOL_WINDOWS), POOL_GROUP, POOL_GROUP), lsel4),
            pl.BlockSpec((DEPTH, D_POOL), const2),
            pl.BlockSpec((1, CONV_WIDTH, D_CONV), lsel3),
            pl.BlockSpec((DEPTH, D_CONV), const2),
            pl.BlockSpec(memory_space=pl.ANY),
            pl.BlockSpec((DEPTH, D_MODEL), const2),
            pl.BlockSpec((DEPTH, D_MODEL), const2),
            pl.BlockSpec((N_EXPERTS, D_MODEL), const2),
            pl.BlockSpec((N_EXPERTS, 1), const2),
            pl.BlockSpec((TS, TS), const2),
        ],
        out_specs=[
            pl.BlockSpec((TS * ROW_TILE, LANES), tile),
            pl.BlockSpec((2, TS), lane_tile),
            pl.BlockSpec((CLASS_ROWS, LANES), const2),
        ],
        out_shape=[
            jax.ShapeDtypeStruct((t * ROW_TILE, LANES), F32),
            jax.ShapeDtypeStruct((2, t), I32),
            jax.ShapeDtypeStruct((CLASS_ROWS, LANES), F32),
        ],
        scratch_shapes=[
            pltpu.VMEM((TS + V_HALO, D_POOL), F32),
            pltpu.VMEM((2, TS + V_HALO, D_POOL), F32),
            pltpu.VMEM((TS + Z_HALO, D_CONV), F32),
            pltpu.VMEM((TS, D_MODEL), BF16),
            pltpu.VMEM((CLASS_ROWS, LANES), F32),
            pltpu.VMEM((D_MODEL, D_PROJ), BF16),
            pltpu.VMEM((D_MODEL, D_MODEL), BF16),
            pltpu.VMEM((2, W_STAGE_ROWS, D_PROJ), F32),
            pltpu.SemaphoreType.DMA((2,)),
            pltpu.VMEM((2, TS * ROW_TILE, LANES), F32),
            pltpu.SemaphoreType.DMA((2,)),
        ],
        compiler_params=pltpu.CompilerParams(
            dimension_semantics=("arbitrary",), vmem_limit_bytes=VMEM_LIMIT),
        name="layer_mix_route",
    )(*head_args, w_in, pool_w, pool_scale, conv_w, conv_b, w_out, ln_g, ln_b, wr_t, rbias, tri)


TABLE_EA, TABLE_EB, TABLE_VALID, TABLE_NUSED = 0, 1, 2, 3
TABLE_TAIL_START, TABLE_TAIL_LEN = 4, 5


def _dispatch_kernel(cnt_ref, route_ref, slot_ref, table_ref):
    log_bm = BM.bit_length() - 1
    starts, ends, fills = [], [], []
    end = jnp.zeros((1, LANES), I32)
    for k in range(N_CLASSES):
        count = cnt_ref[k:k + 1, :].astype(I32)
        starts.append(end)
        fills.append(end + count)
        end = end + lax.shift_left(lax.shift_right_logical(count + (BM - 1), log_bm), log_bm)
        ends.append(end)
    nused = lax.shift_right_logical(end, log_bm)

    cls = route_ref[0:1, :]
    start = jnp.zeros(cls.shape, I32)
    for k in range(N_CLASSES):
        start = jnp.where(cls == k, starts[k][:, 0:1], start)
    slots = (start + route_ref[1:2, :]) * ROW_TILE
    for j in range(slot_ref.shape[0]):
        slot_ref[j] = slots[:, j * TS:(j + 1) * TS]

    step = jnp.minimum(lax.broadcasted_iota(I32, (1, LANES), 1), nused - 1)
    first_row = step * BM
    c = jnp.zeros((1, LANES), I32)
    fill = fills[0]
    for k in range(N_CLASSES - 1):
        beyond = first_row >= ends[k]
        c = c + beyond.astype(I32)
        fill = jnp.where(beyond, fills[k + 1], fill)
    g = sum((c >= k * PAIRS_PER_GROUP).astype(I32) for k in range(1, N_EXPERT_GROUPS))
    p = c - g * PAIRS_PER_GROUP
    ge = lambda k: (p >= k).astype(I32)
    table_ref[TABLE_EA:TABLE_EA + 1, :] = g * EXPERTS_PER_GROUP + 2 * ge(1) + ge(3)
    table_ref[TABLE_EB:TABLE_EB + 1, :] = g * EXPERTS_PER_GROUP + 1 - ge(2) + ge(4) + ge(5)
    table_ref[TABLE_VALID:TABLE_VALID + 1, :] = jnp.clip(fill - first_row, 0, BM)
    table_ref[TABLE_NUSED:TABLE_NUSED + 1, :] = nused

    lane = lax.broadcasted_iota(I32, (1, LANES), 1)
    tail_start = jnp.zeros((1, LANES), I32)
    tail_len = jnp.zeros((1, LANES), I32)
    for k in range(N_CLASSES):
        tail_start = jnp.where(lane == k, fills[k], tail_start)
        tail_len = jnp.where(lane == k, ends[k] - fills[k], tail_len)
    table_ref[TABLE_TAIL_START:TABLE_TAIL_START + 1, :] = tail_start
    table_ref[TABLE_TAIL_LEN:TABLE_TAIL_LEN + 1, :] = tail_len
    table_ref[TABLE_TAIL_LEN + 1:, :] = jnp.zeros((SUBLANES - TABLE_TAIL_LEN - 1, LANES), I32)


def _dispatch_plan(cnt, route):
    t = route.shape[1]
    return pl.pallas_call(
        _dispatch_kernel,
        grid=(1,),
        in_specs=[pl.BlockSpec((CLASS_ROWS, LANES), lambda i: (0, 0)),
                  pl.BlockSpec((2, t), lambda i: (0, 0))],
        out_specs=[pl.BlockSpec((t // TS, 1, TS), lambda i: (0, 0, 0)),
                   pl.BlockSpec((SUBLANES, LANES), lambda i: (0, 0))],
        out_shape=[jax.ShapeDtypeStruct((t // TS, 1, TS), I32),
                   jax.ShapeDtypeStruct((SUBLANES, LANES), I32)],
        name="dispatch_plan",
    )(cnt, route)


def _scatter_kernel(slot_ref, h1t_ref, xin_ref, xout_ref, sem):
    del xin_ref
    for t in range(TS):
        dst = _row(xout_ref, pl.multiple_of(slot_ref[0, 0, t], ROW_TILE))
        pltpu.make_async_copy(_row(h1t_ref, t * ROW_TILE), dst, sem).start(priority=t % DMA_QUEUES)
    pltpu.make_async_copy(h1t_ref, xout_ref.at[pl.ds(0, TS * ROW_TILE)], sem).wait()


def _first_scatter_kernel(plan_ref, slot_ref, h1t_ref, xout_ref, zeros, sem, zsem):
    i = pl.program_id(0)
    block = BM * ROW_TILE
    n_blocks = xout_ref.shape[0] // block

    def fill_rows(visit):
        for k in range(N_CLASSES):
            length = plan_ref[TABLE_TAIL_LEN, k]
            row = plan_ref[TABLE_TAIL_START, k]
            piece = BM // 2
            while piece >= 1:
                take = (length & piece) != 0
                dst = xout_ref.at[pl.ds(pl.multiple_of(row * ROW_TILE, ROW_TILE), piece * ROW_TILE)]
                cp = pltpu.make_async_copy(zeros.at[pl.ds(0, piece * ROW_TILE)], dst, zsem)
                pl.when(take)(functools.partial(visit, cp))
                row = row + jnp.where(take, piece, 0)
                piece //= 2

        def idle_block(j, carry):
            dst = xout_ref.at[pl.ds(pl.multiple_of(j * block, block), block)]
            visit(pltpu.make_async_copy(zeros, dst, zsem))
            return carry
        lax.fori_loop(plan_ref[TABLE_NUSED, 0], n_blocks, idle_block, 0)

    for t in range(TS):
        dst = _row(xout_ref, pl.multiple_of(slot_ref[0, 0, t], ROW_TILE))
        pltpu.make_async_copy(_row(h1t_ref, t * ROW_TILE), dst, sem).start(priority=t % DMA_QUEUES)

    @pl.when(i == 0)
    def _():
        zeros[...] = jnp.zeros(zeros.shape, F32)
        fill_rows(lambda cp: cp.start())

    pltpu.make_async_copy(h1t_ref, xout_ref.at[pl.ds(0, TS * ROW_TILE)], sem).wait()

    @pl.when(i == pl.num_programs(0) - 1)
    def _():
        fill_rows(lambda cp: cp.wait())


def _first_scatter_rows(plan, slot3, h1t, n_rows):
    nt = slot3.shape[0]
    grid_spec = pltpu.PrefetchScalarGridSpec(
        num_scalar_prefetch=1,
        grid=(nt,),
        in_specs=[
            pl.BlockSpec((1, 1, TS), lambda i, *_: (i, 0, 0), memory_space=pltpu.SMEM),
            pl.BlockSpec((TS * ROW_TILE, LANES), lambda i, *_: (i, 0)),
        ],
        out_specs=pl.BlockSpec(memory_space=pl.ANY),
        scratch_shapes=[pltpu.VMEM((BM * ROW_TILE, LANES), F32), pltpu.SemaphoreType.DMA(()),
                        pltpu.SemaphoreType.DMA(())],
    )
    return pl.pallas_call(
        _first_scatter_kernel,
        grid_spec=grid_spec,
        out_shape=jax.ShapeDtypeStruct((n_rows * ROW_TILE, LANES), F32),
        compiler_params=pltpu.CompilerParams(dimension_semantics=("arbitrary",)),
        name="first_scatter_rows",
    )(plan, slot3, h1t)


def _scatter_rows(slot3, h1t, xbuf):
    nt = slot3.shape[0]
    return pl.pallas_call(
        _scatter_kernel,
        grid=(nt,),
        in_specs=[
            pl.BlockSpec((1, 1, TS), lambda i: (i, 0, 0), memory_space=pltpu.SMEM),
            pl.BlockSpec((TS * ROW_TILE, LANES), lambda i: (i, 0)),
            pl.BlockSpec(memory_space=pl.ANY),
        ],
        out_specs=pl.BlockSpec(memory_space=pl.ANY),
        out_shape=jax.ShapeDtypeStruct(xbuf.shape, xbuf.dtype),
        scratch_shapes=[pltpu.SemaphoreType.DMA(())],
        input_output_aliases={2: 0},
        compiler_params=pltpu.CompilerParams(dimension_semantics=("arbitrary",)),
        name="scatter_rows",
    )(slot3, h1t, xbuf)


def _expert_kernel(plan_ref, x_ref, w1_hbm, w3_hbm, w2_hbm, wr_ref, y_ref,
                   wup, w2a, w2b, s1a, s3a, s2a, s1b, s3b, s2b, wsem, *, layer):
    i = pl.program_id(0)
    nb = pl.num_programs(0)
    nu = plan_ref[TABLE_NUSED, 0]
    used = i < nu
    ea = plan_ref[TABLE_EA, i]
    eb = plan_ref[TABLE_EB, i]

    def swap_weights(table_row, slot, live, staged, first_fetch_only=False):
        e_ref = plan_ref.at[table_row]
        e = e_ref[i]

        def fetch(expert):
            return [pltpu.make_async_copy(src.at[layer, expert], dst, wsem.at[slot])
                    for src, dst in zip((w1_hbm, w3_hbm, w2_hbm), staged)]

        if first_fetch_only:
            @pl.when(used & (i == 0))
            def _():
                for cp in fetch(e):
                    cp.start(priority=WEIGHT_DMA_QUEUE)
            return

        @pl.when(used & ((i == 0) | (e != e_ref[jnp.maximum(i - 1, 0)])))
        def _():
            for cp in fetch(e):
                cp.wait()
            for dst, src in zip(live, staged):
                dst[...] = src[...].astype(BF16)
            nxt = lax.while_loop(lambda j: (j < nu) & (e_ref[jnp.minimum(j, nb - 1)] == e),
                                 lambda j: j + 1, i + 1)

            @pl.when(nxt < nu)
            def _():
                for cp in fetch(e_ref[jnp.minimum(nxt, nb - 1)]):
                    cp.start(priority=WEIGHT_DMA_QUEUE)

    def up_cols(k):
        return wup.at[:, pl.ds(k * D_EXPERT, D_EXPERT)]

    slots = ((TABLE_EA, 0, (up_cols(0), up_cols(2), w2a), (s1a, s3a, s2a)),
             (TABLE_EB, 1, (up_cols(1), up_cols(3), w2b), (s1b, s3b, s2b)))
    for args in slots:
        swap_weights(*args, first_fetch_only=True)
    for args in slots:
        swap_weights(*args)

    def ffn_rows(rows):
        sublanes = pl.ds(0, rows * ROW_TILE)
        xb = _load_row_tiled(x_ref.at[sublanes], rows).astype(BF16)

        logits = _dot(xb, wr_ref[...].astype(BF16))
        lane = lax.broadcasted_iota(I32, logits.shape, 1)
        logits = jnp.where(lane < N_EXPERTS, logits, -jnp.inf)
        ex = jnp.exp(logits - jnp.max(logits, axis=1, keepdims=True))
        probs = ex / jnp.sum(ex, axis=1, keepdims=True)
        pa = jnp.sum(jnp.where(lane == ea, probs, 0.0), axis=1, keepdims=True)
        pb = jnp.sum(jnp.where(lane == eb, probs, 0.0), axis=1, keepdims=True)
        den = pa + pb

        up = _dot(xb, wup[...])
        a = up[:, 0:2 * D_EXPERT]
        hid = (a * jax.nn.sigmoid(a) * up[:, 2 * D_EXPERT:4 * D_EXPERT]).astype(BF16)
        y = ((pa / den) * _dot(hid[:, 0:D_EXPERT], w2a[...])
             + (pb / den) * _dot(hid[:, D_EXPERT:2 * D_EXPERT], w2b[...]))
        _store_row_tiled(y_ref.at[sublanes], y)

    few = plan_ref[TABLE_VALID, i] <= BM // 2

    @pl.when(used & jnp.logical_not(few))
    def _():
        ffn_rows(BM)

    @pl.when(used & few)
    def _():
        ffn_rows(BM // 2)
        rest = pl.ds(BM // 2 * ROW_TILE, BM // 2 * ROW_TILE)
        y_ref[rest, :] = jnp.zeros((BM // 2 * ROW_TILE, LANES), F32)


def _expert_ffn(layer, plan, xbuf, w1, w3, w2, wr_pad):
    n_blocks = xbuf.shape[0] // (BM * ROW_TILE)
    used_blk = lambda i, plan_ref: (jnp.minimum(i, plan_ref[TABLE_NUSED, 0] - 1), 0)
    up = (D_MODEL, D_EXPERT)
    down = (D_EXPERT, D_MODEL)
    expert_slot = lambda dtype: [pltpu.VMEM(up, dtype), pltpu.VMEM(up, dtype), pltpu.VMEM(down, dtype)]
    grid_spec = pltpu.PrefetchScalarGridSpec(
        num_scalar_prefetch=1,
        grid=(n_blocks,),
        in_specs=[
            pl.BlockSpec((BM * ROW_TILE, LANES), used_blk),
            pl.BlockSpec(memory_space=pl.ANY),
            pl.BlockSpec(memory_space=pl.ANY),
            pl.BlockSpec(memory_space=pl.ANY),
            pl.BlockSpec((D_MODEL, LANES), lambda i, plan_ref: (0, 0)),
        ],
        out_specs=pl.BlockSpec((BM * ROW_TILE, LANES), used_blk),
        scratch_shapes=([pltpu.VMEM((D_MODEL, 4 * D_EXPERT), BF16), pltpu.VMEM(down, BF16),
                         pltpu.VMEM(down, BF16)] + expert_slot(F32) + expert_slot(F32)
                        + [pltpu.SemaphoreType.DMA((2,))]),
    )
    return pl.pallas_call(
        functools.partial(_expert_kernel, layer=layer),
        grid_spec=grid_spec,
        out_shape=jax.ShapeDtypeStruct(xbuf.shape, F32),
        input_output_aliases={1: 0},
        compiler_params=pltpu.CompilerParams(
            dimension_semantics=("arbitrary",), vmem_limit_bytes=VMEM_LIMIT),
        name="expert_ffn",
    )(plan, xbuf, w1, w3, w2, wr_pad)


def _final_kernel(scur_ref, snext_ref, h1p_ref, y_hbm, g2_ref, b2_ref, o_ref, ybuf, gsem):
    i = pl.program_id(0)
    y = _prefetch_expert_rows(i, pl.num_programs(0), scur_ref, snext_ref, y_hbm, ybuf, gsem)
    o_ref[...] = _layer_norm(ALPHA * _load_row_tiled(h1p_ref, TS) + y,
                             g2_ref[DEPTH - 1:DEPTH, :], b2_ref[DEPTH - 1:DEPTH, :])


def _final_call(slot_prev, h1t_prev, y_prev, g2, b2):
    t = h1t_prev.shape[0] // ROW_TILE
    nt = t // TS
    return pl.pallas_call(
        _final_kernel,
        grid=(nt,),
        in_specs=[
            pl.BlockSpec((1, 1, TS), lambda i: (i, 0, 0), memory_space=pltpu.SMEM),
            pl.BlockSpec((1, 1, TS), lambda i: (jnp.minimum(i + 1, nt - 1), 0, 0),
                         memory_space=pltpu.SMEM),
            pl.BlockSpec((TS * ROW_TILE, LANES), lambda i: (i, 0)),
            pl.BlockSpec(memory_space=pl.ANY),
            pl.BlockSpec((DEPTH, D_MODEL), lambda i: (0, 0)),
            pl.BlockSpec((DEPTH, D_MODEL), lambda i: (0, 0)),
        ],
        out_specs=pl.BlockSpec((TS, D_MODEL), lambda i: (i, 0)),
        out_shape=jax.ShapeDtypeStruct((t, D_MODEL), F32),
        scratch_shapes=[pltpu.VMEM((2, TS * ROW_TILE, LANES), F32), pltpu.SemaphoreType.DMA((2,))],
        compiler_params=pltpu.CompilerParams(
            dimension_semantics=("arbitrary",), vmem_limit_bytes=VMEM_LIMIT),
        name="final_combine_ln",
    )(slot_prev, slot_prev, h1t_prev, y_prev, g2, b2)


def kernel(x, ln_in_g, ln_in_b, w_in, pool_w, pool_scale, conv_w, conv_b, w_out, ln1_g, ln1_b,
           w_router, router_bias, exp_w1, exp_w3, exp_w2, ln2_g, ln2_b):
    bsz, seq, d = x.shape
    t = bsz * seq
    n_rows = t + N_CLASSES * BM
    assert n_rows // BM <= LANES

    tri = (lax.broadcasted_iota(I32, (TS, TS), 0)
           < lax.broadcasted_iota(I32, (TS, TS), 1)).astype(BF16)
    wr_t = w_router.T
    wr_pad = jnp.pad(w_router, ((0, 0), (0, LANES - N_EXPERTS)))
    rbias = router_bias.reshape(N_EXPERTS, 1).astype(F32)

    prev = (x.reshape(t, d), ln_in_g.reshape(1, -1), ln_in_b.reshape(1, -1))
    xbuf = None
    for l in range(DEPTH):
        h1t, route, cnt = _layer_call(l, bsz, seq, prev, w_in, pool_w, pool_scale, conv_w, conv_b,
                                      w_out, ln1_g, ln1_b, wr_t, rbias, tri)
        slot3, plan = _dispatch_plan(cnt, route)
        if xbuf is None:
            xbuf = _first_scatter_rows(plan, slot3, h1t, n_rows)
        else:
            xbuf = _scatter_rows(slot3, h1t, xbuf)
        xbuf = _expert_ffn(l, plan, xbuf, exp_w1, exp_w3, exp_w2, wr_pad)
        prev = (slot3, h1t, xbuf, ln2_g, ln2_b)
    out = _final_call(*prev)
    return out.reshape(bsz, seq, d)
```
